```python
import jax, jax.numpy as jnp
from jax import lax
import numpy as np

D_MODEL = 1024
BATCH = 32
SEQ = 256
DEPTH = 2
DEC_BATCH = 4
DEC_SEQ = 2048
PAST_LEN = 256

GRID_W = 64
HEAD_DIM = 128
N_Q_HEADS = 4
N_KV_HEADS = 2
Q_PER_KV = N_Q_HEADS // N_KV_HEADS
ATTN_WIDTH = N_Q_HEADS * HEAD_DIM
KV_WIDTH = N_KV_HEADS * HEAD_DIM
WINDOW = 128
BLOCK = 128
ATTN_SCALE = HEAD_DIM ** -0.5
ROPE_BASE = 10000.0
NEG_INF = -1e30
RNN_WIDTH = 512
RNN_BLOCKS = 8
RNN_BLOCK_W = RNN_WIDTH // RNN_BLOCKS
CONV_W = 4
CONV_PAD_LEFT = 2
LRU_C = 8.0
AB_IN_WIDTH = ATTN_WIDTH + 2 * KV_WIDTH + 2 * RNN_WIDTH
AB_SPLITS = (ATTN_WIDTH, ATTN_WIDTH + KV_WIDTH, ATTN_WIDTH + 2 * KV_WIDTH, ATTN_WIDTH + 2 * KV_WIDTH + RNN_WIDTH)
AB_MIX_WIDTH = ATTN_WIDTH + RNN_WIDTH
CHUNK = 128
SGU_WIDTH = D_MODEL
SGU_GROUPS = 8
SGU_GROUP_W = SGU_WIDTH // SGU_GROUPS
N_EXPERTS = 16
EXPERT_FF = 2048
EC_FACTOR = 2
N_EVEN = (DEPTH + 1) // 2
N_ODD = DEPTH // 2
ALPHA = (2 * DEPTH) ** 0.25
BETA = (8 * DEPTH) ** -0.25
LN_EPS = 1e-6

kernel_name = "hybrid_diffusion_trunk_step"


def layer_norm(x, g, b):
    xf = x.astype(jnp.float32)
    mu = xf.mean(-1, keepdims=True)
    var = jnp.square(xf - mu).mean(-1, keepdims=True)
    return ((xf - mu) * lax.rsqrt(var + LN_EPS)).astype(x.dtype) * g + b


def adaln(cvec, w, b):
    m = jax.nn.silu(cvec) @ w + b
    return jnp.split(m[:, None, :], 6, axis=-1)


def modulate(x, shift, scale):
    return x * (1 + scale) + shift


def axial_rope(x):
    T = x.shape[1]
    rows = T // GRID_W
    row = jnp.repeat(jnp.arange(rows, dtype=jnp.float32), GRID_W)
    col = jnp.tile(jnp.arange(GRID_W, dtype=jnp.float32), rows)
    n_freq = HEAD_DIM // 4
    freqs = ROPE_BASE ** (-jnp.arange(n_freq, dtype=jnp.float32) / n_freq)
    ang = jnp.concatenate([row[:, None] * freqs, col[:, None] * freqs], axis=-1)
    cos = jnp.cos(ang)[None, :, None, :]
    sin = jnp.sin(ang)[None, :, None, :]
    xf = x.astype(jnp.float32)
    x1, x2 = xf[..., 0::2], xf[..., 1::2]
    out = jnp.stack([x1 * cos - x2 * sin, x1 * sin + x2 * cos], axis=-1).reshape(x.shape)
    return out.astype(x.dtype)


def sink_softmax(scores, sink):
    s = jnp.concatenate([scores, jnp.broadcast_to(sink, scores.shape[:-1] + (1,))], axis=-1)
    return jax.nn.softmax(s, axis=-1)[..., :-1]


def ab_project(h, in_w):
    B, T, _ = h.shape
    q, k, v, xr, xg = jnp.split(h @ in_w, AB_SPLITS, axis=-1)
    return (q.reshape(B, T, N_Q_HEADS, HEAD_DIM), k.reshape(B, T, N_KV_HEADS, HEAD_DIM),
            v.reshape(B, T, N_KV_HEADS, HEAD_DIM), xr, xg)


def context_attention(q, k, v, sink):
    B, S = q.shape[:2]
    qg = q.reshape(B, S, N_KV_HEADS, Q_PER_KV, HEAD_DIM)
    s = jnp.einsum('bqkgd,bskd->bkgqs', qg, k).astype(jnp.float32) * ATTN_SCALE
    p = sink_softmax(s, sink.reshape(N_KV_HEADS, Q_PER_KV)[:, :, None, None].astype(jnp.float32))
    o = jnp.einsum('bkgqs,bskd->bqkgd', p.astype(v.dtype), v)
    return o.reshape(B, S, ATTN_WIDTH)


def latent_attention(q, k, v, k_ctx, v_ctx, sink):
    B, T = q.shape[:2]
    P = k_ctx.shape[1]
    nb = T // BLOCK
    qb = q.reshape(B, nb, BLOCK, N_KV_HEADS, Q_PER_KV, HEAD_DIM)

    def bands(t):
        tp = jnp.pad(t, ((0, 0), (BLOCK, BLOCK), (0, 0), (0, 0))).reshape(B, nb + 2, BLOCK, N_KV_HEADS, HEAD_DIM)
        return jnp.concatenate([tp[:, :-2], tp[:, 1:-1], tp[:, 2:]], axis=2)

    kw, vw = bands(k), bands(v)
    blk = jnp.arange(nb)[:, None, None] * BLOCK
    qpos = blk + jnp.arange(BLOCK)[None, :, None]
    kpos = blk - BLOCK + jnp.arange(3 * BLOCK)[None, None, :]
    valid = (jnp.abs(qpos - kpos) <= WINDOW) & (kpos >= 0) & (kpos < T)
    s_win = jnp.einsum('bnqkgd,bnskd->bnkgqs', qb, kw).astype(jnp.float32) * ATTN_SCALE
    s_win = jnp.where(valid[None, :, None, None], s_win, NEG_INF)
    s_ctx = jnp.einsum('bnqkgd,bskd->bnkgqs', qb, k_ctx).astype(jnp.float32) * ATTN_SCALE
    p = sink_softmax(jnp.concatenate([s_ctx, s_win], axis=-1),
                     sink.reshape(N_KV_HEADS, Q_PER_KV)[:, :, None, None].astype(jnp.float32)).astype(v.dtype)
    o = (jnp.einsum('bnkgqs,bskd->bnqkgd', p[..., :P], v_ctx)
         + jnp.einsum('bnkgqs,bnskd->bnqkgd', p[..., P:], vw))
    return o.reshape(B, T, ATTN_WIDTH)


def centred_conv(x, w, b):
    T = x.shape[1]
    xp = jnp.pad(x, ((0, 0), (CONV_PAD_LEFT, CONV_W - 1 - CONV_PAD_LEFT), (0, 0)))
    return sum(xp[:, i:i + T] * w[i] for i in range(CONV_W)) + b


def block_diag(x, w, b):
    B, T, _ = x.shape
    y = jnp.einsum('btnc,ncd->btnd', x.reshape(B, T, RNN_BLOCKS, RNN_BLOCK_W), w)
    return y.reshape(B, T, RNN_WIDTH) + b


def _linrec(e1, e2):
    a1, b1 = e1
    a2, b2 = e2
    return a1 * a2, a2 * b1 + b2


def lru_scan(x, wa, ba, wx, bx, lam, h0):
    r = jax.nn.sigmoid(block_diag(x, wa, ba).astype(jnp.float32))
    i = jax.nn.sigmoid(block_diag(x, wx, bx).astype(jnp.float32))
    log_a = -LRU_C * r * jax.nn.softplus(-lam.astype(jnp.float32))
    a = jnp.exp(log_a)
    u = jnp.sqrt(-jnp.expm1(2.0 * log_a)) * (i * x.astype(jnp.float32))
    u = u.at[:, 0].add(a[:, 0] * h0.astype(jnp.float32))
    _, h = lax.associative_scan(_linrec, (a, u), axis=1)
    return h


def rglru_bidirectional(xr, xg, conv_w, conv_b, wa, ba, wx, bx, lam, h0):
    xc = centred_conv(xr, conv_w, conv_b)
    hf = lru_scan(xc, wa[0], ba[0], wx[0], bx[0], lam[0], h0[:, 0])
    hb = jnp.flip(lru_scan(jnp.flip(xc, axis=1), wa[1], ba[1], wx[1], bx[1], lam[1], h0[:, 1]), axis=1)
    y = (hf + hb).astype(xr.dtype) * jax.nn.gelu(xg)
    return y, hf, hb


def sgu_mixer(h, in_w, in_b, ln_g, ln_b, sp_w, sp_b, out_w):
    B, T, _ = h.shape
    nc = T // CHUNK
    u, v = jnp.split(jax.nn.gelu(h @ in_w + in_b), 2, axis=-1)
    v = layer_norm(v, ln_g, ln_b).reshape(B, nc, CHUNK, SGU_GROUPS, SGU_GROUP_W)
    mixed = jnp.einsum('gpq,bnqgc->bnpgc', sp_w, v) + sp_b.T[None, None, :, :, None]
    return (u * mixed.reshape(B, T, SGU_WIDTH)) @ out_w


def expert_choice_moe(x, router_w, w1, w3, w2):
    B, T, D = x.shape
    cap = EC_FACTOR * T // N_EXPERTS
    aff = jax.nn.softmax((x @ router_w).astype(jnp.float32), axis=-1)
    gate, idx = lax.top_k(jnp.swapaxes(aff, 1, 2), cap)
    xg = jax.vmap(lambda xb, ib: xb[ib])(x, idx)
    hid = jax.nn.silu(jnp.einsum('becd,edf->becf', xg, w1)) * jnp.einsum('becd,edf->becf', xg, w3)
    yg = jnp.einsum('becf,efd->becd', hid, w2) * gate[..., None].astype(x.dtype)
    return jax.vmap(lambda ib, yb: jnp.zeros((T, D), yb.dtype).at[ib.reshape(-1)].add(yb.reshape(-1, D)))(idx, yg)


def setup_inputs(seed: int = 0) -> dict:
    key = jax.random.key(seed)
    ks = iter(jax.random.split(key, 64))
    f32 = jnp.float32

    def nrm(shape, std):
        return jax.random.normal(next(ks), shape, f32) * std

    def gain(shape):
        return 1.0 + nrm(shape, 0.02)

    a_init = jax.random.uniform(next(ks), (N_EVEN, 2, RNN_WIDTH), f32, minval=0.9, maxval=0.999) ** (1.0 / LRU_C)
    return {
        "x_prompt": nrm((BATCH, SEQ, D_MODEL), 1.0),
        "x_sample": nrm((DEC_BATCH, DEC_SEQ, D_MODEL), 1.0),
        "cache_k": nrm((DEC_BATCH, N_EVEN, PAST_LEN, N_KV_HEADS, HEAD_DIM), 1.0),
        "cache_v": nrm((DEC_BATCH, N_EVEN, PAST_LEN, N_KV_HEADS, HEAD_DIM), 1.0),
        "state_rglru": nrm((DEC_BATCH, N_EVEN, 2, RNN_WIDTH), 0.5),
        "c": nrm((DEC_BATCH, D_MODEL), 1.0),
        "c_ctx": nrm((D_MODEL,), 1.0),
        "mod_w": nrm((DEPTH, D_MODEL, 6 * D_MODEL), 0.5 * D_MODEL ** -0.5),
        "mod_b": nrm((DEPTH, 6 * D_MODEL), 0.02),
        "ln_mix_g": gain((DEPTH, D_MODEL)),
        "ln_mix_b": nrm((DEPTH, D_MODEL), 0.02),
        "ln_ffn_g": gain((DEPTH, D_MODEL)),
        "ln_ffn_b": nrm((DEPTH, D_MODEL), 0.02),
        "ab_in_w": nrm((N_EVEN, D_MODEL, AB_IN_WIDTH), D_MODEL ** -0.5),
        "attn_sink": nrm((N_EVEN, N_Q_HEADS), 0.5),
        "rnn_conv_w": nrm((N_EVEN, CONV_W, RNN_WIDTH), CONV_W ** -0.5),
        "rnn_conv_b": nrm((N_EVEN, RNN_WIDTH), 0.02),
        "lru_wa": nrm((N_EVEN, 2, RNN_BLOCKS, RNN_BLOCK_W, RNN_BLOCK_W), RNN_BLOCK_W ** -0.5),
        "lru_ba": nrm((N_EVEN, 2, RNN_WIDTH), 0.05),
        "lru_wx": nrm((N_EVEN, 2, RNN_BLOCKS, RNN_BLOCK_W, RNN_BLOCK_W), RNN_BLOCK_W ** -0.5),
        "lru_bx": nrm((N_EVEN, 2, RNN_WIDTH), 0.05),
        "lru_lambda": jnp.log(a_init) - jnp.log1p(-a_init),
        "ab_out_w": nrm((N_EVEN, AB_MIX_WIDTH, D_MODEL), BETA * AB_MIX_WIDTH ** -0.5),
        "sgu_in_w": nrm((N_ODD, D_MODEL, 2 * SGU_WIDTH), D_MODEL ** -0.5),
        "sgu_in_b": nrm((N_ODD, 2 * SGU_WIDTH), 0.02),
        "sgu_ln_g": gain((N_ODD, SGU_WIDTH)),
        "sgu_ln_b": nrm((N_ODD, SGU_WIDTH), 0.02),
        "sgu_spatial_w": nrm((N_ODD, SGU_GROUPS, CHUNK, CHUNK), 0.5 * CHUNK ** -0.5),
        "sgu_spatial_b": 1.0 + nrm((N_ODD, SGU_GROUPS, CHUNK), 0.1),
        "sgu_out_w": nrm((N_ODD, SGU_WIDTH, D_MODEL), BETA * SGU_WIDTH ** -0.5),
        "router_w": nrm((DEPTH, D_MODEL, N_EXPERTS), D_MODEL ** -0.5),
        "moe_w1": nrm((DEPTH, N_EXPERTS, D_MODEL, EXPERT_FF), D_MODEL ** -0.5),
        "moe_w3": nrm((DEPTH, N_EXPERTS, D_MODEL, EXPERT_FF), D_MODEL ** -0.5),
        "moe_w2": nrm((DEPTH, N_EXPERTS, EXPERT_FF, D_MODEL), BETA * EXPERT_FF ** -0.5),
    }


def reference(x_prompt, x_sample, cache_k, cache_v, state_rglru, c, c_ctx,
              mod_w, mod_b, ln_mix_g, ln_mix_b, ln_ffn_g, ln_ffn_b,
              ab_in_w, attn_sink, rnn_conv_w, rnn_conv_b, lru_wa, lru_ba, lru_wx, lru_bx, lru_lambda, ab_out_w,
              sgu_in_w, sgu_in_b, sgu_ln_g, sgu_ln_b, sgu_spatial_w, sgu_spatial_b, sgu_out_w,
              router_w, moe_w1, moe_w3, moe_w2):
    xp, xs = x_prompt, x_sample
    ctx_keys, ctx_vals, ctx_states = [], [], []
    for l in range(DEPTH):
        mp = adaln(c_ctx[None], mod_w[l], mod_b[l])
        ms = adaln(c, mod_w[l], mod_b[l])
        hp = modulate(xp, mp[0], mp[1])
        hs = modulate(xs, ms[0], ms[1])
        e = l // 2
        if l % 2 == 0:
            q, k, v, xr, xg = ab_project(hp, ab_in_w[e])
            att = context_attention(q, k, v, attn_sink[e])
            h0 = jnp.zeros((xp.shape[0], 2, RNN_WIDTH), xp.dtype)
            rnn, hf, hb = rglru_bidirectional(xr, xg, rnn_conv_w[e], rnn_conv_b[e], lru_wa[e], lru_ba[e],
                                             lru_wx[e], lru_bx[e], lru_lambda[e], h0)
            op = jnp.concatenate([att, rnn], axis=-1) @ ab_out_w[e]
            ctx_keys.append(k)
            ctx_vals.append(v)
            ctx_states.append(jnp.stack([hf[:, -1], hb[:, 0]], axis=1).astype(xp.dtype))
            q, k, v, xr, xg = ab_project(hs, ab_in_w[e])
            att = latent_attention(axial_rope(q), axial_rope(k), v, cache_k[:, e], cache_v[:, e], attn_sink[e])
            rnn, _, _ = rglru_bidirectional(xr, xg, rnn_conv_w[e], rnn_conv_b[e], lru_wa[e], lru_ba[e],
                                           lru_wx[e], lru_bx[e], lru_lambda[e], state_rglru[:, e])
            os_ = jnp.concatenate([att, rnn], axis=-1) @ ab_out_w[e]
        else:
            op = sgu_mixer(hp, sgu_in_w[e], sgu_in_b[e], sgu_ln_g[e], sgu_ln_b[e],
                           sgu_spatial_w[e], sgu_spatial_b[e], sgu_out_w[e])
            os_ = sgu_mixer(hs, sgu_in_w[e], sgu_in_b[e], sgu_ln_g[e], sgu_ln_b[e],
                            sgu_spatial_w[e], sgu_spatial_b[e], sgu_out_w[e])
        xp = layer_norm(ALPHA * xp + mp[2] * op, ln_mix_g[l], ln_mix_b[l])
        xs = layer_norm(ALPHA * xs + ms[2] * os_, ln_mix_g[l], ln_mix_b[l])
        fp = expert_choice_moe(modulate(xp, mp[3], mp[4]), router_w[l], moe_w1[l], moe_w3[l], moe_w2[l])
        fs = expert_choice_moe(modulate(xs, ms[3], ms[4]), router_w[l], moe_w1[l], moe_w3[l], moe_w2[l])
        xp = layer_norm(ALPHA * xp + mp[5] * fp, ln_ffn_g[l], ln_ffn_b[l])
        xs = layer_norm(ALPHA * xs + ms[5] * fs, ln_ffn_g[l], ln_ffn_b[l])
    new_cache_k = jnp.stack(ctx_keys, axis=1)
    new_cache_v = jnp.stack(ctx_vals, axis=1)
    new_state_rglru = jnp.stack(ctx_states, axis=1)
    return (xp, xs, new_cache_k, new_cache_v, new_state_rglru)
```

```python
import functools

import jax
import jax.numpy as jnp
from jax import lax
from jax.experimental import pallas as pl
from jax.experimental.pallas import tpu as pltpu

F32 = jnp.float32
BF16 = jnp.bfloat16
I32 = jnp.int32

D_MODEL = 1024
BATCH = 32
SEQ = 256
DEPTH = 2
DEC_BATCH = 4
DEC_SEQ = 2048
PAST_LEN = 256
GRID_W = 64
HEAD_DIM = 128
N_Q_HEADS = 4
N_KV_HEADS = 2
Q_PER_KV = N_Q_HEADS // N_KV_HEADS
ATTN_WIDTH = N_Q_HEADS * HEAD_DIM
KV_WIDTH = N_KV_HEADS * HEAD_DIM
WINDOW = 128
BLOCK = 128
ATTN_SCALE = HEAD_DIM ** -0.5
ROPE_BASE = 10000.0
NEG_INF = -1e30
RNN_WIDTH = 512
RNN_BLOCKS = 8
RNN_BLOCK_W = RNN_WIDTH // RNN_BLOCKS
CONV_W = 4
CONV_PAD_LEFT = 2
LRU_C = 8.0
AB_IN_WIDTH = ATTN_WIDTH + 2 * KV_WIDTH + 2 * RNN_WIDTH
CHUNK = 128
SGU_WIDTH = D_MODEL
SGU_GROUPS = 8
SGU_GROUP_W = SGU_WIDTH // SGU_GROUPS
N_EXPERTS = 16
EXPERT_FF = 2048
EC_FACTOR = 2
ALPHA = (2 * DEPTH) ** 0.25
LN_EPS = 1e-6

N_PROMPT = BATCH * SEQ
N_SAMPLE = DEC_BATCH * DEC_SEQ
CAP_P = EC_FACTOR * SEQ // N_EXPERTS
CAP_S = EC_FACTOR * DEC_SEQ // N_EXPERTS
SLOTS_P = BATCH * CAP_P
SLOTS_S = DEC_BATCH * CAP_S

LANES = 128
SUBLANES = 8
ROW_TILE = 512
PREFIX_BLOCK = 256
VMEM_LIMIT = 56 * 1024 * 1024


def _params(n_axes=1):
    return pltpu.CompilerParams(dimension_semantics=("arbitrary",) * n_axes,
                                vmem_limit_bytes=VMEM_LIMIT)


def _layer_norm(x, g, b):
    mu = jnp.mean(x, axis=-1, keepdims=True)
    xc = x - mu
    var = jnp.mean(xc * xc, axis=-1, keepdims=True)
    return xc * lax.rsqrt(var + LN_EPS) * g + b


def _gelu_tanh(x):
    return 0.5 * x * (1.0 + jnp.tanh(0.7978845608028654 * (x + 0.044715 * (x * x * x))))


def _dot(a, b):
    return jnp.dot(a, b, preferred_element_type=F32)


def _dot_nt(a, b):
    return lax.dot_general(a, b, (((1,), (1,)), ((), ())), preferred_element_type=F32)


def _mod_kernel(c_ref, w_ref, b_ref, o_ref):
    c = c_ref[...]
    s = c * jax.nn.sigmoid(c)
    o_ref[...] = _dot(s.astype(BF16), w_ref[...].astype(BF16)) + b_ref[...]


def _modulation(cvec8, mod_w, mod_b):
    n_col = 6 * D_MODEL // D_MODEL
    return pl.pallas_call(
        _mod_kernel,
        grid=(DEPTH, n_col),
        in_specs=[pl.BlockSpec((SUBLANES, D_MODEL), lambda l, j: (0, 0)),
                  pl.BlockSpec((None, D_MODEL, D_MODEL), lambda l, j: (l, 0, j)),
                  pl.BlockSpec((None, 1, D_MODEL), lambda l, j: (l, 0, j))],
        out_specs=pl.BlockSpec((None, SUBLANES, D_MODEL), lambda l, j: (l, 0, j)),
        out_shape=jax.ShapeDtypeStruct((DEPTH, SUBLANES, 6 * D_MODEL), F32),
        compiler_params=_params(2),
        name="adaln_modulation",
    )(cvec8, mod_w, mod_b.reshape(DEPTH, 1, 6 * D_MODEL))


def _group_map(group0, rows_per_group):
    tiles_per_group = rows_per_group // ROW_TILE
    return lambda i: (group0 + i // tiles_per_group, 0, 0)


def _rope(t, cos, sin_signed):
    lane = lax.broadcasted_iota(I32, t.shape, 1)
    swapped = jnp.where((lane & 1) == 0, pltpu.roll(t, HEAD_DIM - 1, 1), pltpu.roll(t, 1, 1))
    return t * cos + swapped * sin_signed


def _ab_in_kernel(*refs, rope):
    if rope:
        x_ref, mod_ref, w_ref, cos_ref, sin_ref, q_ref, k_ref, v_ref, xr_ref, xg_ref = refs
    else:
        x_ref, mod_ref, w_ref, q_ref, k_ref, v_ref, xr_ref, xg_ref = refs
    m = mod_ref[...]
    h = x_ref[...] * (1.0 + m[1:2]) + m[0:1]
    p = _dot(h.astype(BF16), w_ref[...])
    q = p[:, :ATTN_WIDTH]
    k = p[:, ATTN_WIDTH:ATTN_WIDTH + KV_WIDTH]
    v = p[:, ATTN_WIDTH + KV_WIDTH:ATTN_WIDTH + 2 * KV_WIDTH]
    if rope:
        cos = cos_ref[...]
        sin = sin_ref[...]
        q = jnp.concatenate([_rope(q[:, i * HEAD_DIM:(i + 1) * HEAD_DIM], cos, sin)
                             for i in range(N_Q_HEADS)], axis=1)
        k = jnp.concatenate([_rope(k[:, i * HEAD_DIM:(i + 1) * HEAD_DIM], cos, sin)
                             for i in range(N_KV_HEADS)], axis=1)
    q_ref[...] = q.astype(q_ref.dtype)
    k_ref[...] = k.astype(k_ref.dtype)
    v_ref[...] = v.astype(v_ref.dtype)
    xr_ref[...] = p[:, ATTN_WIDTH + 2 * KV_WIDTH:ATTN_WIDTH + 2 * KV_WIDTH + RNN_WIDTH]
    xg_ref[...] = p[:, ATTN_WIDTH + 2 * KV_WIDTH + RNN_WIDTH:]


def _ab_in(x, mod_l, w_bf16, group0, rows_per_group, rope_tables, kv_dtype):
    n = x.shape[0]
    rope = rope_tables is not None
    row = lambda width: pl.BlockSpec((ROW_TILE, width), lambda i: (i, 0))
    in_specs = [row(D_MODEL),
                pl.BlockSpec((None, 6, D_MODEL), _group_map(group0, rows_per_group)),
                pl.BlockSpec((D_MODEL, AB_IN_WIDTH), lambda i: (0, 0))]
    args = [x, mod_l, w_bf16]
    if rope:
        tiles_per_seq = DEC_SEQ // ROW_TILE
        in_specs += [pl.BlockSpec((ROW_TILE, HEAD_DIM), lambda i: (i % tiles_per_seq, 0))] * 2
        args += list(rope_tables)
    return pl.pallas_call(
        functools.partial(_ab_in_kernel, rope=rope),
        grid=(n // ROW_TILE,),
        in_specs=in_specs,
        out_specs=[row(ATTN_WIDTH), row(KV_WIDTH), row(KV_WIDTH), row(RNN_WIDTH), row(RNN_WIDTH)],
        out_shape=[jax.ShapeDtypeStruct((n, ATTN_WIDTH), BF16),
                   jax.ShapeDtypeStruct((n, KV_WIDTH), kv_dtype),
                   jax.ShapeDtypeStruct((n, KV_WIDTH), kv_dtype),
                   jax.ShapeDtypeStruct((n, RNN_WIDTH), F32),
                   jax.ShapeDtypeStruct((n, RNN_WIDTH), F32)],
        compiler_params=_params(1),
        name="ab_in_rope" if rope else "ab_in",
    )(*args)


def _rope_tables():
    rows = DEC_SEQ // GRID_W
    row = jnp.repeat(jnp.arange(rows, dtype=F32), GRID_W)
    col = jnp.tile(jnp.arange(GRID_W, dtype=F32), rows)
    n_freq = HEAD_DIM // 4
    freqs = ROPE_BASE ** (-jnp.arange(n_freq, dtype=F32) / n_freq)
    ang = jnp.concatenate([row[:, None] * freqs, col[:, None] * freqs], axis=-1)
    cos = jnp.repeat(jnp.cos(ang), 2, axis=-1)
    sin = jnp.repeat(jnp.sin(ang), 2, axis=-1)
    sign = jnp.tile(jnp.array([-1.0, 1.0], F32), HEAD_DIM // 2)
    return cos, sin * sign


def _sink_attention_head(s_list, v_list, sink):
    m = sink
    for s in s_list:
        m = jnp.maximum(m, jnp.max(s, axis=-1, keepdims=True))
    p_list = [jnp.exp(s - m) for s in s_list]
    denom = jnp.exp(sink - m)
    for p in p_list:
        denom = denom + jnp.sum(p, axis=-1, keepdims=True)
    out = None
    for p, v in zip(p_list, v_list):
        o = _dot(p.astype(BF16), v)
        out = o if out is None else out + o
    return out * (1.0 / denom)


def _ctx_attn_kernel(sink_ref, q_ref, k_ref, v_ref, o_ref):
    q = q_ref[...]
    k = k_ref[...].astype(BF16)
    v = v_ref[...].astype(BF16)
    outs = []
    for h in range(N_Q_HEADS):
        kv = h // Q_PER_KV
        qh = q[:, h * HEAD_DIM:(h + 1) * HEAD_DIM]
        kh = k[:, kv * HEAD_DIM:(kv + 1) * HEAD_DIM]
        vh = v[:, kv * HEAD_DIM:(kv + 1) * HEAD_DIM]
        s = _dot_nt(qh, kh) * ATTN_SCALE
        outs.append(_sink_attention_head([s], [vh], sink_ref[h]))
    o_ref[...] = jnp.concatenate(outs, axis=1).astype(o_ref.dtype)


def _ctx_attention(q, k, v, sink):
    seq = lambda width: pl.BlockSpec((SEQ, width), lambda b: (b, 0))
    return pl.pallas_call(
        _ctx_attn_kernel,
        grid=(BATCH,),
        in_specs=[pl.BlockSpec(memory_space=pltpu.SMEM), seq(ATTN_WIDTH), seq(KV_WIDTH), seq(KV_WIDTH)],
        out_specs=seq(ATTN_WIDTH),
        out_shape=jax.ShapeDtypeStruct((N_PROMPT, ATTN_WIDTH), BF16),
        compiler_params=_params(1),
        name="context_attention",
    )(sink, q, k, v)


LAT_Q = 256


def _lat_attn_kernel(sink_ref, q_ref, kp_ref, kc_ref, kn_ref, vp_ref, vc_ref, vn_ref,
                     kx_ref, vx_ref, o_ref):
    n = pl.program_id(1)
    nb = pl.num_programs(1)
    q = q_ref[...]
    kw = jnp.concatenate([kp_ref[...], kc_ref[...], kn_ref[...]], axis=0)
    vw = jnp.concatenate([vp_ref[...], vc_ref[...], vn_ref[...]], axis=0)
    kx = kx_ref[...].astype(BF16)
    vx = vx_ref[...].astype(BF16)
    n_key = LAT_Q + 2 * WINDOW
    qi = lax.broadcasted_iota(I32, (LAT_Q, n_key), 0)
    kj = lax.broadcasted_iota(I32, (LAT_Q, n_key), 1)
    rel = kj - qi
    valid = (rel >= 0) & (rel <= 2 * WINDOW)
    valid = valid & ((n > 0) | (kj >= WINDOW)) & ((n < nb - 1) | (kj < LAT_Q + WINDOW))
    outs = []
    for h in range(N_Q_HEADS):
        kv = h // Q_PER_KV
        sl = slice(kv * HEAD_DIM, (kv + 1) * HEAD_DIM)
        qh = q[:, h * HEAD_DIM:(h + 1) * HEAD_DIM]
        s_ctx = _dot_nt(qh, kx[:, sl]) * ATTN_SCALE
        s_win = jnp.where(valid, _dot_nt(qh, kw[:, sl]) * ATTN_SCALE, NEG_INF)
        outs.append(_sink_attention_head([s_ctx, s_win], [vx[:, sl], vw[:, sl]], sink_ref[h]))
    o_ref[...] = jnp.concatenate(outs, axis=1).astype(o_ref.dtype)


def _lat_attention(q, k, v, k_ctx, v_ctx, sink):
    nb = DEC_SEQ // LAT_Q
    nw = DEC_SEQ // WINDOW
    per = LAT_Q // WINDOW
    cur = lambda b, n: (b * nb + n, 0)
    prev = lambda b, n: (b * nw + jnp.maximum(n * per - 1, 0), 0)
    nxt = lambda b, n: (b * nw + jnp.minimum((n + 1) * per, nw - 1), 0)
    tile = lambda width: pl.BlockSpec((LAT_Q, width), cur)
    edge = lambda imap: pl.BlockSpec((WINDOW, KV_WIDTH), imap)
    ctx = pl.BlockSpec((PAST_LEN, KV_WIDTH), lambda b, n: (b, 0))
    return pl.pallas_call(
        _lat_attn_kernel,
        grid=(DEC_BATCH, nb),
        in_specs=[pl.BlockSpec(memory_space=pltpu.SMEM), tile(ATTN_WIDTH),
                  edge(prev), tile(KV_WIDTH), edge(nxt),
                  edge(prev), tile(KV_WIDTH), edge(nxt), ctx, ctx],
        out_specs=tile(ATTN_WIDTH),
        out_shape=jax.ShapeDtypeStruct((N_SAMPLE, ATTN_WIDTH), BF16),
        compiler_params=_params(2),
        name="latent_attention",
    )(sink, q, k, k, k, v, v, v, k_ctx, v_ctx)


RNN_CHUNK = 256


def _rglru_kernel(xr_ref, xg_ref, cw_ref, cb_ref, wa_ref, ba_ref, wx_ref, bx_ref, lam_ref,
                  h0f_ref, h0b_ref, y_ref, hfl_ref, hbf_ref, xc_s, a_s, uf_s, ub_s, *, rows, seq_len):
    n_seq = rows // seq_len
    n_chunk = rows // RNN_CHUNK
    cw = cw_ref[...]
    cb = cb_ref[...]
    zeros_halo = jnp.zeros((SUBLANES, RNN_WIDTH), F32)
    row8 = lax.broadcasted_iota(I32, (SUBLANES, RNN_WIDTH), 0)

    def conv_chunk(c):
        r0 = c * RNN_CHUNK
        first = r0 % seq_len == 0
        last = (r0 + RNN_CHUNK) % seq_len == 0
        before = zeros_halo if first else xr_ref[r0 - SUBLANES:r0, :]
        after = zeros_halo if last else xr_ref[r0 + RNN_CHUNK:r0 + RNN_CHUNK + SUBLANES, :]
        win = jnp.concatenate([before, xr_ref[r0:r0 + RNN_CHUNK, :], after], axis=0)
        xc = cb
        n_win = RNN_CHUNK + 2 * SUBLANES
        for i in range(CONV_W):
            shift = (CONV_PAD_LEFT - i) % n_win
            rolled = win if shift == 0 else pltpu.roll(win, shift, 0)
            xc = xc + rolled[SUBLANES:SUBLANES + RNN_CHUNK, :] * cw[i:i + 1, :]
        return xc

    def group_scan(a, u, reverse):
        for k in (1, 2, 4):
            if reverse:
                shift, ok = SUBLANES - k, row8 < SUBLANES - k
            else:
                shift, ok = k, row8 >= k
            a_nb = jnp.where(ok, pltpu.roll(a, shift, 0), 1.0)
            u_nb = jnp.where(ok, pltpu.roll(u, shift, 0), 0.0)
            u = a * u_nb + u
            a = a * a_nb
        return a, u

    for c in range(n_chunk):
        xc_s[c * RNN_CHUNK:(c + 1) * RNN_CHUNK, :] = conv_chunk(c)

    finals = []
    for d, (u_s, h0_ref) in enumerate(((uf_s, h0f_ref), (ub_s, h0b_ref))):
        reverse = d == 1
        neg = -lam_ref[d:d + 1, :]
        softplus = jnp.maximum(neg, 0.0) + jnp.log1p(jnp.exp(-jnp.abs(neg)))
        decay = -LRU_C * softplus
        wa = wa_ref[d]
        wx = wx_ref[d]
        ba = ba_ref[d:d + 1, :]
        bx = bx_ref[d:d + 1, :]
        for c in range(n_chunk):
            xc = xc_s[c * RNN_CHUNK:(c + 1) * RNN_CHUNK, :]
            xcb = xc.astype(BF16)
            r = jax.nn.sigmoid(_dot(xcb, wa) + ba)
            i = jax.nn.sigmoid(_dot(xcb, wx) + bx)
            log_a = r * decay
            a = jnp.exp(log_a)
            a_s[c * RNN_CHUNK:(c + 1) * RNN_CHUNK, :] = a
            one_minus_a2 = -jnp.tanh(log_a) * (a * a + 1.0)
            u_s[c * RNN_CHUNK:(c + 1) * RNN_CHUNK, :] = jnp.sqrt(one_minus_a2) * (i * xc)

        n_group = seq_len // SUBLANES

        def body(g, carries, u_s=u_s, reverse=reverse):
            gg = n_group - 1 - g if reverse else g
            new = []
            for s in range(n_seq):
                r0 = pl.multiple_of(s * seq_len + gg * SUBLANES, SUBLANES)
                a, u = group_scan(a_s[pl.ds(r0, SUBLANES), :], u_s[pl.ds(r0, SUBLANES), :], reverse)
                h = u + a * carries[s]
                u_s[pl.ds(r0, SUBLANES), :] = h
                new.append(h[0:1, :] if reverse else h[SUBLANES - 1:SUBLANES, :])
            return tuple(new)

        init = tuple(h0_ref[s:s + 1, :] for s in range(n_seq))
        finals.append(lax.fori_loop(0, n_group, body, init))

    for s in range(n_seq):
        hfl_ref[s:s + 1, :] = finals[0][s]
        hbf_ref[s:s + 1, :] = finals[1][s]
    for c in range(n_chunk):
        sl = slice(c * RNN_CHUNK, (c + 1) * RNN_CHUNK)
        y_ref[sl, :] = ((uf_s[sl, :] + ub_s[sl, :]) * _gelu_tanh(xg_ref[sl, :])).astype(y_ref.dtype)


def _rglru(xr, xg, conv_w, conv_b, wa, ba, wx, bx, lam, h0f, h0b, rows, seq_len):
    n = xr.shape[0]
    n_seq = rows // seq_len
    row = pl.BlockSpec((rows, RNN_WIDTH), lambda i: (i, 0))
    full = lambda shape: pl.BlockSpec(shape, lambda i: (0,) * len(shape))
    state = pl.BlockSpec((None, n_seq, RNN_WIDTH), lambda i: (i, 0, 0))
    state_shape = jax.ShapeDtypeStruct((n // rows, n_seq, RNN_WIDTH), F32)
    return pl.pallas_call(
        functools.partial(_rglru_kernel, rows=rows, seq_len=seq_len),
        grid=(n // rows,),
        in_specs=[row, row, full((CONV_W, RNN_WIDTH)), full((1, RNN_WIDTH)),
                  full((2, RNN_WIDTH, RNN_WIDTH)), full((2, RNN_WIDTH)),
                  full((2, RNN_WIDTH, RNN_WIDTH)), full((2, RNN_WIDTH)), full((2, RNN_WIDTH)),
                  state, state],
        out_specs=[row, state, state],
        out_shape=[jax.ShapeDtypeStruct((n, RNN_WIDTH), BF16), state_shape, state_shape],
        scratch_shapes=[pltpu.VMEM((rows, RNN_WIDTH), F32)] * 4,
        compiler_params=_params(1),
        name="rglru_%d" % seq_len,
    )(xr, xg, conv_w, conv_b, wa, ba, wx, bx, lam, h0f, h0b)


def _block_diag_dense(w):
    eye = jnp.eye(RNN_BLOCKS, dtype=w.dtype)
    dense = w[:, :, :, None, :] * eye[None, :, None, :, None]
    return dense.reshape(2, RNN_WIDTH, RNN_WIDTH).astype(BF16)


def _residual_router(x, o, m, lg_ref, lb_ref, rw_ref, x1_ref, xm_ref, aff_ref, affp_ref):
    x1 = _layer_norm(ALPHA * x + m[2:3] * o, lg_ref[...], lb_ref[...])
    x1_ref[...] = x1
    xm = (x1 * (1.0 + m[4:5]) + m[3:4]).astype(BF16)
    xm_ref[...] = xm
    lgt = _dot(xm, rw_ref[...])
    lane = lax.broadcasted_iota(I32, lgt.shape, 1)
    lgt = jnp.where(lane < N_EXPERTS, lgt, NEG_INF)
    ex = jnp.exp(lgt - jnp.max(lgt, axis=-1, keepdims=True))
    aff = ex / jnp.sum(ex, axis=-1, keepdims=True)
    aff_ref[...] = aff[:, :N_EXPERTS]
    hi = aff.astype(BF16).astype(F32)
    rem = aff - hi
    mid = rem.astype(BF16).astype(F32)
    low = (rem - mid).astype(BF16).astype(F32)
    pack = hi + pltpu.roll(mid, N_EXPERTS, 1) + pltpu.roll(low, 2 * N_EXPERTS, 1)
    affp_ref[...] = pack.astype(BF16)


def _mix_out_kernel(att_ref, rnn_ref, x_ref, mod_ref, w_ref, lg_ref, lb_ref, rw_ref,
                    x1_ref, xm_ref, aff_ref, affp_ref):
    o = _dot(att_ref[...], w_ref[:ATTN_WIDTH, :]) + _dot(rnn_ref[...], w_ref[ATTN_WIDTH:, :])
    _residual_router(x_ref[...], o, mod_ref[...], lg_ref, lb_ref, rw_ref, x1_ref, xm_ref, aff_ref, affp_ref)


def _sgu_out_kernel(u_ref, v_ref, x_ref, mod_ref, spw_ref, spb_ref, w_ref, lg_ref, lb_ref, rw_ref,
                    x1_ref, xm_ref, aff_ref, affp_ref, mixed_s):
    for c in range(ROW_TILE // CHUNK):
        rows = slice(c * CHUNK, (c + 1) * CHUNK)
        for g in range(SGU_GROUPS):
            cols = slice(g * SGU_GROUP_W, (g + 1) * SGU_GROUP_W)
            mixed_s[rows, cols] = _dot(spw_ref[g], v_ref[rows, cols]) + spb_ref[:, cols]
    gated = (u_ref[...] * mixed_s[...]).astype(BF16)
    o = _dot(gated, w_ref[...])
    _residual_router(x_ref[...], o, mod_ref[...], lg_ref, lb_ref, rw_ref, x1_ref, xm_ref, aff_ref, affp_ref)


def _mixer_out(kernel, name, acts, x, mod_l, weights, ln_g, ln_b, router_w, group0, rows_per_group,
               scratch=()):
    n = x.shape[0]
    row = lambda width: pl.BlockSpec((ROW_TILE, width), lambda i: (i, 0))
    full = lambda a: pl.BlockSpec(a.shape, lambda i: (0,) * a.ndim)
    return pl.pallas_call(
        kernel,
        grid=(n // ROW_TILE,),
        in_specs=([row(a.shape[1]) for a in acts]
                  + [row(D_MODEL), pl.BlockSpec((None, 6, D_MODEL), _group_map(group0, rows_per_group))]
                  + [full(w) for w in weights] + [full(ln_g), full(ln_b), full(router_w)]),
        out_specs=[row(D_MODEL), row(D_MODEL), row(N_EXPERTS), row(LANES)],
        out_shape=[jax.ShapeDtypeStruct((n, D_MODEL), F32),
                   jax.ShapeDtypeStruct((n, D_MODEL), BF16),
                   jax.ShapeDtypeStruct((n, N_EXPERTS), F32),
                   jax.ShapeDtypeStruct((n, LANES), BF16)],
        scratch_shapes=list(scratch),
        compiler_params=_params(1),
        name=name,
    )(*acts, x, mod_l, *weights, ln_g, ln_b, router_w)


def _sgu_in_kernel(x_ref, mod_ref, w_ref, b_ref, lg_ref, lb_ref, u_ref, v_ref):
    m = mod_ref[...]
    h = x_ref[...] * (1.0 + m[1:2]) + m[0:1]
    p = _gelu_tanh(_dot(h.astype(BF16), w_ref[...]) + b_ref[...])
    u_ref[...] = p[:, :SGU_WIDTH]
    v_ref[...] = _layer_norm(p[:, SGU_WIDTH:], lg_ref[...], lb_ref[...]).astype(v_ref.dtype)


def _sgu_in(x, mod_l, w_bf16, b, ln_g, ln_b, group0, rows_per_group):
    n = x.shape[0]
    row = lambda width: pl.BlockSpec((ROW_TILE, width), lambda i: (i, 0))
    full = lambda a: pl.BlockSpec(a.shape, lambda i: (0,) * a.ndim)
    return pl.pallas_call(
        _sgu_in_kernel,
        grid=(n // ROW_TILE,),
        in_specs=[row(D_MODEL), pl.BlockSpec((None, 6, D_MODEL), _group_map(group0, rows_per_group)),
                  full(w_bf16), full(b), full(ln_g), full(ln_b)],
        out_specs=[row(SGU_WIDTH), row(SGU_WIDTH)],
        out_shape=[jax.ShapeDtypeStruct((n, SGU_WIDTH), F32),
                   jax.ShapeDtypeStruct((n, SGU_WIDTH), BF16)],
        compiler_params=_params(1),
        name="sgu_in",
    )(x, mod_l, w_bf16, b, ln_g, ln_b)


def _route_kernel(aff_ref, tri_ref, posm_ref, posr_ref, *, tokens, cap):
    aff = aff_ref[...]
    n_col = aff.shape[1]

    def bisect(_, lo_hi):
        lo, hi = lo_hi
        mid = lo + ((hi - lo) >> 1)
        cnt = jnp.sum(jnp.where(aff >= pltpu.bitcast(mid, F32), 1.0, 0.0), axis=0, keepdims=True)
        ge = cnt >= cap
        return jnp.where(ge, mid, lo), jnp.where(ge, hi, mid)

    lo0 = jnp.zeros((1, n_col), I32)
    hi0 = jnp.full((1, n_col), 0x7F800000, I32)
    thr_bits, _ = lax.fori_loop(0, 31, bisect, (lo0, hi0))
    thr = pltpu.bitcast(thr_bits, F32)
    above = pltpu.bitcast(thr_bits + 1, F32)

    tri = tri_ref[...]

    def excl_prefix(x01):
        outs = []
        off = jnp.zeros((1, n_col), F32)
        for blk in range(tokens // PREFIX_BLOCK):
            xb = x01[blk * PREFIX_BLOCK:(blk + 1) * PREFIX_BLOCK, :]
            outs.append(_dot(tri, xb.astype(BF16)) + off)
            off = off + jnp.sum(xb, axis=0, keepdims=True)
        return outs[0] if len(outs) == 1 else jnp.concatenate(outs, axis=0)

    gt = jnp.where(aff >= above, 1.0, 0.0)
    eq = jnp.where(aff >= thr, 1.0, 0.0) - gt
    need = cap - jnp.sum(gt, axis=0, keepdims=True)
    eq_rank = excl_prefix(eq)
    sel = gt + jnp.where(eq_rank < need, eq, 0.0)
    pos = excl_prefix(sel)
    posm = jnp.where(sel > 0.0, pos, -1.0)
    posm_ref[...] = posm
    posr_ref[...] = posm.T


def _route(aff_t, tri, tokens, cap):
    n_col = aff_t.shape[1]
    return pl.pallas_call(
        functools.partial(_route_kernel, tokens=tokens, cap=cap),
        grid=(1,),
        in_specs=[pl.BlockSpec((tokens, n_col), lambda i: (0, 0)),
                  pl.BlockSpec((PREFIX_BLOCK, PREFIX_BLOCK), lambda i: (0, 0))],
        out_specs=[pl.BlockSpec((tokens, n_col), lambda i: (0, 0)),
                   pl.BlockSpec((n_col, tokens), lambda i: (0, 0))],
        out_shape=[jax.ShapeDtypeStruct((tokens, n_col), F32),
                   jax.ShapeDtypeStruct((n_col, tokens), F32)],
        compiler_params=_params(1),
        name="route_%d" % tokens,
    )(aff_t, tri)


def _route_requests(aff, tri, n_req, tokens, cap):
    n_col = n_req * N_EXPERTS
    aff_t = aff.reshape(n_req, tokens, N_EXPERTS).transpose(1, 0, 2).reshape(tokens, n_col)
    pad = (-n_col) % LANES
    if pad:
        aff_t = jnp.pad(aff_t, ((0, 0), (0, pad)))
    posm, posr = _route(aff_t, tri, tokens, cap)
    posm = posm[:, :n_col].reshape(tokens, n_req, N_EXPERTS).transpose(1, 0, 2)
    return posm.reshape(n_req * tokens, N_EXPERTS), posr[:n_col].reshape(n_req, N_EXPERTS, tokens)


def _slot_gate(r, expert):
    lane = lax.broadcasted_iota(I32, r.shape, 1)
    pick = ((lane & (N_EXPERTS - 1)) == expert) & (lane < 3 * N_EXPERTS)
    g = jnp.sum(jnp.where(pick, r, 0.0), axis=-1, keepdims=True)
    return jnp.broadcast_to(g, r.shape)


def _gather_prompt_kernel(x_ref, posr_ref, affp_ref, xg_ref, g_ref):
    posr = posr_ref[...]
    slot = lax.broadcasted_iota(I32, (CAP_P, SEQ), 0).astype(F32)
    onehot = jnp.concatenate([jnp.where(posr[e:e + 1, :] == slot, 1.0, 0.0) for e in range(N_EXPERTS)],
                             axis=0).astype(BF16)
    xg = _dot(onehot, x_ref[...])
    r = _dot(onehot, affp_ref[...])
    for e in range(N_EXPERTS):
        rows = slice(e * CAP_P, (e + 1) * CAP_P)
        xg_ref[e] = xg[rows, :].astype(xg_ref.dtype)
        g_ref[e] = _slot_gate(r[rows, :], e)


def _gather_prompt(xm, posr, affp):
    return pl.pallas_call(
        _gather_prompt_kernel,
        grid=(BATCH,),
        in_specs=[pl.BlockSpec((SEQ, D_MODEL), lambda b: (b, 0)),
                  pl.BlockSpec((None, N_EXPERTS, SEQ), lambda b: (b, 0, 0)),
                  pl.BlockSpec((SEQ, LANES), lambda b: (b, 0))],
        out_specs=[pl.BlockSpec((N_EXPERTS, CAP_P, D_MODEL), lambda b: (0, b, 0)),
                   pl.BlockSpec((N_EXPERTS, CAP_P, LANES), lambda b: (0, b, 0))],
        out_shape=[jax.ShapeDtypeStruct((N_EXPERTS, SLOTS_P, D_MODEL), BF16),
                   jax.ShapeDtypeStruct((N_EXPERTS, SLOTS_P, LANES), F32)],
        compiler_params=_params(1),
        name="gather_prompt",
    )(xm, posr, affp)


def _gather_sample_kernel(x_ref, posr_ref, affp_ref, xg_ref, g_ref):
    e = pl.program_id(1)
    slot = lax.broadcasted_iota(I32, (CAP_S, DEC_SEQ), 0).astype(F32)
    onehot = jnp.where(posr_ref[...] == slot, 1.0, 0.0).astype(BF16)
    xg_ref[...] = _dot(onehot, x_ref[...]).astype(xg_ref.dtype)
    g_ref[...] = _slot_gate(_dot(onehot, affp_ref[...]), e)


def _gather_sample(xm, posr, affp):
    return pl.pallas_call(
        _gather_sample_kernel,
        grid=(DEC_BATCH, N_EXPERTS),
        in_specs=[pl.BlockSpec((DEC_SEQ, D_MODEL), lambda b, e: (b, 0)),
                  pl.BlockSpec((None, None, 1, DEC_SEQ), lambda b, e: (b, e, 0, 0)),
                  pl.BlockSpec((DEC_SEQ, LANES), lambda b, e: (b, 0))],
        out_specs=[pl.BlockSpec((None, CAP_S, D_MODEL), lambda b, e: (e, b, 0)),
                   pl.BlockSpec((None, CAP_S, LANES), lambda b, e: (e, b, 0))],
        out_shape=[jax.ShapeDtypeStruct((N_EXPERTS, SLOTS_S, D_MODEL), BF16),
                   jax.ShapeDtypeStruct((N_EXPERTS, SLOTS_S, LANES), F32)],
        compiler_params=_params(2),
        name="gather_sample",
    )(xm, posr.reshape(DEC_BATCH, N_EXPERTS, 1, DEC_SEQ), affp)


FF_TILE = 512
SLOT_CHUNK = 512


def _ffn_kernel(xp_ref, xs_ref, gp_ref, gs_ref, w1_ref, w3_ref, w2_ref, yp_ref, ys_ref, acc_ref):
    f = pl.program_id(1)
    n_f = pl.num_programs(1)
    w1 = w1_ref[...].astype(BF16)
    w3 = w3_ref[...].astype(BF16)
    w2 = w2_ref[...].astype(BF16)

    @pl.when(f == 0)
    def _():
        acc_ref[...] = jnp.zeros_like(acc_ref)

    halves = ((xp_ref, SLOTS_P, 0), (xs_ref, SLOTS_S, SLOTS_P))
    for x_ref, n_slot, base in halves:
        for c in range(n_slot // SLOT_CHUNK):
            x = x_ref[c * SLOT_CHUNK:(c + 1) * SLOT_CHUNK, :]
            h1 = _dot(x, w1)
            h3 = _dot(x, w3)
            hid = (h1 * jax.nn.sigmoid(h1) * h3).astype(BF16)
            rows = slice(base + c * SLOT_CHUNK, base + (c + 1) * SLOT_CHUNK)
            acc_ref[rows, :] += _dot(hid, w2)

    @pl.when(f == n_f - 1)
    def _():
        for y_ref, g_ref, n_slot, base in ((yp_ref, gp_ref, SLOTS_P, 0), (ys_ref, gs_ref, SLOTS_S, SLOTS_P)):
            for c in range(n_slot // SLOT_CHUNK):
                rows = slice(c * SLOT_CHUNK, (c + 1) * SLOT_CHUNK)
                gate = jnp.concatenate([g_ref[rows, :]] * (D_MODEL // LANES), axis=1)
                y_ref[rows, :] = (acc_ref[base + c * SLOT_CHUNK:base + (c + 1) * SLOT_CHUNK, :]
                                  * gate).astype(y_ref.dtype)


def _expert_ffn(xgp, xgs, gp, gs, w1, w3, w2, layer):
    slots = lambda n, width: pl.BlockSpec((None, n, width), lambda e, f: (e, 0, 0))
    return pl.pallas_call(
        _ffn_kernel,
        grid=(N_EXPERTS, EXPERT_FF // FF_TILE),
        in_specs=[slots(SLOTS_P, D_MODEL), slots(SLOTS_S, D_MODEL), slots(SLOTS_P, LANES), slots(SLOTS_S, LANES),
                  pl.BlockSpec((None, None, D_MODEL, FF_TILE), lambda e, f: (layer, e, 0, f)),
                  pl.BlockSpec((None, None, D_MODEL, FF_TILE), lambda e, f: (layer, e, 0, f)),
                  pl.BlockSpec((None, None, FF_TILE, D_MODEL), lambda e, f: (layer, e, f, 0))],
        out_specs=[slots(SLOTS_P, D_MODEL), slots(SLOTS_S, D_MODEL)],
        out_shape=[jax.ShapeDtypeStruct((N_EXPERTS, SLOTS_P, D_MODEL), BF16),
                   jax.ShapeDtypeStruct((N_EXPERTS, SLOTS_S, D_MODEL), BF16)],
        scratch_shapes=[pltpu.VMEM((SLOTS_P + SLOTS_S, D_MODEL), F32)],
        compiler_params=_params(2),
        name="expert_ffn",
    )(xgp, xgs, gp, gs, w1, w3, w2)


def _combine_prompt_kernel(posm_ref, y_ref, x_ref, mod_ref, exp_ref, lg_ref, lb_ref, o_ref):
    spread = _dot(posm_ref[...].astype(BF16), exp_ref[...])
    slot = (lax.broadcasted_iota(I32, spread.shape, 1) & (CAP_P - 1)).astype(F32)
    onehot = jnp.where(spread == slot, 1.0, 0.0).astype(BF16)
    y = jnp.concatenate([y_ref[e] for e in range(N_EXPERTS)], axis=0)
    f = _dot(onehot, y)
    m = mod_ref[...]
    o_ref[...] = _layer_norm(ALPHA * x_ref[...] + m[5:6] * f, lg_ref[...], lb_ref[...])


def _combine_prompt(posm, y, x1, mod_l, expand, ln_g, ln_b):
    full = lambda a: pl.BlockSpec(a.shape, lambda b: (0,) * a.ndim)
    return pl.pallas_call(
        _combine_prompt_kernel,
        grid=(BATCH,),
        in_specs=[pl.BlockSpec((SEQ, N_EXPERTS), lambda b: (b, 0)),
                  pl.BlockSpec((N_EXPERTS, CAP_P, D_MODEL), lambda b: (0, b, 0)),
                  pl.BlockSpec((SEQ, D_MODEL), lambda b: (b, 0)),
                  pl.BlockSpec((None, 6, D_MODEL), lambda b: (0, 0, 0)),
                  full(expand), full(ln_g), full(ln_b)],
        out_specs=pl.BlockSpec((SEQ, D_MODEL), lambda b: (b, 0)),
        out_shape=jax.ShapeDtypeStruct((N_PROMPT, D_MODEL), F32),
        compiler_params=_params(1),
        name="combine_prompt",
    )(posm, y, x1, mod_l, expand, ln_g, ln_b)


def _combine_sample_kernel(posm_ref, y_ref, x_ref, mod_ref, lg_ref, lb_ref, o_ref):
    posm = posm_ref[...]
    slot = lax.broadcasted_iota(I32, (ROW_TILE, CAP_S), 1).astype(F32)
    f = jnp.zeros((ROW_TILE, D_MODEL), F32)
    for e in range(N_EXPERTS):
        onehot = jnp.where(posm[:, e:e + 1] == slot, 1.0, 0.0).astype(BF16)
        f = f + _dot(onehot, y_ref[e])
    m = mod_ref[...]
    o_ref[...] = _layer_norm(ALPHA * x_ref[...] + m[5:6] * f, lg_ref[...], lb_ref[...])


def _combine_sample(posm, y, x1, mod_l, ln_g, ln_b):
    tiles = DEC_SEQ // ROW_TILE
    row = lambda width: pl.BlockSpec((ROW_TILE, width), lambda b, t: (b * tiles + t, 0))
    full = lambda a: pl.BlockSpec(a.shape, lambda b, t: (0,) * a.ndim)
    return pl.pallas_call(
        _combine_sample_kernel,
        grid=(DEC_BATCH, tiles),
        in_specs=[row(N_EXPERTS),
                  pl.BlockSpec((N_EXPERTS, CAP_S, D_MODEL), lambda b, t: (0, b, 0)),
                  row(D_MODEL),
                  pl.BlockSpec((None, 6, D_MODEL), lambda b, t: (1 + b, 0, 0)),
                  full(ln_g), full(ln_b)],
        out_specs=row(D_MODEL),
        out_shape=jax.ShapeDtypeStruct((N_SAMPLE, D_MODEL), F32),
        compiler_params=_params(2),
        name="combine_sample",
    )(posm, y, x1, mod_l, ln_g, ln_b)


def _moe(prompt, sample, mod_l, w1, w3, w2, layer, ln_g, ln_b, tri, expand):
    x1_p, xm_p, aff_p, affp_p = prompt
    x1_s, xm_s, aff_s, affp_s = sample
    posm_p, posr_p = _route_requests(aff_p, tri, BATCH, SEQ, CAP_P)
    posm_s, posr_s = _route_requests(aff_s, tri, DEC_BATCH, DEC_SEQ, CAP_S)
    xg_p, g_p = _gather_prompt(xm_p, posr_p, affp_p)
    xg_s, g_s = _gather_sample(xm_s, posr_s, affp_s)
    y_p, y_s = _expert_ffn(xg_p, xg_s, g_p, g_s, w1, w3, w2, layer)
    x2_p = _combine_prompt(posm_p, y_p, x1_p, mod_l, expand, ln_g, ln_b)
    x2_s = _combine_sample(posm_s, y_s, x1_s, mod_l, ln_g, ln_b)
    return x2_p, x2_s


def kernel(x_prompt, x_sample, cache_k, cache_v, state_rglru, c, c_ctx, mod_w, mod_b, ln_mix_g, ln_mix_b, ln_ffn_g, ln_ffn_b, ab_in_w, attn_sink, rnn_conv_w, rnn_conv_b, lru_wa, lru_ba, lru_wx, lru_bx, lru_lambda, ab_out_w, sgu_in_w, sgu_in_b, sgu_ln_g, sgu_ln_b, sgu_spatial_w, sgu_spatial_b, sgu_out_w, router_w, moe_w1, moe_w3, moe_w2):
    xp = x_prompt.reshape(N_PROMPT, D_MODEL)
    xs = x_sample.reshape(N_SAMPLE, D_MODEL)

    cvec = jnp.concatenate([c_ctx[None], c, jnp.zeros((SUBLANES - 1 - DEC_BATCH, D_MODEL), F32)], axis=0)
    mod = _modulation(cvec, mod_w, mod_b).reshape(DEPTH, SUBLANES, 6, D_MODEL)

    idx = jnp.arange(PREFIX_BLOCK)
    tri = (idx[None, :] < idx[:, None]).astype(BF16)
    lane = jnp.arange(N_EXPERTS * CAP_P)
    expand = ((lane[None, :] // CAP_P) == jnp.arange(N_EXPERTS)[:, None]).astype(BF16)
    rope_tables = _rope_tables()
    vec = lambda a: a.reshape(1, -1)
    router_pad = lambda w: jnp.pad(w, ((0, 0), (0, LANES - N_EXPERTS))).astype(BF16)

    new_k = new_v = new_state = None
    for l in range(DEPTH):
        e = l // 2
        mod_l = mod[l]
        rw = router_pad(router_w[l])
        lg, lb = vec(ln_mix_g[l]), vec(ln_mix_b[l])
        if l % 2 == 0:
            w_in = ab_in_w[e].astype(BF16)
            w_out = ab_out_w[e].astype(BF16)
            wa = _block_diag_dense(lru_wa[e])
            wx = _block_diag_dense(lru_wx[e])
            rnn_w = (rnn_conv_w[e], vec(rnn_conv_b[e]), wa, lru_ba[e], wx, lru_bx[e], lru_lambda[e])

            q, k, v, xr, xg = _ab_in(xp, mod_l, w_in, 0, N_PROMPT, None, F32)
            new_k, new_v = k, v
            att = _ctx_attention(q, k, v, attn_sink[e])
            rows = 4 * SEQ
            zeros = jnp.zeros((N_PROMPT // rows, rows // SEQ, RNN_WIDTH), F32)
            rnn, hf_last, hb_first = _rglru(xr, xg, *rnn_w, zeros, zeros, rows, SEQ)
            new_state = jnp.stack([hf_last.reshape(BATCH, RNN_WIDTH), hb_first.reshape(BATCH, RNN_WIDTH)], axis=1)
            prompt = _mixer_out(_mix_out_kernel, "mix_out", (att, rnn), xp, mod_l, (w_out,),
                                lg, lb, rw, 0, N_PROMPT)

            q, k, v, xr, xg = _ab_in(xs, mod_l, w_in, 1, DEC_SEQ, rope_tables, BF16)
            att = _lat_attention(q, k, v, cache_k[:, e].reshape(DEC_BATCH * PAST_LEN, KV_WIDTH),
                                 cache_v[:, e].reshape(DEC_BATCH * PAST_LEN, KV_WIDTH), attn_sink[e])
            h0 = state_rglru[:, e]
            rnn, _, _ = _rglru(xr, xg, *rnn_w, h0[:, 0][:, None, :], h0[:, 1][:, None, :], DEC_SEQ, DEC_SEQ)
            sample = _mixer_out(_mix_out_kernel, "mix_out", (att, rnn), xs, mod_l, (w_out,),
                                lg, lb, rw, 1, DEC_SEQ)
        else:
            w_in = sgu_in_w[e].astype(BF16)
            w_out = sgu_out_w[e].astype(BF16)
            spw = sgu_spatial_w[e].astype(BF16)
            spb = jnp.repeat(sgu_spatial_b[e].T, SGU_GROUP_W, axis=1)
            sgu_args = (w_in, vec(sgu_in_b[e]), vec(sgu_ln_g[e]), vec(sgu_ln_b[e]))
            scratch = (pltpu.VMEM((ROW_TILE, SGU_WIDTH), F32),)
            u, vn = _sgu_in(xp, mod_l, *sgu_args, 0, N_PROMPT)
            prompt = _mixer_out(_sgu_out_kernel, "sgu_out", (u, vn), xp, mod_l, (spw, spb, w_out),
                                lg, lb, rw, 0, N_PROMPT, scratch)
            u, vn = _sgu_in(xs, mod_l, *sgu_args, 1, DEC_SEQ)
            sample = _mixer_out(_sgu_out_kernel, "sgu_out", (u, vn), xs, mod_l, (spw, spb, w_out),
                                lg, lb, rw, 1, DEC_SEQ, scratch)

        xp, xs = _moe(prompt, sample, mod_l, moe_w1, moe_w3, moe_w2, l,
                      vec(ln_ffn_g[l]), vec(ln_ffn_b[l]), tri, expand)

    return (xp.reshape(BATCH, SEQ, D_MODEL),
            xs.reshape(DEC_BATCH, DEC_SEQ, D_MODEL),
            new_k.reshape(BATCH, 1, SEQ, N_KV_HEADS, HEAD_DIM),
            new_v.reshape(BATCH, 1, SEQ, N_KV_HEADS, HEAD_DIM),
            new_state.reshape(BATCH, 1, 2, RNN_WIDTH))
```

```python
import functools

import jax
import jax.numpy as jnp
from jax import lax
from jax.experimental import pallas as pl
from jax.experimental.pallas import tpu as pltpu

F32 = jnp.float32
BF16 = jnp.bfloat16
I32 = jnp.int32

D_MODEL = 1024
BATCH = 32
SEQ = 256
DEPTH = 2
DEC_BATCH = 4
DEC_SEQ = 2048
PAST_LEN = 256
GRID_W = 64
HEAD_DIM = 128
N_Q_HEADS = 4
N_KV_HEADS = 2
Q_PER_KV = N_Q_HEADS // N_KV_HEADS
ATTN_WIDTH = N_Q_HEADS * HEAD_DIM
KV_WIDTH = N_KV_HEADS * HEAD_DIM
WINDOW = 128
BLOCK = 128
ATTN_SCALE = HEAD_DIM ** -0.5
ROPE_BASE = 10000.0
NEG_INF = -1e30
RNN_WIDTH = 512
RNN_BLOCKS = 8
RNN_BLOCK_W = RNN_WIDTH // RNN_BLOCKS
CONV_W = 4
CONV_PAD_LEFT = 2
LRU_C = 8.0
AB_IN_WIDTH = ATTN_WIDTH + 2 * KV_WIDTH + 2 * RNN_WIDTH
CHUNK = 128
SGU_WIDTH = D_MODEL
SGU_GROUPS = 8
SGU_GROUP_W = SGU_WIDTH // SGU_GROUPS
N_EXPERTS = 16
EXPERT_FF = 2048
EC_FACTOR = 2
ALPHA = (2 * DEPTH) ** 0.25
LN_EPS = 1e-6

N_PROMPT = BATCH * SEQ
N_SAMPLE = DEC_BATCH * DEC_SEQ
CAP_P = EC_FACTOR * SEQ // N_EXPERTS
CAP_S = EC_FACTOR * DEC_SEQ // N_EXPERTS
SLOTS_P = BATCH * CAP_P
SLOTS_S = DEC_BATCH * CAP_S

LANES = 128
SUBLANES = 8
ROW_TILE = 512
PREFIX_BLOCK = 256
VMEM_LIMIT = 56 * 1024 * 1024


def _params(n_axes=1):
    return pltpu.CompilerParams(dimension_semantics=("arbitrary",) * n_axes,
                                vmem_limit_bytes=VMEM_LIMIT)


def _layer_norm(x, g, b):
    mu = jnp.mean(x, axis=-1, keepdims=True)
    xc = x - mu
    var = jnp.mean(xc * xc, axis=-1, keepdims=True)
    return xc * lax.rsqrt(var + LN_EPS) * g + b


def _gelu_tanh(x):
    return 0.5 * x * (1.0 + jnp.tanh(0.7978845608028654 * (x + 0.044715 * (x * x * x))))


def _dot(a, b):
    return jnp.dot(a, b, preferred_element_type=F32)


def _dot_nt(a, b):
    return lax.dot_general(a, b, (((1,), (1,)), ((), ())), preferred_element_type=F32)


def _mod_kernel(c_ref, w_ref, b_ref, o_ref):
    c = c_ref[...]
    s = c * jax.nn.sigmoid(c)
    o_ref[...] = _dot(s.astype(BF16), w_ref[...].astype(BF16)) + b_ref[...]


def _modulation(cvec8, mod_w, mod_b):
    n_col = 6 * D_MODEL // D_MODEL
    return pl.pallas_call(
        _mod_kernel,
        grid=(DEPTH, n_col),
        in_specs=[pl.BlockSpec((SUBLANES, D_MODEL), lambda l, j: (0, 0)),
                  pl.BlockSpec((None, D_MODEL, D_MODEL), lambda l, j: (l, 0, j)),
                  pl.BlockSpec((None, 1, D_MODEL), lambda l, j: (l, 0, j))],
        out_specs=pl.BlockSpec((None, SUBLANES, D_MODEL), lambda l, j: (l, 0, j)),
        out_shape=jax.ShapeDtypeStruct((DEPTH, SUBLANES, 6 * D_MODEL), F32),
        compiler_params=_params(2),
        name="adaln_modulation",
    )(cvec8, mod_w, mod_b.reshape(DEPTH, 1, 6 * D_MODEL))


def _group_map(group0, rows_per_group):
    tiles_per_group = rows_per_group // ROW_TILE
    return lambda i: (group0 + i // tiles_per_group, 0, 0)


def _rope(t, cos, sin_signed):
    lane = lax.broadcasted_iota(I32, t.shape, 1)
    swapped = jnp.where((lane & 1) == 0, pltpu.roll(t, HEAD_DIM - 1, 1), pltpu.roll(t, 1, 1))
    return t * cos + swapped * sin_signed


def _ab_in_kernel(*refs, rope):
    if rope:
        x_ref, mod_ref, w_ref, cos_ref, sin_ref, q_ref, k_ref, v_ref, xr_ref, xg_ref = refs
    else:
        x_ref, mod_ref, w_ref, q_ref, k_ref, v_ref, xr_ref, xg_ref = refs
    m = mod_ref[...]
    h = x_ref[...] * (1.0 + m[1:2]) + m[0:1]
    p = _dot(h.astype(BF16), w_ref[...])
    q = p[:, :ATTN_WIDTH]
    k = p[:, ATTN_WIDTH:ATTN_WIDTH + KV_WIDTH]
    v = p[:, ATTN_WIDTH + KV_WIDTH:ATTN_WIDTH + 2 * KV_WIDTH]
    if rope:
        cos = cos_ref[...]
        sin = sin_ref[...]
        q = jnp.concatenate([_rope(q[:, i * HEAD_DIM:(i + 1) * HEAD_DIM], cos, sin)
                             for i in range(N_Q_HEADS)], axis=1)
        k = jnp.concatenate([_rope(k[:, i * HEAD_DIM:(i + 1) * HEAD_DIM], cos, sin)
                             for i in range(N_KV_HEADS)], axis=1)
    q_ref[...] = q.astype(q_ref.dtype)
    k_ref[...] = k.astype(k_ref.dtype)
    v_ref[...] = v.astype(v_ref.dtype)
    xr_ref[...] = p[:, ATTN_WIDTH + 2 * KV_WIDTH:ATTN_WIDTH + 2 * KV_WIDTH + RNN_WIDTH]
    xg_ref[...] = p[:, ATTN_WIDTH + 2 * KV_WIDTH + RNN_WIDTH:]


def _ab_in(x, mod_l, w_bf16, group0, rows_per_group, rope_tables, kv_dtype):
    n = x.shape[0]
    rope = rope_tables is not None
    row = lambda width: pl.BlockSpec((ROW_TILE, width), lambda i: (i, 0))
    in_specs = [row(D_MODEL),
                pl.BlockSpec((None, 6, D_MODEL), _group_map(group0, rows_per_group)),
                pl.BlockSpec((D_MODEL, AB_IN_WIDTH), lambda i: (0, 0))]
    args = [x, mod_l, w_bf16]
    if rope:
        tiles_per_seq = DEC_SEQ // ROW_TILE
        in_specs += [pl.BlockSpec((ROW_TILE, HEAD_DIM), lambda i: (i % tiles_per_seq, 0))] * 2
        args += list(rope_tables)
    return pl.pallas_call(
        functools.partial(_ab_in_kernel, rope=rope),
        grid=(n // ROW_TILE,),
        in_specs=in_specs,
        out_specs=[row(ATTN_WIDTH), row(KV_WIDTH), row(KV_WIDTH), row(RNN_WIDTH), row(RNN_WIDTH)],
        out_shape=[jax.ShapeDtypeStruct((n, ATTN_WIDTH), BF16),
                   jax.ShapeDtypeStruct((n, KV_WIDTH), kv_dtype),
                   jax.ShapeDtypeStruct((n, KV_WIDTH), kv_dtype),
                   jax.ShapeDtypeStruct((n, RNN_WIDTH), F32),
                   jax.ShapeDtypeStruct((n, RNN_WIDTH), F32)],
        compiler_params=_params(1),
        name="ab_in_rope" if rope else "ab_in",
    )(*args)


def _rope_tables():
    rows = DEC_SEQ // GRID_W
    row = jnp.repeat(jnp.arange(rows, dtype=F32), GRID_W)
    col = jnp.tile(jnp.arange(GRID_W, dtype=F32), rows)
    n_freq = HEAD_DIM // 4
    freqs = ROPE_BASE ** (-jnp.arange(n_freq, dtype=F32) / n_freq)
    ang = jnp.concatenate([row[:, None] * freqs, col[:, None] * freqs], axis=-1)
    cos = jnp.repeat(jnp.cos(ang), 2, axis=-1)
    sin = jnp.repeat(jnp.sin(ang), 2, axis=-1)
    sign = jnp.tile(jnp.array([-1.0, 1.0], F32), HEAD_DIM // 2)
    return cos, sin * sign


def _sink_attention_head(s_list, v_list, sink):
    m = sink
    for s in s_list:
        m = jnp.maximum(m, jnp.max(s, axis=-1, keepdims=True))
    p_list = [jnp.exp(s - m) for s in s_list]
    denom = jnp.exp(sink - m)
    for p in p_list:
        denom = denom + jnp.sum(p, axis=-1, keepdims=True)
    out = None
    for p, v in zip(p_list, v_list):
        o = _dot(p.astype(BF16), v)
        out = o if out is None else out + o
    return out * (1.0 / denom)


def _ctx_attn_kernel(sink_ref, q_ref, k_ref, v_ref, o_ref):
    q = q_ref[...]
    k = k_ref[...].astype(BF16)
    v = v_ref[...].astype(BF16)
    outs = []
    for h in range(N_Q_HEADS):
        kv = h // Q_PER_KV
        qh = q[:, h * HEAD_DIM:(h + 1) * HEAD_DIM]
        kh = k[:, kv * HEAD_DIM:(kv + 1) * HEAD_DIM]
        vh = v[:, kv * HEAD_DIM:(kv + 1) * HEAD_DIM]
        s = _dot_nt(qh, kh) * ATTN_SCALE
        outs.append(_sink_attention_head([s], [vh], sink_ref[h]))
    o_ref[...] = jnp.concatenate(outs, axis=1).astype(o_ref.dtype)


def _ctx_attention(q, k, v, sink):
    seq = lambda width: pl.BlockSpec((SEQ, width), lambda b: (b, 0))
    return pl.pallas_call(
        _ctx_attn_kernel,
        grid=(BATCH,),
        in_specs=[pl.BlockSpec(memory_space=pltpu.SMEM), seq(ATTN_WIDTH), seq(KV_WIDTH), seq(KV_WIDTH)],
        out_specs=seq(ATTN_WIDTH),
        out_shape=jax.ShapeDtypeStruct((N_PROMPT, ATTN_WIDTH), BF16),
        compiler_params=_params(1),
        name="context_attention",
    )(sink, q, k, v)


LAT_Q = 256


def _lat_attn_kernel(sink_ref, q_ref, kp_ref, kc_ref, kn_ref, vp_ref, vc_ref, vn_ref,
                     kx_ref, vx_ref, o_ref):
    n = pl.program_id(1)
    nb = pl.num_programs(1)
    q = q_ref[...]
    kw = jnp.concatenate([kp_ref[...], kc_ref[...], kn_ref[...]], axis=0)
    vw = jnp.concatenate([vp_ref[...], vc_ref[...], vn_ref[...]], axis=0)
    kx = kx_ref[...].astype(BF16)
    vx = vx_ref[...].astype(BF16)
    n_key = LAT_Q + 2 * WINDOW
    qi = lax.broadcasted_iota(I32, (LAT_Q, n_key), 0)
    kj = lax.broadcasted_iota(I32, (LAT_Q, n_key), 1)
    rel = kj - qi
    valid = (rel >= 0) & (rel <= 2 * WINDOW)
    valid = valid & ((n > 0) | (kj >= WINDOW)) & ((n < nb - 1) | (kj < LAT_Q + WINDOW))
    outs = []
    for h in range(N_Q_HEADS):
        kv = h // Q_PER_KV
        sl = slice(kv * HEAD_DIM, (kv + 1) * HEAD_DIM)
        qh = q[:, h * HEAD_DIM:(h + 1) * HEAD_DIM]
        s_ctx = _dot_nt(qh, kx[:, sl]) * ATTN_SCALE
        s_win = jnp.where(valid, _dot_nt(qh, kw[:, sl]) * ATTN_SCALE, NEG_INF)
        outs.append(_sink_attention_head([s_ctx, s_win], [vx[:, sl], vw[:, sl]], sink_ref[h]))
    o_ref[...] = jnp.concatenate(outs, axis=1).astype(o_ref.dtype)


def _lat_attention(q, k, v, k_ctx, v_ctx, sink):
    nb = DEC_SEQ // LAT_Q
    nw = DEC_SEQ // WINDOW
    per = LAT_Q // WINDOW
    cur = lambda b, n: (b * nb + n, 0)
    prev = lambda b, n: (b * nw + jnp.maximum(n * per - 1, 0), 0)
    nxt = lambda b, n: (b * nw + jnp.minimum((n + 1) * per, nw - 1), 0)
    tile = lambda width: pl.BlockSpec((LAT_Q, width), cur)
    edge = lambda imap: pl.BlockSpec((WINDOW, KV_WIDTH), imap)
    ctx = pl.BlockSpec((PAST_LEN, KV_WIDTH), lambda b, n: (b, 0))
    return pl.pallas_call(
        _lat_attn_kernel,
        grid=(DEC_BATCH, nb),
        in_specs=[pl.BlockSpec(memory_space=pltpu.SMEM), tile(ATTN_WIDTH),
                  edge(prev), tile(KV_WIDTH), edge(nxt),
                  edge(prev), tile(KV_WIDTH), edge(nxt), ctx, ctx],
        out_specs=tile(ATTN_WIDTH),
        out_shape=jax.ShapeDtypeStruct((N_SAMPLE, ATTN_WIDTH), BF16),
        compiler_params=_params(2),
        name="latent_attention",
    )(sink, q, k, k, k, v, v, v, k_ctx, v_ctx)


RNN_CHUNK = 256


def _rglru_kernel(xr_ref, xg_ref, cw_ref, cb_ref, wa_ref, ba_ref, wx_ref, bx_ref, lam_ref,
                  h0f_ref, h0b_ref, y_ref, hfl_ref, hbf_ref, xc_s, a_s, uf_s, ub_s, *, rows, seq_len):
    n_seq = rows // seq_len
    n_chunk = rows // RNN_CHUNK
    cw = cw_ref[...]
    cb = cb_ref[...]
    zeros_halo = jnp.zeros((SUBLANES, RNN_WIDTH), F32)
    row8 = lax.broadcasted_iota(I32, (SUBLANES, RNN_WIDTH), 0)

    def conv_chunk(c):
        r0 = c * RNN_CHUNK
        first = r0 % seq_len == 0
        last = (r0 + RNN_CHUNK) % seq_len == 0
        before = zeros_halo if first else xr_ref[r0 - SUBLANES:r0, :]
        after = zeros_halo if last else xr_ref[r0 + RNN_CHUNK:r0 + RNN_CHUNK + SUBLANES, :]
        win = jnp.concatenate([before, xr_ref[r0:r0 + RNN_CHUNK, :], after], axis=0)
        xc = cb
        n_win = RNN_CHUNK + 2 * SUBLANES
        for i in range(CONV_W):
            shift = (CONV_PAD_LEFT - i) % n_win
            rolled = win if shift == 0 else pltpu.roll(win, shift, 0)
            xc = xc + rolled[SUBLANES:SUBLANES + RNN_CHUNK, :] * cw[i:i + 1, :]
        return xc

    def group_scan(a, u, reverse):
        for k in (1, 2, 4):
            if reverse:
                shift, ok = SUBLANES - k, row8 < SUBLANES - k
            else:
                shift, ok = k, row8 >= k
            a_nb = jnp.where(ok, pltpu.roll(a, shift, 0), 1.0)
            u_nb = jnp.where(ok, pltpu.roll(u, shift, 0), 0.0)
            u = a * u_nb + u
            a = a * a_nb
        return a, u

    for c in range(n_chunk):
        xc_s[c * RNN_CHUNK:(c + 1) * RNN_CHUNK, :] = conv_chunk(c)

    finals = []
    for d, (u_s, h0_ref) in enumerate(((uf_s, h0f_ref), (ub_s, h0b_ref))):
        reverse = d == 1
        neg = -lam_ref[d:d + 1, :]
        softplus = jnp.maximum(neg, 0.0) + jnp.log1p(jnp.exp(-jnp.abs(neg)))
        decay = -LRU_C * softplus
        wa = wa_ref[d]
        wx = wx_ref[d]
        ba = ba_ref[d:d + 1, :]
        bx = bx_ref[d:d + 1, :]
        for c in range(n_chunk):
            xc = xc_s[c * RNN_CHUNK:(c + 1) * RNN_CHUNK, :]
            xcb = xc.astype(BF16)
            r = jax.nn.sigmoid(_dot(xcb, wa) + ba)
            i = jax.nn.sigmoid(_dot(xcb, wx) + bx)
            log_a = r * decay
            a = jnp.exp(log_a)
            a_s[c * RNN_CHUNK:(c + 1) * RNN_CHUNK, :] = a
            one_minus_a2 = -jnp.tanh(log_a) * (a * a + 1.0)
            u_s[c * RNN_CHUNK:(c + 1) * RNN_CHUNK, :] = jnp.sqrt(one_minus_a2) * (i * xc)

        n_group = seq_len // SUBLANES

        def body(g, carries, u_s=u_s, reverse=reverse):
            gg = n_group - 1 - g if reverse else g
            new = []
            for s in range(n_seq):
                r0 = pl.multiple_of(s * seq_len + gg * SUBLANES, SUBLANES)
                a, u = group_scan(a_s[pl.ds(r0, SUBLANES), :], u_s[pl.ds(r0, SUBLANES), :], reverse)
                h = u + a * carries[s]
                u_s[pl.ds(r0, SUBLANES), :] = h
                new.append(h[0:1, :] if reverse else h[SUBLANES - 1:SUBLANES, :])
            return tuple(new)

        init = tuple(h0_ref[s:s + 1, :] for s in range(n_seq))
        finals.append(lax.fori_loop(0, n_group, body, init))

    for s in range(n_seq):
        hfl_ref[s:s + 1, :] = finals[0][s]
        hbf_ref[s:s + 1, :] = finals[1][s]
    for c in range(n_chunk):
        sl = slice(c * RNN_CHUNK, (c + 1) * RNN_CHUNK)
        y_ref[sl, :] = ((uf_s[sl, :] + ub_s[sl, :]) * _gelu_tanh(xg_ref[sl, :])).astype(y_ref.dtype)


def _rglru(xr, xg, conv_w, conv_b, wa, ba, wx, bx, lam, h0f, h0b, rows, seq_len):
    n = xr.shape[0]
    n_seq = rows // seq_len
    row = pl.BlockSpec((rows, RNN_WIDTH), lambda i: (i, 0))
    full = lambda shape: pl.BlockSpec(shape, lambda i: (0,) * len(shape))
    state = pl.BlockSpec((None, n_seq, RNN_WIDTH), lambda i: (i, 0, 0))
    state_shape = jax.ShapeDtypeStruct((n // rows, n_seq, RNN_WIDTH), F32)
    return pl.pallas_call(
        functools.partial(_rglru_kernel, rows=rows, seq_len=seq_len),
        grid=(n // rows,),
        in_specs=[row, row, full((CONV_W, RNN_WIDTH)), full((1, RNN_WIDTH)),
                  full((2, RNN_WIDTH, RNN_WIDTH)), full((2, RNN_WIDTH)),
                  full((2, RNN_WIDTH, RNN_WIDTH)), full((2, RNN_WIDTH)), full((2, RNN_WIDTH)),
                  state, state],
        out_specs=[row, state, state],
        out_shape=[jax.ShapeDtypeStruct((n, RNN_WIDTH), BF16), state_shape, state_shape],
        scratch_shapes=[pltpu.VMEM((rows, RNN_WIDTH), F32)] * 4,
        compiler_params=_params(1),
        name="rglru_%d" % seq_len,
    )(xr, xg, conv_w, conv_b, wa, ba, wx, bx, lam, h0f, h0b)


def _block_diag_dense(w):
    eye = jnp.eye(RNN_BLOCKS, dtype=w.dtype)
    dense = w[:, :, :, None, :] * eye[None, :, None, :, None]
    return dense.reshape(2, RNN_WIDTH, RNN_WIDTH).astype(BF16)


SUB_TILE = 256
SUB_TILES = tuple(slice(s, s + SUB_TILE) for s in range(0, ROW_TILE, SUB_TILE))


def _residual_router(rows, x, o, m, lg_ref, lb_ref, rw_ref, x1_ref, xm_ref, aff_ref, affp_ref):
    x1 = _layer_norm(ALPHA * x + m[2:3] * o, lg_ref[...], lb_ref[...])
    x1_ref[rows, :] = x1
    xm = (x1 * (1.0 + m[4:5]) + m[3:4]).astype(BF16)
    xm_ref[rows, :] = xm
    lgt = _dot(xm, rw_ref[...])
    lane = lax.broadcasted_iota(I32, lgt.shape, 1)
    lgt = jnp.where(lane < N_EXPERTS, lgt, NEG_INF)
    ex = jnp.exp(lgt - jnp.max(lgt, axis=-1, keepdims=True))
    aff = ex / jnp.sum(ex, axis=-1, keepdims=True)
    aff_ref[rows, :] = aff[:, :N_EXPERTS]
    hi = aff.astype(BF16).astype(F32)
    rem = aff - hi
    mid = rem.astype(BF16).astype(F32)
    low = (rem - mid).astype(BF16).astype(F32)
    pack = hi + pltpu.roll(mid, N_EXPERTS, 1) + pltpu.roll(low, 2 * N_EXPERTS, 1)
    affp_ref[rows, :] = pack.astype(BF16)


def _mix_out_kernel(att_ref, rnn_ref, x_ref, mod_ref, w_ref, lg_ref, lb_ref, rw_ref,
                    x1_ref, xm_ref, aff_ref, affp_ref):
    m = mod_ref[...]
    for rows in SUB_TILES:
        o = _dot(att_ref[rows, :], w_ref[:ATTN_WIDTH, :]) + _dot(rnn_ref[rows, :], w_ref[ATTN_WIDTH:, :])
        _residual_router(rows, x_ref[rows, :], o, m, lg_ref, lb_ref, rw_ref, x1_ref, xm_ref, aff_ref, affp_ref)


def _sgu_out_kernel(u_ref, v_ref, x_ref, mod_ref, spw_ref, spb_ref, w_ref, lg_ref, lb_ref, rw_ref,
                    x1_ref, xm_ref, aff_ref, affp_ref, mixed_s):
    m = mod_ref[...]
    for rows in SUB_TILES:
        for c in range(rows.start, rows.stop, CHUNK):
            for g in range(SGU_GROUPS):
                cols = slice(g * SGU_GROUP_W, (g + 1) * SGU_GROUP_W)
                mixed_s[c:c + CHUNK, cols] = _dot(spw_ref[g], v_ref[c:c + CHUNK, cols]) + spb_ref[:, cols]
        gated = (u_ref[rows, :] * mixed_s[rows, :]).astype(BF16)
        o = _dot(gated, w_ref[...])
        _residual_router(rows, x_ref[rows, :], o, m, lg_ref, lb_ref, rw_ref, x1_ref, xm_ref, aff_ref, affp_ref)


def _mixer_out(kernel, name, acts, x, mod_l, weights, ln_g, ln_b, router_w, group0, rows_per_group,
               scratch=()):
    n = x.shape[0]
    row = lambda width: pl.BlockSpec((ROW_TILE, width), lambda i: (i, 0))
    full = lambda a: pl.BlockSpec(a.shape, lambda i: (0,) * a.ndim)
    return pl.pallas_call(
        kernel,
        grid=(n // ROW_TILE,),
        in_specs=([row(a.shape[1]) for a in acts]
                  + [row(D_MODEL), pl.BlockSpec((None, 6, D_MODEL), _group_map(group0, rows_per_group))]
                  + [full(w) for w in weights] + [full(ln_g), full(ln_b), full(router_w)]),
        out_specs=[row(D_MODEL), row(D_MODEL), row(N_EXPERTS), row(LANES)],
        out_shape=[jax.ShapeDtypeStruct((n, D_MODEL), F32),
                   jax.ShapeDtypeStruct((n, D_MODEL), BF16),
                   jax.ShapeDtypeStruct((n, N_EXPERTS), F32),
                   jax.ShapeDtypeStruct((n, LANES), BF16)],
        scratch_shapes=list(scratch),
        compiler_params=_params(1),
        name=name,
    )(*acts, x, mod_l, *weights, ln_g, ln_b, router_w)


def _sgu_in_kernel(x_ref, mod_ref, w_ref, b_ref, lg_ref, lb_ref, u_ref, v_ref):
    m = mod_ref[...]
    for rows in SUB_TILES:
        h = x_ref[rows, :] * (1.0 + m[1:2]) + m[0:1]
        p = _gelu_tanh(_dot(h.astype(BF16), w_ref[...]) + b_ref[...])
        u_ref[rows, :] = p[:, :SGU_WIDTH]
        v_ref[rows, :] = _layer_norm(p[:, SGU_WIDTH:], lg_ref[...], lb_ref[...]).astype(v_ref.dtype)


def _sgu_in(x, mod_l, w_bf16, b, ln_g, ln_b, group0, rows_per_group):
    n = x.shape[0]
    row = lambda width: pl.BlockSpec((ROW_TILE, width), lambda i: (i, 0))
    full = lambda a: pl.BlockSpec(a.shape, lambda i: (0,) * a.ndim)
    return pl.pallas_call(
        _sgu_in_kernel,
        grid=(n // ROW_TILE,),
        in_specs=[row(D_MODEL), pl.BlockSpec((None, 6, D_MODEL), _group_map(group0, rows_per_group)),
                  full(w_bf16), full(b), full(ln_g), full(ln_b)],
        out_specs=[row(SGU_WIDTH), row(SGU_WIDTH)],
        out_shape=[jax.ShapeDtypeStruct((n, SGU_WIDTH), F32),
                   jax.ShapeDtypeStruct((n, SGU_WIDTH), BF16)],
        compiler_params=_params(1),
        name="sgu_in",
    )(x, mod_l, w_bf16, b, ln_g, ln_b)


def _route_kernel(aff_ref, tri_ref, posm_ref, posr_ref, blk_ref, *, tokens, cap):
    aff = aff_ref[...]
    n_col = aff.shape[1]
    n_blk = tokens // PREFIX_BLOCK

    def bisect(_, lo_hi):
        lo, hi = lo_hi
        mid = lo + ((hi - lo) >> 1)
        cnt = jnp.sum(jnp.where(aff >= pltpu.bitcast(mid, F32), 1.0, 0.0), axis=0, keepdims=True)
        ge = cnt >= cap
        return jnp.where(ge, mid, lo), jnp.where(ge, hi, mid)

    lo0 = jnp.zeros((1, n_col), I32)
    hi0 = jnp.full((1, n_col), 0x7F800000, I32)
    thr_bits, _ = lax.fori_loop(0, 31, bisect, (lo0, hi0))
    thr = pltpu.bitcast(thr_bits, F32)
    above = pltpu.bitcast(thr_bits + 1, F32)

    tri = tri_ref[...]

    def excl_prefix(x01):
        outs, before, inside = [], [], []
        off = jnp.zeros((1, n_col), F32)
        for blk in range(n_blk):
            xb = x01[blk * PREFIX_BLOCK:(blk + 1) * PREFIX_BLOCK, :]
            outs.append(_dot(tri, xb.astype(BF16)) + off)
            cnt = jnp.sum(xb, axis=0, keepdims=True)
            before.append(off)
            inside.append(cnt)
            off = off + cnt
        return (outs[0] if n_blk == 1 else jnp.concatenate(outs, axis=0)), before + inside

    gt = jnp.where(aff >= above, 1.0, 0.0)
    eq = jnp.where(aff >= thr, 1.0, 0.0) - gt
    need = cap - jnp.sum(gt, axis=0, keepdims=True)
    eq_rank, _ = excl_prefix(eq)
    sel = gt + jnp.where(eq_rank < need, eq, 0.0)
    pos, blk_rows = excl_prefix(sel)
    posm = jnp.where(sel > 0.0, pos, -1.0)
    posm_ref[...] = posm
    posr_ref[...] = posm.T
    for i, row in enumerate(blk_rows):
        blk_ref[i:i + 1, :] = row


def _route(aff_t, tri, tokens, cap):
    n_col = aff_t.shape[1]
    n_blk = tokens // PREFIX_BLOCK
    whole = lambda shape: pl.BlockSpec(shape, lambda i: (0, 0))
    return pl.pallas_call(
        functools.partial(_route_kernel, tokens=tokens, cap=cap),
        grid=(1,),
        in_specs=[whole((tokens, n_col)), whole((PREFIX_BLOCK, PREFIX_BLOCK))],
        out_specs=[whole((tokens, n_col)), whole((n_col, tokens)), whole((2 * n_blk, n_col))],
        out_shape=[jax.ShapeDtypeStruct((tokens, n_col), F32),
                   jax.ShapeDtypeStruct((n_col, tokens), F32),
                   jax.ShapeDtypeStruct((2 * n_blk, n_col), F32)],
        compiler_params=_params(1),
        name="route_%d" % tokens,
    )(aff_t, tri)


def _route_requests(aff, tri, n_req, tokens, cap):
    n_col = n_req * N_EXPERTS
    n_blk = tokens // PREFIX_BLOCK
    aff_t = aff.reshape(n_req, tokens, N_EXPERTS).transpose(1, 0, 2).reshape(tokens, n_col)
    pad = (-n_col) % LANES
    if pad:
        aff_t = jnp.pad(aff_t, ((0, 0), (0, pad)))
    posm, posr, blk = _route(aff_t, tri, tokens, cap)
    posm = posm[:, :n_col].reshape(tokens, n_req, N_EXPERTS).transpose(1, 0, 2)
    blk = blk[:, :n_col].astype(I32).reshape(2, n_blk, n_req, N_EXPERTS).transpose(0, 2, 1, 3)
    return (posm.reshape(n_req * tokens, N_EXPERTS), posr[:n_col].reshape(n_req, N_EXPERTS, tokens),
            blk[0], blk[1])


def _slot_gate(r, expert):
    lane = lax.broadcasted_iota(I32, r.shape, 1)
    pick = ((lane & (N_EXPERTS - 1)) == expert) & (lane < 3 * N_EXPERTS)
    g = jnp.sum(jnp.where(pick, r, 0.0), axis=-1, keepdims=True)
    return jnp.broadcast_to(g, r.shape)


def _gather_prompt_kernel(x_ref, posr_ref, affp_ref, xg_ref, g_ref):
    posr = posr_ref[...]
    slot = lax.broadcasted_iota(I32, (CAP_P, SEQ), 0).astype(F32)
    onehot = jnp.concatenate([jnp.where(posr[e:e + 1, :] == slot, 1.0, 0.0) for e in range(N_EXPERTS)],
                             axis=0).astype(BF16)
    xg = _dot(onehot, x_ref[...])
    r = _dot(onehot, affp_ref[...])
    for e in range(N_EXPERTS):
        rows = slice(e * CAP_P, (e + 1) * CAP_P)
        xg_ref[e] = xg[rows, :].astype(xg_ref.dtype)
        g_ref[e] = _slot_gate(r[rows, :], e)


def _gather_prompt(xm, posr, affp):
    return pl.pallas_call(
        _gather_prompt_kernel,
        grid=(BATCH,),
        in_specs=[pl.BlockSpec((SEQ, D_MODEL), lambda b: (b, 0)),
                  pl.BlockSpec((None, N_EXPERTS, SEQ), lambda b: (b, 0, 0)),
                  pl.BlockSpec((SEQ, LANES), lambda b: (b, 0))],
        out_specs=[pl.BlockSpec((N_EXPERTS, CAP_P, D_MODEL), lambda b: (0, b, 0)),
                   pl.BlockSpec((N_EXPERTS, CAP_P, LANES), lambda b: (0, b, 0))],
        out_shape=[jax.ShapeDtypeStruct((N_EXPERTS, SLOTS_P, D_MODEL), BF16),
                   jax.ShapeDtypeStruct((N_EXPERTS, SLOTS_P, LANES), F32)],
        compiler_params=_params(1),
        name="gather_prompt",
    )(xm, posr, affp)


TOKEN_BLOCK = PREFIX_BLOCK
N_TOKEN_BLOCKS = DEC_SEQ // TOKEN_BLOCK
SLOT_ALIGN = 16
WINDOW_SLOTS = 80
LAST_WINDOW = CAP_S - WINDOW_SLOTS
ALL_WINDOWS = N_EXPERTS * WINDOW_SLOTS


def _window_plan(before, inside):
    start = (before // SLOT_ALIGN) * SLOT_ALIGN
    span = before - start + inside
    passes = jnp.where(inside > 0, (span + WINDOW_SLOTS - 1) // WINDOW_SLOTS, 0)
    return start.reshape(-1), jnp.max(passes, axis=-1).reshape(-1)


def _window_bounds(start_ref, step, expert, k):
    lower = start_ref[step * N_EXPERTS + expert] + WINDOW_SLOTS * k
    begin = pl.multiple_of(jnp.minimum(lower, LAST_WINDOW), SLOT_ALIGN)
    return lower, begin


def _gather_sample_kernel(start_ref, passes_ref, x_ref, posr_ref, affp_ref, xg_ref, g_ref):
    t = pl.program_id(1)
    step = pl.program_id(0) * N_TOKEN_BLOCKS + t

    @pl.when(t == 0)
    def _():
        xg_ref[...] = jnp.zeros_like(xg_ref)
        g_ref[...] = jnp.zeros_like(g_ref)

    posr = posr_ref[...]
    x = x_ref[...]
    affp = affp_ref[...]
    row = lax.broadcasted_iota(I32, (WINDOW_SLOTS, TOKEN_BLOCK), 0).astype(F32)

    def one_pass(k, carry):
        pieces, begins = [], []
        for e in range(N_EXPERTS):
            lower, begin = _window_bounds(start_ref, step, e, k)
            pe = posr[e:e + 1, :]
            hit = (pe - begin.astype(F32) == row) & (pe >= lower.astype(F32))
            pieces.append(jnp.where(hit, 1.0, 0.0))
            begins.append(begin)
        onehot = jnp.concatenate(pieces, axis=0).astype(BF16)
        rows_x = _dot(onehot, x)
        rows_g = _dot(onehot, affp)
        for e in range(N_EXPERTS):
            win = pl.ds(begins[e], WINDOW_SLOTS)
            src = slice(e * WINDOW_SLOTS, (e + 1) * WINDOW_SLOTS)
            xg_ref[e, win, :] += rows_x[src, :].astype(xg_ref.dtype)
            g_ref[e, win, :] += rows_g[src, :]
        return carry

    lax.fori_loop(0, passes_ref[step], one_pass, 0)

    @pl.when(t == N_TOKEN_BLOCKS - 1)
    def _():
        for e in range(N_EXPERTS):
            g_ref[e] = _slot_gate(g_ref[e], e)


def _gather_sample(xm, posr, affp, starts, passes):
    grid_spec = pltpu.PrefetchScalarGridSpec(
        num_scalar_prefetch=2,
        grid=(DEC_BATCH, N_TOKEN_BLOCKS),
        in_specs=[pl.BlockSpec((TOKEN_BLOCK, D_MODEL), lambda b, t, s, p: (b * N_TOKEN_BLOCKS + t, 0)),
                  pl.BlockSpec((None, N_EXPERTS, TOKEN_BLOCK), lambda b, t, s, p: (b, 0, t)),
                  pl.BlockSpec((TOKEN_BLOCK, LANES), lambda b, t, s, p: (b * N_TOKEN_BLOCKS + t, 0))],
        out_specs=[pl.BlockSpec((N_EXPERTS, CAP_S, D_MODEL), lambda b, t, s, p: (0, b, 0)),
                   pl.BlockSpec((N_EXPERTS, CAP_S, LANES), lambda b, t, s, p: (0, b, 0))],
    )
    return pl.pallas_call(
        _gather_sample_kernel,
        grid_spec=grid_spec,
        out_shape=[jax.ShapeDtypeStruct((N_EXPERTS, SLOTS_S, D_MODEL), BF16),
                   jax.ShapeDtypeStruct((N_EXPERTS, SLOTS_S, LANES), F32)],
        compiler_params=_params(2),
        name="gather_sample",
    )(starts, passes, xm, posr, affp)


FF_TILE = 512
SLOT_CHUNK = 512


def _ffn_kernel(xp_ref, xs_ref, gp_ref, gs_ref, w1_ref, w3_ref, w2_ref, yp_ref, ys_ref, acc_ref):
    f = pl.program_id(1)
    n_f = pl.num_programs(1)
    @pl.when(f == 0)
    def _():
        acc_ref[...] = jnp.zeros_like(acc_ref)

    w1 = w1_ref[...].astype(BF16)
    w3 = w3_ref[...].astype(BF16)
    w2 = w2_ref[...].astype(BF16)

    halves = ((xp_ref, SLOTS_P, 0), (xs_ref, SLOTS_S, SLOTS_P))
    for x_ref, n_slot, base in halves:
        for c in range(n_slot // SLOT_CHUNK):
            x = x_ref[c * SLOT_CHUNK:(c + 1) * SLOT_CHUNK, :]
            h1 = _dot(x, w1)
            h3 = _dot(x, w3)
            hid = (h1 * jax.nn.sigmoid(h1) * h3).astype(BF16)
            rows = slice(base + c * SLOT_CHUNK, base + (c + 1) * SLOT_CHUNK)
            acc_ref[rows, :] += _dot(hid, w2)

    @pl.when(f == n_f - 1)
    def _():
        for y_ref, g_ref, n_slot, base in ((yp_ref, gp_ref, SLOTS_P, 0), (ys_ref, gs_ref, SLOTS_S, SLOTS_P)):
            for c in range(n_slot // SLOT_CHUNK):
                rows = slice(c * SLOT_CHUNK, (c + 1) * SLOT_CHUNK)
                gate = jnp.concatenate([g_ref[rows, :]] * (D_MODEL // LANES), axis=1)
                y_ref[rows, :] = (acc_ref[base + c * SLOT_CHUNK:base + (c + 1) * SLOT_CHUNK, :]
                                  * gate).astype(y_ref.dtype)


def _expert_ffn(xgp, xgs, gp, gs, w1, w3, w2, layer):
    slots = lambda n, width: pl.BlockSpec((None, n, width), lambda e, f: (e, 0, 0))
    return pl.pallas_call(
        _ffn_kernel,
        grid=(N_EXPERTS, EXPERT_FF // FF_TILE),
        in_specs=[slots(SLOTS_P, D_MODEL), slots(SLOTS_S, D_MODEL), slots(SLOTS_P, LANES), slots(SLOTS_S, LANES),
                  pl.BlockSpec((None, None, D_MODEL, FF_TILE), lambda e, f: (layer, e, 0, f)),
                  pl.BlockSpec((None, None, D_MODEL, FF_TILE), lambda e, f: (layer, e, 0, f)),
                  pl.BlockSpec((None, None, FF_TILE, D_MODEL), lambda e, f: (layer, e, f, 0))],
        out_specs=[slots(SLOTS_P, D_MODEL), slots(SLOTS_S, D_MODEL)],
        out_shape=[jax.ShapeDtypeStruct((N_EXPERTS, SLOTS_P, D_MODEL), BF16),
                   jax.ShapeDtypeStruct((N_EXPERTS, SLOTS_S, D_MODEL), BF16)],
        scratch_shapes=[pltpu.VMEM((SLOTS_P + SLOTS_S, D_MODEL), F32)],
        compiler_params=_params(2),
        name="expert_ffn",
    )(xgp, xgs, gp, gs, w1, w3, w2)


def _combine_prompt_kernel(posm_ref, y_ref, x_ref, mod_ref, exp_ref, lg_ref, lb_ref, o_ref):
    spread = _dot(posm_ref[...].astype(BF16), exp_ref[...])
    slot = (lax.broadcasted_iota(I32, spread.shape, 1) & (CAP_P - 1)).astype(F32)
    onehot = jnp.where(spread == slot, 1.0, 0.0).astype(BF16)
    y = jnp.concatenate([y_ref[e] for e in range(N_EXPERTS)], axis=0)
    f = _dot(onehot, y)
    m = mod_ref[...]
    o_ref[...] = _layer_norm(ALPHA * x_ref[...] + m[5:6] * f, lg_ref[...], lb_ref[...])


def _combine_prompt(posm, y, x1, mod_l, expand, ln_g, ln_b):
    full = lambda a: pl.BlockSpec(a.shape, lambda b: (0,) * a.ndim)
    return pl.pallas_call(
        _combine_prompt_kernel,
        grid=(BATCH,),
        in_specs=[pl.BlockSpec((SEQ, N_EXPERTS), lambda b: (b, 0)),
                  pl.BlockSpec((N_EXPERTS, CAP_P, D_MODEL), lambda b: (0, b, 0)),
                  pl.BlockSpec((SEQ, D_MODEL), lambda b: (b, 0)),
                  pl.BlockSpec((None, 6, D_MODEL), lambda b: (0, 0, 0)),
                  full(expand), full(ln_g), full(ln_b)],
        out_specs=pl.BlockSpec((SEQ, D_MODEL), lambda b: (b, 0)),
        out_shape=jax.ShapeDtypeStruct((N_PROMPT, D_MODEL), F32),
        compiler_params=_params(1),
        name="combine_prompt",
    )(posm, y, x1, mod_l, expand, ln_g, ln_b)


def _combine_sample_kernel(start_ref, passes_ref, posm_ref, y_ref, x_ref, mod_ref, exp_ref, off_ref,
                           lg_ref, lb_ref, o_ref):
    step = pl.program_id(0) * N_TOKEN_BLOCKS + pl.program_id(1)
    exp_bf16 = exp_ref[...]
    spread = _dot(posm_ref[...].astype(BF16), exp_bf16)
    offset = off_ref[...]

    def one_pass(k, f):
        begin_row = jnp.zeros((1, ALL_WINDOWS), F32)
        lower_row = jnp.zeros((1, ALL_WINDOWS), F32)
        windows = []
        for e in range(N_EXPERTS):
            lower, begin = _window_bounds(start_ref, step, e, k)
            lanes_e = exp_bf16[e:e + 1, :].astype(F32)
            begin_row = begin_row + begin.astype(F32) * lanes_e
            lower_row = lower_row + lower.astype(F32) * lanes_e
            windows.append(y_ref[e, pl.ds(begin, WINDOW_SLOTS), :])
        hit = (spread - begin_row == offset) & (spread >= lower_row)
        onehot = jnp.where(hit, 1.0, 0.0).astype(BF16)
        return f + _dot(onehot, jnp.concatenate(windows, axis=0))

    f = lax.fori_loop(0, passes_ref[step], one_pass, jnp.zeros((TOKEN_BLOCK, D_MODEL), F32))
    m = mod_ref[...]
    o_ref[...] = _layer_norm(ALPHA * x_ref[...] + m[5:6] * f, lg_ref[...], lb_ref[...])


def _combine_sample(posm, y, x1, mod_l, ln_g, ln_b, starts, passes):
    lane = jnp.arange(ALL_WINDOWS)
    expand = ((lane[None, :] // WINDOW_SLOTS) == jnp.arange(N_EXPERTS)[:, None]).astype(BF16)
    offset = (lane % WINDOW_SLOTS).astype(F32).reshape(1, ALL_WINDOWS)
    row = lambda width: pl.BlockSpec((TOKEN_BLOCK, width), lambda b, t, s, p: (b * N_TOKEN_BLOCKS + t, 0))
    full = lambda a: pl.BlockSpec(a.shape, lambda b, t, s, p: (0,) * a.ndim)
    grid_spec = pltpu.PrefetchScalarGridSpec(
        num_scalar_prefetch=2,
        grid=(DEC_BATCH, N_TOKEN_BLOCKS),
        in_specs=[row(N_EXPERTS),
                  pl.BlockSpec((N_EXPERTS, CAP_S, D_MODEL), lambda b, t, s, p: (0, b, 0)),
                  row(D_MODEL),
                  pl.BlockSpec((None, 6, D_MODEL), lambda b, t, s, p: (1 + b, 0, 0)),
                  full(expand), full(offset), full(ln_g), full(ln_b)],
        out_specs=row(D_MODEL),
    )
    return pl.pallas_call(
        _combine_sample_kernel,
        grid_spec=grid_spec,
        out_shape=jax.ShapeDtypeStruct((N_SAMPLE, D_MODEL), F32),
        compiler_params=_params(2),
        name="combine_sample",
    )(starts, passes, posm, y, x1, mod_l, expand, offset, ln_g, ln_b)


def _moe(prompt, sample, mod_l, w1, w3, w2, layer, ln_g, ln_b, tri, expand):
    x1_p, xm_p, aff_p, affp_p = prompt
    x1_s, xm_s, aff_s, affp_s = sample
    posm_p, posr_p, _, _ = _route_requests(aff_p, tri, BATCH, SEQ, CAP_P)
    posm_s, posr_s, before_s, inside_s = _route_requests(aff_s, tri, DEC_BATCH, DEC_SEQ, CAP_S)
    starts, passes = _window_plan(before_s, inside_s)
    xg_p, g_p = _gather_prompt(xm_p, posr_p, affp_p)
    xg_s, g_s = _gather_sample(xm_s, posr_s, affp_s, starts, passes)
    y_p, y_s = _expert_ffn(xg_p, xg_s, g_p, g_s, w1, w3, w2, layer)
    x2_p = _combine_prompt(posm_p, y_p, x1_p, mod_l, expand, ln_g, ln_b)
    x2_s = _combine_sample(posm_s, y_s, x1_s, mod_l, ln_g, ln_b, starts, passes)
    return x2_p, x2_s


def kernel(x_prompt, x_sample, cache_k, cache_v, state_rglru, c, c_ctx, mod_w, mod_b, ln_mix_g, ln_mix_b, ln_ffn_g, ln_ffn_b, ab_in_w, attn_sink, rnn_conv_w, rnn_conv_b, lru_wa, lru_ba, lru_wx, lru_bx, lru_lambda, ab_out_w, sgu_in_w, sgu_in_b, sgu_ln_g, sgu_ln_b, sgu_spatial_w, sgu_spatial_b, sgu_out_w, router_w, moe_w1, moe_w3, moe_w2):
    xp = x_prompt.reshape(N_PROMPT, D_MODEL)
    xs = x_sample.reshape(N_SAMPLE, D_MODEL)

    cvec = jnp.concatenate([c_ctx[None], c, jnp.zeros((SUBLANES - 1 - DEC_BATCH, D_MODEL), F32)], axis=0)
    mod = _modulation(cvec, mod_w, mod_b).reshape(DEPTH, SUBLANES, 6, D_MODEL)

    idx = jnp.arange(PREFIX_BLOCK)
    tri = (idx[None, :] < idx[:, None]).astype(BF16)
    lane = jnp.arange(N_EXPERTS * CAP_P)
    expand = ((lane[None, :] // CAP_P) == jnp.arange(N_EXPERTS)[:, None]).astype(BF16)
    rope_tables = _rope_tables()
    vec = lambda a: a.reshape(1, -1)
    router_pad = lambda w: jnp.pad(w, ((0, 0), (0, LANES - N_EXPERTS))).astype(BF16)

    new_k = new_v = new_state = None
    for l in range(DEPTH):
        e = l // 2
        mod_l = mod[l]
        rw = router_pad(router_w[l])
        lg, lb = vec(ln_mix_g[l]), vec(ln_mix_b[l])
        if l % 2 == 0:
            w_in = ab_in_w[e].astype(BF16)
            w_out = ab_out_w[e].astype(BF16)
            wa = _block_diag_dense(lru_wa[e])
            wx = _block_diag_dense(lru_wx[e])
            rnn_w = (rnn_conv_w[e], vec(rnn_conv_b[e]), wa, lru_ba[e], wx, lru_bx[e], lru_lambda[e])

            q, k, v, xr, xg = _ab_in(xp, mod_l, w_in, 0, N_PROMPT, None, F32)
            new_k, new_v = k, v
            att = _ctx_attention(q, k, v, attn_sink[e])
            rows = 4 * SEQ
            zeros = jnp.zeros((N_PROMPT // rows, rows // SEQ, RNN_WIDTH), F32)
            rnn, hf_last, hb_first = _rglru(xr, xg, *rnn_w, zeros, zeros, rows, SEQ)
            new_state = jnp.stack([hf_last.reshape(BATCH, RNN_WIDTH), hb_first.reshape(BATCH, RNN_WIDTH)], axis=1)
            prompt = _mixer_out(_mix_out_kernel, "mix_out", (att, rnn), xp, mod_l, (w_out,),
                                lg, lb, rw, 0, N_PROMPT)

            q, k, v, xr, xg = _ab_in(xs, mod_l, w_in, 1, DEC_SEQ, rope_tables, BF16)
            att = _lat_attention(q, k, v, cache_k[:, e].reshape(DEC_BATCH * PAST_LEN, KV_WIDTH),
                                 cache_v[:, e].reshape(DEC_BATCH * PAST_LEN, KV_WIDTH), attn_sink[e])
            h0 = state_rglru[:, e]
            rnn, _, _ = _rglru(xr, xg, *rnn_w, h0[:, 0][:, None, :], h0[:, 1][:, None, :], DEC_SEQ, DEC_SEQ)
            sample = _mixer_out(_mix_out_kernel, "mix_out", (att, rnn), xs, mod_l, (w_out,),
                                lg, lb, rw, 1, DEC_SEQ)
        else:
            w_in = sgu_in_w[e].astype(BF16)
            w_out = sgu_out_w[e].astype(BF16)
            spw = sgu_spatial_w[e].astype(BF16)
            spb = jnp.repeat(sgu_spatial_b[e].T, SGU_GROUP_W, axis=1)
            sgu_args = (w_in, vec(sgu_in_b[e]), vec(sgu_ln_g[e]), vec(sgu_ln_b[e]))
            scratch = (pltpu.VMEM((ROW_TILE, SGU_WIDTH), F32),)
            u, vn = _sgu_in(xp, mod_l, *sgu_args, 0, N_PROMPT)
            prompt = _mixer_out(_sgu_out_kernel, "sgu_out", (u, vn), xp, mod_l, (spw, spb, w_out),
                                lg, lb, rw, 0, N_PROMPT, scratch)
            u, vn = _sgu_in(xs, mod_l, *sgu_args, 1, DEC_SEQ)
            sample = _mixer_out(_sgu_out_kernel, "sgu_out", (u, vn), xs, mod_l, (spw, spb, w_out),
                                lg, lb, rw, 1, DEC_SEQ, scratch)

        xp, xs = _moe(prompt, sample, mod_l, moe_w1, moe_w3, moe_w2, l,
                      vec(ln_ffn_g[l]), vec(ln_ffn_b[l]), tri, expand)

    return (xp.reshape(BATCH, SEQ, D_MODEL),
            xs.reshape(DEC_BATCH, DEC_SEQ, D_MODEL),
            new_k.reshape(BATCH, 1, SEQ, N_KV_HEADS, HEAD_DIM),
            new_v.reshape(BATCH, 1, SEQ, N_KV_HEADS, HEAD_DIM),
            new_state.reshape(BATCH, 1, 2, RNN_WIDTH))
```

```python
import functools

import jax
import jax.numpy as jnp
from jax import lax
from jax.experimental import pallas as pl
from jax.experimental.pallas import tpu as pltpu

F32 = jnp.float32
BF16 = jnp.bfloat16
I32 = jnp.int32

D_MODEL = 1024
BATCH = 32
SEQ = 256
DEPTH = 2
DEC_BATCH = 4
DEC_SEQ = 2048
PAST_LEN = 256
GRID_W = 64
HEAD_DIM = 128
N_Q_HEADS = 4
N_KV_HEADS = 2
Q_PER_KV = N_Q_HEADS // N_KV_HEADS
ATTN_WIDTH = N_Q_HEADS * HEAD_DIM
KV_WIDTH = N_KV_HEADS * HEAD_DIM
WINDOW = 128
BLOCK = 128
ATTN_SCALE = HEAD_DIM ** -0.5
ROPE_BASE = 10000.0
NEG_INF = -1e30
RNN_WIDTH = 512
RNN_BLOCKS = 8
RNN_BLOCK_W = RNN_WIDTH // RNN_BLOCKS
CONV_W = 4
CONV_PAD_LEFT = 2
LRU_C = 8.0
AB_IN_WIDTH = ATTN_WIDTH + 2 * KV_WIDTH + 2 * RNN_WIDTH
CHUNK = 128
SGU_WIDTH = D_MODEL
SGU_GROUPS = 8
SGU_GROUP_W = SGU_WIDTH // SGU_GROUPS
N_EXPERTS = 16
EXPERT_FF = 2048
EC_FACTOR = 2
ALPHA = (2 * DEPTH) ** 0.25
LN_EPS = 1e-6

N_PROMPT = BATCH * SEQ
N_SAMPLE = DEC_BATCH * DEC_SEQ
CAP_P = EC_FACTOR * SEQ // N_EXPERTS
CAP_S = EC_FACTOR * DEC_SEQ // N_EXPERTS
SLOTS_P = BATCH * CAP_P
SLOTS_S = DEC_BATCH * CAP_S

LANES = 128
SUBLANES = 8
ROW_TILE = 1024
REQ_TILE = 4
PREFIX_BLOCK = 256
VMEM_LIMIT = 56 * 1024 * 1024


def _params(n_axes=1):
    return pltpu.CompilerParams(dimension_semantics=("arbitrary",) * n_axes,
                                vmem_limit_bytes=VMEM_LIMIT)


def _layer_norm(x, g, b):
    mu = jnp.mean(x, axis=-1, keepdims=True)
    xc = x - mu
    var = jnp.mean(xc * xc, axis=-1, keepdims=True)
    return xc * lax.rsqrt(var + LN_EPS) * g + b


def _gelu_tanh(x):
    return 0.5 * x * (1.0 + jnp.tanh(0.7978845608028654 * (x + 0.044715 * (x * x * x))))


def _dot(a, b):
    return jnp.dot(a, b, preferred_element_type=F32)


def _dot_nt(a, b):
    return lax.dot_general(a, b, (((1,), (1,)), ((), ())), preferred_element_type=F32)


def _mod_kernel(c_ref, w_ref, b_ref, o_ref):
    c = c_ref[...]
    s = c * jax.nn.sigmoid(c)
    o_ref[...] = _dot(s.astype(BF16), w_ref[...].astype(BF16)) + b_ref[...]


def _modulation(cvec8, mod_w, mod_b):
    n_col = 6 * D_MODEL // D_MODEL
    return pl.pallas_call(
        _mod_kernel,
        grid=(DEPTH, n_col),
        in_specs=[pl.BlockSpec((SUBLANES, D_MODEL), lambda l, j: (0, 0)),
                  pl.BlockSpec((None, D_MODEL, D_MODEL), lambda l, j: (l, 0, j)),
                  pl.BlockSpec((None, 1, D_MODEL), lambda l, j: (l, 0, j))],
        out_specs=pl.BlockSpec((None, SUBLANES, D_MODEL), lambda l, j: (l, 0, j)),
        out_shape=jax.ShapeDtypeStruct((DEPTH, SUBLANES, 6 * D_MODEL), F32),
        compiler_params=_params(2),
        name="adaln_modulation",
    )(cvec8, mod_w, mod_b.reshape(DEPTH, 1, 6 * D_MODEL))


def _group_map(group0, rows_per_group):
    tiles_per_group = rows_per_group // ROW_TILE
    return lambda i: (group0 + i // tiles_per_group, 0, 0)


def _rope(t, cos, sin_signed):
    lane = lax.broadcasted_iota(I32, t.shape, 1)
    swapped = jnp.where((lane & 1) == 0, pltpu.roll(t, HEAD_DIM - 1, 1), pltpu.roll(t, 1, 1))
    return t * cos + swapped * sin_signed


def _ab_in_kernel(*refs, rope):
    if rope:
        x_ref, mod_ref, w_ref, cos_ref, sin_ref, q_ref, k_ref, v_ref, xr_ref, xg_ref = refs
    else:
        x_ref, mod_ref, w_ref, q_ref, k_ref, v_ref, xr_ref, xg_ref = refs
    m = mod_ref[...]
    h = x_ref[...] * (1.0 + m[1:2]) + m[0:1]
    p = _dot(h.astype(BF16), w_ref[...])
    q = p[:, :ATTN_WIDTH]
    k = p[:, ATTN_WIDTH:ATTN_WIDTH + KV_WIDTH]
    v = p[:, ATTN_WIDTH + KV_WIDTH:ATTN_WIDTH + 2 * KV_WIDTH]
    if rope:
        cos = cos_ref[...]
        sin = sin_ref[...]
        q = jnp.concatenate([_rope(q[:, i * HEAD_DIM:(i + 1) * HEAD_DIM], cos, sin)
                             for i in range(N_Q_HEADS)], axis=1)
        k = jnp.concatenate([_rope(k[:, i * HEAD_DIM:(i + 1) * HEAD_DIM], cos, sin)
                             for i in range(N_KV_HEADS)], axis=1)
    q_ref[...] = q.astype(q_ref.dtype)
    k_ref[...] = k.astype(k_ref.dtype)
    v_ref[...] = v.astype(v_ref.dtype)
    xr_ref[...] = p[:, ATTN_WIDTH + 2 * KV_WIDTH:ATTN_WIDTH + 2 * KV_WIDTH + RNN_WIDTH]
    xg_ref[...] = p[:, ATTN_WIDTH + 2 * KV_WIDTH + RNN_WIDTH:]


def _ab_in(x, mod_l, w_bf16, group0, rows_per_group, rope_tables, kv_dtype):
    n = x.shape[0]
    rope = rope_tables is not None
    row = lambda width: pl.BlockSpec((ROW_TILE, width), lambda i: (i, 0))
    in_specs = [row(D_MODEL),
                pl.BlockSpec((None, 6, D_MODEL), _group_map(group0, rows_per_group)),
                pl.BlockSpec((D_MODEL, AB_IN_WIDTH), lambda i: (0, 0))]
    args = [x, mod_l, w_bf16]
    if rope:
        tiles_per_seq = DEC_SEQ // ROW_TILE
        in_specs += [pl.BlockSpec((ROW_TILE, HEAD_DIM), lambda i: (i % tiles_per_seq, 0))] * 2
        args += list(rope_tables)
    return pl.pallas_call(
        functools.partial(_ab_in_kernel, rope=rope),
        grid=(n // ROW_TILE,),
        in_specs=in_specs,
        out_specs=[row(ATTN_WIDTH), row(KV_WIDTH), row(KV_WIDTH), row(RNN_WIDTH), row(RNN_WIDTH)],
        out_shape=[jax.ShapeDtypeStruct((n, ATTN_WIDTH), BF16),
                   jax.ShapeDtypeStruct((n, KV_WIDTH), kv_dtype),
                   jax.ShapeDtypeStruct((n, KV_WIDTH), kv_dtype),
                   jax.ShapeDtypeStruct((n, RNN_WIDTH), F32),
                   jax.ShapeDtypeStruct((n, RNN_WIDTH), F32)],
        compiler_params=_params(1),
        name="ab_in_rope" if rope else "ab_in",
    )(*args)


def _rope_tables():
    rows = DEC_SEQ // GRID_W
    row = jnp.repeat(jnp.arange(rows, dtype=F32), GRID_W)
    col = jnp.tile(jnp.arange(GRID_W, dtype=F32), rows)
    n_freq = HEAD_DIM // 4
    freqs = ROPE_BASE ** (-jnp.arange(n_freq, dtype=F32) / n_freq)
    ang = jnp.concatenate([row[:, None] * freqs, col[:, None] * freqs], axis=-1)
    cos = jnp.repeat(jnp.cos(ang), 2, axis=-1)
    sin = jnp.repeat(jnp.sin(ang), 2, axis=-1)
    sign = jnp.tile(jnp.array([-1.0, 1.0], F32), HEAD_DIM // 2)
    return cos, sin * sign


def _sink_attention_head(s_list, v_list, sink):
    m = sink
    for s in s_list:
        m = jnp.maximum(m, jnp.max(s, axis=-1, keepdims=True))
    p_list = [jnp.exp(s - m) for s in s_list]
    denom = jnp.exp(sink - m)
    for p in p_list:
        denom = denom + jnp.sum(p, axis=-1, keepdims=True)
    out = None
    for p, v in zip(p_list, v_list):
        o = _dot(p.astype(BF16), v)
        out = o if out is None else out + o
    return out * (1.0 / denom)


def _ctx_attn_kernel(sink_ref, q_ref, k_ref, v_ref, o_ref):
    for r in range(REQ_TILE):
        rows = slice(r * SEQ, (r + 1) * SEQ)
        q = q_ref[rows, :]
        k = k_ref[rows, :].astype(BF16)
        v = v_ref[rows, :].astype(BF16)
        outs = []
        for h in range(N_Q_HEADS):
            kv = h // Q_PER_KV
            qh = q[:, h * HEAD_DIM:(h + 1) * HEAD_DIM]
            kh = k[:, kv * HEAD_DIM:(kv + 1) * HEAD_DIM]
            vh = v[:, kv * HEAD_DIM:(kv + 1) * HEAD_DIM]
            s = _dot_nt(qh, kh) * ATTN_SCALE
            outs.append(_sink_attention_head([s], [vh], sink_ref[h]))
        o_ref[rows, :] = jnp.concatenate(outs, axis=1).astype(o_ref.dtype)


def _ctx_attention(q, k, v, sink):
    seq = lambda width: pl.BlockSpec((REQ_TILE * SEQ, width), lambda b: (b, 0))
    return pl.pallas_call(
        _ctx_attn_kernel,
        grid=(BATCH // REQ_TILE,),
        in_specs=[pl.BlockSpec(memory_space=pltpu.SMEM), seq(ATTN_WIDTH), seq(KV_WIDTH), seq(KV_WIDTH)],
        out_specs=seq(ATTN_WIDTH),
        out_shape=jax.ShapeDtypeStruct((N_PROMPT, ATTN_WIDTH), BF16),
        compiler_params=_params(1),
        name="context_attention",
    )(sink, q, k, v)


LAT_Q = 256


def _lat_attn_kernel(sink_ref, q_ref, kp_ref, kc_ref, kn_ref, vp_ref, vc_ref, vn_ref,
                     kx_ref, vx_ref, o_ref):
    n = pl.program_id(1)
    nb = pl.num_programs(1)
    q = q_ref[...]
    kw = jnp.concatenate([kp_ref[...], kc_ref[...], kn_ref[...]], axis=0)
    vw = jnp.concatenate([vp_ref[...], vc_ref[...], vn_ref[...]], axis=0)
    kx = kx_ref[...].astype(BF16)
    vx = vx_ref[...].astype(BF16)
    n_key = LAT_Q + 2 * WINDOW
    qi = lax.broadcasted_iota(I32, (LAT_Q, n_key), 0)
    kj = lax.broadcasted_iota(I32, (LAT_Q, n_key), 1)
    rel = kj - qi
    valid = (rel >= 0) & (rel <= 2 * WINDOW)
    valid = valid & ((n > 0) | (kj >= WINDOW)) & ((n < nb - 1) | (kj < LAT_Q + WINDOW))
    outs = []
    for h in range(N_Q_HEADS):
        kv = h // Q_PER_KV
        sl = slice(kv * HEAD_DIM, (kv + 1) * HEAD_DIM)
        qh = q[:, h * HEAD_DIM:(h + 1) * HEAD_DIM]
        s_ctx = _dot_nt(qh, kx[:, sl]) * ATTN_SCALE
        s_win = jnp.where(valid, _dot_nt(qh, kw[:, sl]) * ATTN_SCALE, NEG_INF)
        outs.append(_sink_attention_head([s_ctx, s_win], [vx[:, sl], vw[:, sl]], sink_ref[h]))
    o_ref[...] = jnp.concatenate(outs, axis=1).astype(o_ref.dtype)


def _lat_attention(q, k, v, k_ctx, v_ctx, sink):
    nb = DEC_SEQ // LAT_Q
    nw = DEC_SEQ // WINDOW
    per = LAT_Q // WINDOW
    cur = lambda b, n: (b * nb + n, 0)
    prev = lambda b, n: (b * nw + jnp.maximum(n * per - 1, 0), 0)
    nxt = lambda b, n: (b * nw + jnp.minimum((n + 1) * per, nw - 1), 0)
    tile = lambda width: pl.BlockSpec((LAT_Q, width), cur)
    edge = lambda imap: pl.BlockSpec((WINDOW, KV_WIDTH), imap)
    ctx = pl.BlockSpec((PAST_LEN, KV_WIDTH), lambda b, n: (b, 0))
    return pl.pallas_call(
        _lat_attn_kernel,
        grid=(DEC_BATCH, nb),
        in_specs=[pl.BlockSpec(memory_space=pltpu.SMEM), tile(ATTN_WIDTH),
                  edge(prev), tile(KV_WIDTH), edge(nxt),
                  edge(prev), tile(KV_WIDTH), edge(nxt), ctx, ctx],
        out_specs=tile(ATTN_WIDTH),
        out_shape=jax.ShapeDtypeStruct((N_SAMPLE, ATTN_WIDTH), BF16),
        compiler_params=_params(2),
        name="latent_attention",
    )(sink, q, k, k, k, v, v, v, k_ctx, v_ctx)


RNN_CHUNK = 256


def _rglru_kernel(xr_ref, xg_ref, cw_ref, cb_ref, wa_ref, ba_ref, wx_ref, bx_ref, lam_ref,
                  h0f_ref, h0b_ref, y_ref, hfl_ref, hbf_ref, xc_s, a_s, uf_s, ub_s, *, rows, seq_len):
    n_seq = rows // seq_len
    n_chunk = rows // RNN_CHUNK
    cw = cw_ref[...]
    cb = cb_ref[...]
    zeros_halo = jnp.zeros((SUBLANES, RNN_WIDTH), F32)
    row8 = lax.broadcasted_iota(I32, (SUBLANES, RNN_WIDTH), 0)

    def conv_chunk(c):
        r0 = c * RNN_CHUNK
        first = r0 % seq_len == 0
        last = (r0 + RNN_CHUNK) % seq_len == 0
        before = zeros_halo if first else xr_ref[r0 - SUBLANES:r0, :]
        after = zeros_halo if last else xr_ref[r0 + RNN_CHUNK:r0 + RNN_CHUNK + SUBLANES, :]
        win = jnp.concatenate([before, xr_ref[r0:r0 + RNN_CHUNK, :], after], axis=0)
        xc = cb
        n_win = RNN_CHUNK + 2 * SUBLANES
        for i in range(CONV_W):
            shift = (CONV_PAD_LEFT - i) % n_win
            rolled = win if shift == 0 else pltpu.roll(win, shift, 0)
            xc = xc + rolled[SUBLANES:SUBLANES + RNN_CHUNK, :] * cw[i:i + 1, :]
        return xc

    def group_scan(a, u, reverse):
        for k in (1, 2, 4):
            if reverse:
                shift, ok = SUBLANES - k, row8 < SUBLANES - k
            else:
                shift, ok = k, row8 >= k
            a_nb = jnp.where(ok, pltpu.roll(a, shift, 0), 1.0)
            u_nb = jnp.where(ok, pltpu.roll(u, shift, 0), 0.0)
            u = a * u_nb + u
            a = a * a_nb
        return a, u

    for c in range(n_chunk):
        xc_s[c * RNN_CHUNK:(c + 1) * RNN_CHUNK, :] = conv_chunk(c)

    finals = []
    for d, (u_s, h0_ref) in enumerate(((uf_s, h0f_ref), (ub_s, h0b_ref))):
        reverse = d == 1
        neg = -lam_ref[d:d + 1, :]
        softplus = jnp.maximum(neg, 0.0) + jnp.log1p(jnp.exp(-jnp.abs(neg)))
        decay = -LRU_C * softplus
        wa = wa_ref[d]
        wx = wx_ref[d]
        ba = ba_ref[d:d + 1, :]
        bx = bx_ref[d:d + 1, :]
        for c in range(n_chunk):
            xc = xc_s[c * RNN_CHUNK:(c + 1) * RNN_CHUNK, :]
            xcb = xc.astype(BF16)
            r = jax.nn.sigmoid(_dot(xcb, wa) + ba)
            i = jax.nn.sigmoid(_dot(xcb, wx) + bx)
            log_a = r * decay
            a = jnp.exp(log_a)
            a_s[c * RNN_CHUNK:(c + 1) * RNN_CHUNK, :] = a
            one_minus_a2 = -jnp.tanh(log_a) * (a * a + 1.0)
            u_s[c * RNN_CHUNK:(c + 1) * RNN_CHUNK, :] = jnp.sqrt(one_minus_a2) * (i * xc)

        n_group = seq_len // SUBLANES

        def body(g, carries, u_s=u_s, reverse=reverse):
            gg = n_group - 1 - g if reverse else g
            new = []
            for s in range(n_seq):
                r0 = pl.multiple_of(s * seq_len + gg * SUBLANES, SUBLANES)
                a, u = group_scan(a_s[pl.ds(r0, SUBLANES), :], u_s[pl.ds(r0, SUBLANES), :], reverse)
                h = u + a * carries[s]
                u_s[pl.ds(r0, SUBLANES), :] = h
                new.append(h[0:1, :] if reverse else h[SUBLANES - 1:SUBLANES, :])
            return tuple(new)

        init = tuple(h0_ref[s:s + 1, :] for s in range(n_seq))
        finals.append(lax.fori_loop(0, n_group, body, init))

    for s in range(n_seq):
        hfl_ref[s:s + 1, :] = finals[0][s]
        hbf_ref[s:s + 1, :] = finals[1][s]
    for c in range(n_chunk):
        sl = slice(c * RNN_CHUNK, (c + 1) * RNN_CHUNK)
        y_ref[sl, :] = ((uf_s[sl, :] + ub_s[sl, :]) * _gelu_tanh(xg_ref[sl, :])).astype(y_ref.dtype)


def _rglru(xr, xg, conv_w, conv_b, wa, ba, wx, bx, lam, h0f, h0b, rows, seq_len):
    n = xr.shape[0]
    n_seq = rows // seq_len
    row = pl.BlockSpec((rows, RNN_WIDTH), lambda i: (i, 0))
    full = lambda shape: pl.BlockSpec(shape, lambda i: (0,) * len(shape))
    state = pl.BlockSpec((None, n_seq, RNN_WIDTH), lambda i: (i, 0, 0))
    state_shape = jax.ShapeDtypeStruct((n // rows, n_seq, RNN_WIDTH), F32)
    return pl.pallas_call(
        functools.partial(_rglru_kernel, rows=rows, seq_len=seq_len),
        grid=(n // rows,),
        in_specs=[row, row, full((CONV_W, RNN_WIDTH)), full((1, RNN_WIDTH)),
                  full((2, RNN_WIDTH, RNN_WIDTH)), full((2, RNN_WIDTH)),
                  full((2, RNN_WIDTH, RNN_WIDTH)), full((2, RNN_WIDTH)), full((2, RNN_WIDTH)),
                  state, state],
        out_specs=[row, state, state],
        out_shape=[jax.ShapeDtypeStruct((n, RNN_WIDTH), BF16), state_shape, state_shape],
        scratch_shapes=[pltpu.VMEM((rows, RNN_WIDTH), F32)] * 4,
        compiler_params=_params(1),
        name="rglru_%d" % seq_len,
    )(xr, xg, conv_w, conv_b, wa, ba, wx, bx, lam, h0f, h0b)


def _block_diag_dense(w):
    eye = jnp.eye(RNN_BLOCKS, dtype=w.dtype)
    dense = w[:, :, :, None, :] * eye[None, :, None, :, None]
    return dense.reshape(2, RNN_WIDTH, RNN_WIDTH).astype(BF16)


SUB_TILE = 256
SUB_TILES = tuple(slice(s, s + SUB_TILE) for s in range(0, ROW_TILE, SUB_TILE))


def _residual_router(rows, x, o, m, lg_ref, lb_ref, rw_ref, x1_ref, xm_ref, aff_ref, affp_ref):
    x1 = _layer_norm(ALPHA * x + m[2:3] * o, lg_ref[...], lb_ref[...])
    x1_ref[rows, :] = x1
    xm = (x1 * (1.0 + m[4:5]) + m[3:4]).astype(BF16)
    xm_ref[rows, :] = xm
    lgt = _dot(xm, rw_ref[...])
    lane = lax.broadcasted_iota(I32, lgt.shape, 1)
    lgt = jnp.where(lane < N_EXPERTS, lgt, NEG_INF)
    ex = jnp.exp(lgt - jnp.max(lgt, axis=-1, keepdims=True))
    aff = ex / jnp.sum(ex, axis=-1, keepdims=True)
    aff_ref[rows, :] = aff[:, :N_EXPERTS]
    hi = aff.astype(BF16).astype(F32)
    rem = aff - hi
    mid = rem.astype(BF16).astype(F32)
    low = (rem - mid).astype(BF16).astype(F32)
    pack = hi + pltpu.roll(mid, N_EXPERTS, 1) + pltpu.roll(low, 2 * N_EXPERTS, 1)
    affp_ref[rows, :] = pack.astype(BF16)


def _mix_out_kernel(att_ref, rnn_ref, x_ref, mod_ref, w_ref, lg_ref, lb_ref, rw_ref,
                    x1_ref, xm_ref, aff_ref, affp_ref):
    m = mod_ref[...]
    for rows in SUB_TILES:
        o = _dot(att_ref[rows, :], w_ref[:ATTN_WIDTH, :]) + _dot(rnn_ref[rows, :], w_ref[ATTN_WIDTH:, :])
        _residual_router(rows, x_ref[rows, :], o, m, lg_ref, lb_ref, rw_ref, x1_ref, xm_ref, aff_ref, affp_ref)


def _sgu_out_kernel(u_ref, v_ref, x_ref, mod_ref, spw_ref, spb_ref, w_ref, lg_ref, lb_ref, rw_ref,
                    x1_ref, xm_ref, aff_ref, affp_ref, mixed_s):
    m = mod_ref[...]
    for rows in SUB_TILES:
        for c in range(rows.start, rows.stop, CHUNK):
            for g in range(SGU_GROUPS):
                cols = slice(g * SGU_GROUP_W, (g + 1) * SGU_GROUP_W)
                mixed_s[c:c + CHUNK, cols] = _dot(spw_ref[g], v_ref[c:c + CHUNK, cols]) + spb_ref[:, cols]
        gated = (u_ref[rows, :] * mixed_s[rows, :]).astype(BF16)
        o = _dot(gated, w_ref[...])
        _residual_router(rows, x_ref[rows, :], o, m, lg_ref, lb_ref, rw_ref, x1_ref, xm_ref, aff_ref, affp_ref)


def _mixer_out(kernel, name, acts, x, mod_l, weights, ln_g, ln_b, router_w, group0, rows_per_group,
               scratch=()):
    n = x.shape[0]
    row = lambda width: pl.BlockSpec((ROW_TILE, width), lambda i: (i, 0))
    full = lambda a: pl.BlockSpec(a.shape, lambda i: (0,) * a.ndim)
    return pl.pallas_call(
        kernel,
        grid=(n // ROW_TILE,),
        in_specs=([row(a.shape[1]) for a in acts]
                  + [row(D_MODEL), pl.BlockSpec((None, 6, D_MODEL), _group_map(group0, rows_per_group))]
                  + [full(w) for w in weights] + [full(ln_g), full(ln_b), full(router_w)]),
        out_specs=[row(D_MODEL), row(D_MODEL), row(N_EXPERTS), row(LANES)],
        out_shape=[jax.ShapeDtypeStruct((n, D_MODEL), F32),
                   jax.ShapeDtypeStruct((n, D_MODEL), BF16),
                   jax.ShapeDtypeStruct((n, N_EXPERTS), F32),
                   jax.ShapeDtypeStruct((n, LANES), BF16)],
        scratch_shapes=list(scratch),
        compiler_params=_params(1),
        name=name,
    )(*acts, x, mod_l, *weights, ln_g, ln_b, router_w)


def _sgu_in_kernel(x_ref, mod_ref, w_ref, b_ref, lg_ref, lb_ref, u_ref, v_ref):
    m = mod_ref[...]
    for rows in SUB_TILES:
        h = x_ref[rows, :] * (1.0 + m[1:2]) + m[0:1]
        p = _gelu_tanh(_dot(h.astype(BF16), w_ref[...]) + b_ref[...])
        u_ref[rows, :] = p[:, :SGU_WIDTH]
        v_ref[rows, :] = _layer_norm(p[:, SGU_WIDTH:], lg_ref[...], lb_ref[...]).astype(v_ref.dtype)


def _sgu_in(x, mod_l, w_bf16, b, ln_g, ln_b, group0, rows_per_group):
    n = x.shape[0]
    row = lambda width: pl.BlockSpec((ROW_TILE, width), lambda i: (i, 0))
    full = lambda a: pl.BlockSpec(a.shape, lambda i: (0,) * a.ndim)
    return pl.pallas_call(
        _sgu_in_kernel,
        grid=(n // ROW_TILE,),
        in_specs=[row(D_MODEL), pl.BlockSpec((None, 6, D_MODEL), _group_map(group0, rows_per_group)),
                  full(w_bf16), full(b), full(ln_g), full(ln_b)],
        out_specs=[row(SGU_WIDTH), row(SGU_WIDTH)],
        out_shape=[jax.ShapeDtypeStruct((n, SGU_WIDTH), F32),
                   jax.ShapeDtypeStruct((n, SGU_WIDTH), BF16)],
        compiler_params=_params(1),
        name="sgu_in",
    )(x, mod_l, w_bf16, b, ln_g, ln_b)


def _route_kernel(aff_ref, tri_ref, posm_ref, posr_ref, blk_ref, *, tokens, cap):
    aff = aff_ref[...]
    n_col = aff.shape[1]
    n_blk = tokens // PREFIX_BLOCK

    def bisect(_, lo_hi):
        lo, hi = lo_hi
        mid = lo + ((hi - lo) >> 1)
        cnt = jnp.sum(jnp.where(aff >= pltpu.bitcast(mid, F32), 1.0, 0.0), axis=0, keepdims=True)
        ge = cnt >= cap
        return jnp.where(ge, mid, lo), jnp.where(ge, hi, mid)

    lo0 = jnp.zeros((1, n_col), I32)
    hi0 = jnp.full((1, n_col), 0x7F800000, I32)
    thr_bits, _ = lax.fori_loop(0, 31, bisect, (lo0, hi0))
    thr = pltpu.bitcast(thr_bits, F32)
    above = pltpu.bitcast(thr_bits + 1, F32)

    tri = tri_ref[...]

    def excl_prefix(x01):
        outs, before, inside = [], [], []
        off = jnp.zeros((1, n_col), F32)
        for blk in range(n_blk):
            xb = x01[blk * PREFIX_BLOCK:(blk + 1) * PREFIX_BLOCK, :]
            outs.append(_dot(tri, xb.astype(BF16)) + off)
            cnt = jnp.sum(xb, axis=0, keepdims=True)
            before.append(off)
            inside.append(cnt)
            off = off + cnt
        return (outs[0] if n_blk == 1 else jnp.concatenate(outs, axis=0)), before + inside

    gt = jnp.where(aff >= above, 1.0, 0.0)
    eq = jnp.where(aff >= thr, 1.0, 0.0) - gt
    need = cap - jnp.sum(gt, axis=0, keepdims=True)
    eq_rank, _ = excl_prefix(eq)
    sel = gt + jnp.where(eq_rank < need, eq, 0.0)
    pos, blk_rows = excl_prefix(sel)
    posm = jnp.where(sel > 0.0, pos, -1.0)
    posm_ref[...] = posm
    posr_ref[...] = posm.T
    for i, row in enumerate(blk_rows):
        blk_ref[i:i + 1, :] = row


def _route(aff_t, tri, tokens, cap):
    n_col = aff_t.shape[1]
    n_blk = tokens // PREFIX_BLOCK
    whole = lambda shape: pl.BlockSpec(shape, lambda i: (0, 0))
    return pl.pallas_call(
        functools.partial(_route_kernel, tokens=tokens, cap=cap),
        grid=(1,),
        in_specs=[whole((tokens, n_col)), whole((PREFIX_BLOCK, PREFIX_BLOCK))],
        out_specs=[whole((tokens, n_col)), whole((n_col, tokens)), whole((2 * n_blk, n_col))],
        out_shape=[jax.ShapeDtypeStruct((tokens, n_col), F32),
                   jax.ShapeDtypeStruct((n_col, tokens), F32),
                   jax.ShapeDtypeStruct((2 * n_blk, n_col), F32)],
        compiler_params=_params(1),
        name="route_%d" % tokens,
    )(aff_t, tri)


def _route_requests(aff, tri, n_req, tokens, cap):
    n_col = n_req * N_EXPERTS
    n_blk = tokens // PREFIX_BLOCK
    aff_t = aff.reshape(n_req, tokens, N_EXPERTS).transpose(1, 0, 2).reshape(tokens, n_col)
    pad = (-n_col) % LANES
    if pad:
        aff_t = jnp.pad(aff_t, ((0, 0), (0, pad)))
    posm, posr, blk = _route(aff_t, tri, tokens, cap)
    posm = posm[:, :n_col].reshape(tokens, n_req, N_EXPERTS).transpose(1, 0, 2)
    blk = blk[:, :n_col].astype(I32).reshape(2, n_blk, n_req, N_EXPERTS).transpose(0, 2, 1, 3)
    return (posm.reshape(n_req * tokens, N_EXPERTS), posr[:n_col].reshape(n_req, N_EXPERTS, tokens),
            blk[0], blk[1])


def _slot_gate(r, expert):
    lane = lax.broadcasted_iota(I32, r.shape, 1)
    pick = ((lane & (N_EXPERTS - 1)) == expert) & (lane < 3 * N_EXPERTS)
    g = jnp.sum(jnp.where(pick, r, 0.0), axis=-1, keepdims=True)
    return jnp.broadcast_to(g, r.shape)


def _gather_prompt_kernel(x_ref, posr_ref, affp_ref, xg_ref, g_ref):
    slot = lax.broadcasted_iota(I32, (CAP_P, SEQ), 0).astype(F32)
    for r in range(REQ_TILE):
        tokens = slice(r * SEQ, (r + 1) * SEQ)
        posr = posr_ref[r]
        onehot = jnp.concatenate([jnp.where(posr[e:e + 1, :] == slot, 1.0, 0.0) for e in range(N_EXPERTS)],
                                 axis=0).astype(BF16)
        xg = _dot(onehot, x_ref[tokens, :])
        pieces = _dot(onehot, affp_ref[tokens, :])
        for e in range(N_EXPERTS):
            rows = slice(e * CAP_P, (e + 1) * CAP_P)
            slots = slice(r * CAP_P, (r + 1) * CAP_P)
            xg_ref[e, slots, :] = xg[rows, :].astype(xg_ref.dtype)
            g_ref[e, slots, :] = _slot_gate(pieces[rows, :], e)


def _gather_prompt(xm, posr, affp):
    return pl.pallas_call(
        _gather_prompt_kernel,
        grid=(BATCH // REQ_TILE,),
        in_specs=[pl.BlockSpec((REQ_TILE * SEQ, D_MODEL), lambda b: (b, 0)),
                  pl.BlockSpec((REQ_TILE, N_EXPERTS, SEQ), lambda b: (b, 0, 0)),
                  pl.BlockSpec((REQ_TILE * SEQ, LANES), lambda b: (b, 0))],
        out_specs=[pl.BlockSpec((N_EXPERTS, REQ_TILE * CAP_P, D_MODEL), lambda b: (0, b, 0)),
                   pl.BlockSpec((N_EXPERTS, REQ_TILE * CAP_P, LANES), lambda b: (0, b, 0))],
        out_shape=[jax.ShapeDtypeStruct((N_EXPERTS, SLOTS_P, D_MODEL), BF16),
                   jax.ShapeDtypeStruct((N_EXPERTS, SLOTS_P, LANES), F32)],
        compiler_params=_params(1),
        name="gather_prompt",
    )(xm, posr, affp)


TOKEN_BLOCK = PREFIX_BLOCK
N_TOKEN_BLOCKS = DEC_SEQ // TOKEN_BLOCK
SLOT_ALIGN = 16
WINDOW_SLOTS = 80
LAST_WINDOW = CAP_S - WINDOW_SLOTS
ALL_WINDOWS = N_EXPERTS * WINDOW_SLOTS


def _window_plan(before, inside):
    start = (before // SLOT_ALIGN) * SLOT_ALIGN
    span = before - start + inside
    passes = jnp.where(inside > 0, (span + WINDOW_SLOTS - 1) // WINDOW_SLOTS, 0)
    return start.reshape(-1), jnp.max(passes, axis=-1).reshape(-1)


def _window_bounds(start_ref, step, expert, k):
    lower = start_ref[step * N_EXPERTS + expert] + WINDOW_SLOTS * k
    begin = pl.multiple_of(jnp.minimum(lower, LAST_WINDOW), SLOT_ALIGN)
    return lower, begin


def _gather_sample_kernel(start_ref, passes_ref, x_ref, posr_ref, affp_ref, xg_ref, g_ref):
    t = pl.program_id(1)
    step = pl.program_id(0) * N_TOKEN_BLOCKS + t

    @pl.when(t == 0)
    def _():
        xg_ref[...] = jnp.zeros_like(xg_ref)
        g_ref[...] = jnp.zeros_like(g_ref)

    posr = posr_ref[...]
    x = x_ref[...]
    affp = affp_ref[...]
    row = lax.broadcasted_iota(I32, (WINDOW_SLOTS, TOKEN_BLOCK), 0).astype(F32)

    def one_pass(k, carry):
        pieces, begins = [], []
        for e in range(N_EXPERTS):
            lower, begin = _window_bounds(start_ref, step, e, k)
            pe = posr[e:e + 1, :]
            hit = (pe - begin.astype(F32) == row) & (pe >= lower.astype(F32))
            pieces.append(jnp.where(hit, 1.0, 0.0))
            begins.append(begin)
        onehot = jnp.concatenate(pieces, axis=0).astype(BF16)
        rows_x = _dot(onehot, x)
        rows_g = _dot(onehot, affp)
        for e in range(N_EXPERTS):
            win = pl.ds(begins[e], WINDOW_SLOTS)
            src = slice(e * WINDOW_SLOTS, (e + 1) * WINDOW_SLOTS)
            xg_ref[e, win, :] += rows_x[src, :].astype(xg_ref.dtype)
            g_ref[e, win, :] += rows_g[src, :]
        return carry

    lax.fori_loop(0, passes_ref[step], one_pass, 0)

    @pl.when(t == N_TOKEN_BLOCKS - 1)
    def _():
        for e in range(N_EXPERTS):
            g_ref[e] = _slot_gate(g_ref[e], e)


def _gather_sample(xm, posr, affp, starts, passes):
    grid_spec = pltpu.PrefetchScalarGridSpec(
        num_scalar_prefetch=2,
        grid=(DEC_BATCH, N_TOKEN_BLOCKS),
        in_specs=[pl.BlockSpec((TOKEN_BLOCK, D_MODEL), lambda b, t, s, p: (b * N_TOKEN_BLOCKS + t, 0)),
                  pl.BlockSpec((None, N_EXPERTS, TOKEN_BLOCK), lambda b, t, s, p: (b, 0, t)),
                  pl.BlockSpec((TOKEN_BLOCK, LANES), lambda b, t, s, p: (b * N_TOKEN_BLOCKS + t, 0))],
        out_specs=[pl.BlockSpec((N_EXPERTS, CAP_S, D_MODEL), lambda b, t, s, p: (0, b, 0)),
                   pl.BlockSpec((N_EXPERTS, CAP_S, LANES), lambda b, t, s, p: (0, b, 0))],
    )
    return pl.pallas_call(
        _gather_sample_kernel,
        grid_spec=grid_spec,
        out_shape=[jax.ShapeDtypeStruct((N_EXPERTS, SLOTS_S, D_MODEL), BF16),
                   jax.ShapeDtypeStruct((N_EXPERTS, SLOTS_S, LANES), F32)],
        compiler_params=_params(2),
        name="gather_sample",
    )(starts, passes, xm, posr, affp)


FF_TILE = 512
SLOT_CHUNK = 512


def _ffn_kernel(xp_ref, xs_ref, gp_ref, gs_ref, w1_ref, w3_ref, w2_ref, yp_ref, ys_ref, acc_ref):
    f = pl.program_id(1)
    n_f = pl.num_programs(1)
    @pl.when(f == 0)
    def _():
        acc_ref[...] = jnp.zeros_like(acc_ref)

    w1 = w1_ref[...].astype(BF16)
    w3 = w3_ref[...].astype(BF16)
    w2 = w2_ref[...].astype(BF16)

    halves = ((xp_ref, SLOTS_P, 0), (xs_ref, SLOTS_S, SLOTS_P))
    for x_ref, n_slot, base in halves:
        for c in range(n_slot // SLOT_CHUNK):
            x = x_ref[c * SLOT_CHUNK:(c + 1) * SLOT_CHUNK, :]
            h1 = _dot(x, w1)
            h3 = _dot(x, w3)
            hid = (h1 * jax.nn.sigmoid(h1) * h3).astype(BF16)
            rows = slice(base + c * SLOT_CHUNK, base + (c + 1) * SLOT_CHUNK)
            acc_ref[rows, :] += _dot(hid, w2)

    @pl.when(f == n_f - 1)
    def _():
        for y_ref, g_ref, n_slot, base in ((yp_ref, gp_ref, SLOTS_P, 0), (ys_ref, gs_ref, SLOTS_S, SLOTS_P)):
            for c in range(n_slot // SLOT_CHUNK):
                rows = slice(c * SLOT_CHUNK, (c + 1) * SLOT_CHUNK)
                gate = jnp.concatenate([g_ref[rows, :]] * (D_MODEL // LANES), axis=1)
                y_ref[rows, :] = (acc_ref[base + c * SLOT_CHUNK:base + (c + 1) * SLOT_CHUNK, :]
                                  * gate).astype(y_ref.dtype)


def _expert_ffn(xgp, xgs, gp, gs, w1, w3, w2, layer):
    slots = lambda n, width: pl.BlockSpec((None, n, width), lambda e, f: (e, 0, 0))
    return pl.pallas_call(
        _ffn_kernel,
        grid=(N_EXPERTS, EXPERT_FF // FF_TILE),
        in_specs=[slots(SLOTS_P, D_MODEL), slots(SLOTS_S, D_MODEL), slots(SLOTS_P, LANES), slots(SLOTS_S, LANES),
                  pl.BlockSpec((None, None, D_MODEL, FF_TILE), lambda e, f: (layer, e, 0, f)),
                  pl.BlockSpec((None, None, D_MODEL, FF_TILE), lambda e, f: (layer, e, 0, f)),
                  pl.BlockSpec((None, None, FF_TILE, D_MODEL), lambda e, f: (layer, e, f, 0))],
        out_specs=[slots(SLOTS_P, D_MODEL), slots(SLOTS_S, D_MODEL)],
        out_shape=[jax.ShapeDtypeStruct((N_EXPERTS, SLOTS_P, D_MODEL), BF16),
                   jax.ShapeDtypeStruct((N_EXPERTS, SLOTS_S, D_MODEL), BF16)],
        scratch_shapes=[pltpu.VMEM((SLOTS_P + SLOTS_S, D_MODEL), F32)],
        compiler_params=_params(2),
        name="expert_ffn",
    )(xgp, xgs, gp, gs, w1, w3, w2)


def _combine_prompt_kernel(posm_ref, y_ref, x_ref, mod_ref, exp_ref, lg_ref, lb_ref, o_ref):
    m = mod_ref[...]
    slot = (lax.broadcasted_iota(I32, (SEQ, N_EXPERTS * CAP_P), 1) & (CAP_P - 1)).astype(F32)
    for r in range(REQ_TILE):
        tokens = slice(r * SEQ, (r + 1) * SEQ)
        spread = _dot(posm_ref[tokens, :].astype(BF16), exp_ref[...])
        onehot = jnp.where(spread == slot, 1.0, 0.0).astype(BF16)
        y = jnp.concatenate([y_ref[e, r * CAP_P:(r + 1) * CAP_P, :] for e in range(N_EXPERTS)], axis=0)
        f = _dot(onehot, y)
        o_ref[tokens, :] = _layer_norm(ALPHA * x_ref[tokens, :] + m[5:6] * f, lg_ref[...], lb_ref[...])


def _combine_prompt(posm, y, x1, mod_l, expand, ln_g, ln_b):
    full = lambda a: pl.BlockSpec(a.shape, lambda b: (0,) * a.ndim)
    return pl.pallas_call(
        _combine_prompt_kernel,
        grid=(BATCH // REQ_TILE,),
        in_specs=[pl.BlockSpec((REQ_TILE * SEQ, N_EXPERTS), lambda b: (b, 0)),
                  pl.BlockSpec((N_EXPERTS, REQ_TILE * CAP_P, D_MODEL), lambda b: (0, b, 0)),
                  pl.BlockSpec((REQ_TILE * SEQ, D_MODEL), lambda b: (b, 0)),
                  pl.BlockSpec((None, 6, D_MODEL), lambda b: (0, 0, 0)),
                  full(expand), full(ln_g), full(ln_b)],
        out_specs=pl.BlockSpec((REQ_TILE * SEQ, D_MODEL), lambda b: (b, 0)),
        out_shape=jax.ShapeDtypeStruct((N_PROMPT, D_MODEL), F32),
        compiler_params=_params(1),
        name="combine_prompt",
    )(posm, y, x1, mod_l, expand, ln_g, ln_b)


def _combine_sample_kernel(start_ref, passes_ref, posm_ref, y_ref, x_ref, mod_ref, exp_ref, off_ref,
                           lg_ref, lb_ref, o_ref):
    step = pl.program_id(0) * N_TOKEN_BLOCKS + pl.program_id(1)
    exp_bf16 = exp_ref[...]
    spread = _dot(posm_ref[...].astype(BF16), exp_bf16)
    offset = off_ref[...]

    def one_pass(k, f):
        begin_row = jnp.zeros((1, ALL_WINDOWS), F32)
        lower_row = jnp.zeros((1, ALL_WINDOWS), F32)
        windows = []
        for e in range(N_EXPERTS):
            lower, begin = _window_bounds(start_ref, step, e, k)
            lanes_e = exp_bf16[e:e + 1, :].astype(F32)
            begin_row = begin_row + begin.astype(F32) * lanes_e
            lower_row = lower_row + lower.astype(F32) * lanes_e
            windows.append(y_ref[e, pl.ds(begin, WINDOW_SLOTS), :])
        hit = (spread - begin_row == offset) & (spread >= lower_row)
        onehot = jnp.where(hit, 1.0, 0.0).astype(BF16)
        return f + _dot(onehot, jnp.concatenate(windows, axis=0))

    f = lax.fori_loop(0, passes_ref[step], one_pass, jnp.zeros((TOKEN_BLOCK, D_MODEL), F32))
    m = mod_ref[...]
    o_ref[...] = _layer_norm(ALPHA * x_ref[...] + m[5:6] * f, lg_ref[...], lb_ref[...])


def _combine_sample(posm, y, x1, mod_l, ln_g, ln_b, starts, passes):
    lane = jnp.arange(ALL_WINDOWS)
    expand = ((lane[None, :] // WINDOW_SLOTS) == jnp.arange(N_EXPERTS)[:, None]).astype(BF16)
    offset = (lane % WINDOW_SLOTS).astype(F32).reshape(1, ALL_WINDOWS)
    row = lambda width: pl.BlockSpec((TOKEN_BLOCK, width), lambda b, t, s, p: (b * N_TOKEN_BLOCKS + t, 0))
    full = lambda a: pl.BlockSpec(a.shape, lambda b, t, s, p: (0,) * a.ndim)
    grid_spec = pltpu.PrefetchScalarGridSpec(
        num_scalar_prefetch=2,
        grid=(DEC_BATCH, N_TOKEN_BLOCKS),
        in_specs=[row(N_EXPERTS),
                  pl.BlockSpec((N_EXPERTS, CAP_S, D_MODEL), lambda b, t, s, p: (0, b, 0)),
                  row(D_MODEL),
                  pl.BlockSpec((None, 6, D_MODEL), lambda b, t, s, p: (1 + b, 0, 0)),
                  full(expand), full(offset), full(ln_g), full(ln_b)],
        out_specs=row(D_MODEL),
    )
    return pl.pallas_call(
        _combine_sample_kernel,
        grid_spec=grid_spec,
        out_shape=jax.ShapeDtypeStruct((N_SAMPLE, D_MODEL), F32),
        compiler_params=_params(2),
        name="combine_sample",
    )(starts, passes, posm, y, x1, mod_l, expand, offset, ln_g, ln_b)


def _moe(prompt, sample, mod_l, w1, w3, w2, layer, ln_g, ln_b, tri, expand):
    x1_p, xm_p, aff_p, affp_p = prompt
    x1_s, xm_s, aff_s, affp_s = sample
    posm_p, posr_p, _, _ = _route_requests(aff_p, tri, BATCH, SEQ, CAP_P)
    posm_s, posr_s, before_s, inside_s = _route_requests(aff_s, tri, DEC_BATCH, DEC_SEQ, CAP_S)
    starts, passes = _window_plan(before_s, inside_s)
    xg_p, g_p = _gather_prompt(xm_p, posr_p, affp_p)
    xg_s, g_s = _gather_sample(xm_s, posr_s, affp_s, starts, passes)
    y_p, y_s = _expert_ffn(xg_p, xg_s, g_p, g_s, w1, w3, w2, layer)
    x2_p = _combine_prompt(posm_p, y_p, x1_p, mod_l, expand, ln_g, ln_b)
    x2_s = _combine_sample(posm_s, y_s, x1_s, mod_l, ln_g, ln_b, starts, passes)
    return x2_p, x2_s


def kernel(x_prompt, x_sample, cache_k, cache_v, state_rglru, c, c_ctx, mod_w, mod_b, ln_mix_g, ln_mix_b, ln_ffn_g, ln_ffn_b, ab_in_w, attn_sink, rnn_conv_w, rnn_conv_b, lru_wa, lru_ba, lru_wx, lru_bx, lru_lambda, ab_out_w, sgu_in_w, sgu_in_b, sgu_ln_g, sgu_ln_b, sgu_spatial_w, sgu_spatial_b, sgu_out_w, router_w, moe_w1, moe_w3, moe_w2):
    xp = x_prompt.reshape(N_PROMPT, D_MODEL)
    xs = x_sample.reshape(N_SAMPLE, D_MODEL)

    cvec = jnp.concatenate([c_ctx[None], c, jnp.zeros((SUBLANES - 1 - DEC_BATCH, D_MODEL), F32)], axis=0)
    mod = _modulation(cvec, mod_w, mod_b).reshape(DEPTH, SUBLANES, 6, D_MODEL)

    idx = jnp.arange(PREFIX_BLOCK)
    tri = (idx[None, :] < idx[:, None]).astype(BF16)
    lane = jnp.arange(N_EXPERTS * CAP_P)
    expand = ((lane[None, :] // CAP_P) == jnp.arange(N_EXPERTS)[:, None]).astype(BF16)
    rope_tables = _rope_tables()
    vec = lambda a: a.reshape(1, -1)
    router_pad = lambda w: jnp.pad(w, ((0, 0), (0, LANES - N_EXPERTS))).astype(BF16)

    new_k = new_v = new_state = None
    for l in range(DEPTH):
        e = l // 2
        mod_l = mod[l]
        rw = router_pad(router_w[l])
        lg, lb = vec(ln_mix_g[l]), vec(ln_mix_b[l])
        if l % 2 == 0:
            w_in = ab_in_w[e].astype(BF16)
            w_out = ab_out_w[e].astype(BF16)
            wa = _block_diag_dense(lru_wa[e])
            wx = _block_diag_dense(lru_wx[e])
            rnn_w = (rnn_conv_w[e], vec(rnn_conv_b[e]), wa, lru_ba[e], wx, lru_bx[e], lru_lambda[e])

            q, k, v, xr, xg = _ab_in(xp, mod_l, w_in, 0, N_PROMPT, None, F32)
            new_k, new_v = k, v
            att = _ctx_attention(q, k, v, attn_sink[e])
            rows = 8 * SEQ
            zeros = jnp.zeros((N_PROMPT // rows, rows // SEQ, RNN_WIDTH), F32)
            rnn, hf_last, hb_first = _rglru(xr, xg, *rnn_w, zeros, zeros, rows, SEQ)
            new_state = jnp.stack([hf_last.reshape(BATCH, RNN_WIDTH), hb_first.reshape(BATCH, RNN_WIDTH)], axis=1)
            prompt = _mixer_out(_mix_out_kernel, "mix_out", (att, rnn), xp, mod_l, (w_out,),
                                lg, lb, rw, 0, N_PROMPT)

            q, k, v, xr, xg = _ab_in(xs, mod_l, w_in, 1, DEC_SEQ, rope_tables, BF16)
            att = _lat_attention(q, k, v, cache_k[:, e].reshape(DEC_BATCH * PAST_LEN, KV_WIDTH),
                                 cache_v[:, e].reshape(DEC_BATCH * PAST_LEN, KV_WIDTH), attn_sink[e])
            h0 = state_rglru[:, e]
            rnn, _, _ = _rglru(xr, xg, *rnn_w, h0[:, 0][:, None, :], h0[:, 1][:, None, :], DEC_SEQ, DEC_SEQ)
            sample = _mixer_out(_mix_out_kernel, "mix_out", (att, rnn), xs, mod_l, (w_out,),
                                lg, lb, rw, 1, DEC_SEQ)
        else:
            w_in = sgu_in_w[e].astype(BF16)
            w_out = sgu_out_w[e].astype(BF16)
            spw = sgu_spatial_w[e].astype(BF16)
            spb = jnp.repeat(sgu_spatial_b[e].T, SGU_GROUP_W, axis=1)
            sgu_args = (w_in, vec(sgu_in_b[e]), vec(sgu_ln_g[e]), vec(sgu_ln_b[e]))
            scratch = (pltpu.VMEM((ROW_TILE, SGU_WIDTH), F32),)
            u, vn = _sgu_in(xp, mod_l, *sgu_args, 0, N_PROMPT)
            prompt = _mixer_out(_sgu_out_kernel, "sgu_out", (u, vn), xp, mod_l, (spw, spb, w_out),
                                lg, lb, rw, 0, N_PROMPT, scratch)
            u, vn = _sgu_in(xs, mod_l, *sgu_args, 1, DEC_SEQ)
            sample = _mixer_out(_sgu_out_kernel, "sgu_out", (u, vn), xs, mod_l, (spw, spb, w_out),
                                lg, lb, rw, 1, DEC_SEQ, scratch)

        xp, xs = _moe(prompt, sample, mod_l, moe_w1, moe_w3, moe_w2, l,
                      vec(ln_ffn_g[l]), vec(ln_ffn_b[l]), tri, expand)

    return (xp.reshape(BATCH, SEQ, D_MODEL),
            xs.reshape(DEC_BATCH, DEC_SEQ, D_MODEL),
            new_k.reshape(BATCH, 1, SEQ, N_KV_HEADS, HEAD_DIM),
            new_v.reshape(BATCH, 1, SEQ, N_KV_HEADS, HEAD_DIM),
            new_state.reshape(BATCH, 1, 2, RNN_WIDTH))
```

```python
import functools

import jax
import jax.numpy as jnp
from jax import lax
from jax.experimental import pallas as pl
from jax.experimental.pallas import tpu as pltpu

F32 = jnp.float32
BF16 = jnp.bfloat16
I32 = jnp.int32

D_MODEL = 1024
BATCH = 32
SEQ = 256
DEPTH = 2
DEC_BATCH = 4
DEC_SEQ = 2048
PAST_LEN = 256
GRID_W = 64
HEAD_DIM = 128
N_Q_HEADS = 4
N_KV_HEADS = 2
Q_PER_KV = N_Q_HEADS // N_KV_HEADS
ATTN_WIDTH = N_Q_HEADS * HEAD_DIM
KV_WIDTH = N_KV_HEADS * HEAD_DIM
WINDOW = 128
BLOCK = 128
ATTN_SCALE = HEAD_DIM ** -0.5
ROPE_BASE = 10000.0
NEG_INF = -1e30
RNN_WIDTH = 512
RNN_BLOCKS = 8
RNN_BLOCK_W = RNN_WIDTH // RNN_BLOCKS
CONV_W = 4
CONV_PAD_LEFT = 2
LRU_C = 8.0
AB_IN_WIDTH = ATTN_WIDTH + 2 * KV_WIDTH + 2 * RNN_WIDTH
CHUNK = 128
SGU_WIDTH = D_MODEL
SGU_GROUPS = 8
SGU_GROUP_W = SGU_WIDTH // SGU_GROUPS
N_EXPERTS = 16
EXPERT_FF = 2048
EC_FACTOR = 2
ALPHA = (2 * DEPTH) ** 0.25
LN_EPS = 1e-6

N_PROMPT = BATCH * SEQ
N_SAMPLE = DEC_BATCH * DEC_SEQ
CAP_P = EC_FACTOR * SEQ // N_EXPERTS
CAP_S = EC_FACTOR * DEC_SEQ // N_EXPERTS
SLOTS_P = BATCH * CAP_P
SLOTS_S = DEC_BATCH * CAP_S

LANES = 128
SUBLANES = 8
ROW_TILE = 1024
REQ_TILE = 4
PREFIX_BLOCK = 256
VMEM_LIMIT = 56 * 1024 * 1024


def _params(n_axes=1):
    return pltpu.CompilerParams(dimension_semantics=("arbitrary",) * n_axes,
                                vmem_limit_bytes=VMEM_LIMIT)


def _layer_norm(x, g, b):
    mu = jnp.mean(x, axis=-1, keepdims=True)
    xc = x - mu
    var = jnp.mean(xc * xc, axis=-1, keepdims=True)
    return xc * lax.rsqrt(var + LN_EPS) * g + b


def _gelu_tanh(x):
    return 0.5 * x * (1.0 + jnp.tanh(0.7978845608028654 * (x + 0.044715 * (x * x * x))))


def _dot(a, b):
    return jnp.dot(a, b, preferred_element_type=F32)


def _dot_nt(a, b):
    return lax.dot_general(a, b, (((1,), (1,)), ((), ())), preferred_element_type=F32)


def _mod_kernel(c_ref, w_ref, b_ref, o_ref):
    c = c_ref[...]
    s = c * jax.nn.sigmoid(c)
    o_ref[...] = _dot(s.astype(BF16), w_ref[...].astype(BF16)) + b_ref[...]


def _modulation(cvec8, mod_w, mod_b):
    n_col = 6 * D_MODEL // D_MODEL
    return pl.pallas_call(
        _mod_kernel,
        grid=(DEPTH, n_col),
        in_specs=[pl.BlockSpec((SUBLANES, D_MODEL), lambda l, j: (0, 0)),
                  pl.BlockSpec((None, D_MODEL, D_MODEL), lambda l, j: (l, 0, j)),
                  pl.BlockSpec((None, 1, D_MODEL), lambda l, j: (l, 0, j))],
        out_specs=pl.BlockSpec((None, SUBLANES, D_MODEL), lambda l, j: (l, 0, j)),
        out_shape=jax.ShapeDtypeStruct((DEPTH, SUBLANES, 6 * D_MODEL), F32),
        compiler_params=_params(2),
        name="adaln_modulation",
    )(cvec8, mod_w, mod_b.reshape(DEPTH, 1, 6 * D_MODEL))


def _group_map(group0, rows_per_group):
    tiles_per_group = rows_per_group // ROW_TILE
    return lambda i: (group0 + i // tiles_per_group, 0, 0)


def _rope(t, cos, sin_signed):
    lane = lax.broadcasted_iota(I32, t.shape, 1)
    swapped = jnp.where((lane & 1) == 0, pltpu.roll(t, HEAD_DIM - 1, 1), pltpu.roll(t, 1, 1))
    return t * cos + swapped * sin_signed


def _ab_in_kernel(*refs, rope):
    if rope:
        x_ref, mod_ref, w_ref, cos_ref, sin_ref, q_ref, k_ref, v_ref, xr_ref, xg_ref = refs
    else:
        x_ref, mod_ref, w_ref, q_ref, k_ref, v_ref, xr_ref, xg_ref = refs
    m = mod_ref[...]
    h = x_ref[...] * (1.0 + m[1:2]) + m[0:1]
    p = _dot(h.astype(BF16), w_ref[...])
    q = p[:, :ATTN_WIDTH]
    k = p[:, ATTN_WIDTH:ATTN_WIDTH + KV_WIDTH]
    v = p[:, ATTN_WIDTH + KV_WIDTH:ATTN_WIDTH + 2 * KV_WIDTH]
    if rope:
        cos = cos_ref[...]
        sin = sin_ref[...]
        q = jnp.concatenate([_rope(q[:, i * HEAD_DIM:(i + 1) * HEAD_DIM], cos, sin)
                             for i in range(N_Q_HEADS)], axis=1)
        k = jnp.concatenate([_rope(k[:, i * HEAD_DIM:(i + 1) * HEAD_DIM], cos, sin)
                             for i in range(N_KV_HEADS)], axis=1)
    q_ref[...] = q.astype(q_ref.dtype)
    k_ref[...] = k.astype(k_ref.dtype)
    v_ref[...] = v.astype(v_ref.dtype)
    xr_ref[...] = p[:, ATTN_WIDTH + 2 * KV_WIDTH:ATTN_WIDTH + 2 * KV_WIDTH + RNN_WIDTH]
    xg_ref[...] = p[:, ATTN_WIDTH + 2 * KV_WIDTH + RNN_WIDTH:]


def _ab_in(x, mod_l, w_bf16, group0, rows_per_group, rope_tables, kv_dtype):
    n = x.shape[0]
    rope = rope_tables is not None
    row = lambda width: pl.BlockSpec((ROW_TILE, width), lambda i: (i, 0))
    in_specs = [row(D_MODEL),
                pl.BlockSpec((None, 6, D_MODEL), _group_map(group0, rows_per_group)),
                pl.BlockSpec((D_MODEL, AB_IN_WIDTH), lambda i: (0, 0))]
    args = [x, mod_l, w_bf16]
    if rope:
        tiles_per_seq = DEC_SEQ // ROW_TILE
        in_specs += [pl.BlockSpec((ROW_TILE, HEAD_DIM), lambda i: (i % tiles_per_seq, 0))] * 2
        args += list(rope_tables)
    return pl.pallas_call(
        functools.partial(_ab_in_kernel, rope=rope),
        grid=(n // ROW_TILE,),
        in_specs=in_specs,
        out_specs=[row(ATTN_WIDTH), row(KV_WIDTH), row(KV_WIDTH), row(RNN_WIDTH), row(RNN_WIDTH)],
        out_shape=[jax.ShapeDtypeStruct((n, ATTN_WIDTH), BF16),
                   jax.ShapeDtypeStruct((n, KV_WIDTH), kv_dtype),
                   jax.ShapeDtypeStruct((n, KV_WIDTH), kv_dtype),
                   jax.ShapeDtypeStruct((n, RNN_WIDTH), F32),
                   jax.ShapeDtypeStruct((n, RNN_WIDTH), F32)],
        compiler_params=_params(1),
        name="ab_in_rope" if rope else "ab_in",
    )(*args)


def _rope_tables():
    rows = DEC_SEQ // GRID_W
    row = jnp.repeat(jnp.arange(rows, dtype=F32), GRID_W)
    col = jnp.tile(jnp.arange(GRID_W, dtype=F32), rows)
    n_freq = HEAD_DIM // 4
    freqs = ROPE_BASE ** (-jnp.arange(n_freq, dtype=F32) / n_freq)
    ang = jnp.concatenate([row[:, None] * freqs, col[:, None] * freqs], axis=-1)
    cos = jnp.repeat(jnp.cos(ang), 2, axis=-1)
    sin = jnp.repeat(jnp.sin(ang), 2, axis=-1)
    sign = jnp.tile(jnp.array([-1.0, 1.0], F32), HEAD_DIM // 2)
    return cos, sin * sign


def _sink_attention_head(s_list, v_list, sink):
    m = sink
    for s in s_list:
        m = jnp.maximum(m, jnp.max(s, axis=-1, keepdims=True))
    p_list = [jnp.exp(s - m) for s in s_list]
    denom = jnp.exp(sink - m)
    for p in p_list:
        denom = denom + jnp.sum(p, axis=-1, keepdims=True)
    out = None
    for p, v in zip(p_list, v_list):
        o = _dot(p.astype(BF16), v)
        out = o if out is None else out + o
    return out * (1.0 / denom)


def _ctx_attn_kernel(sink_ref, q_ref, k_ref, v_ref, o_ref):
    for r in range(REQ_TILE):
        rows = slice(r * SEQ, (r + 1) * SEQ)
        q = q_ref[rows, :]
        k = k_ref[rows, :].astype(BF16)
        v = v_ref[rows, :].astype(BF16)
        outs = []
        for h in range(N_Q_HEADS):
            kv = h // Q_PER_KV
            qh = q[:, h * HEAD_DIM:(h + 1) * HEAD_DIM]
            kh = k[:, kv * HEAD_DIM:(kv + 1) * HEAD_DIM]
            vh = v[:, kv * HEAD_DIM:(kv + 1) * HEAD_DIM]
            s = _dot_nt(qh, kh) * ATTN_SCALE
            outs.append(_sink_attention_head([s], [vh], sink_ref[h]))
        o_ref[rows, :] = jnp.concatenate(outs, axis=1).astype(o_ref.dtype)


def _ctx_attention(q, k, v, sink):
    seq = lambda width: pl.BlockSpec((REQ_TILE * SEQ, width), lambda b: (b, 0))
    return pl.pallas_call(
        _ctx_attn_kernel,
        grid=(BATCH // REQ_TILE,),
        in_specs=[pl.BlockSpec(memory_space=pltpu.SMEM), seq(ATTN_WIDTH), seq(KV_WIDTH), seq(KV_WIDTH)],
        out_specs=seq(ATTN_WIDTH),
        out_shape=jax.ShapeDtypeStruct((N_PROMPT, ATTN_WIDTH), BF16),
        compiler_params=_params(1),
        name="context_attention",
    )(sink, q, k, v)


LAT_Q = 256


def _lat_attn_kernel(sink_ref, q_ref, kp_ref, kc_ref, kn_ref, vp_ref, vc_ref, vn_ref,
                     kx_ref, vx_ref, o_ref):
    n = pl.program_id(1)
    nb = pl.num_programs(1)
    q = q_ref[...]
    kw = jnp.concatenate([kp_ref[...], kc_ref[...], kn_ref[...]], axis=0)
    vw = jnp.concatenate([vp_ref[...], vc_ref[...], vn_ref[...]], axis=0)
    kx = kx_ref[...].astype(BF16)
    vx = vx_ref[...].astype(BF16)
    n_key = LAT_Q + 2 * WINDOW
    qi = lax.broadcasted_iota(I32, (LAT_Q, n_key), 0)
    kj = lax.broadcasted_iota(I32, (LAT_Q, n_key), 1)
    rel = kj - qi
    valid = (rel >= 0) & (rel <= 2 * WINDOW)
    valid = valid & ((n > 0) | (kj >= WINDOW)) & ((n < nb - 1) | (kj < LAT_Q + WINDOW))
    outs = []
    for h in range(N_Q_HEADS):
        kv = h // Q_PER_KV
        sl = slice(kv * HEAD_DIM, (kv + 1) * HEAD_DIM)
        qh = q[:, h * HEAD_DIM:(h + 1) * HEAD_DIM]
        s_ctx = _dot_nt(qh, kx[:, sl]) * ATTN_SCALE
        s_win = jnp.where(valid, _dot_nt(qh, kw[:, sl]) * ATTN_SCALE, NEG_INF)
        outs.append(_sink_attention_head([s_ctx, s_win], [vx[:, sl], vw[:, sl]], sink_ref[h]))
    o_ref[...] = jnp.concatenate(outs, axis=1).astype(o_ref.dtype)


def _lat_attention(q, k, v, k_ctx, v_ctx, sink):
    nb = DEC_SEQ // LAT_Q
    nw = DEC_SEQ // WINDOW
    per = LAT_Q // WINDOW
    cur = lambda b, n: (b * nb + n, 0)
    prev = lambda b, n: (b * nw + jnp.maximum(n * per - 1, 0), 0)
    nxt = lambda b, n: (b * nw + jnp.minimum((n + 1) * per, nw - 1), 0)
    tile = lambda width: pl.BlockSpec((LAT_Q, width), cur)
    edge = lambda imap: pl.BlockSpec((WINDOW, KV_WIDTH), imap)
    ctx = pl.BlockSpec((PAST_LEN, KV_WIDTH), lambda b, n: (b, 0))
    return pl.pallas_call(
        _lat_attn_kernel,
        grid=(DEC_BATCH, nb),
        in_specs=[pl.BlockSpec(memory_space=pltpu.SMEM), tile(ATTN_WIDTH),
                  edge(prev), tile(KV_WIDTH), edge(nxt),
                  edge(prev), tile(KV_WIDTH), edge(nxt), ctx, ctx],
        out_specs=tile(ATTN_WIDTH),
        out_shape=jax.ShapeDtypeStruct((N_SAMPLE, ATTN_WIDTH), BF16),
        compiler_params=_params(2),
        name="latent_attention",
    )(sink, q, k, k, k, v, v, v, k_ctx, v_ctx)


RNN_CHUNK = 256


def _rglru_kernel(xr_ref, xg_ref, cw_ref, cb_ref, wa_ref, ba_ref, wx_ref, bx_ref, lam_ref,
                  h0f_ref, h0b_ref, y_ref, hfl_ref, hbf_ref, xc_s, a_s, uf_s, ub_s, *, rows, seq_len):
    n_seq = rows // seq_len
    n_chunk = rows // RNN_CHUNK
    cw = cw_ref[...]
    cb = cb_ref[...]
    zeros_halo = jnp.zeros((SUBLANES, RNN_WIDTH), F32)
    row8 = lax.broadcasted_iota(I32, (SUBLANES, RNN_WIDTH), 0)

    def conv_chunk(c):
        r0 = c * RNN_CHUNK
        first = r0 % seq_len == 0
        last = (r0 + RNN_CHUNK) % seq_len == 0
        before = zeros_halo if first else xr_ref[r0 - SUBLANES:r0, :]
        after = zeros_halo if last else xr_ref[r0 + RNN_CHUNK:r0 + RNN_CHUNK + SUBLANES, :]
        win = jnp.concatenate([before, xr_ref[r0:r0 + RNN_CHUNK, :], after], axis=0)
        xc = cb
        n_win = RNN_CHUNK + 2 * SUBLANES
        for i in range(CONV_W):
            shift = (CONV_PAD_LEFT - i) % n_win
            rolled = win if shift == 0 else pltpu.roll(win, shift, 0)
            xc = xc + rolled[SUBLANES:SUBLANES + RNN_CHUNK, :] * cw[i:i + 1, :]
        return xc

    def group_scan(a, u, reverse):
        for k in (1, 2, 4):
            if reverse:
                shift, ok = SUBLANES - k, row8 < SUBLANES - k
            else:
                shift, ok = k, row8 >= k
            a_nb = jnp.where(ok, pltpu.roll(a, shift, 0), 1.0)
            u_nb = jnp.where(ok, pltpu.roll(u, shift, 0), 0.0)
            u = a * u_nb + u
            a = a * a_nb
        return a, u

    for c in range(n_chunk):
        xc_s[c * RNN_CHUNK:(c + 1) * RNN_CHUNK, :] = conv_chunk(c)

    finals = []
    for d, (u_s, h0_ref) in enumerate(((uf_s, h0f_ref), (ub_s, h0b_ref))):
        reverse = d == 1
        neg = -lam_ref[d:d + 1, :]
        softplus = jnp.maximum(neg, 0.0) + jnp.log1p(jnp.exp(-jnp.abs(neg)))
        decay = -LRU_C * softplus
        wa = wa_ref[d]
        wx = wx_ref[d]
        ba = ba_ref[d:d + 1, :]
        bx = bx_ref[d:d + 1, :]
        for c in range(n_chunk):
            xc = xc_s[c * RNN_CHUNK:(c + 1) * RNN_CHUNK, :]
            xcb = xc.astype(BF16)
            r = jax.nn.sigmoid(_dot(xcb, wa) + ba)
            i = jax.nn.sigmoid(_dot(xcb, wx) + bx)
            log_a = r * decay
            a = jnp.exp(log_a)
            a_s[c * RNN_CHUNK:(c + 1) * RNN_CHUNK, :] = a
            one_minus_a2 = -jnp.tanh(log_a) * (a * a + 1.0)
            u_s[c * RNN_CHUNK:(c + 1) * RNN_CHUNK, :] = jnp.sqrt(one_minus_a2) * (i * xc)

        n_group = seq_len // SUBLANES

        def body(g, carries, u_s=u_s, reverse=reverse):
            gg = n_group - 1 - g if reverse else g
            new = []
            for s in range(n_seq):
                r0 = pl.multiple_of(s * seq_len + gg * SUBLANES, SUBLANES)
                a, u = group_scan(a_s[pl.ds(r0, SUBLANES), :], u_s[pl.ds(r0, SUBLANES), :], reverse)
                h = u + a * carries[s]
                u_s[pl.ds(r0, SUBLANES), :] = h
                new.append(h[0:1, :] if reverse else h[SUBLANES - 1:SUBLANES, :])
            return tuple(new)

        init = tuple(h0_ref[s:s + 1, :] for s in range(n_seq))
        finals.append(lax.fori_loop(0, n_group, body, init))

    for s in range(n_seq):
        hfl_ref[s:s + 1, :] = finals[0][s]
        hbf_ref[s:s + 1, :] = finals[1][s]
    for c in range(n_chunk):
        sl = slice(c * RNN_CHUNK, (c + 1) * RNN_CHUNK)
        y_ref[sl, :] = ((uf_s[sl, :] + ub_s[sl, :]) * _gelu_tanh(xg_ref[sl, :])).astype(y_ref.dtype)


def _rglru(xr, xg, conv_w, conv_b, wa, ba, wx, bx, lam, h0f, h0b, rows, seq_len):
    n = xr.shape[0]
    n_seq = rows // seq_len
    row = pl.BlockSpec((rows, RNN_WIDTH), lambda i: (i, 0))
    full = lambda shape: pl.BlockSpec(shape, lambda i: (0,) * len(shape))
    state = pl.BlockSpec((None, n_seq, RNN_WIDTH), lambda i: (i, 0, 0))
    state_shape = jax.ShapeDtypeStruct((n // rows, n_seq, RNN_WIDTH), F32)
    return pl.pallas_call(
        functools.partial(_rglru_kernel, rows=rows, seq_len=seq_len),
        grid=(n // rows,),
        in_specs=[row, row, full((CONV_W, RNN_WIDTH)), full((1, RNN_WIDTH)),
                  full((2, RNN_WIDTH, RNN_WIDTH)), full((2, RNN_WIDTH)),
                  full((2, RNN_WIDTH, RNN_WIDTH)), full((2, RNN_WIDTH)), full((2, RNN_WIDTH)),
                  state, state],
        out_specs=[row, state, state],
        out_shape=[jax.ShapeDtypeStruct((n, RNN_WIDTH), BF16), state_shape, state_shape],
        scratch_shapes=[pltpu.VMEM((rows, RNN_WIDTH), F32)] * 4,
        compiler_params=_params(1),
        name="rglru_%d" % seq_len,
    )(xr, xg, conv_w, conv_b, wa, ba, wx, bx, lam, h0f, h0b)


def _block_diag_dense(w):
    eye = jnp.eye(RNN_BLOCKS, dtype=w.dtype)
    dense = w[:, :, :, None, :] * eye[None, :, None, :, None]
    return dense.reshape(2, RNN_WIDTH, RNN_WIDTH).astype(BF16)


SUB_TILE = 256
SUB_TILES = tuple(slice(s, s + SUB_TILE) for s in range(0, ROW_TILE, SUB_TILE))


def _residual_router(rows, x, o, m, lg_ref, lb_ref, rw_ref, x1_ref, xm_ref, aff_ref):
    x1 = _layer_norm(ALPHA * x + m[2:3] * o, lg_ref[...], lb_ref[...])
    x1_ref[rows, :] = x1
    xm = (x1 * (1.0 + m[4:5]) + m[3:4]).astype(BF16)
    xm_ref[rows, :] = xm
    lgt = _dot(xm, rw_ref[...])
    lane = lax.broadcasted_iota(I32, lgt.shape, 1)
    lgt = jnp.where(lane < N_EXPERTS, lgt, NEG_INF)
    ex = jnp.exp(lgt - jnp.max(lgt, axis=-1, keepdims=True))
    aff = ex / jnp.sum(ex, axis=-1, keepdims=True)
    aff_ref[rows, :] = aff[:, :N_EXPERTS]


def _mix_out_kernel(att_ref, rnn_ref, x_ref, mod_ref, w_ref, lg_ref, lb_ref, rw_ref,
                    x1_ref, xm_ref, aff_ref):
    m = mod_ref[...]
    for rows in SUB_TILES:
        o = _dot(att_ref[rows, :], w_ref[:ATTN_WIDTH, :]) + _dot(rnn_ref[rows, :], w_ref[ATTN_WIDTH:, :])
        _residual_router(rows, x_ref[rows, :], o, m, lg_ref, lb_ref, rw_ref, x1_ref, xm_ref, aff_ref)


def _sgu_out_kernel(u_ref, v_ref, x_ref, mod_ref, spw_ref, spb_ref, w_ref, lg_ref, lb_ref, rw_ref,
                    x1_ref, xm_ref, aff_ref, mixed_s):
    m = mod_ref[...]
    for rows in SUB_TILES:
        for c in range(rows.start, rows.stop, CHUNK):
            for g in range(SGU_GROUPS):
                cols = slice(g * SGU_GROUP_W, (g + 1) * SGU_GROUP_W)
                mixed_s[c:c + CHUNK, cols] = _dot(spw_ref[g], v_ref[c:c + CHUNK, cols]) + spb_ref[:, cols]
        gated = (u_ref[rows, :] * mixed_s[rows, :]).astype(BF16)
        o = _dot(gated, w_ref[...])
        _residual_router(rows, x_ref[rows, :], o, m, lg_ref, lb_ref, rw_ref, x1_ref, xm_ref, aff_ref)


def _mixer_out(kernel, name, acts, x, mod_l, weights, ln_g, ln_b, router_w, group0, rows_per_group,
               scratch=()):
    n = x.shape[0]
    row = lambda width: pl.BlockSpec((ROW_TILE, width), lambda i: (i, 0))
    full = lambda a: pl.BlockSpec(a.shape, lambda i: (0,) * a.ndim)
    return pl.pallas_call(
        kernel,
        grid=(n // ROW_TILE,),
        in_specs=([row(a.shape[1]) for a in acts]
                  + [row(D_MODEL), pl.BlockSpec((None, 6, D_MODEL), _group_map(group0, rows_per_group))]
                  + [full(w) for w in weights] + [full(ln_g), full(ln_b), full(router_w)]),
        out_specs=[row(D_MODEL), row(D_MODEL), row(N_EXPERTS)],
        out_shape=[jax.ShapeDtypeStruct((n, D_MODEL), F32),
                   jax.ShapeDtypeStruct((n, D_MODEL), BF16),
                   jax.ShapeDtypeStruct((n, N_EXPERTS), F32)],
        scratch_shapes=list(scratch),
        compiler_params=_params(1),
        name=name,
    )(*acts, x, mod_l, *weights, ln_g, ln_b, router_w)


def _sgu_in_kernel(x_ref, mod_ref, w_ref, b_ref, lg_ref, lb_ref, u_ref, v_ref):
    m = mod_ref[...]
    for rows in SUB_TILES:
        h = x_ref[rows, :] * (1.0 + m[1:2]) + m[0:1]
        p = _gelu_tanh(_dot(h.astype(BF16), w_ref[...]) + b_ref[...])
        u_ref[rows, :] = p[:, :SGU_WIDTH]
        v_ref[rows, :] = _layer_norm(p[:, SGU_WIDTH:], lg_ref[...], lb_ref[...]).astype(v_ref.dtype)


def _sgu_in(x, mod_l, w_bf16, b, ln_g, ln_b, group0, rows_per_group):
    n = x.shape[0]
    row = lambda width: pl.BlockSpec((ROW_TILE, width), lambda i: (i, 0))
    full = lambda a: pl.BlockSpec(a.shape, lambda i: (0,) * a.ndim)
    return pl.pallas_call(
        _sgu_in_kernel,
        grid=(n // ROW_TILE,),
        in_specs=[row(D_MODEL), pl.BlockSpec((None, 6, D_MODEL), _group_map(group0, rows_per_group)),
                  full(w_bf16), full(b), full(ln_g), full(ln_b)],
        out_specs=[row(SGU_WIDTH), row(SGU_WIDTH)],
        out_shape=[jax.ShapeDtypeStruct((n, SGU_WIDTH), F32),
                   jax.ShapeDtypeStruct((n, SGU_WIDTH), BF16)],
        compiler_params=_params(1),
        name="sgu_in",
    )(x, mod_l, w_bf16, b, ln_g, ln_b)


def _route_kernel(aff_ref, tri_ref, posm_ref, posr_ref, affr_ref, blk_ref, *, tokens, cap):
    aff = aff_ref[...]
    n_col = aff.shape[1]
    n_blk = tokens // PREFIX_BLOCK

    def bisect(_, lo_hi):
        lo, hi = lo_hi
        mid = lo + ((hi - lo) >> 1)
        cnt = jnp.sum(jnp.where(aff >= pltpu.bitcast(mid, F32), 1.0, 0.0), axis=0, keepdims=True)
        ge = cnt >= cap
        return jnp.where(ge, mid, lo), jnp.where(ge, hi, mid)

    lo0 = jnp.zeros((1, n_col), I32)
    hi0 = jnp.full((1, n_col), 0x7F800000, I32)
    thr_bits, _ = lax.fori_loop(0, 31, bisect, (lo0, hi0))
    thr = pltpu.bitcast(thr_bits, F32)
    above = pltpu.bitcast(thr_bits + 1, F32)

    tri = tri_ref[...]

    def excl_prefix(x01):
        outs, before, inside = [], [], []
        off = jnp.zeros((1, n_col), F32)
        for blk in range(n_blk):
            xb = x01[blk * PREFIX_BLOCK:(blk + 1) * PREFIX_BLOCK, :]
            outs.append(_dot(tri, xb.astype(BF16)) + off)
            cnt = jnp.sum(xb, axis=0, keepdims=True)
            before.append(off)
            inside.append(cnt)
            off = off + cnt
        return (outs[0] if n_blk == 1 else jnp.concatenate(outs, axis=0)), before + inside

    gt = jnp.where(aff >= above, 1.0, 0.0)
    eq = jnp.where(aff >= thr, 1.0, 0.0) - gt
    need = cap - jnp.sum(gt, axis=0, keepdims=True)
    eq_rank, _ = excl_prefix(eq)
    sel = gt + jnp.where(eq_rank < need, eq, 0.0)
    pos, blk_rows = excl_prefix(sel)
    posm = jnp.where(sel > 0.0, pos, -1.0)
    posm_ref[...] = posm
    posr_ref[...] = posm.T
    affr_ref[...] = aff.T
    for i, row in enumerate(blk_rows):
        blk_ref[i:i + 1, :] = row


def _route(aff_t, tri, tokens, cap):
    n_col = aff_t.shape[1]
    n_blk = tokens // PREFIX_BLOCK
    whole = lambda shape: pl.BlockSpec(shape, lambda i: (0, 0))
    return pl.pallas_call(
        functools.partial(_route_kernel, tokens=tokens, cap=cap),
        grid=(1,),
        in_specs=[whole((tokens, n_col)), whole((PREFIX_BLOCK, PREFIX_BLOCK))],
        out_specs=[whole((tokens, n_col)), whole((n_col, tokens)), whole((n_col, tokens)),
                   whole((2 * n_blk, n_col))],
        out_shape=[jax.ShapeDtypeStruct((tokens, n_col), F32),
                   jax.ShapeDtypeStruct((n_col, tokens), F32),
                   jax.ShapeDtypeStruct((n_col, tokens), F32),
                   jax.ShapeDtypeStruct((2 * n_blk, n_col), F32)],
        compiler_params=_params(1),
        name="route_%d" % tokens,
    )(aff_t, tri)


def _route_requests(aff, tri, n_req, tokens, cap):
    n_col = n_req * N_EXPERTS
    n_blk = tokens // PREFIX_BLOCK
    aff_t = aff.reshape(n_req, tokens, N_EXPERTS).transpose(1, 0, 2).reshape(tokens, n_col)
    pad = (-n_col) % LANES
    if pad:
        aff_t = jnp.pad(aff_t, ((0, 0), (0, pad)))
    posm, posr, affr, blk = _route(aff_t, tri, tokens, cap)
    posm = posm[:, :n_col].reshape(tokens, n_req, N_EXPERTS).transpose(1, 0, 2)
    blk = blk[:, :n_col].astype(I32).reshape(2, n_blk, n_req, N_EXPERTS).transpose(0, 2, 1, 3)
    expert_major = lambda a: a[:n_col].reshape(n_req, N_EXPERTS, tokens)
    return (posm.reshape(n_req * tokens, N_EXPERTS), expert_major(posr), expert_major(affr), blk[0], blk[1])


def _slot_gate(hit, aff_row):
    g = jnp.sum(jnp.where(hit, aff_row, 0.0), axis=-1, keepdims=True)
    return jnp.broadcast_to(g, (hit.shape[0], LANES))


def _gather_prompt_kernel(x_ref, posr_ref, affr_ref, xg_ref, g_ref):
    slot = lax.broadcasted_iota(I32, (CAP_P, SEQ), 0).astype(F32)
    for r in range(REQ_TILE):
        tokens = slice(r * SEQ, (r + 1) * SEQ)
        slots = slice(r * CAP_P, (r + 1) * CAP_P)
        posr = posr_ref[r]
        affr = affr_ref[r]
        hits = [posr[e:e + 1, :] == slot for e in range(N_EXPERTS)]
        onehot = jnp.concatenate([jnp.where(h, 1.0, 0.0) for h in hits], axis=0).astype(BF16)
        xg = _dot(onehot, x_ref[tokens, :])
        for e in range(N_EXPERTS):
            xg_ref[e, slots, :] = xg[e * CAP_P:(e + 1) * CAP_P, :].astype(xg_ref.dtype)
            g_ref[e, slots, :] = _slot_gate(hits[e], affr[e:e + 1, :])


def _gather_prompt(xm, posr, affr):
    return pl.pallas_call(
        _gather_prompt_kernel,
        grid=(BATCH // REQ_TILE,),
        in_specs=[pl.BlockSpec((REQ_TILE * SEQ, D_MODEL), lambda b: (b, 0)),
                  pl.BlockSpec((REQ_TILE, N_EXPERTS, SEQ), lambda b: (b, 0, 0)),
                  pl.BlockSpec((REQ_TILE, N_EXPERTS, SEQ), lambda b: (b, 0, 0))],
        out_specs=[pl.BlockSpec((N_EXPERTS, REQ_TILE * CAP_P, D_MODEL), lambda b: (0, b, 0)),
                   pl.BlockSpec((N_EXPERTS, REQ_TILE * CAP_P, LANES), lambda b: (0, b, 0))],
        out_shape=[jax.ShapeDtypeStruct((N_EXPERTS, SLOTS_P, D_MODEL), BF16),
                   jax.ShapeDtypeStruct((N_EXPERTS, SLOTS_P, LANES), F32)],
        compiler_params=_params(1),
        name="gather_prompt",
    )(xm, posr, affr)


TOKEN_BLOCK = PREFIX_BLOCK
N_TOKEN_BLOCKS = DEC_SEQ // TOKEN_BLOCK
SLOT_ALIGN = 16
WINDOW_SLOTS = 64
LAST_WINDOW = CAP_S - WINDOW_SLOTS
ALL_WINDOWS = N_EXPERTS * WINDOW_SLOTS


def _window_plan(before, inside):
    start = (before // SLOT_ALIGN) * SLOT_ALIGN
    span = before - start + inside
    passes = jnp.where(inside > 0, (span + WINDOW_SLOTS - 1) // WINDOW_SLOTS, 0)
    return start.reshape(-1), jnp.max(passes, axis=-1).reshape(-1)


def _window_bounds(start_ref, step, expert, k):
    lower = start_ref[step * N_EXPERTS + expert] + WINDOW_SLOTS * k
    begin = pl.multiple_of(jnp.minimum(lower, LAST_WINDOW), SLOT_ALIGN)
    return lower, begin


def _gather_sample_kernel(start_ref, passes_ref, x_ref, posr_ref, affr_ref, xg_ref, g_ref):
    t = pl.program_id(1)
    step = pl.program_id(0) * N_TOKEN_BLOCKS + t

    @pl.when(t == 0)
    def _():
        xg_ref[...] = jnp.zeros_like(xg_ref)
        g_ref[...] = jnp.zeros_like(g_ref)

    posr = posr_ref[...]
    affr = affr_ref[...]
    x = x_ref[...]
    row = lax.broadcasted_iota(I32, (WINDOW_SLOTS, TOKEN_BLOCK), 0).astype(F32)

    def one_pass(k):
        hits, begins = [], []
        for e in range(N_EXPERTS):
            lower, begin = _window_bounds(start_ref, step, e, k)
            pe = posr[e:e + 1, :]
            hits.append((pe - begin.astype(F32) == row) & (pe >= lower.astype(F32)))
            begins.append(begin)
        onehot = jnp.concatenate([jnp.where(h, 1.0, 0.0) for h in hits], axis=0).astype(BF16)
        rows_x = _dot(onehot, x)
        for e in range(N_EXPERTS):
            win = pl.ds(begins[e], WINDOW_SLOTS)
            xg_ref[e, win, :] += rows_x[e * WINDOW_SLOTS:(e + 1) * WINDOW_SLOTS, :].astype(xg_ref.dtype)
            g_ref[e, win, :] += _slot_gate(hits[e], affr[e:e + 1, :])

    one_pass(0)

    def extra_pass(k, carry):
        one_pass(k)
        return carry

    lax.fori_loop(1, passes_ref[step], extra_pass, 0)


def _gather_sample(xm, posr, affr, starts, passes):
    expert_rows = pl.BlockSpec((None, N_EXPERTS, TOKEN_BLOCK), lambda b, t, s, p: (b, 0, t))
    grid_spec = pltpu.PrefetchScalarGridSpec(
        num_scalar_prefetch=2,
        grid=(DEC_BATCH, N_TOKEN_BLOCKS),
        in_specs=[pl.BlockSpec((TOKEN_BLOCK, D_MODEL), lambda b, t, s, p: (b * N_TOKEN_BLOCKS + t, 0)),
                  expert_rows, expert_rows],
        out_specs=[pl.BlockSpec((N_EXPERTS, CAP_S, D_MODEL), lambda b, t, s, p: (0, b, 0)),
                   pl.BlockSpec((N_EXPERTS, CAP_S, LANES), lambda b, t, s, p: (0, b, 0))],
    )
    return pl.pallas_call(
        _gather_sample_kernel,
        grid_spec=grid_spec,
        out_shape=[jax.ShapeDtypeStruct((N_EXPERTS, SLOTS_S, D_MODEL), BF16),
                   jax.ShapeDtypeStruct((N_EXPERTS, SLOTS_S, LANES), F32)],
        compiler_params=_params(2),
        name="gather_sample",
    )(starts, passes, xm, posr, affr)


FF_TILE = 512
SLOT_CHUNK = 512


def _ffn_kernel(xp_ref, xs_ref, gp_ref, gs_ref, w1_ref, w3_ref, w2_ref, yp_ref, ys_ref, acc_ref):
    f = pl.program_id(1)
    n_f = pl.num_programs(1)
    @pl.when(f == 0)
    def _():
        acc_ref[...] = jnp.zeros_like(acc_ref)

    w1 = w1_ref[...].astype(BF16)
    w3 = w3_ref[...].astype(BF16)
    w2 = w2_ref[...].astype(BF16)

    halves = ((xp_ref, SLOTS_P, 0), (xs_ref, SLOTS_S, SLOTS_P))
    for x_ref, n_slot, base in halves:
        for c in range(n_slot // SLOT_CHUNK):
            x = x_ref[c * SLOT_CHUNK:(c + 1) * SLOT_CHUNK, :]
            h1 = _dot(x, w1)
            h3 = _dot(x, w3)
            hid = (h1 * jax.nn.sigmoid(h1) * h3).astype(BF16)
            rows = slice(base + c * SLOT_CHUNK, base + (c + 1) * SLOT_CHUNK)
            acc_ref[rows, :] += _dot(hid, w2)

    @pl.when(f == n_f - 1)
    def _():
        for y_ref, g_ref, n_slot, base in ((yp_ref, gp_ref, SLOTS_P, 0), (ys_ref, gs_ref, SLOTS_S, SLOTS_P)):
            for c in range(n_slot // SLOT_CHUNK):
                rows = slice(c * SLOT_CHUNK, (c + 1) * SLOT_CHUNK)
                gate = jnp.concatenate([g_ref[rows, :]] * (D_MODEL // LANES), axis=1)
                y_ref[rows, :] = (acc_ref[base + c * SLOT_CHUNK:base + (c + 1) * SLOT_CHUNK, :]
                                  * gate).astype(y_ref.dtype)


def _expert_ffn(xgp, xgs, gp, gs, w1, w3, w2, layer):
    slots = lambda n, width: pl.BlockSpec((None, n, width), lambda e, f: (e, 0, 0))
    return pl.pallas_call(
        _ffn_kernel,
        grid=(N_EXPERTS, EXPERT_FF // FF_TILE),
        in_specs=[slots(SLOTS_P, D_MODEL), slots(SLOTS_S, D_MODEL), slots(SLOTS_P, LANES), slots(SLOTS_S, LANES),
                  pl.BlockSpec((None, None, D_MODEL, FF_TILE), lambda e, f: (layer, e, 0, f)),
                  pl.BlockSpec((None, None, D_MODEL, FF_TILE), lambda e, f: (layer, e, 0, f)),
                  pl.BlockSpec((None, None, FF_TILE, D_MODEL), lambda e, f: (layer, e, f, 0))],
        out_specs=[slots(SLOTS_P, D_MODEL), slots(SLOTS_S, D_MODEL)],
        out_shape=[jax.ShapeDtypeStruct((N_EXPERTS, SLOTS_P, D_MODEL), BF16),
                   jax.ShapeDtypeStruct((N_EXPERTS, SLOTS_S, D_MODEL), BF16)],
        scratch_shapes=[pltpu.VMEM((SLOTS_P + SLOTS_S, D_MODEL), F32)],
        compiler_params=_params(2),
        name="expert_ffn",
    )(xgp, xgs, gp, gs, w1, w3, w2)


def _combine_prompt_kernel(posm_ref, y_ref, x_ref, mod_ref, exp_ref, lg_ref, lb_ref, o_ref):
    m = mod_ref[...]
    slot = (lax.broadcasted_iota(I32, (SEQ, N_EXPERTS * CAP_P), 1) & (CAP_P - 1)).astype(F32)
    for r in range(REQ_TILE):
        tokens = slice(r * SEQ, (r + 1) * SEQ)
        spread = _dot(posm_ref[tokens, :].astype(BF16), exp_ref[...])
        onehot = jnp.where(spread == slot, 1.0, 0.0).astype(BF16)
        y = jnp.concatenate([y_ref[e, r * CAP_P:(r + 1) * CAP_P, :] for e in range(N_EXPERTS)], axis=0)
        f = _dot(onehot, y)
        o_ref[tokens, :] = _layer_norm(ALPHA * x_ref[tokens, :] + m[5:6] * f, lg_ref[...], lb_ref[...])


def _combine_prompt(posm, y, x1, mod_l, expand, ln_g, ln_b):
    full = lambda a: pl.BlockSpec(a.shape, lambda b: (0,) * a.ndim)
    return pl.pallas_call(
        _combine_prompt_kernel,
        grid=(BATCH // REQ_TILE,),
        in_specs=[pl.BlockSpec((REQ_TILE * SEQ, N_EXPERTS), lambda b: (b, 0)),
                  pl.BlockSpec((N_EXPERTS, REQ_TILE * CAP_P, D_MODEL), lambda b: (0, b, 0)),
                  pl.BlockSpec((REQ_TILE * SEQ, D_MODEL), lambda b: (b, 0)),
                  pl.BlockSpec((None, 6, D_MODEL), lambda b: (0, 0, 0)),
                  full(expand), full(ln_g), full(ln_b)],
        out_specs=pl.BlockSpec((REQ_TILE * SEQ, D_MODEL), lambda b: (b, 0)),
        out_shape=jax.ShapeDtypeStruct((N_PROMPT, D_MODEL), F32),
        compiler_params=_params(1),
        name="combine_prompt",
    )(posm, y, x1, mod_l, expand, ln_g, ln_b)


def _combine_sample_kernel(start_ref, passes_ref, posm_ref, y_ref, x_ref, mod_ref, exp_ref, off_ref,
                           lg_ref, lb_ref, o_ref):
    step = pl.program_id(0) * N_TOKEN_BLOCKS + pl.program_id(1)
    exp_bf16 = exp_ref[...]
    spread = _dot(posm_ref[...].astype(BF16), exp_bf16)
    offset = off_ref[...]

    def one_pass(k):
        begin_row = jnp.zeros((1, ALL_WINDOWS), F32)
        lower_row = jnp.zeros((1, ALL_WINDOWS), F32)
        windows = []
        for e in range(N_EXPERTS):
            lower, begin = _window_bounds(start_ref, step, e, k)
            lanes_e = exp_bf16[e:e + 1, :].astype(F32)
            begin_row = begin_row + begin.astype(F32) * lanes_e
            lower_row = lower_row + lower.astype(F32) * lanes_e
            windows.append(y_ref[e, pl.ds(begin, WINDOW_SLOTS), :])
        hit = (spread - begin_row == offset) & (spread >= lower_row)
        onehot = jnp.where(hit, 1.0, 0.0).astype(BF16)
        return _dot(onehot, jnp.concatenate(windows, axis=0))

    f = lax.fori_loop(1, passes_ref[step], lambda k, f: f + one_pass(k), one_pass(0))
    m = mod_ref[...]
    o_ref[...] = _layer_norm(ALPHA * x_ref[...] + m[5:6] * f, lg_ref[...], lb_ref[...])


def _combine_sample(posm, y, x1, mod_l, ln_g, ln_b, starts, passes):
    lane = jnp.arange(ALL_WINDOWS)
    expand = ((lane[None, :] // WINDOW_SLOTS) == jnp.arange(N_EXPERTS)[:, None]).astype(BF16)
    offset = (lane % WINDOW_SLOTS).astype(F32).reshape(1, ALL_WINDOWS)
    row = lambda width: pl.BlockSpec((TOKEN_BLOCK, width), lambda b, t, s, p: (b * N_TOKEN_BLOCKS + t, 0))
    full = lambda a: pl.BlockSpec(a.shape, lambda b, t, s, p: (0,) * a.ndim)
    grid_spec = pltpu.PrefetchScalarGridSpec(
        num_scalar_prefetch=2,
        grid=(DEC_BATCH, N_TOKEN_BLOCKS),
        in_specs=[row(N_EXPERTS),
                  pl.BlockSpec((N_EXPERTS, CAP_S, D_MODEL), lambda b, t, s, p: (0, b, 0)),
                  row(D_MODEL),
                  pl.BlockSpec((None, 6, D_MODEL), lambda b, t, s, p: (1 + b, 0, 0)),
                  full(expand), full(offset), full(ln_g), full(ln_b)],
        out_specs=row(D_MODEL),
    )
    return pl.pallas_call(
        _combine_sample_kernel,
        grid_spec=grid_spec,
        out_shape=jax.ShapeDtypeStruct((N_SAMPLE, D_MODEL), F32),
        compiler_params=_params(2),
        name="combine_sample",
    )(starts, passes, posm, y, x1, mod_l, expand, offset, ln_g, ln_b)


def _moe(prompt, sample, mod_l, w1, w3, w2, layer, ln_g, ln_b, tri, expand):
    x1_p, xm_p, aff_p = prompt
    x1_s, xm_s, aff_s = sample
    posm_p, posr_p, affr_p, _, _ = _route_requests(aff_p, tri, BATCH, SEQ, CAP_P)
    posm_s, posr_s, affr_s, before_s, inside_s = _route_requests(aff_s, tri, DEC_BATCH, DEC_SEQ, CAP_S)
    starts, passes = _window_plan(before_s, inside_s)
    xg_p, g_p = _gather_prompt(xm_p, posr_p, affr_p)
    xg_s, g_s = _gather_sample(xm_s, posr_s, affr_s, starts, passes)
    y_p, y_s = _expert_ffn(xg_p, xg_s, g_p, g_s, w1, w3, w2, layer)
    x2_p = _combine_prompt(posm_p, y_p, x1_p, mod_l, expand, ln_g, ln_b)
    x2_s = _combine_sample(posm_s, y_s, x1_s, mod_l, ln_g, ln_b, starts, passes)
    return x2_p, x2_s


def kernel(x_prompt, x_sample, cache_k, cache_v, state_rglru, c, c_ctx, mod_w, mod_b, ln_mix_g, ln_mix_b, ln_ffn_g, ln_ffn_b, ab_in_w, attn_sink, rnn_conv_w, rnn_conv_b, lru_wa, lru_ba, lru_wx, lru_bx, lru_lambda, ab_out_w, sgu_in_w, sgu_in_b, sgu_ln_g, sgu_ln_b, sgu_spatial_w, sgu_spatial_b, sgu_out_w, router_w, moe_w1, moe_w3, moe_w2):
    xp = x_prompt.reshape(N_PROMPT, D_MODEL)
    xs = x_sample.reshape(N_SAMPLE, D_MODEL)

    cvec = jnp.concatenate([c_ctx[None], c, jnp.zeros((SUBLANES - 1 - DEC_BATCH, D_MODEL), F32)], axis=0)
    mod = _modulation(cvec, mod_w, mod_b).reshape(DEPTH, SUBLANES, 6, D_MODEL)

    idx = jnp.arange(PREFIX_BLOCK)
    tri = (idx[None, :] < idx[:, None]).astype(BF16)
    lane = jnp.arange(N_EXPERTS * CAP_P)
    expand = ((lane[None, :] // CAP_P) == jnp.arange(N_EXPERTS)[:, None]).astype(BF16)
    rope_tables = _rope_tables()
    vec = lambda a: a.reshape(1, -1)
    router_pad = lambda w: jnp.pad(w, ((0, 0), (0, LANES - N_EXPERTS))).astype(BF16)

    new_k = new_v = new_state = None
    for l in range(DEPTH):
        e = l // 2
        mod_l = mod[l]
        rw = router_pad(router_w[l])
        lg, lb = vec(ln_mix_g[l]), vec(ln_mix_b[l])
        if l % 2 == 0:
            w_in = ab_in_w[e].astype(BF16)
            w_out = ab_out_w[e].astype(BF16)
            wa = _block_diag_dense(lru_wa[e])
            wx = _block_diag_dense(lru_wx[e])
            rnn_w = (rnn_conv_w[e], vec(rnn_conv_b[e]), wa, lru_ba[e], wx, lru_bx[e], lru_lambda[e])

            q, k, v, xr, xg = _ab_in(xp, mod_l, w_in, 0, N_PROMPT, None, F32)
            new_k, new_v = k, v
            att = _ctx_attention(q, k, v, attn_sink[e])
            rows = 8 * SEQ
            zeros = jnp.zeros((N_PROMPT // rows, rows // SEQ, RNN_WIDTH), F32)
            rnn, hf_last, hb_first = _rglru(xr, xg, *rnn_w, zeros, zeros, rows, SEQ)
            new_state = jnp.stack([hf_last.reshape(BATCH, RNN_WIDTH), hb_first.reshape(BATCH, RNN_WIDTH)], axis=1)
            prompt = _mixer_out(_mix_out_kernel, "mix_out", (att, rnn), xp, mod_l, (w_out,),
                                lg, lb, rw, 0, N_PROMPT)

            q, k, v, xr, xg = _ab_in(xs, mod_l, w_in, 1, DEC_SEQ, rope_tables, BF16)
            att = _lat_attention(q, k, v, cache_k[:, e].reshape(DEC_BATCH * PAST_LEN, KV_WIDTH),
                                 cache_v[:, e].reshape(DEC_BATCH * PAST_LEN, KV_WIDTH), attn_sink[e])
            h0 = state_rglru[:, e]
            rnn, _, _ = _rglru(xr, xg, *rnn_w, h0[:, 0][:, None, :], h0[:, 1][:, None, :], DEC_SEQ, DEC_SEQ)
            sample = _mixer_out(_mix_out_kernel, "mix_out", (att, rnn), xs, mod_l, (w_out,),
                                lg, lb, rw, 1, DEC_SEQ)
        else:
            w_in = sgu_in_w[e].astype(BF16)
            w_out = sgu_out_w[e].astype(BF16)
            spw = sgu_spatial_w[e].astype(BF16)
            spb = jnp.repeat(sgu_spatial_b[e].T, SGU_GROUP_W, axis=1)
            sgu_args = (w_in, vec(sgu_in_b[e]), vec(sgu_ln_g[e]), vec(sgu_ln_b[e]))
            scratch = (pltpu.VMEM((ROW_TILE, SGU_WIDTH), F32),)
            u, vn = _sgu_in(xp, mod_l, *sgu_args, 0, N_PROMPT)
            prompt = _mixer_out(_sgu_out_kernel, "sgu_out", (u, vn), xp, mod_l, (spw, spb, w_out),
                                lg, lb, rw, 0, N_PROMPT, scratch)
            u, vn = _sgu_in(xs, mod_l, *sgu_args, 1, DEC_SEQ)
            sample = _mixer_out(_sgu_out_kernel, "sgu_out", (u, vn), xs, mod_l, (spw, spb, w_out),
                                lg, lb, rw, 1, DEC_SEQ, scratch)

        xp, xs = _moe(prompt, sample, mod_l, moe_w1, moe_w3, moe_w2, l,
                      vec(ln_ffn_g[l]), vec(ln_ffn_b[l]), tri, expand)

    return (xp.reshape(BATCH, SEQ, D_MODEL),
            xs.reshape(DEC_BATCH, DEC_SEQ, D_MODEL),
            new_k.reshape(BATCH, 1, SEQ, N_KV_HEADS, HEAD_DIM),
            new_v.reshape(BATCH, 1, SEQ, N_KV_HEADS, HEAD_DIM),
            new_state.reshape(BATCH, 1, 2, RNN_WIDTH))
```

```python
import functools

import jax
import jax.numpy as jnp
from jax import lax
from jax.experimental import pallas as pl
from jax.experimental.pallas import tpu as pltpu

F32 = jnp.float32
BF16 = jnp.bfloat16
I32 = jnp.int32

D_MODEL = 1024
BATCH = 32
SEQ = 256
DEPTH = 2
DEC_BATCH = 4
DEC_SEQ = 2048
PAST_LEN = 256
GRID_W = 64
HEAD_DIM = 128
N_Q_HEADS = 4
N_KV_HEADS = 2
Q_PER_KV = N_Q_HEADS // N_KV_HEADS
ATTN_WIDTH = N_Q_HEADS * HEAD_DIM
KV_WIDTH = N_KV_HEADS * HEAD_DIM
WINDOW = 128
BLOCK = 128
ATTN_SCALE = HEAD_DIM ** -0.5
ROPE_BASE = 10000.0
NEG_INF = -1e30
RNN_WIDTH = 512
RNN_BLOCKS = 8
RNN_BLOCK_W = RNN_WIDTH // RNN_BLOCKS
CONV_W = 4
CONV_PAD_LEFT = 2
LRU_C = 8.0
AB_IN_WIDTH = ATTN_WIDTH + 2 * KV_WIDTH + 2 * RNN_WIDTH
CHUNK = 128
SGU_WIDTH = D_MODEL
SGU_GROUPS = 8
SGU_GROUP_W = SGU_WIDTH // SGU_GROUPS
N_EXPERTS = 16
EXPERT_FF = 2048
EC_FACTOR = 2
ALPHA = (2 * DEPTH) ** 0.25
LN_EPS = 1e-6

N_PROMPT = BATCH * SEQ
N_SAMPLE = DEC_BATCH * DEC_SEQ
CAP_P = EC_FACTOR * SEQ // N_EXPERTS
CAP_S = EC_FACTOR * DEC_SEQ // N_EXPERTS
SLOTS_P = BATCH * CAP_P
SLOTS_S = DEC_BATCH * CAP_S

LANES = 128
SUBLANES = 8
ROW_TILE = 1024
REQ_TILE = 4
PREFIX_BLOCK = 256
VMEM_LIMIT = 56 * 1024 * 1024


def _params(n_axes=1):
    return pltpu.CompilerParams(dimension_semantics=("arbitrary",) * n_axes,
                                vmem_limit_bytes=VMEM_LIMIT)


def _layer_norm(x, g, b):
    mu = jnp.mean(x, axis=-1, keepdims=True)
    xc = x - mu
    var = jnp.mean(xc * xc, axis=-1, keepdims=True)
    return xc * lax.rsqrt(var + LN_EPS) * g + b


def _gelu_tanh(x):
    return 0.5 * x * (1.0 + jnp.tanh(0.7978845608028654 * (x + 0.044715 * (x * x * x))))


def _dot(a, b):
    return jnp.dot(a, b, preferred_element_type=F32)


def _dot_nt(a, b):
    return lax.dot_general(a, b, (((1,), (1,)), ((), ())), preferred_element_type=F32)


def _software_pipeline(items, first_stage, second_stage):
    staged = first_stage(items[0])
    for i, item in enumerate(items):
        upcoming = first_stage(items[i + 1]) if i + 1 < len(items) else None
        second_stage(item, staged)
        staged = upcoming


def _mod_kernel(c_ref, w_ref, b_ref, o_ref):
    c = c_ref[...]
    s = c * jax.nn.sigmoid(c)
    o_ref[...] = _dot(s.astype(BF16), w_ref[...].astype(BF16)) + b_ref[...]


def _modulation(cvec8, mod_w, mod_b):
    n_col = 6 * D_MODEL // D_MODEL
    return pl.pallas_call(
        _mod_kernel,
        grid=(DEPTH, n_col),
        in_specs=[pl.BlockSpec((SUBLANES, D_MODEL), lambda l, j: (0, 0)),
                  pl.BlockSpec((None, D_MODEL, D_MODEL), lambda l, j: (l, 0, j)),
                  pl.BlockSpec((None, 1, D_MODEL), lambda l, j: (l, 0, j))],
        out_specs=pl.BlockSpec((None, SUBLANES, D_MODEL), lambda l, j: (l, 0, j)),
        out_shape=jax.ShapeDtypeStruct((DEPTH, SUBLANES, 6 * D_MODEL), F32),
        compiler_params=_params(2),
        name="adaln_modulation",
    )(cvec8, mod_w, mod_b.reshape(DEPTH, 1, 6 * D_MODEL))


def _group_map(group0, rows_per_group):
    tiles_per_group = rows_per_group // ROW_TILE
    return lambda i: (group0 + i // tiles_per_group, 0, 0)


def _rope(t, cos, sin_signed):
    lane = lax.broadcasted_iota(I32, t.shape, 1)
    swapped = jnp.where((lane & 1) == 0, pltpu.roll(t, HEAD_DIM - 1, 1), pltpu.roll(t, 1, 1))
    return t * cos + swapped * sin_signed


def _ab_in_kernel(*refs, rope):
    if rope:
        x_ref, mod_ref, w_ref, cos_ref, sin_ref, q_ref, k_ref, v_ref, xr_ref, xg_ref = refs
    else:
        x_ref, mod_ref, w_ref, q_ref, k_ref, v_ref, xr_ref, xg_ref = refs
    m = mod_ref[...]
    h = x_ref[...] * (1.0 + m[1:2]) + m[0:1]
    p = _dot(h.astype(BF16), w_ref[...])
    q = p[:, :ATTN_WIDTH]
    k = p[:, ATTN_WIDTH:ATTN_WIDTH + KV_WIDTH]
    v = p[:, ATTN_WIDTH + KV_WIDTH:ATTN_WIDTH + 2 * KV_WIDTH]
    if rope:
        cos = cos_ref[...]
        sin = sin_ref[...]
        q = jnp.concatenate([_rope(q[:, i * HEAD_DIM:(i + 1) * HEAD_DIM], cos, sin)
                             for i in range(N_Q_HEADS)], axis=1)
        k = jnp.concatenate([_rope(k[:, i * HEAD_DIM:(i + 1) * HEAD_DIM], cos, sin)
                             for i in range(N_KV_HEADS)], axis=1)
    q_ref[...] = q.astype(q_ref.dtype)
    k_ref[...] = k.astype(k_ref.dtype)
    v_ref[...] = v.astype(v_ref.dtype)
    xr_ref[...] = p[:, ATTN_WIDTH + 2 * KV_WIDTH:ATTN_WIDTH + 2 * KV_WIDTH + RNN_WIDTH]
    xg_ref[...] = p[:, ATTN_WIDTH + 2 * KV_WIDTH + RNN_WIDTH:]


def _ab_in(x, mod_l, w_bf16, group0, rows_per_group, rope_tables, kv_dtype):
    n = x.shape[0]
    rope = rope_tables is not None
    row = lambda width: pl.BlockSpec((ROW_TILE, width), lambda i: (i, 0))
    in_specs = [row(D_MODEL),
                pl.BlockSpec((None, 6, D_MODEL), _group_map(group0, rows_per_group)),
                pl.BlockSpec((D_MODEL, AB_IN_WIDTH), lambda i: (0, 0))]
    args = [x, mod_l, w_bf16]
    if rope:
        tiles_per_seq = DEC_SEQ // ROW_TILE
        in_specs += [pl.BlockSpec((ROW_TILE, HEAD_DIM), lambda i: (i % tiles_per_seq, 0))] * 2
        args += list(rope_tables)
    return pl.pallas_call(
        functools.partial(_ab_in_kernel, rope=rope),
        grid=(n // ROW_TILE,),
        in_specs=in_specs,
        out_specs=[row(ATTN_WIDTH), row(KV_WIDTH), row(KV_WIDTH), row(RNN_WIDTH), row(RNN_WIDTH)],
        out_shape=[jax.ShapeDtypeStruct((n, ATTN_WIDTH), BF16),
                   jax.ShapeDtypeStruct((n, KV_WIDTH), kv_dtype),
                   jax.ShapeDtypeStruct((n, KV_WIDTH), kv_dtype),
                   jax.ShapeDtypeStruct((n, RNN_WIDTH), F32),
                   jax.ShapeDtypeStruct((n, RNN_WIDTH), F32)],
        compiler_params=_params(1),
        name="ab_in_rope" if rope else "ab_in",
    )(*args)


def _rope_tables():
    rows = DEC_SEQ // GRID_W
    row = jnp.repeat(jnp.arange(rows, dtype=F32), GRID_W)
    col = jnp.tile(jnp.arange(GRID_W, dtype=F32), rows)
    n_freq = HEAD_DIM // 4
    freqs = ROPE_BASE ** (-jnp.arange(n_freq, dtype=F32) / n_freq)
    ang = jnp.concatenate([row[:, None] * freqs, col[:, None] * freqs], axis=-1)
    cos = jnp.repeat(jnp.cos(ang), 2, axis=-1)
    sin = jnp.repeat(jnp.sin(ang), 2, axis=-1)
    sign = jnp.tile(jnp.array([-1.0, 1.0], F32), HEAD_DIM // 2)
    return cos, sin * sign


def _sink_attention_head(s_list, v_list, sink):
    m = sink
    for s in s_list:
        m = jnp.maximum(m, jnp.max(s, axis=-1, keepdims=True))
    p_list = [jnp.exp(s - m) for s in s_list]
    denom = jnp.exp(sink - m)
    for p in p_list:
        denom = denom + jnp.sum(p, axis=-1, keepdims=True)
    out = None
    for p, v in zip(p_list, v_list):
        o = _dot(p.astype(BF16), v)
        out = o if out is None else out + o
    return out * (1.0 / denom)


def _head_cols(h):
    return slice(h * HEAD_DIM, (h + 1) * HEAD_DIM)


def _ctx_attn_kernel(sink_ref, q_ref, k_ref, v_ref, o_ref):
    def scores(item):
        r, h = item
        rows = slice(r * SEQ, (r + 1) * SEQ)
        kh = k_ref[rows, _head_cols(h // Q_PER_KV)].astype(BF16)
        return _dot_nt(q_ref[rows, _head_cols(h)], kh) * ATTN_SCALE

    def finish(item, s):
        r, h = item
        rows = slice(r * SEQ, (r + 1) * SEQ)
        vh = v_ref[rows, _head_cols(h // Q_PER_KV)].astype(BF16)
        o_ref[rows, _head_cols(h)] = _sink_attention_head([s], [vh], sink_ref[h]).astype(o_ref.dtype)

    items = [(r, h) for r in range(REQ_TILE) for h in range(N_Q_HEADS)]
    _software_pipeline(items, scores, finish)


def _ctx_attention(q, k, v, sink):
    seq = lambda width: pl.BlockSpec((REQ_TILE * SEQ, width), lambda b: (b, 0))
    return pl.pallas_call(
        _ctx_attn_kernel,
        grid=(BATCH // REQ_TILE,),
        in_specs=[pl.BlockSpec(memory_space=pltpu.SMEM), seq(ATTN_WIDTH), seq(KV_WIDTH), seq(KV_WIDTH)],
        out_specs=seq(ATTN_WIDTH),
        out_shape=jax.ShapeDtypeStruct((N_PROMPT, ATTN_WIDTH), BF16),
        compiler_params=_params(1),
        name="context_attention",
    )(sink, q, k, v)


LAT_Q = 256


def _lat_attn_kernel(sink_ref, q_ref, kp_ref, kc_ref, kn_ref, vp_ref, vc_ref, vn_ref,
                     kx_ref, vx_ref, o_ref):
    n = pl.program_id(1)
    nb = pl.num_programs(1)
    q = q_ref[...]
    kw = jnp.concatenate([kp_ref[...], kc_ref[...], kn_ref[...]], axis=0)
    vw = jnp.concatenate([vp_ref[...], vc_ref[...], vn_ref[...]], axis=0)
    kx = kx_ref[...].astype(BF16)
    vx = vx_ref[...].astype(BF16)
    n_key = LAT_Q + 2 * WINDOW
    qi = lax.broadcasted_iota(I32, (LAT_Q, n_key), 0)
    kj = lax.broadcasted_iota(I32, (LAT_Q, n_key), 1)
    rel = kj - qi
    valid = (rel >= 0) & (rel <= 2 * WINDOW)
    valid = valid & ((n > 0) | (kj >= WINDOW)) & ((n < nb - 1) | (kj < LAT_Q + WINDOW))
    def scores(h):
        sl = _head_cols(h // Q_PER_KV)
        qh = q[:, _head_cols(h)]
        s_ctx = _dot_nt(qh, kx[:, sl]) * ATTN_SCALE
        s_win = jnp.where(valid, _dot_nt(qh, kw[:, sl]) * ATTN_SCALE, NEG_INF)
        return s_ctx, s_win

    def finish(h, s):
        sl = _head_cols(h // Q_PER_KV)
        o_ref[:, _head_cols(h)] = _sink_attention_head(list(s), [vx[:, sl], vw[:, sl]],
                                                       sink_ref[h]).astype(o_ref.dtype)

    _software_pipeline(range(N_Q_HEADS), scores, finish)


def _lat_attention(q, k, v, k_ctx, v_ctx, sink):
    nb = DEC_SEQ // LAT_Q
    nw = DEC_SEQ // WINDOW
    per = LAT_Q // WINDOW
    cur = lambda b, n: (b * nb + n, 0)
    prev = lambda b, n: (b * nw + jnp.maximum(n * per - 1, 0), 0)
    nxt = lambda b, n: (b * nw + jnp.minimum((n + 1) * per, nw - 1), 0)
    tile = lambda width: pl.BlockSpec((LAT_Q, width), cur)
    edge = lambda imap: pl.BlockSpec((WINDOW, KV_WIDTH), imap)
    ctx = pl.BlockSpec((PAST_LEN, KV_WIDTH), lambda b, n: (b, 0))
    return pl.pallas_call(
        _lat_attn_kernel,
        grid=(DEC_BATCH, nb),
        in_specs=[pl.BlockSpec(memory_space=pltpu.SMEM), tile(ATTN_WIDTH),
                  edge(prev), tile(KV_WIDTH), edge(nxt),
                  edge(prev), tile(KV_WIDTH), edge(nxt), ctx, ctx],
        out_specs=tile(ATTN_WIDTH),
        out_shape=jax.ShapeDtypeStruct((N_SAMPLE, ATTN_WIDTH), BF16),
        compiler_params=_params(2),
        name="latent_attention",
    )(sink, q, k, k, k, v, v, v, k_ctx, v_ctx)


RNN_CHUNK = 256


def _rglru_kernel(xr_ref, xg_ref, cw_ref, cb_ref, wa_ref, ba_ref, wx_ref, bx_ref, lam_ref,
                  h0f_ref, h0b_ref, y_ref, hfl_ref, hbf_ref, xc_s, a_s, uf_s, ub_s, *, rows, seq_len):
    n_seq = rows // seq_len
    n_chunk = rows // RNN_CHUNK
    cw = cw_ref[...]
    cb = cb_ref[...]
    zeros_halo = jnp.zeros((SUBLANES, RNN_WIDTH), F32)
    row8 = lax.broadcasted_iota(I32, (SUBLANES, RNN_WIDTH), 0)

    def conv_chunk(c):
        r0 = c * RNN_CHUNK
        first = r0 % seq_len == 0
        last = (r0 + RNN_CHUNK) % seq_len == 0
        before = zeros_halo if first else xr_ref[r0 - SUBLANES:r0, :]
        after = zeros_halo if last else xr_ref[r0 + RNN_CHUNK:r0 + RNN_CHUNK + SUBLANES, :]
        win = jnp.concatenate([before, xr_ref[r0:r0 + RNN_CHUNK, :], after], axis=0)
        xc = cb
        n_win = RNN_CHUNK + 2 * SUBLANES
        for i in range(CONV_W):
            shift = (CONV_PAD_LEFT - i) % n_win
            rolled = win if shift == 0 else pltpu.roll(win, shift, 0)
            xc = xc + rolled[SUBLANES:SUBLANES + RNN_CHUNK, :] * cw[i:i + 1, :]
        return xc

    def group_scan(a, u, reverse):
        for k in (1, 2, 4):
            if reverse:
                shift, ok = SUBLANES - k, row8 < SUBLANES - k
            else:
                shift, ok = k, row8 >= k
            a_nb = jnp.where(ok, pltpu.roll(a, shift, 0), 1.0)
            u_nb = jnp.where(ok, pltpu.roll(u, shift, 0), 0.0)
            u = a * u_nb + u
            a = a * a_nb
        return a, u

    for c in range(n_chunk):
        xc_s[c * RNN_CHUNK:(c + 1) * RNN_CHUNK, :] = conv_chunk(c)

    finals = []
    for d, (u_s, h0_ref) in enumerate(((uf_s, h0f_ref), (ub_s, h0b_ref))):
        reverse = d == 1
        neg = -lam_ref[d:d + 1, :]
        softplus = jnp.maximum(neg, 0.0) + jnp.log1p(jnp.exp(-jnp.abs(neg)))
        decay = -LRU_C * softplus
        wa = wa_ref[d]
        wx = wx_ref[d]
        ba = ba_ref[d:d + 1, :]
        bx = bx_ref[d:d + 1, :]
        for c in range(n_chunk):
            xc = xc_s[c * RNN_CHUNK:(c + 1) * RNN_CHUNK, :]
            xcb = xc.astype(BF16)
            r = jax.nn.sigmoid(_dot(xcb, wa) + ba)
            i = jax.nn.sigmoid(_dot(xcb, wx) + bx)
            log_a = r * decay
            a = jnp.exp(log_a)
            a_s[c * RNN_CHUNK:(c + 1) * RNN_CHUNK, :] = a
            one_minus_a2 = -jnp.tanh(log_a) * (a * a + 1.0)
            u_s[c * RNN_CHUNK:(c + 1) * RNN_CHUNK, :] = jnp.sqrt(one_minus_a2) * (i * xc)

        n_group = seq_len // SUBLANES

        def body(g, carries, u_s=u_s, reverse=reverse):
            gg = n_group - 1 - g if reverse else g
            new = []
            for s in range(n_seq):
                r0 = pl.multiple_of(s * seq_len + gg * SUBLANES, SUBLANES)
                a, u = group_scan(a_s[pl.ds(r0, SUBLANES), :], u_s[pl.ds(r0, SUBLANES), :], reverse)
                h = u + a * carries[s]
                u_s[pl.ds(r0, SUBLANES), :] = h
                new.append(h[0:1, :] if reverse else h[SUBLANES - 1:SUBLANES, :])
            return tuple(new)

        init = tuple(h0_ref[s:s + 1, :] for s in range(n_seq))
        finals.append(lax.fori_loop(0, n_group, body, init))

    for s in range(n_seq):
        hfl_ref[s:s + 1, :] = finals[0][s]
        hbf_ref[s:s + 1, :] = finals[1][s]
    for c in range(n_chunk):
        sl = slice(c * RNN_CHUNK, (c + 1) * RNN_CHUNK)
        y_ref[sl, :] = ((uf_s[sl, :] + ub_s[sl, :]) * _gelu_tanh(xg_ref[sl, :])).astype(y_ref.dtype)


def _rglru(xr, xg, conv_w, conv_b, wa, ba, wx, bx, lam, h0f, h0b, rows, seq_len):
    n = xr.shape[0]
    n_seq = rows // seq_len
    row = pl.BlockSpec((rows, RNN_WIDTH), lambda i: (i, 0))
    full = lambda shape: pl.BlockSpec(shape, lambda i: (0,) * len(shape))
    state = pl.BlockSpec((None, n_seq, RNN_WIDTH), lambda i: (i, 0, 0))
    state_shape = jax.ShapeDtypeStruct((n // rows, n_seq, RNN_WIDTH), F32)
    return pl.pallas_call(
        functools.partial(_rglru_kernel, rows=rows, seq_len=seq_len),
        grid=(n // rows,),
        in_specs=[row, row, full((CONV_W, RNN_WIDTH)), full((1, RNN_WIDTH)),
                  full((2, RNN_WIDTH, RNN_WIDTH)), full((2, RNN_WIDTH)),
                  full((2, RNN_WIDTH, RNN_WIDTH)), full((2, RNN_WIDTH)), full((2, RNN_WIDTH)),
                  state, state],
        out_specs=[row, state, state],
        out_shape=[jax.ShapeDtypeStruct((n, RNN_WIDTH), BF16), state_shape, state_shape],
        scratch_shapes=[pltpu.VMEM((rows, RNN_WIDTH), F32)] * 4,
        compiler_params=_params(1),
        name="rglru_%d" % seq_len,
    )(xr, xg, conv_w, conv_b, wa, ba, wx, bx, lam, h0f, h0b)


def _block_diag_dense(w):
    eye = jnp.eye(RNN_BLOCKS, dtype=w.dtype)
    dense = w[:, :, :, None, :] * eye[None, :, None, :, None]
    return dense.reshape(2, RNN_WIDTH, RNN_WIDTH).astype(BF16)


SUB_TILE = 256
SUB_TILES = tuple(slice(s, s + SUB_TILE) for s in range(0, ROW_TILE, SUB_TILE))


def _residual_router(rows, x, o, m, lg_ref, lb_ref, rw_ref, x1_ref, xm_ref, aff_ref):
    x1 = _layer_norm(ALPHA * x + m[2:3] * o, lg_ref[...], lb_ref[...])
    x1_ref[rows, :] = x1
    xm = (x1 * (1.0 + m[4:5]) + m[3:4]).astype(BF16)
    xm_ref[rows, :] = xm
    lgt = _dot(xm, rw_ref[...])
    lane = lax.broadcasted_iota(I32, lgt.shape, 1)
    lgt = jnp.where(lane < N_EXPERTS, lgt, NEG_INF)
    ex = jnp.exp(lgt - jnp.max(lgt, axis=-1, keepdims=True))
    aff = ex / jnp.sum(ex, axis=-1, keepdims=True)
    aff_ref[rows, :] = aff[:, :N_EXPERTS]


def _mix_out_kernel(att_ref, rnn_ref, x_ref, mod_ref, w_ref, lg_ref, lb_ref, rw_ref,
                    x1_ref, xm_ref, aff_ref):
    m = mod_ref[...]

    def project(rows):
        return _dot(att_ref[rows, :], w_ref[:ATTN_WIDTH, :]) + _dot(rnn_ref[rows, :], w_ref[ATTN_WIDTH:, :])

    def finish(rows, o):
        _residual_router(rows, x_ref[rows, :], o, m, lg_ref, lb_ref, rw_ref, x1_ref, xm_ref, aff_ref)

    _software_pipeline(SUB_TILES, project, finish)


def _sgu_out_kernel(u_ref, v_ref, x_ref, mod_ref, spw_ref, spb_ref, w_ref, lg_ref, lb_ref, rw_ref,
                    x1_ref, xm_ref, aff_ref, mixed_s):
    m = mod_ref[...]

    def gate_and_project(rows):
        for c in range(rows.start, rows.stop, CHUNK):
            for g in range(SGU_GROUPS):
                cols = slice(g * SGU_GROUP_W, (g + 1) * SGU_GROUP_W)
                mixed_s[c:c + CHUNK, cols] = _dot(spw_ref[g], v_ref[c:c + CHUNK, cols]) + spb_ref[:, cols]
        gated = (u_ref[rows, :] * mixed_s[rows, :]).astype(BF16)
        return _dot(gated, w_ref[...])

    def finish(rows, o):
        _residual_router(rows, x_ref[rows, :], o, m, lg_ref, lb_ref, rw_ref, x1_ref, xm_ref, aff_ref)

    _software_pipeline(SUB_TILES, gate_and_project, finish)


def _mixer_out(kernel, name, acts, x, mod_l, weights, ln_g, ln_b, router_w, group0, rows_per_group,
               scratch=()):
    n = x.shape[0]
    row = lambda width: pl.BlockSpec((ROW_TILE, width), lambda i: (i, 0))
    full = lambda a: pl.BlockSpec(a.shape, lambda i: (0,) * a.ndim)
    return pl.pallas_call(
        kernel,
        grid=(n // ROW_TILE,),
        in_specs=([row(a.shape[1]) for a in acts]
                  + [row(D_MODEL), pl.BlockSpec((None, 6, D_MODEL), _group_map(group0, rows_per_group))]
                  + [full(w) for w in weights] + [full(ln_g), full(ln_b), full(router_w)]),
        out_specs=[row(D_MODEL), row(D_MODEL), row(N_EXPERTS)],
        out_shape=[jax.ShapeDtypeStruct((n, D_MODEL), F32),
                   jax.ShapeDtypeStruct((n, D_MODEL), BF16),
                   jax.ShapeDtypeStruct((n, N_EXPERTS), F32)],
        scratch_shapes=list(scratch),
        compiler_params=_params(1),
        name=name,
    )(*acts, x, mod_l, *weights, ln_g, ln_b, router_w)


def _sgu_in_kernel(x_ref, mod_ref, w_ref, b_ref, lg_ref, lb_ref, u_ref, v_ref):
    m = mod_ref[...]

    def project(rows):
        h = x_ref[rows, :] * (1.0 + m[1:2]) + m[0:1]
        return _dot(h.astype(BF16), w_ref[...])

    def finish(rows, p):
        p = _gelu_tanh(p + b_ref[...])
        u_ref[rows, :] = p[:, :SGU_WIDTH]
        v_ref[rows, :] = _layer_norm(p[:, SGU_WIDTH:], lg_ref[...], lb_ref[...]).astype(v_ref.dtype)

    _software_pipeline(SUB_TILES, project, finish)


def _sgu_in(x, mod_l, w_bf16, b, ln_g, ln_b, group0, rows_per_group):
    n = x.shape[0]
    row = lambda width: pl.BlockSpec((ROW_TILE, width), lambda i: (i, 0))
    full = lambda a: pl.BlockSpec(a.shape, lambda i: (0,) * a.ndim)
    return pl.pallas_call(
        _sgu_in_kernel,
        grid=(n // ROW_TILE,),
        in_specs=[row(D_MODEL), pl.BlockSpec((None, 6, D_MODEL), _group_map(group0, rows_per_group)),
                  full(w_bf16), full(b), full(ln_g), full(ln_b)],
        out_specs=[row(SGU_WIDTH), row(SGU_WIDTH)],
        out_shape=[jax.ShapeDtypeStruct((n, SGU_WIDTH), F32),
                   jax.ShapeDtypeStruct((n, SGU_WIDTH), BF16)],
        compiler_params=_params(1),
        name="sgu_in",
    )(x, mod_l, w_bf16, b, ln_g, ln_b)


def _route_kernel(aff_ref, tri_ref, posm_ref, posr_ref, affr_ref, blk_ref, *, tokens, cap):
    aff = aff_ref[...]
    n_col = aff.shape[1]
    n_blk = tokens // PREFIX_BLOCK

    def bisect(_, lo_hi):
        lo, hi = lo_hi
        mid = lo + ((hi - lo) >> 1)
        cnt = jnp.sum(jnp.where(aff >= pltpu.bitcast(mid, F32), 1.0, 0.0), axis=0, keepdims=True)
        ge = cnt >= cap
        return jnp.where(ge, mid, lo), jnp.where(ge, hi, mid)

    lo0 = jnp.zeros((1, n_col), I32)
    hi0 = jnp.full((1, n_col), 0x7F800000, I32)
    thr_bits, _ = lax.fori_loop(0, 31, bisect, (lo0, hi0))
    thr = pltpu.bitcast(thr_bits, F32)
    above = pltpu.bitcast(thr_bits + 1, F32)

    tri = tri_ref[...]

    def excl_prefix(x01):
        outs, before, inside = [], [], []
        off = jnp.zeros((1, n_col), F32)
        for blk in range(n_blk):
            xb = x01[blk * PREFIX_BLOCK:(blk + 1) * PREFIX_BLOCK, :]
            outs.append(_dot(tri, xb.astype(BF16)) + off)
            cnt = jnp.sum(xb, axis=0, keepdims=True)
            before.append(off)
            inside.append(cnt)
            off = off + cnt
        return (outs[0] if n_blk == 1 else jnp.concatenate(outs, axis=0)), before + inside

    gt = jnp.where(aff >= above, 1.0, 0.0)
    eq = jnp.where(aff >= thr, 1.0, 0.0) - gt
    need = cap - jnp.sum(gt, axis=0, keepdims=True)
    eq_rank, _ = excl_prefix(eq)
    sel = gt + jnp.where(eq_rank < need, eq, 0.0)
    pos, blk_rows = excl_prefix(sel)
    posm = jnp.where(sel > 0.0, pos, -1.0)
    posm_ref[...] = posm
    posr_ref[...] = posm.T
    affr_ref[...] = aff.T
    for i, row in enumerate(blk_rows):
        blk_ref[i:i + 1, :] = row


def _route(aff_t, tri, tokens, cap):
    n_col = aff_t.shape[1]
    n_blk = tokens // PREFIX_BLOCK
    whole = lambda shape: pl.BlockSpec(shape, lambda i: (0, 0))
    return pl.pallas_call(
        functools.partial(_route_kernel, tokens=tokens, cap=cap),
        grid=(1,),
        in_specs=[whole((tokens, n_col)), whole((PREFIX_BLOCK, PREFIX_BLOCK))],
        out_specs=[whole((tokens, n_col)), whole((n_col, tokens)), whole((n_col, tokens)),
                   whole((2 * n_blk, n_col))],
        out_shape=[jax.ShapeDtypeStruct((tokens, n_col), F32),
                   jax.ShapeDtypeStruct((n_col, tokens), F32),
                   jax.ShapeDtypeStruct((n_col, tokens), F32),
                   jax.ShapeDtypeStruct((2 * n_blk, n_col), F32)],
        compiler_params=_params(1),
        name="route_%d" % tokens,
    )(aff_t, tri)


def _route_requests(aff, tri, n_req, tokens, cap):
    n_col = n_req * N_EXPERTS
    n_blk = tokens // PREFIX_BLOCK
    aff_t = aff.reshape(n_req, tokens, N_EXPERTS).transpose(1, 0, 2).reshape(tokens, n_col)
    pad = (-n_col) % LANES
    if pad:
        aff_t = jnp.pad(aff_t, ((0, 0), (0, pad)))
    posm, posr, affr, blk = _route(aff_t, tri, tokens, cap)
    posm = posm[:, :n_col].reshape(tokens, n_req, N_EXPERTS).transpose(1, 0, 2)
    blk = blk[:, :n_col].astype(I32).reshape(2, n_blk, n_req, N_EXPERTS).transpose(0, 2, 1, 3)
    expert_major = lambda a: a[:n_col].reshape(n_req, N_EXPERTS, tokens)
    return (posm.reshape(n_req * tokens, N_EXPERTS), expert_major(posr), expert_major(affr), blk[0], blk[1])


def _slot_gate(hit, aff_row):
    g = jnp.sum(jnp.where(hit, aff_row, 0.0), axis=-1, keepdims=True)
    return jnp.broadcast_to(g, (hit.shape[0], LANES))


def _gather_prompt_kernel(x_ref, posr_ref, affr_ref, xg_ref, g_ref):
    slot = lax.broadcasted_iota(I32, (CAP_P, SEQ), 0).astype(F32)
    for r in range(REQ_TILE):
        tokens = slice(r * SEQ, (r + 1) * SEQ)
        slots = slice(r * CAP_P, (r + 1) * CAP_P)
        posr = posr_ref[r]
        affr = affr_ref[r]
        hits = [posr[e:e + 1, :] == slot for e in range(N_EXPERTS)]
        onehot = jnp.concatenate([jnp.where(h, 1.0, 0.0) for h in hits], axis=0).astype(BF16)
        xg = _dot(onehot, x_ref[tokens, :])
        for e in range(N_EXPERTS):
            xg_ref[e, slots, :] = xg[e * CAP_P:(e + 1) * CAP_P, :].astype(xg_ref.dtype)
            g_ref[e, slots, :] = _slot_gate(hits[e], affr[e:e + 1, :])


def _gather_prompt(xm, posr, affr):
    return pl.pallas_call(
        _gather_prompt_kernel,
        grid=(BATCH // REQ_TILE,),
        in_specs=[pl.BlockSpec((REQ_TILE * SEQ, D_MODEL), lambda b: (b, 0)),
                  pl.BlockSpec((REQ_TILE, N_EXPERTS, SEQ), lambda b: (b, 0, 0)),
                  pl.BlockSpec((REQ_TILE, N_EXPERTS, SEQ), lambda b: (b, 0, 0))],
        out_specs=[pl.BlockSpec((N_EXPERTS, REQ_TILE * CAP_P, D_MODEL), lambda b: (0, b, 0)),
                   pl.BlockSpec((N_EXPERTS, REQ_TILE * CAP_P, LANES), lambda b: (0, b, 0))],
        out_shape=[jax.ShapeDtypeStruct((N_EXPERTS, SLOTS_P, D_MODEL), BF16),
                   jax.ShapeDtypeStruct((N_EXPERTS, SLOTS_P, LANES), F32)],
        compiler_params=_params(1),
        name="gather_prompt",
    )(xm, posr, affr)


TOKEN_BLOCK = PREFIX_BLOCK
N_TOKEN_BLOCKS = DEC_SEQ // TOKEN_BLOCK
SLOT_ALIGN = 16
WINDOW_SLOTS = 64
LAST_WINDOW = CAP_S - WINDOW_SLOTS
ALL_WINDOWS = N_EXPERTS * WINDOW_SLOTS


def _window_plan(before, inside):
    start = (before // SLOT_ALIGN) * SLOT_ALIGN
    span = before - start + inside
    passes = jnp.where(inside > 0, (span + WINDOW_SLOTS - 1) // WINDOW_SLOTS, 0)
    return start.reshape(-1), jnp.max(passes, axis=-1).reshape(-1)


def _window_bounds(start_ref, step, expert, k):
    lower = start_ref[step * N_EXPERTS + expert] + WINDOW_SLOTS * k
    begin = pl.multiple_of(jnp.minimum(lower, LAST_WINDOW), SLOT_ALIGN)
    return lower, begin


def _gather_sample_kernel(start_ref, passes_ref, x_ref, posr_ref, affr_ref, xg_ref, g_ref):
    t = pl.program_id(1)
    step = pl.program_id(0) * N_TOKEN_BLOCKS + t

    @pl.when(t == 0)
    def _():
        xg_ref[...] = jnp.zeros_like(xg_ref)
        g_ref[...] = jnp.zeros_like(g_ref)

    posr = posr_ref[...]
    affr = affr_ref[...]
    x = x_ref[...]
    row = lax.broadcasted_iota(I32, (WINDOW_SLOTS, TOKEN_BLOCK), 0).astype(F32)

    def one_pass(k):
        hits, begins = [], []
        for e in range(N_EXPERTS):
            lower, begin = _window_bounds(start_ref, step, e, k)
            pe = posr[e:e + 1, :]
            hits.append((pe - begin.astype(F32) == row) & (pe >= lower.astype(F32)))
            begins.append(begin)
        onehot = jnp.concatenate([jnp.where(h, 1.0, 0.0) for h in hits], axis=0).astype(BF16)
        rows_x = _dot(onehot, x)
        for e in range(N_EXPERTS):
            win = pl.ds(begins[e], WINDOW_SLOTS)
            xg_ref[e, win, :] += rows_x[e * WINDOW_SLOTS:(e + 1) * WINDOW_SLOTS, :].astype(xg_ref.dtype)
            g_ref[e, win, :] += _slot_gate(hits[e], affr[e:e + 1, :])

    one_pass(0)

    def extra_pass(k, carry):
        one_pass(k)
        return carry

    lax.fori_loop(1, passes_ref[step], extra_pass, 0)


def _gather_sample(xm, posr, affr, starts, passes):
    expert_rows = pl.BlockSpec((None, N_EXPERTS, TOKEN_BLOCK), lambda b, t, s, p: (b, 0, t))
    grid_spec = pltpu.PrefetchScalarGridSpec(
        num_scalar_prefetch=2,
        grid=(DEC_BATCH, N_TOKEN_BLOCKS),
        in_specs=[pl.BlockSpec((TOKEN_BLOCK, D_MODEL), lambda b, t, s, p: (b * N_TOKEN_BLOCKS + t, 0)),
                  expert_rows, expert_rows],
        out_specs=[pl.BlockSpec((N_EXPERTS, CAP_S, D_MODEL), lambda b, t, s, p: (0, b, 0)),
                   pl.BlockSpec((N_EXPERTS, CAP_S, LANES), lambda b, t, s, p: (0, b, 0))],
    )
    return pl.pallas_call(
        _gather_sample_kernel,
        grid_spec=grid_spec,
        out_shape=[jax.ShapeDtypeStruct((N_EXPERTS, SLOTS_S, D_MODEL), BF16),
                   jax.ShapeDtypeStruct((N_EXPERTS, SLOTS_S, LANES), F32)],
        compiler_params=_params(2),
        name="gather_sample",
    )(starts, passes, xm, posr, affr)


FF_TILE = 512
SLOT_CHUNK = 512


def _ffn_kernel(xp_ref, xs_ref, gp_ref, gs_ref, w1_ref, w3_ref, w2_ref, yp_ref, ys_ref, acc_ref):
    f = pl.program_id(1)
    n_f = pl.num_programs(1)
    @pl.when(f == 0)
    def _():
        acc_ref[...] = jnp.zeros_like(acc_ref)

    w1 = w1_ref[...].astype(BF16)
    w3 = w3_ref[...].astype(BF16)
    w2 = w2_ref[...].astype(BF16)

    def up_project(chunk):
        x_ref, c, _ = chunk
        x = x_ref[c * SLOT_CHUNK:(c + 1) * SLOT_CHUNK, :]
        return _dot(x, w1), _dot(x, w3)

    def down_project(chunk, h):
        _, c, base = chunk
        h1, h3 = h
        hid = (h1 * jax.nn.sigmoid(h1) * h3).astype(BF16)
        rows = slice(base + c * SLOT_CHUNK, base + (c + 1) * SLOT_CHUNK)
        acc_ref[rows, :] += _dot(hid, w2)

    chunks = [(x_ref, c, base) for x_ref, n_slot, base in ((xp_ref, SLOTS_P, 0), (xs_ref, SLOTS_S, SLOTS_P))
              for c in range(n_slot // SLOT_CHUNK)]
    _software_pipeline(chunks, up_project, down_project)

    @pl.when(f == n_f - 1)
    def _():
        for y_ref, g_ref, n_slot, base in ((yp_ref, gp_ref, SLOTS_P, 0), (ys_ref, gs_ref, SLOTS_S, SLOTS_P)):
            for c in range(n_slot // SLOT_CHUNK):
                rows = slice(c * SLOT_CHUNK, (c + 1) * SLOT_CHUNK)
                gate = jnp.concatenate([g_ref[rows, :]] * (D_MODEL // LANES), axis=1)
                y_ref[rows, :] = (acc_ref[base + c * SLOT_CHUNK:base + (c + 1) * SLOT_CHUNK, :]
                                  * gate).astype(y_ref.dtype)


def _expert_ffn(xgp, xgs, gp, gs, w1, w3, w2, layer):
    slots = lambda n, width: pl.BlockSpec((None, n, width), lambda e, f: (e, 0, 0))
    return pl.pallas_call(
        _ffn_kernel,
        grid=(N_EXPERTS, EXPERT_FF // FF_TILE),
        in_specs=[slots(SLOTS_P, D_MODEL), slots(SLOTS_S, D_MODEL), slots(SLOTS_P, LANES), slots(SLOTS_S, LANES),
                  pl.BlockSpec((None, None, D_MODEL, FF_TILE), lambda e, f: (layer, e, 0, f)),
                  pl.BlockSpec((None, None, D_MODEL, FF_TILE), lambda e, f: (layer, e, 0, f)),
                  pl.BlockSpec((None, None, FF_TILE, D_MODEL), lambda e, f: (layer, e, f, 0))],
        out_specs=[slots(SLOTS_P, D_MODEL), slots(SLOTS_S, D_MODEL)],
        out_shape=[jax.ShapeDtypeStruct((N_EXPERTS, SLOTS_P, D_MODEL), BF16),
                   jax.ShapeDtypeStruct((N_EXPERTS, SLOTS_S, D_MODEL), BF16)],
        scratch_shapes=[pltpu.VMEM((SLOTS_P + SLOTS_S, D_MODEL), F32)],
        compiler_params=_params(2),
        name="expert_ffn",
    )(xgp, xgs, gp, gs, w1, w3, w2)


def _combine_prompt_kernel(posm_ref, y_ref, x_ref, mod_ref, exp_ref, lg_ref, lb_ref, o_ref):
    m = mod_ref[...]
    slot = (lax.broadcasted_iota(I32, (SEQ, N_EXPERTS * CAP_P), 1) & (CAP_P - 1)).astype(F32)
    def scatter(r):
        tokens = slice(r * SEQ, (r + 1) * SEQ)
        spread = _dot(posm_ref[tokens, :].astype(BF16), exp_ref[...])
        onehot = jnp.where(spread == slot, 1.0, 0.0).astype(BF16)
        y = jnp.concatenate([y_ref[e, r * CAP_P:(r + 1) * CAP_P, :] for e in range(N_EXPERTS)], axis=0)
        return _dot(onehot, y)

    def finish(r, f):
        tokens = slice(r * SEQ, (r + 1) * SEQ)
        o_ref[tokens, :] = _layer_norm(ALPHA * x_ref[tokens, :] + m[5:6] * f, lg_ref[...], lb_ref[...])

    _software_pipeline(range(REQ_TILE), scatter, finish)


def _combine_prompt(posm, y, x1, mod_l, expand, ln_g, ln_b):
    full = lambda a: pl.BlockSpec(a.shape, lambda b: (0,) * a.ndim)
    return pl.pallas_call(
        _combine_prompt_kernel,
        grid=(BATCH // REQ_TILE,),
        in_specs=[pl.BlockSpec((REQ_TILE * SEQ, N_EXPERTS), lambda b: (b, 0)),
                  pl.BlockSpec((N_EXPERTS, REQ_TILE * CAP_P, D_MODEL), lambda b: (0, b, 0)),
                  pl.BlockSpec((REQ_TILE * SEQ, D_MODEL), lambda b: (b, 0)),
                  pl.BlockSpec((None, 6, D_MODEL), lambda b: (0, 0, 0)),
                  full(expand), full(ln_g), full(ln_b)],
        out_specs=pl.BlockSpec((REQ_TILE * SEQ, D_MODEL), lambda b: (b, 0)),
        out_shape=jax.ShapeDtypeStruct((N_PROMPT, D_MODEL), F32),
        compiler_params=_params(1),
        name="combine_prompt",
    )(posm, y, x1, mod_l, expand, ln_g, ln_b)


def _combine_sample_kernel(start_ref, passes_ref, posm_ref, y_ref, x_ref, mod_ref, exp_ref, off_ref,
                           lg_ref, lb_ref, o_ref):
    step = pl.program_id(0) * N_TOKEN_BLOCKS + pl.program_id(1)
    exp_bf16 = exp_ref[...]
    spread = _dot(posm_ref[...].astype(BF16), exp_bf16)
    offset = off_ref[...]

    def one_pass(k):
        begin_row = jnp.zeros((1, ALL_WINDOWS), F32)
        lower_row = jnp.zeros((1, ALL_WINDOWS), F32)
        windows = []
        for e in range(N_EXPERTS):
            lower, begin = _window_bounds(start_ref, step, e, k)
            lanes_e = exp_bf16[e:e + 1, :].astype(F32)
            begin_row = begin_row + begin.astype(F32) * lanes_e
            lower_row = lower_row + lower.astype(F32) * lanes_e
            windows.append(y_ref[e, pl.ds(begin, WINDOW_SLOTS), :])
        hit = (spread - begin_row == offset) & (spread >= lower_row)
        onehot = jnp.where(hit, 1.0, 0.0).astype(BF16)
        return _dot(onehot, jnp.concatenate(windows, axis=0))

    f = lax.fori_loop(1, passes_ref[step], lambda k, f: f + one_pass(k), one_pass(0))
    m = mod_ref[...]
    o_ref[...] = _layer_norm(ALPHA * x_ref[...] + m[5:6] * f, lg_ref[...], lb_ref[...])


def _combine_sample(posm, y, x1, mod_l, ln_g, ln_b, starts, passes):
    lane = jnp.arange(ALL_WINDOWS)
    expand = ((lane[None, :] // WINDOW_SLOTS) == jnp.arange(N_EXPERTS)[:, None]).astype(BF16)
    offset = (lane % WINDOW_SLOTS).astype(F32).reshape(1, ALL_WINDOWS)
    row = lambda width: pl.BlockSpec((TOKEN_BLOCK, width), lambda b, t, s, p: (b * N_TOKEN_BLOCKS + t, 0))
    full = lambda a: pl.BlockSpec(a.shape, lambda b, t, s, p: (0,) * a.ndim)
    grid_spec = pltpu.PrefetchScalarGridSpec(
        num_scalar_prefetch=2,
        grid=(DEC_BATCH, N_TOKEN_BLOCKS),
        in_specs=[row(N_EXPERTS),
                  pl.BlockSpec((N_EXPERTS, CAP_S, D_MODEL), lambda b, t, s, p: (0, b, 0)),
                  row(D_MODEL),
                  pl.BlockSpec((None, 6, D_MODEL), lambda b, t, s, p: (1 + b, 0, 0)),
                  full(expand), full(offset), full(ln_g), full(ln_b)],
        out_specs=row(D_MODEL),
    )
    return pl.pallas_call(
        _combine_sample_kernel,
        grid_spec=grid_spec,
        out_shape=jax.ShapeDtypeStruct((N_SAMPLE, D_MODEL), F32),
        compiler_params=_params(2),
        name="combine_sample",
    )(starts, passes, posm, y, x1, mod_l, expand, offset, ln_g, ln_b)


def _moe(prompt, sample, mod_l, w1, w3, w2, layer, ln_g, ln_b, tri, expand):
    x1_p, xm_p, aff_p = prompt
    x1_s, xm_s, aff_s = sample
    posm_p, posr_p, affr_p, _, _ = _route_requests(aff_p, tri, BATCH, SEQ, CAP_P)
    posm_s, posr_s, affr_s, before_s, inside_s = _route_requests(aff_s, tri, DEC_BATCH, DEC_SEQ, CAP_S)
    starts, passes = _window_plan(before_s, inside_s)
    xg_p, g_p = _gather_prompt(xm_p, posr_p, affr_p)
    xg_s, g_s = _gather_sample(xm_s, posr_s, affr_s, starts, passes)
    y_p, y_s = _expert_ffn(xg_p, xg_s, g_p, g_s, w1, w3, w2, layer)
    x2_p = _combine_prompt(posm_p, y_p, x1_p, mod_l, expand, ln_g, ln_b)
    x2_s = _combine_sample(posm_s, y_s, x1_s, mod_l, ln_g, ln_b, starts, passes)
    return x2_p, x2_s


def kernel(x_prompt, x_sample, cache_k, cache_v, state_rglru, c, c_ctx, mod_w, mod_b, ln_mix_g, ln_mix_b, ln_ffn_g, ln_ffn_b, ab_in_w, attn_sink, rnn_conv_w, rnn_conv_b, lru_wa, lru_ba, lru_wx, lru_bx, lru_lambda, ab_out_w, sgu_in_w, sgu_in_b, sgu_ln_g, sgu_ln_b, sgu_spatial_w, sgu_spatial_b, sgu_out_w, router_w, moe_w1, moe_w3, moe_w2):
    xp = x_prompt.reshape(N_PROMPT, D_MODEL)
    xs = x_sample.reshape(N_SAMPLE, D_MODEL)

    cvec = jnp.concatenate([c_ctx[None], c, jnp.zeros((SUBLANES - 1 - DEC_BATCH, D_MODEL), F32)], axis=0)
    mod = _modulation(cvec, mod_w, mod_b).reshape(DEPTH, SUBLANES, 6, D_MODEL)

    idx = jnp.arange(PREFIX_BLOCK)
    tri = (idx[None, :] < idx[:, None]).astype(BF16)
    lane = jnp.arange(N_EXPERTS * CAP_P)
    expand = ((lane[None, :] // CAP_P) == jnp.arange(N_EXPERTS)[:, None]).astype(BF16)
    rope_tables = _rope_tables()
    vec = lambda a: a.reshape(1, -1)
    router_pad = lambda w: jnp.pad(w, ((0, 0), (0, LANES - N_EXPERTS))).astype(BF16)

    new_k = new_v = new_state = None
    for l in range(DEPTH):
        e = l // 2
        mod_l = mod[l]
        rw = router_pad(router_w[l])
        lg, lb = vec(ln_mix_g[l]), vec(ln_mix_b[l])
        if l % 2 == 0:
            w_in = ab_in_w[e].astype(BF16)
            w_out = ab_out_w[e].astype(BF16)
            wa = _block_diag_dense(lru_wa[e])
            wx = _block_diag_dense(lru_wx[e])
            rnn_w = (rnn_conv_w[e], vec(rnn_conv_b[e]), wa, lru_ba[e], wx, lru_bx[e], lru_lambda[e])

            q, k, v, xr, xg = _ab_in(xp, mod_l, w_in, 0, N_PROMPT, None, F32)
            new_k, new_v = k, v
            att = _ctx_attention(q, k, v, attn_sink[e])
            rows = 8 * SEQ
            zeros = jnp.zeros((N_PROMPT // rows, rows // SEQ, RNN_WIDTH), F32)
            rnn, hf_last, hb_first = _rglru(xr, xg, *rnn_w, zeros, zeros, rows, SEQ)
            new_state = jnp.stack([hf_last.reshape(BATCH, RNN_WIDTH), hb_first.reshape(BATCH, RNN_WIDTH)], axis=1)
            prompt = _mixer_out(_mix_out_kernel, "mix_out", (att, rnn), xp, mod_l, (w_out,),
                                lg, lb, rw, 0, N_PROMPT)

            q, k, v, xr, xg = _ab_in(xs, mod_l, w_in, 1, DEC_SEQ, rope_tables, BF16)
            att = _lat_attention(q, k, v, cache_k[:, e].reshape(DEC_BATCH * PAST_LEN, KV_WIDTH),
                                 cache_v[:, e].reshape(DEC_BATCH * PAST_LEN, KV_WIDTH), attn_sink[e])
            h0 = state_rglru[:, e]
            rnn, _, _ = _rglru(xr, xg, *rnn_w, h0[:, 0][:, None, :], h0[:, 1][:, None, :], DEC_SEQ, DEC_SEQ)
            sample = _mixer_out(_mix_out_kernel, "mix_out", (att, rnn), xs, mod_l, (w_out,),
                                lg, lb, rw, 1, DEC_SEQ)
        else:
            w_in = sgu_in_w[e].astype(BF16)
            w_out = sgu_out_w[e].astype(BF16)
            spw = sgu_spatial_w[e].astype(BF16)
            spb = jnp.repeat(sgu_spatial_b[e].T, SGU_GROUP_W, axis=1)
            sgu_args = (w_in, vec(sgu_in_b[e]), vec(sgu_ln_g[e]), vec(sgu_ln_b[e]))
            scratch = (pltpu.VMEM((ROW_TILE, SGU_WIDTH), F32),)
            u, vn = _sgu_in(xp, mod_l, *sgu_args, 0, N_PROMPT)
            prompt = _mixer_out(_sgu_out_kernel, "sgu_out", (u, vn), xp, mod_l, (spw, spb, w_out),
                                lg, lb, rw, 0, N_PROMPT, scratch)
            u, vn = _sgu_in(xs, mod_l, *sgu_args, 1, DEC_SEQ)
            sample = _mixer_out(_sgu_out_kernel, "sgu_out", (u, vn), xs, mod_l, (spw, spb, w_out),
                                lg, lb, rw, 1, DEC_SEQ, scratch)

        xp, xs = _moe(prompt, sample, mod_l, moe_w1, moe_w3, moe_w2, l,
                      vec(ln_ffn_g[l]), vec(ln_ffn_b[l]), tri, expand)

    return (xp.reshape(BATCH, SEQ, D_MODEL),
            xs.reshape(DEC_BATCH, DEC_SEQ, D_MODEL),
            new_k.reshape(BATCH, 1, SEQ, N_KV_HEADS, HEAD_DIM),
            new_v.reshape(BATCH, 1, SEQ, N_KV_HEADS, HEAD_DIM),
            new_state.reshape(BATCH, 1, 2, RNN_WIDTH))
```

```python
import functools

import jax
import jax.numpy as jnp
from jax import lax
from jax.experimental import pallas as pl
from jax.experimental.pallas import tpu as pltpu

F32 = jnp.float32
BF16 = jnp.bfloat16
I32 = jnp.int32

D_MODEL = 1024
BATCH = 32
SEQ = 256
DEPTH = 2
DEC_BATCH = 4
DEC_SEQ = 2048
PAST_LEN = 256
GRID_W = 64
HEAD_DIM = 128
N_Q_HEADS = 4
N_KV_HEADS = 2
Q_PER_KV = N_Q_HEADS // N_KV_HEADS
ATTN_WIDTH = N_Q_HEADS * HEAD_DIM
KV_WIDTH = N_KV_HEADS * HEAD_DIM
WINDOW = 128
BLOCK = 128
ATTN_SCALE = HEAD_DIM ** -0.5
ROPE_BASE = 10000.0
NEG_INF = -1e30
RNN_WIDTH = 512
RNN_BLOCKS = 8
RNN_BLOCK_W = RNN_WIDTH // RNN_BLOCKS
CONV_W = 4
CONV_PAD_LEFT = 2
LRU_C = 8.0
AB_IN_WIDTH = ATTN_WIDTH + 2 * KV_WIDTH + 2 * RNN_WIDTH
CHUNK = 128
SGU_WIDTH = D_MODEL
SGU_GROUPS = 8
SGU_GROUP_W = SGU_WIDTH // SGU_GROUPS
N_EXPERTS = 16
EXPERT_FF = 2048
EC_FACTOR = 2
ALPHA = (2 * DEPTH) ** 0.25
LN_EPS = 1e-6

N_PROMPT = BATCH * SEQ
N_SAMPLE = DEC_BATCH * DEC_SEQ
CAP_P = EC_FACTOR * SEQ // N_EXPERTS
CAP_S = EC_FACTOR * DEC_SEQ // N_EXPERTS
SLOTS_P = BATCH * CAP_P
SLOTS_S = DEC_BATCH * CAP_S

LANES = 128
SUBLANES = 8
ROW_TILE = 1024
REQ_TILE = 4
PREFIX_BLOCK = 256
VMEM_LIMIT = 56 * 1024 * 1024


def _params(n_axes=1):
    return pltpu.CompilerParams(dimension_semantics=("arbitrary",) * n_axes,
                                vmem_limit_bytes=VMEM_LIMIT)


def _layer_norm(x, g, b):
    mu = jnp.mean(x, axis=-1, keepdims=True)
    xc = x - mu
    var = jnp.mean(xc * xc, axis=-1, keepdims=True)
    return xc * lax.rsqrt(var + LN_EPS) * g + b


_LOG2_E = 1.4426950408889634
_GELU_K0 = -2.0 * 0.7978845608028654 * _LOG2_E
_GELU_K1 = _GELU_K0 * 0.044715


def _gelu_tanh(x):
    return x / (1.0 + jnp.exp2(x * (_GELU_K0 + _GELU_K1 * (x * x))))


def _sigmoid(z):
    return 0.5 + 0.5 * jnp.tanh(0.5 * z)


def _dot(a, b):
    return jnp.dot(a, b, preferred_element_type=F32)


def _dot_nt(a, b):
    return lax.dot_general(a, b, (((1,), (1,)), ((), ())), preferred_element_type=F32)


def _software_pipeline(items, first_stage, second_stage):
    staged = first_stage(items[0])
    for i, item in enumerate(items):
        upcoming = first_stage(items[i + 1]) if i + 1 < len(items) else None
        second_stage(item, staged)
        staged = upcoming


def _mod_kernel(c_ref, w_ref, b_ref, o_ref):
    c = c_ref[...]
    s = c * jax.nn.sigmoid(c)
    o_ref[...] = _dot(s.astype(BF16), w_ref[...].astype(BF16)) + b_ref[...]


def _modulation(cvec8, mod_w, mod_b):
    n_col = 6 * D_MODEL // D_MODEL
    return pl.pallas_call(
        _mod_kernel,
        grid=(DEPTH, n_col),
        in_specs=[pl.BlockSpec((SUBLANES, D_MODEL), lambda l, j: (0, 0)),
                  pl.BlockSpec((None, D_MODEL, D_MODEL), lambda l, j: (l, 0, j)),
                  pl.BlockSpec((None, 1, D_MODEL), lambda l, j: (l, 0, j))],
        out_specs=pl.BlockSpec((None, SUBLANES, D_MODEL), lambda l, j: (l, 0, j)),
        out_shape=jax.ShapeDtypeStruct((DEPTH, SUBLANES, 6 * D_MODEL), F32),
        compiler_params=_params(2),
        name="adaln_modulation",
    )(cvec8, mod_w, mod_b.reshape(DEPTH, 1, 6 * D_MODEL))


def _group_map(group0, rows_per_group):
    tiles_per_group = rows_per_group // ROW_TILE
    return lambda i: (group0 + i // tiles_per_group, 0, 0)


def _rope(t, cos, sin_signed):
    lane = lax.broadcasted_iota(I32, t.shape, 1)
    swapped = jnp.where((lane & 1) == 0, pltpu.roll(t, HEAD_DIM - 1, 1), pltpu.roll(t, 1, 1))
    return t * cos + swapped * sin_signed


def _ab_in_kernel(*refs, rope):
    if rope:
        x_ref, mod_ref, w_ref, cos_ref, sin_ref, q_ref, k_ref, v_ref, xr_ref, xg_ref = refs
    else:
        x_ref, mod_ref, w_ref, q_ref, k_ref, v_ref, xr_ref, xg_ref, ck_ref, cv_ref = refs
    m = mod_ref[...]
    h = x_ref[...] * (1.0 + m[1:2]) + m[0:1]
    p = _dot(h.astype(BF16), w_ref[...])
    q = p[:, :ATTN_WIDTH]
    k = p[:, ATTN_WIDTH:ATTN_WIDTH + KV_WIDTH]
    v = p[:, ATTN_WIDTH + KV_WIDTH:ATTN_WIDTH + 2 * KV_WIDTH]
    if rope:
        cos = cos_ref[...]
        sin = sin_ref[...]
        q = jnp.concatenate([_rope(q[:, i * HEAD_DIM:(i + 1) * HEAD_DIM], cos, sin)
                             for i in range(N_Q_HEADS)], axis=1)
        k = jnp.concatenate([_rope(k[:, i * HEAD_DIM:(i + 1) * HEAD_DIM], cos, sin)
                             for i in range(N_KV_HEADS)], axis=1)
    q_ref[...] = q.astype(q_ref.dtype)
    k_ref[...] = k.astype(k_ref.dtype)
    v_ref[...] = v.astype(v_ref.dtype)
    xr_ref[...] = p[:, ATTN_WIDTH + 2 * KV_WIDTH:ATTN_WIDTH + 2 * KV_WIDTH + RNN_WIDTH]
    xg_ref[...] = p[:, ATTN_WIDTH + 2 * KV_WIDTH + RNN_WIDTH:]
    if not rope:
        for r in range(ROW_TILE // SEQ):
            for i in range(N_KV_HEADS):
                ck_ref[r, 0, :, i, :] = k[r * SEQ:(r + 1) * SEQ, _head_cols(i)]
                cv_ref[r, 0, :, i, :] = v[r * SEQ:(r + 1) * SEQ, _head_cols(i)]


def _ab_in(x, mod_l, w_bf16, group0, rows_per_group, rope_tables):
    n = x.shape[0]
    rope = rope_tables is not None
    row = lambda width: pl.BlockSpec((ROW_TILE, width), lambda i: (i, 0))
    in_specs = [row(D_MODEL),
                pl.BlockSpec((None, 6, D_MODEL), _group_map(group0, rows_per_group)),
                pl.BlockSpec((D_MODEL, AB_IN_WIDTH), lambda i: (0, 0))]
    args = [x, mod_l, w_bf16]
    out_specs = [row(ATTN_WIDTH), row(KV_WIDTH), row(KV_WIDTH), row(RNN_WIDTH), row(RNN_WIDTH)]
    out_shape = [jax.ShapeDtypeStruct((n, ATTN_WIDTH), BF16),
                 jax.ShapeDtypeStruct((n, KV_WIDTH), BF16),
                 jax.ShapeDtypeStruct((n, KV_WIDTH), BF16),
                 jax.ShapeDtypeStruct((n, RNN_WIDTH), F32),
                 jax.ShapeDtypeStruct((n, RNN_WIDTH), F32)]
    if rope:
        tiles_per_seq = DEC_SEQ // ROW_TILE
        in_specs += [pl.BlockSpec((ROW_TILE, HEAD_DIM), lambda i: (i % tiles_per_seq, 0))] * 2
        args += list(rope_tables)
    else:
        req = ROW_TILE // SEQ
        cache = pl.BlockSpec((req, 1, SEQ, N_KV_HEADS, HEAD_DIM), lambda i: (i, 0, 0, 0, 0))
        out_specs += [cache, cache]
        out_shape += [jax.ShapeDtypeStruct((n // SEQ, 1, SEQ, N_KV_HEADS, HEAD_DIM), F32)] * 2
    return pl.pallas_call(
        functools.partial(_ab_in_kernel, rope=rope),
        grid=(n // ROW_TILE,),
        in_specs=in_specs,
        out_specs=out_specs,
        out_shape=out_shape,
        compiler_params=_params(1),
        name="ab_in_rope" if rope else "ab_in",
    )(*args)


def _rope_tables():
    rows = DEC_SEQ // GRID_W
    row = jnp.repeat(jnp.arange(rows, dtype=F32), GRID_W)
    col = jnp.tile(jnp.arange(GRID_W, dtype=F32), rows)
    n_freq = HEAD_DIM // 4
    freqs = ROPE_BASE ** (-jnp.arange(n_freq, dtype=F32) / n_freq)
    ang = jnp.concatenate([row[:, None] * freqs, col[:, None] * freqs], axis=-1)
    cos = jnp.repeat(jnp.cos(ang), 2, axis=-1)
    sin = jnp.repeat(jnp.sin(ang), 2, axis=-1)
    sign = jnp.tile(jnp.array([-1.0, 1.0], F32), HEAD_DIM // 2)
    return cos, sin * sign


def _sink_attention_head(s_list, v_list, sink):
    m = sink
    for s in s_list:
        m = jnp.maximum(m, jnp.max(s, axis=-1, keepdims=True))
    p_list = [jnp.exp(s - m) for s in s_list]
    denom = jnp.exp(sink - m)
    for p in p_list:
        denom = denom + jnp.sum(p, axis=-1, keepdims=True)
    out = None
    for p, v in zip(p_list, v_list):
        o = _dot(p.astype(BF16), v)
        out = o if out is None else out + o
    return out * (1.0 / denom)


def _head_cols(h):
    return slice(h * HEAD_DIM, (h + 1) * HEAD_DIM)


def _ctx_attn_kernel(sink_ref, q_ref, k_ref, v_ref, o_ref):
    def scores(item):
        r, h = item
        rows = slice(r * SEQ, (r + 1) * SEQ)
        kh = k_ref[rows, _head_cols(h // Q_PER_KV)].astype(BF16)
        return _dot_nt(q_ref[rows, _head_cols(h)], kh) * ATTN_SCALE

    def finish(item, s):
        r, h = item
        rows = slice(r * SEQ, (r + 1) * SEQ)
        vh = v_ref[rows, _head_cols(h // Q_PER_KV)].astype(BF16)
        o_ref[rows, _head_cols(h)] = _sink_attention_head([s], [vh], sink_ref[h]).astype(o_ref.dtype)

    items = [(r, h) for r in range(REQ_TILE) for h in range(N_Q_HEADS)]
    _software_pipeline(items, scores, finish)


def _ctx_attention(q, k, v, sink):
    seq = lambda width: pl.BlockSpec((REQ_TILE * SEQ, width), lambda b: (b, 0))
    return pl.pallas_call(
        _ctx_attn_kernel,
        grid=(BATCH // REQ_TILE,),
        in_specs=[pl.BlockSpec(memory_space=pltpu.SMEM), seq(ATTN_WIDTH), seq(KV_WIDTH), seq(KV_WIDTH)],
        out_specs=seq(ATTN_WIDTH),
        out_shape=jax.ShapeDtypeStruct((N_PROMPT, ATTN_WIDTH), BF16),
        compiler_params=_params(1),
        name="context_attention",
    )(sink, q, k, v)


LAT_Q = 256


def _lat_attn_kernel(sink_ref, q_ref, kp_ref, kc_ref, kn_ref, vp_ref, vc_ref, vn_ref,
                     kx_ref, vx_ref, o_ref):
    n = pl.program_id(1)
    nb = pl.num_programs(1)
    q = q_ref[...]
    kw = jnp.concatenate([kp_ref[...], kc_ref[...], kn_ref[...]], axis=0)
    vw = jnp.concatenate([vp_ref[...], vc_ref[...], vn_ref[...]], axis=0)
    kx = kx_ref[...].astype(BF16)
    vx = vx_ref[...].astype(BF16)
    n_key = LAT_Q + 2 * WINDOW
    qi = lax.broadcasted_iota(I32, (LAT_Q, n_key), 0)
    kj = lax.broadcasted_iota(I32, (LAT_Q, n_key), 1)
    rel = kj - qi
    valid = (rel >= 0) & (rel <= 2 * WINDOW)
    valid = valid & ((n > 0) | (kj >= WINDOW)) & ((n < nb - 1) | (kj < LAT_Q + WINDOW))
    def scores(h):
        sl = _head_cols(h // Q_PER_KV)
        qh = q[:, _head_cols(h)]
        s_ctx = _dot_nt(qh, kx[:, sl]) * ATTN_SCALE
        s_win = jnp.where(valid, _dot_nt(qh, kw[:, sl]) * ATTN_SCALE, NEG_INF)
        return s_ctx, s_win

    def finish(h, s):
        sl = _head_cols(h // Q_PER_KV)
        o_ref[:, _head_cols(h)] = _sink_attention_head(list(s), [vx[:, sl], vw[:, sl]],
                                                       sink_ref[h]).astype(o_ref.dtype)

    _software_pipeline(range(N_Q_HEADS), scores, finish)


def _lat_attention(q, k, v, k_ctx, v_ctx, sink):
    nb = DEC_SEQ // LAT_Q
    nw = DEC_SEQ // WINDOW
    per = LAT_Q // WINDOW
    cur = lambda b, n: (b * nb + n, 0)
    prev = lambda b, n: (b * nw + jnp.maximum(n * per - 1, 0), 0)
    nxt = lambda b, n: (b * nw + jnp.minimum((n + 1) * per, nw - 1), 0)
    tile = lambda width: pl.BlockSpec((LAT_Q, width), cur)
    edge = lambda imap: pl.BlockSpec((WINDOW, KV_WIDTH), imap)
    ctx = pl.BlockSpec((PAST_LEN, KV_WIDTH), lambda b, n: (b, 0))
    return pl.pallas_call(
        _lat_attn_kernel,
        grid=(DEC_BATCH, nb),
        in_specs=[pl.BlockSpec(memory_space=pltpu.SMEM), tile(ATTN_WIDTH),
                  edge(prev), tile(KV_WIDTH), edge(nxt),
                  edge(prev), tile(KV_WIDTH), edge(nxt), ctx, ctx],
        out_specs=tile(ATTN_WIDTH),
        out_shape=jax.ShapeDtypeStruct((N_SAMPLE, ATTN_WIDTH), BF16),
        compiler_params=_params(2),
        name="latent_attention",
    )(sink, q, k, k, k, v, v, v, k_ctx, v_ctx)


RNN_CHUNK = 256


def _rglru_kernel(xr_ref, xg_ref, cw_ref, cb_ref, wa_ref, ba_ref, wx_ref, bx_ref, lam_ref,
                  h0f_ref, h0b_ref, y_ref, hfl_ref, hbf_ref, xc_s, a_s, uf_s, ub_s, *, rows, seq_len):
    n_seq = rows // seq_len
    n_chunk = rows // RNN_CHUNK
    cw = cw_ref[...]
    cb = cb_ref[...]
    zeros_halo = jnp.zeros((SUBLANES, RNN_WIDTH), F32)
    row8 = lax.broadcasted_iota(I32, (SUBLANES, RNN_WIDTH), 0)

    def conv_chunk(c):
        r0 = c * RNN_CHUNK
        first = r0 % seq_len == 0
        last = (r0 + RNN_CHUNK) % seq_len == 0
        before = zeros_halo if first else xr_ref[r0 - SUBLANES:r0, :]
        after = zeros_halo if last else xr_ref[r0 + RNN_CHUNK:r0 + RNN_CHUNK + SUBLANES, :]
        win = jnp.concatenate([before, xr_ref[r0:r0 + RNN_CHUNK, :], after], axis=0)
        xc = cb
        n_win = RNN_CHUNK + 2 * SUBLANES
        for i in range(CONV_W):
            shift = (CONV_PAD_LEFT - i) % n_win
            rolled = win if shift == 0 else pltpu.roll(win, shift, 0)
            xc = xc + rolled[SUBLANES:SUBLANES + RNN_CHUNK, :] * cw[i:i + 1, :]
        return xc

    def group_scan(a, u, reverse):
        for k in (1, 2, 4):
            if reverse:
                shift, ok = SUBLANES - k, row8 < SUBLANES - k
            else:
                shift, ok = k, row8 >= k
            a_nb = jnp.where(ok, pltpu.roll(a, shift, 0), 1.0)
            u_nb = jnp.where(ok, pltpu.roll(u, shift, 0), 0.0)
            u = a * u_nb + u
            a = a * a_nb
        return a, u

    for c in range(n_chunk):
        xc_s[c * RNN_CHUNK:(c + 1) * RNN_CHUNK, :] = conv_chunk(c)

    finals = []
    for d, (u_s, h0_ref) in enumerate(((uf_s, h0f_ref), (ub_s, h0b_ref))):
        reverse = d == 1
        neg = -lam_ref[d:d + 1, :]
        softplus = jnp.maximum(neg, 0.0) + jnp.log1p(jnp.exp(-jnp.abs(neg)))
        decay = -LRU_C * softplus
        wa = wa_ref[d]
        wx = wx_ref[d]
        ba = ba_ref[d:d + 1, :]
        bx = bx_ref[d:d + 1, :]
        for c in range(n_chunk):
            xc = xc_s[c * RNN_CHUNK:(c + 1) * RNN_CHUNK, :]
            xcb = xc.astype(BF16)
            r = _sigmoid(_dot(xcb, wa) + ba)
            i = _sigmoid(_dot(xcb, wx) + bx)
            log_a = r * decay
            a = jnp.exp(log_a)
            a_s[c * RNN_CHUNK:(c + 1) * RNN_CHUNK, :] = a
            one_minus_a2 = -jnp.tanh(log_a) * (a * a + 1.0)
            u_s[c * RNN_CHUNK:(c + 1) * RNN_CHUNK, :] = jnp.sqrt(one_minus_a2) * (i * xc)

        n_group = seq_len // SUBLANES

        def body(g, carries, u_s=u_s, reverse=reverse):
            gg = n_group - 1 - g if reverse else g
            new = []
            for s in range(n_seq):
                r0 = pl.multiple_of(s * seq_len + gg * SUBLANES, SUBLANES)
                a, u = group_scan(a_s[pl.ds(r0, SUBLANES), :], u_s[pl.ds(r0, SUBLANES), :], reverse)
                h = u + a * carries[s]
                u_s[pl.ds(r0, SUBLANES), :] = h
                new.append(h[0:1, :] if reverse else h[SUBLANES - 1:SUBLANES, :])
            return tuple(new)

        init = tuple(h0_ref[s:s + 1, :] for s in range(n_seq))
        finals.append(lax.fori_loop(0, n_group, body, init))

    for s in range(n_seq):
        hfl_ref[s:s + 1, :] = finals[0][s]
        hbf_ref[s:s + 1, :] = finals[1][s]
    for c in range(n_chunk):
        sl = slice(c * RNN_CHUNK, (c + 1) * RNN_CHUNK)
        y_ref[sl, :] = ((uf_s[sl, :] + ub_s[sl, :]) * _gelu_tanh(xg_ref[sl, :])).astype(y_ref.dtype)


def _rglru(xr, xg, conv_w, conv_b, wa, ba, wx, bx, lam, h0f, h0b, rows, seq_len):
    n = xr.shape[0]
    n_seq = rows // seq_len
    row = pl.BlockSpec((rows, RNN_WIDTH), lambda i: (i, 0))
    full = lambda shape: pl.BlockSpec(shape, lambda i: (0,) * len(shape))
    state = pl.BlockSpec((None, n_seq, RNN_WIDTH), lambda i: (i, 0, 0))
    state_shape = jax.ShapeDtypeStruct((n // rows, n_seq, RNN_WIDTH), F32)
    return pl.pallas_call(
        functools.partial(_rglru_kernel, rows=rows, seq_len=seq_len),
        grid=(n // rows,),
        in_specs=[row, row, full((CONV_W, RNN_WIDTH)), full((1, RNN_WIDTH)),
                  full((2, RNN_WIDTH, RNN_WIDTH)), full((2, RNN_WIDTH)),
                  full((2, RNN_WIDTH, RNN_WIDTH)), full((2, RNN_WIDTH)), full((2, RNN_WIDTH)),
                  state, state],
        out_specs=[row, state, state],
        out_shape=[jax.ShapeDtypeStruct((n, RNN_WIDTH), BF16), state_shape, state_shape],
        scratch_shapes=[pltpu.VMEM((rows, RNN_WIDTH), F32)] * 4,
        compiler_params=_params(1),
        name="rglru_%d" % seq_len,
    )(xr, xg, conv_w, conv_b, wa, ba, wx, bx, lam, h0f, h0b)


def _block_diag_dense(w):
    eye = jnp.eye(RNN_BLOCKS, dtype=w.dtype)
    dense = w[:, :, :, None, :] * eye[None, :, None, :, None]
    return dense.reshape(2, RNN_WIDTH, RNN_WIDTH).astype(BF16)


SUB_TILE = 256
SUB_TILES = tuple(slice(s, s + SUB_TILE) for s in range(0, ROW_TILE, SUB_TILE))


def _residual_router(rows, x, o, m, lg_ref, lb_ref, rw_ref, x1_ref, xm_ref, aff_ref):
    x1 = _layer_norm(ALPHA * x + m[2:3] * o, lg_ref[...], lb_ref[...])
    x1_ref[rows, :] = x1
    xm = (x1 * (1.0 + m[4:5]) + m[3:4]).astype(BF16)
    xm_ref[rows, :] = xm
    lgt = _dot(xm, rw_ref[...])
    lane = lax.broadcasted_iota(I32, lgt.shape, 1)
    lgt = jnp.where(lane < N_EXPERTS, lgt, NEG_INF)
    ex = jnp.exp(lgt - jnp.max(lgt, axis=-1, keepdims=True))
    aff = ex / jnp.sum(ex, axis=-1, keepdims=True)
    aff_ref[rows, :] = aff[:, :N_EXPERTS]


def _mix_out_kernel(att_ref, rnn_ref, x_ref, mod_ref, w_ref, lg_ref, lb_ref, rw_ref,
                    x1_ref, xm_ref, aff_ref):
    m = mod_ref[...]

    def project(rows):
        return _dot(att_ref[rows, :], w_ref[:ATTN_WIDTH, :]) + _dot(rnn_ref[rows, :], w_ref[ATTN_WIDTH:, :])

    def finish(rows, o):
        _residual_router(rows, x_ref[rows, :], o, m, lg_ref, lb_ref, rw_ref, x1_ref, xm_ref, aff_ref)

    _software_pipeline(SUB_TILES, project, finish)


def _sgu_out_kernel(u_ref, v_ref, x_ref, mod_ref, spw_ref, spb_ref, w_ref, lg_ref, lb_ref, rw_ref,
                    x1_ref, xm_ref, aff_ref, mixed_s):
    m = mod_ref[...]

    def gate_and_project(rows):
        for c in range(rows.start, rows.stop, CHUNK):
            for g in range(SGU_GROUPS):
                cols = slice(g * SGU_GROUP_W, (g + 1) * SGU_GROUP_W)
                mixed_s[c:c + CHUNK, cols] = _dot(spw_ref[g], v_ref[c:c + CHUNK, cols]) + spb_ref[:, cols]
        gated = (u_ref[rows, :] * mixed_s[rows, :]).astype(BF16)
        return _dot(gated, w_ref[...])

    def finish(rows, o):
        _residual_router(rows, x_ref[rows, :], o, m, lg_ref, lb_ref, rw_ref, x1_ref, xm_ref, aff_ref)

    _software_pipeline(SUB_TILES, gate_and_project, finish)


def _mixer_out(kernel, name, acts, x, mod_l, weights, ln_g, ln_b, router_w, group0, rows_per_group,
               scratch=()):
    n = x.shape[0]
    row = lambda width: pl.BlockSpec((ROW_TILE, width), lambda i: (i, 0))
    full = lambda a: pl.BlockSpec(a.shape, lambda i: (0,) * a.ndim)
    return pl.pallas_call(
        kernel,
        grid=(n // ROW_TILE,),
        in_specs=([row(a.shape[1]) for a in acts]
                  + [row(D_MODEL), pl.BlockSpec((None, 6, D_MODEL), _group_map(group0, rows_per_group))]
                  + [full(w) for w in weights] + [full(ln_g), full(ln_b), full(router_w)]),
        out_specs=[row(D_MODEL), row(D_MODEL), row(N_EXPERTS)],
        out_shape=[jax.ShapeDtypeStruct((n, D_MODEL), F32),
                   jax.ShapeDtypeStruct((n, D_MODEL), BF16),
                   jax.ShapeDtypeStruct((n, N_EXPERTS), F32)],
        scratch_shapes=list(scratch),
        compiler_params=_params(1),
        name=name,
    )(*acts, x, mod_l, *weights, ln_g, ln_b, router_w)


def _sgu_in_kernel(x_ref, mod_ref, w_ref, b_ref, lg_ref, lb_ref, u_ref, v_ref):
    m = mod_ref[...]

    def project(rows):
        h = x_ref[rows, :] * (1.0 + m[1:2]) + m[0:1]
        return _dot(h.astype(BF16), w_ref[...])

    def finish(rows, p):
        p = _gelu_tanh(p + b_ref[...])
        u_ref[rows, :] = p[:, :SGU_WIDTH]
        v_ref[rows, :] = _layer_norm(p[:, SGU_WIDTH:], lg_ref[...], lb_ref[...]).astype(v_ref.dtype)

    _software_pipeline(SUB_TILES, project, finish)


def _sgu_in(x, mod_l, w_bf16, b, ln_g, ln_b, group0, rows_per_group):
    n = x.shape[0]
    row = lambda width: pl.BlockSpec((ROW_TILE, width), lambda i: (i, 0))
    full = lambda a: pl.BlockSpec(a.shape, lambda i: (0,) * a.ndim)
    return pl.pallas_call(
        _sgu_in_kernel,
        grid=(n // ROW_TILE,),
        in_specs=[row(D_MODEL), pl.BlockSpec((None, 6, D_MODEL), _group_map(group0, rows_per_group)),
                  full(w_bf16), full(b), full(ln_g), full(ln_b)],
        out_specs=[row(SGU_WIDTH), row(SGU_WIDTH)],
        out_shape=[jax.ShapeDtypeStruct((n, SGU_WIDTH), F32),
                   jax.ShapeDtypeStruct((n, SGU_WIDTH), BF16)],
        compiler_params=_params(1),
        name="sgu_in",
    )(x, mod_l, w_bf16, b, ln_g, ln_b)


def _route_kernel(aff_ref, tri_ref, posm_ref, posr_ref, affr_ref, blk_ref, *, tokens, cap):
    aff = aff_ref[...]
    n_col = aff.shape[1]
    n_blk = tokens // PREFIX_BLOCK

    def bisect(_, lo_hi):
        lo, hi = lo_hi
        mid = lo + ((hi - lo) >> 1)
        cnt = jnp.sum(jnp.where(aff >= pltpu.bitcast(mid, F32), 1.0, 0.0), axis=0, keepdims=True)
        ge = cnt >= cap
        return jnp.where(ge, mid, lo), jnp.where(ge, hi, mid)

    lo0 = jnp.zeros((1, n_col), I32)
    hi0 = jnp.full((1, n_col), 0x7F800000, I32)
    thr_bits, _ = lax.fori_loop(0, 31, bisect, (lo0, hi0))
    thr = pltpu.bitcast(thr_bits, F32)
    above = pltpu.bitcast(thr_bits + 1, F32)

    tri = tri_ref[...]

    def excl_prefix(x01):
        outs, before, inside = [], [], []
        off = jnp.zeros((1, n_col), F32)
        for blk in range(n_blk):
            xb = x01[blk * PREFIX_BLOCK:(blk + 1) * PREFIX_BLOCK, :]
            outs.append(_dot(tri, xb.astype(BF16)) + off)
            cnt = jnp.sum(xb, axis=0, keepdims=True)
            before.append(off)
            inside.append(cnt)
            off = off + cnt
        return (outs[0] if n_blk == 1 else jnp.concatenate(outs, axis=0)), before + inside

    gt = jnp.where(aff >= above, 1.0, 0.0)
    eq = jnp.where(aff >= thr, 1.0, 0.0) - gt
    need = cap - jnp.sum(gt, axis=0, keepdims=True)
    eq_rank, _ = excl_prefix(eq)
    sel = gt + jnp.where(eq_rank < need, eq, 0.0)
    pos, blk_rows = excl_prefix(sel)
    posm = jnp.where(sel > 0.0, pos, -1.0)
    posm_ref[...] = posm
    posr_ref[...] = posm.T
    affr_ref[...] = aff.T
    for i, row in enumerate(blk_rows):
        blk_ref[i:i + 1, :] = row


def _route(aff_t, tri, tokens, cap):
    n_col = aff_t.shape[1]
    n_blk = tokens // PREFIX_BLOCK
    whole = lambda shape: pl.BlockSpec(shape, lambda i: (0, 0))
    return pl.pallas_call(
        functools.partial(_route_kernel, tokens=tokens, cap=cap),
        grid=(1,),
        in_specs=[whole((tokens, n_col)), whole((PREFIX_BLOCK, PREFIX_BLOCK))],
        out_specs=[whole((tokens, n_col)), whole((n_col, tokens)), whole((n_col, tokens)),
                   whole((2 * n_blk, n_col))],
        out_shape=[jax.ShapeDtypeStruct((tokens, n_col), F32),
                   jax.ShapeDtypeStruct((n_col, tokens), F32),
                   jax.ShapeDtypeStruct((n_col, tokens), F32),
                   jax.ShapeDtypeStruct((2 * n_blk, n_col), F32)],
        compiler_params=_params(1),
        name="route_%d" % tokens,
    )(aff_t, tri)


def _route_requests(aff, tri, n_req, tokens, cap):
    n_col = n_req * N_EXPERTS
    n_blk = tokens // PREFIX_BLOCK
    aff_t = aff.reshape(n_req, tokens, N_EXPERTS).transpose(1, 0, 2).reshape(tokens, n_col)
    pad = (-n_col) % LANES
    if pad:
        aff_t = jnp.pad(aff_t, ((0, 0), (0, pad)))
    posm, posr, affr, blk = _route(aff_t, tri, tokens, cap)
    posm = posm[:, :n_col].reshape(tokens, n_req, N_EXPERTS).transpose(1, 0, 2)
    blk = blk[:, :n_col].astype(I32).reshape(2, n_blk, n_req, N_EXPERTS).transpose(0, 2, 1, 3)
    expert_major = lambda a: a[:n_col].reshape(n_req, N_EXPERTS, tokens)
    return (posm.reshape(n_req * tokens, N_EXPERTS), expert_major(posr), expert_major(affr), blk[0], blk[1])


def _slot_gate(hit, aff_row):
    g = jnp.sum(jnp.where(hit, aff_row, 0.0), axis=-1, keepdims=True)
    return jnp.broadcast_to(g, (hit.shape[0], LANES))


def _gather_prompt_kernel(x_ref, posr_ref, affr_ref, xg_ref, g_ref):
    slot = lax.broadcasted_iota(I32, (CAP_P, SEQ), 0).astype(F32)
    for r in range(REQ_TILE):
        tokens = slice(r * SEQ, (r + 1) * SEQ)
        slots = slice(r * CAP_P, (r + 1) * CAP_P)
        posr = posr_ref[r]
        affr = affr_ref[r]
        hits = [posr[e:e + 1, :] == slot for e in range(N_EXPERTS)]
        onehot = jnp.concatenate([jnp.where(h, 1.0, 0.0) for h in hits], axis=0).astype(BF16)
        xg = _dot(onehot, x_ref[tokens, :])
        for e in range(N_EXPERTS):
            xg_ref[e, slots, :] = xg[e * CAP_P:(e + 1) * CAP_P, :].astype(xg_ref.dtype)
            g_ref[e, slots, :] = _slot_gate(hits[e], affr[e:e + 1, :])


def _gather_prompt(xm, posr, affr):
    return pl.pallas_call(
        _gather_prompt_kernel,
        grid=(BATCH // REQ_TILE,),
        in_specs=[pl.BlockSpec((REQ_TILE * SEQ, D_MODEL), lambda b: (b, 0)),
                  pl.BlockSpec((REQ_TILE, N_EXPERTS, SEQ), lambda b: (b, 0, 0)),
                  pl.BlockSpec((REQ_TILE, N_EXPERTS, SEQ), lambda b: (b, 0, 0))],
        out_specs=[pl.BlockSpec((N_EXPERTS, REQ_TILE * CAP_P, D_MODEL), lambda b: (0, b, 0)),
                   pl.BlockSpec((N_EXPERTS, REQ_TILE * CAP_P, LANES), lambda b: (0, b, 0))],
        out_shape=[jax.ShapeDtypeStruct((N_EXPERTS, SLOTS_P, D_MODEL), BF16),
                   jax.ShapeDtypeStruct((N_EXPERTS, SLOTS_P, LANES), F32)],
        compiler_params=_params(1),
        name="gather_prompt",
    )(xm, posr, affr)


TOKEN_BLOCK = PREFIX_BLOCK
N_TOKEN_BLOCKS = DEC_SEQ // TOKEN_BLOCK
SLOT_ALIGN = 16
WINDOW_SLOTS = 64
LAST_WINDOW = CAP_S - WINDOW_SLOTS
ALL_WINDOWS = N_EXPERTS * WINDOW_SLOTS


def _window_plan(before, inside):
    start = (before // SLOT_ALIGN) * SLOT_ALIGN
    span = before - start + inside
    passes = jnp.where(inside > 0, (span + WINDOW_SLOTS - 1) // WINDOW_SLOTS, 0)
    return start.reshape(-1), jnp.max(passes, axis=-1).reshape(-1)


def _window_bounds(start_ref, step, expert, k):
    lower = start_ref[step * N_EXPERTS + expert] + WINDOW_SLOTS * k
    begin = pl.multiple_of(jnp.minimum(lower, LAST_WINDOW), SLOT_ALIGN)
    return lower, begin


BLOCKS_PER_STEP = 2
STEP_TOKENS = BLOCKS_PER_STEP * TOKEN_BLOCK
STEPS_PER_REQ = N_TOKEN_BLOCKS // BLOCKS_PER_STEP


def _gather_sample_kernel(start_ref, passes_ref, x_ref, posr_ref, affr_ref, xg_ref, g_ref):
    t = pl.program_id(1)
    first_block = (pl.program_id(0) * STEPS_PER_REQ + t) * BLOCKS_PER_STEP

    @pl.when(t == 0)
    def _():
        xg_ref[...] = jnp.zeros_like(xg_ref)
        g_ref[...] = jnp.zeros_like(g_ref)

    row = lax.broadcasted_iota(I32, (WINDOW_SLOTS, TOKEN_BLOCK), 0).astype(F32)

    def pick(j, k):
        tokens = slice(j * TOKEN_BLOCK, (j + 1) * TOKEN_BLOCK)
        hits, begins = [], []
        for e in range(N_EXPERTS):
            lower, begin = _window_bounds(start_ref, first_block + j, e, k)
            pe = posr_ref[e:e + 1, tokens]
            hits.append((pe - begin.astype(F32) == row) & (pe >= lower.astype(F32)))
            begins.append(begin)
        onehot = jnp.concatenate([jnp.where(h, 1.0, 0.0) for h in hits], axis=0).astype(BF16)
        return _dot(onehot, x_ref[tokens, :]), hits, begins

    def place(j, picked):
        rows_x, hits, begins = picked
        tokens = slice(j * TOKEN_BLOCK, (j + 1) * TOKEN_BLOCK)
        for e in range(N_EXPERTS):
            win = pl.ds(begins[e], WINDOW_SLOTS)
            xg_ref[e, win, :] += rows_x[e * WINDOW_SLOTS:(e + 1) * WINDOW_SLOTS, :].astype(xg_ref.dtype)
            g_ref[e, win, :] += _slot_gate(hits[e], affr_ref[e:e + 1, tokens])

    _software_pipeline(range(BLOCKS_PER_STEP), lambda j: pick(j, 0), place)

    for j in range(BLOCKS_PER_STEP):
        def extra_pass(k, carry, j=j):
            place(j, pick(j, k))
            return carry

        lax.fori_loop(1, passes_ref[first_block + j], extra_pass, 0)


def _gather_sample(xm, posr, affr, starts, passes):
    expert_rows = pl.BlockSpec((None, N_EXPERTS, STEP_TOKENS), lambda b, t, s, p: (b, 0, t))
    grid_spec = pltpu.PrefetchScalarGridSpec(
        num_scalar_prefetch=2,
        grid=(DEC_BATCH, STEPS_PER_REQ),
        in_specs=[pl.BlockSpec((STEP_TOKENS, D_MODEL), lambda b, t, s, p: (b * STEPS_PER_REQ + t, 0)),
                  expert_rows, expert_rows],
        out_specs=[pl.BlockSpec((N_EXPERTS, CAP_S, D_MODEL), lambda b, t, s, p: (0, b, 0)),
                   pl.BlockSpec((N_EXPERTS, CAP_S, LANES), lambda b, t, s, p: (0, b, 0))],
    )
    return pl.pallas_call(
        _gather_sample_kernel,
        grid_spec=grid_spec,
        out_shape=[jax.ShapeDtypeStruct((N_EXPERTS, SLOTS_S, D_MODEL), BF16),
                   jax.ShapeDtypeStruct((N_EXPERTS, SLOTS_S, LANES), F32)],
        compiler_params=_params(2),
        name="gather_sample",
    )(starts, passes, xm, posr, affr)


FF_TILE = 512
SLOT_CHUNK = 512


def _ffn_kernel(xp_ref, xs_ref, gp_ref, gs_ref, w1_ref, w3_ref, w2_ref, yp_ref, ys_ref, acc_ref):
    f = pl.program_id(1)
    n_f = pl.num_programs(1)
    @pl.when(f == 0)
    def _():
        acc_ref[...] = jnp.zeros_like(acc_ref)

    w1 = w1_ref[...].astype(BF16)
    w3 = w3_ref[...].astype(BF16)
    w2 = w2_ref[...].astype(BF16)

    def up_project(chunk):
        x_ref, c, _ = chunk
        x = x_ref[c * SLOT_CHUNK:(c + 1) * SLOT_CHUNK, :]
        return _dot(x, w1), _dot(x, w3)

    def down_project(chunk, h):
        _, c, base = chunk
        h1, h3 = h
        hid = (h1 * jax.nn.sigmoid(h1) * h3).astype(BF16)
        rows = slice(base + c * SLOT_CHUNK, base + (c + 1) * SLOT_CHUNK)
        acc_ref[rows, :] += _dot(hid, w2)

    chunks = [(x_ref, c, base) for x_ref, n_slot, base in ((xp_ref, SLOTS_P, 0), (xs_ref, SLOTS_S, SLOTS_P))
              for c in range(n_slot // SLOT_CHUNK)]
    _software_pipeline(chunks, up_project, down_project)

    @pl.when(f == n_f - 1)
    def _():
        for y_ref, g_ref, n_slot, base in ((yp_ref, gp_ref, SLOTS_P, 0), (ys_ref, gs_ref, SLOTS_S, SLOTS_P)):
            for c in range(n_slot // SLOT_CHUNK):
                rows = slice(c * SLOT_CHUNK, (c + 1) * SLOT_CHUNK)
                gate = jnp.concatenate([g_ref[rows, :]] * (D_MODEL // LANES), axis=1)
                y_ref[rows, :] = (acc_ref[base + c * SLOT_CHUNK:base + (c + 1) * SLOT_CHUNK, :]
                                  * gate).astype(y_ref.dtype)


def _expert_ffn(xgp, xgs, gp, gs, w1, w3, w2, layer):
    slots = lambda n, width: pl.BlockSpec((None, n, width), lambda e, f: (e, 0, 0))
    return pl.pallas_call(
        _ffn_kernel,
        grid=(N_EXPERTS, EXPERT_FF // FF_TILE),
        in_specs=[slots(SLOTS_P, D_MODEL), slots(SLOTS_S, D_MODEL), slots(SLOTS_P, LANES), slots(SLOTS_S, LANES),
                  pl.BlockSpec((None, None, D_MODEL, FF_TILE), lambda e, f: (layer, e, 0, f)),
                  pl.BlockSpec((None, None, D_MODEL, FF_TILE), lambda e, f: (layer, e, 0, f)),
                  pl.BlockSpec((None, None, FF_TILE, D_MODEL), lambda e, f: (layer, e, f, 0))],
        out_specs=[slots(SLOTS_P, D_MODEL), slots(SLOTS_S, D_MODEL)],
        out_shape=[jax.ShapeDtypeStruct((N_EXPERTS, SLOTS_P, D_MODEL), BF16),
                   jax.ShapeDtypeStruct((N_EXPERTS, SLOTS_S, D_MODEL), BF16)],
        scratch_shapes=[pltpu.VMEM((SLOTS_P + SLOTS_S, D_MODEL), F32)],
        compiler_params=_params(2),
        name="expert_ffn",
    )(xgp, xgs, gp, gs, w1, w3, w2)


def _combine_prompt_kernel(posm_ref, y_ref, x_ref, mod_ref, exp_ref, lg_ref, lb_ref, o_ref):
    m = mod_ref[...]
    slot = (lax.broadcasted_iota(I32, (SEQ, N_EXPERTS * CAP_P), 1) & (CAP_P - 1)).astype(F32)
    def scatter(r):
        tokens = slice(r * SEQ, (r + 1) * SEQ)
        spread = _dot(posm_ref[tokens, :].astype(BF16), exp_ref[...])
        onehot = jnp.where(spread == slot, 1.0, 0.0).astype(BF16)
        y = jnp.concatenate([y_ref[e, r * CAP_P:(r + 1) * CAP_P, :] for e in range(N_EXPERTS)], axis=0)
        return _dot(onehot, y)

    def finish(r, f):
        tokens = slice(r * SEQ, (r + 1) * SEQ)
        o_ref[tokens, :] = _layer_norm(ALPHA * x_ref[tokens, :] + m[5:6] * f, lg_ref[...], lb_ref[...])

    _software_pipeline(range(REQ_TILE), scatter, finish)


def _combine_prompt(posm, y, x1, mod_l, expand, ln_g, ln_b):
    full = lambda a: pl.BlockSpec(a.shape, lambda b: (0,) * a.ndim)
    return pl.pallas_call(
        _combine_prompt_kernel,
        grid=(BATCH // REQ_TILE,),
        in_specs=[pl.BlockSpec((REQ_TILE * SEQ, N_EXPERTS), lambda b: (b, 0)),
                  pl.BlockSpec((N_EXPERTS, REQ_TILE * CAP_P, D_MODEL), lambda b: (0, b, 0)),
                  pl.BlockSpec((REQ_TILE * SEQ, D_MODEL), lambda b: (b, 0)),
                  pl.BlockSpec((None, 6, D_MODEL), lambda b: (0, 0, 0)),
                  full(expand), full(ln_g), full(ln_b)],
        out_specs=pl.BlockSpec((REQ_TILE * SEQ, D_MODEL), lambda b: (b, 0)),
        out_shape=jax.ShapeDtypeStruct((N_PROMPT, D_MODEL), F32),
        compiler_params=_params(1),
        name="combine_prompt",
    )(posm, y, x1, mod_l, expand, ln_g, ln_b)


def _combine_sample_kernel(start_ref, passes_ref, posm_ref, y_ref, x_ref, mod_ref, exp_ref, off_ref,
                           lg_ref, lb_ref, o_ref):
    first_block = (pl.program_id(0) * STEPS_PER_REQ + pl.program_id(1)) * BLOCKS_PER_STEP
    exp_bf16 = exp_ref[...]
    offset = off_ref[...]
    m = mod_ref[...]

    def tokens_of(j):
        return slice(j * TOKEN_BLOCK, (j + 1) * TOKEN_BLOCK)

    def one_pass(j, k):
        spread = _dot(posm_ref[tokens_of(j), :].astype(BF16), exp_bf16)
        begin_row = jnp.zeros((1, ALL_WINDOWS), F32)
        lower_row = jnp.zeros((1, ALL_WINDOWS), F32)
        windows = []
        for e in range(N_EXPERTS):
            lower, begin = _window_bounds(start_ref, first_block + j, e, k)
            lanes_e = exp_bf16[e:e + 1, :].astype(F32)
            begin_row = begin_row + begin.astype(F32) * lanes_e
            lower_row = lower_row + lower.astype(F32) * lanes_e
            windows.append(y_ref[e, pl.ds(begin, WINDOW_SLOTS), :])
        hit = (spread - begin_row == offset) & (spread >= lower_row)
        onehot = jnp.where(hit, 1.0, 0.0).astype(BF16)
        return _dot(onehot, jnp.concatenate(windows, axis=0))

    def finish(j, f):
        rows = tokens_of(j)
        o_ref[rows, :] = _layer_norm(ALPHA * x_ref[rows, :] + m[5:6] * f, lg_ref[...], lb_ref[...])

    _software_pipeline(range(BLOCKS_PER_STEP), lambda j: one_pass(j, 0), finish)

    for j in range(BLOCKS_PER_STEP):
        n_pass = passes_ref[first_block + j]

        @pl.when(n_pass > 1)
        def _(j=j, n_pass=n_pass):
            f = lax.fori_loop(1, n_pass, lambda k, f: f + one_pass(j, k), one_pass(j, 0))
            finish(j, f)


def _combine_sample(posm, y, x1, mod_l, ln_g, ln_b, starts, passes):
    lane = jnp.arange(ALL_WINDOWS)
    expand = ((lane[None, :] // WINDOW_SLOTS) == jnp.arange(N_EXPERTS)[:, None]).astype(BF16)
    offset = (lane % WINDOW_SLOTS).astype(F32).reshape(1, ALL_WINDOWS)
    row = lambda width: pl.BlockSpec((STEP_TOKENS, width), lambda b, t, s, p: (b * STEPS_PER_REQ + t, 0))
    full = lambda a: pl.BlockSpec(a.shape, lambda b, t, s, p: (0,) * a.ndim)
    grid_spec = pltpu.PrefetchScalarGridSpec(
        num_scalar_prefetch=2,
        grid=(DEC_BATCH, STEPS_PER_REQ),
        in_specs=[row(N_EXPERTS),
                  pl.BlockSpec((N_EXPERTS, CAP_S, D_MODEL), lambda b, t, s, p: (0, b, 0)),
                  row(D_MODEL),
                  pl.BlockSpec((None, 6, D_MODEL), lambda b, t, s, p: (1 + b, 0, 0)),
                  full(expand), full(offset), full(ln_g), full(ln_b)],
        out_specs=row(D_MODEL),
    )
    return pl.pallas_call(
        _combine_sample_kernel,
        grid_spec=grid_spec,
        out_shape=jax.ShapeDtypeStruct((N_SAMPLE, D_MODEL), F32),
        compiler_params=_params(2),
        name="combine_sample",
    )(starts, passes, posm, y, x1, mod_l, expand, offset, ln_g, ln_b)


def _moe(prompt, sample, mod_l, w1, w3, w2, layer, ln_g, ln_b, tri, expand):
    x1_p, xm_p, aff_p = prompt
    x1_s, xm_s, aff_s = sample
    posm_p, posr_p, affr_p, _, _ = _route_requests(aff_p, tri, BATCH, SEQ, CAP_P)
    posm_s, posr_s, affr_s, before_s, inside_s = _route_requests(aff_s, tri, DEC_BATCH, DEC_SEQ, CAP_S)
    starts, passes = _window_plan(before_s, inside_s)
    xg_p, g_p = _gather_prompt(xm_p, posr_p, affr_p)
    xg_s, g_s = _gather_sample(xm_s, posr_s, affr_s, starts, passes)
    y_p, y_s = _expert_ffn(xg_p, xg_s, g_p, g_s, w1, w3, w2, layer)
    x2_p = _combine_prompt(posm_p, y_p, x1_p, mod_l, expand, ln_g, ln_b)
    x2_s = _combine_sample(posm_s, y_s, x1_s, mod_l, ln_g, ln_b, starts, passes)
    return x2_p, x2_s


def kernel(x_prompt, x_sample, cache_k, cache_v, state_rglru, c, c_ctx, mod_w, mod_b, ln_mix_g, ln_mix_b, ln_ffn_g, ln_ffn_b, ab_in_w, attn_sink, rnn_conv_w, rnn_conv_b, lru_wa, lru_ba, lru_wx, lru_bx, lru_lambda, ab_out_w, sgu_in_w, sgu_in_b, sgu_ln_g, sgu_ln_b, sgu_spatial_w, sgu_spatial_b, sgu_out_w, router_w, moe_w1, moe_w3, moe_w2):
    xp = x_prompt.reshape(N_PROMPT, D_MODEL)
    xs = x_sample.reshape(N_SAMPLE, D_MODEL)

    cvec = jnp.concatenate([c_ctx[None], c, jnp.zeros((SUBLANES - 1 - DEC_BATCH, D_MODEL), F32)], axis=0)
    mod = _modulation(cvec, mod_w, mod_b).reshape(DEPTH, SUBLANES, 6, D_MODEL)

    idx = jnp.arange(PREFIX_BLOCK)
    tri = (idx[None, :] < idx[:, None]).astype(BF16)
    lane = jnp.arange(N_EXPERTS * CAP_P)
    expand = ((lane[None, :] // CAP_P) == jnp.arange(N_EXPERTS)[:, None]).astype(BF16)
    rope_tables = _rope_tables()
    vec = lambda a: a.reshape(1, -1)
    router_pad = lambda w: jnp.pad(w, ((0, 0), (0, LANES - N_EXPERTS))).astype(BF16)

    new_k = new_v = new_state = None
    for l in range(DEPTH):
        e = l // 2
        mod_l = mod[l]
        rw = router_pad(router_w[l])
        lg, lb = vec(ln_mix_g[l]), vec(ln_mix_b[l])
        if l % 2 == 0:
            w_in = ab_in_w[e].astype(BF16)
            w_out = ab_out_w[e].astype(BF16)
            wa = _block_diag_dense(lru_wa[e])
            wx = _block_diag_dense(lru_wx[e])
            rnn_w = (rnn_conv_w[e], vec(rnn_conv_b[e]), wa, lru_ba[e], wx, lru_bx[e], lru_lambda[e])

            q, k, v, xr, xg, new_k, new_v = _ab_in(xp, mod_l, w_in, 0, N_PROMPT, None)
            att = _ctx_attention(q, k, v, attn_sink[e])
            rows = 8 * SEQ
            zeros = jnp.zeros((N_PROMPT // rows, rows // SEQ, RNN_WIDTH), F32)
            rnn, hf_last, hb_first = _rglru(xr, xg, *rnn_w, zeros, zeros, rows, SEQ)
            new_state = jnp.stack([hf_last.reshape(BATCH, RNN_WIDTH), hb_first.reshape(BATCH, RNN_WIDTH)], axis=1)
            prompt = _mixer_out(_mix_out_kernel, "mix_out", (att, rnn), xp, mod_l, (w_out,),
                                lg, lb, rw, 0, N_PROMPT)

            q, k, v, xr, xg = _ab_in(xs, mod_l, w_in, 1, DEC_SEQ, rope_tables)
            att = _lat_attention(q, k, v, cache_k[:, e].reshape(DEC_BATCH * PAST_LEN, KV_WIDTH),
                                 cache_v[:, e].reshape(DEC_BATCH * PAST_LEN, KV_WIDTH), attn_sink[e])
            h0 = state_rglru[:, e]
            rnn, _, _ = _rglru(xr, xg, *rnn_w, h0[:, 0][:, None, :], h0[:, 1][:, None, :], DEC_SEQ, DEC_SEQ)
            sample = _mixer_out(_mix_out_kernel, "mix_out", (att, rnn), xs, mod_l, (w_out,),
                                lg, lb, rw, 1, DEC_SEQ)
        else:
            w_in = sgu_in_w[e].astype(BF16)
            w_out = sgu_out_w[e].astype(BF16)
            spw = sgu_spatial_w[e].astype(BF16)
            spb = jnp.repeat(sgu_spatial_b[e].T, SGU_GROUP_W, axis=1)
            sgu_args = (w_in, vec(sgu_in_b[e]), vec(sgu_ln_g[e]), vec(sgu_ln_b[e]))
            scratch = (pltpu.VMEM((ROW_TILE, SGU_WIDTH), F32),)
            u, vn = _sgu_in(xp, mod_l, *sgu_args, 0, N_PROMPT)
            prompt = _mixer_out(_sgu_out_kernel, "sgu_out", (u, vn), xp, mod_l, (spw, spb, w_out),
                                lg, lb, rw, 0, N_PROMPT, scratch)
            u, vn = _sgu_in(xs, mod_l, *sgu_args, 1, DEC_SEQ)
            sample = _mixer_out(_sgu_out_kernel, "sgu_out", (u, vn), xs, mod_l, (spw, spb, w_out),
                                lg, lb, rw, 1, DEC_SEQ, scratch)

        xp, xs = _moe(prompt, sample, mod_l, moe_w1, moe_w3, moe_w2, l,
                      vec(ln_ffn_g[l]), vec(ln_ffn_b[l]), tri, expand)

    return (xp.reshape(BATCH, SEQ, D_MODEL),
            xs.reshape(DEC_BATCH, DEC_SEQ, D_MODEL),
            new_k,
            new_v,
            new_state.reshape(BATCH, 1, 2, RNN_WIDTH))
```

```python
import functools

import jax
import jax.numpy as jnp
from jax import lax
from jax.experimental import pallas as pl
from jax.experimental.pallas import tpu as pltpu

F32 = jnp.float32
BF16 = jnp.bfloat16
I32 = jnp.int32

D_MODEL = 1024
BATCH = 32
SEQ = 256
DEPTH = 2
DEC_BATCH = 4
DEC_SEQ = 2048
PAST_LEN = 256
GRID_W = 64
HEAD_DIM = 128
N_Q_HEADS = 4
N_KV_HEADS = 2
Q_PER_KV = N_Q_HEADS // N_KV_HEADS
ATTN_WIDTH = N_Q_HEADS * HEAD_DIM
KV_WIDTH = N_KV_HEADS * HEAD_DIM
WINDOW = 128
BLOCK = 128
ATTN_SCALE = HEAD_DIM ** -0.5
ROPE_BASE = 10000.0
NEG_INF = -1e30
RNN_WIDTH = 512
RNN_BLOCKS = 8
RNN_BLOCK_W = RNN_WIDTH // RNN_BLOCKS
CONV_W = 4
CONV_PAD_LEFT = 2
LRU_C = 8.0
AB_IN_WIDTH = ATTN_WIDTH + 2 * KV_WIDTH + 2 * RNN_WIDTH
CHUNK = 128
SGU_WIDTH = D_MODEL
SGU_GROUPS = 8
SGU_GROUP_W = SGU_WIDTH // SGU_GROUPS
N_EXPERTS = 16
EXPERT_FF = 2048
EC_FACTOR = 2
ALPHA = (2 * DEPTH) ** 0.25
LN_EPS = 1e-6

N_PROMPT = BATCH * SEQ
N_SAMPLE = DEC_BATCH * DEC_SEQ
CAP_P = EC_FACTOR * SEQ // N_EXPERTS
CAP_S = EC_FACTOR * DEC_SEQ // N_EXPERTS
SLOTS_P = BATCH * CAP_P
SLOTS_S = DEC_BATCH * CAP_S

LANES = 128
SUBLANES = 8
ROW_TILE = 1024
REQ_TILE = 4
PREFIX_BLOCK = 256
VMEM_LIMIT = 56 * 1024 * 1024


def _params(n_axes=1):
    return pltpu.CompilerParams(dimension_semantics=("arbitrary",) * n_axes,
                                vmem_limit_bytes=VMEM_LIMIT)


def _layer_norm(x, g, b):
    mu = jnp.mean(x, axis=-1, keepdims=True)
    xc = x - mu
    var = jnp.mean(xc * xc, axis=-1, keepdims=True)
    return xc * lax.rsqrt(var + LN_EPS) * g + b


_LOG2_E = 1.4426950408889634
_GELU_K0 = -2.0 * 0.7978845608028654 * _LOG2_E
_GELU_K1 = _GELU_K0 * 0.044715


def _gelu_tanh(x):
    return x / (1.0 + jnp.exp2(x * (_GELU_K0 + _GELU_K1 * (x * x))))


def _sigmoid(z):
    return 0.5 + 0.5 * jnp.tanh(0.5 * z)


def _dot(a, b):
    return jnp.dot(a, b, preferred_element_type=F32)


def _dot_nt(a, b):
    return lax.dot_general(a, b, (((1,), (1,)), ((), ())), preferred_element_type=F32)


def _software_pipeline(items, first_stage, second_stage):
    staged = first_stage(items[0])
    for i, item in enumerate(items):
        upcoming = first_stage(items[i + 1]) if i + 1 < len(items) else None
        second_stage(item, staged)
        staged = upcoming


def _mod_kernel(c_ref, w_ref, b_ref, o_ref):
    c = c_ref[...]
    s = c * jax.nn.sigmoid(c)
    o_ref[...] = _dot(s.astype(BF16), w_ref[...].astype(BF16)) + b_ref[...]


def _modulation(cvec8, mod_w, mod_b):
    n_col = 6 * D_MODEL // D_MODEL
    return pl.pallas_call(
        _mod_kernel,
        grid=(DEPTH, n_col),
        in_specs=[pl.BlockSpec((SUBLANES, D_MODEL), lambda l, j: (0, 0)),
                  pl.BlockSpec((None, D_MODEL, D_MODEL), lambda l, j: (l, 0, j)),
                  pl.BlockSpec((None, 1, D_MODEL), lambda l, j: (l, 0, j))],
        out_specs=pl.BlockSpec((None, SUBLANES, D_MODEL), lambda l, j: (l, 0, j)),
        out_shape=jax.ShapeDtypeStruct((DEPTH, SUBLANES, 6 * D_MODEL), F32),
        compiler_params=_params(2),
        name="adaln_modulation",
    )(cvec8, mod_w, mod_b.reshape(DEPTH, 1, 6 * D_MODEL))


def _group_map(group0, rows_per_group):
    tiles_per_group = rows_per_group // ROW_TILE
    return lambda i: (group0 + i // tiles_per_group, 0, 0)


def _rope(t, cos, sin_signed):
    lane = lax.broadcasted_iota(I32, t.shape, 1)
    swapped = jnp.where((lane & 1) == 0, pltpu.roll(t, HEAD_DIM - 1, 1), pltpu.roll(t, 1, 1))
    return t * cos + swapped * sin_signed


def _ab_in_kernel(*refs, rope):
    if rope:
        x_ref, mod_ref, w_ref, cos_ref, sin_ref, q_ref, k_ref, v_ref, xr_ref, xg_ref = refs
    else:
        x_ref, mod_ref, w_ref, q_ref, k_ref, v_ref, xr_ref, xg_ref, ck_ref, cv_ref = refs
    m = mod_ref[...]
    h = x_ref[...] * (1.0 + m[1:2]) + m[0:1]
    p = _dot(h.astype(BF16), w_ref[...])
    q = p[:, :ATTN_WIDTH]
    k = p[:, ATTN_WIDTH:ATTN_WIDTH + KV_WIDTH]
    v = p[:, ATTN_WIDTH + KV_WIDTH:ATTN_WIDTH + 2 * KV_WIDTH]
    if rope:
        cos = cos_ref[...]
        sin = sin_ref[...]
        q = jnp.concatenate([_rope(q[:, i * HEAD_DIM:(i + 1) * HEAD_DIM], cos, sin)
                             for i in range(N_Q_HEADS)], axis=1)
        k = jnp.concatenate([_rope(k[:, i * HEAD_DIM:(i + 1) * HEAD_DIM], cos, sin)
                             for i in range(N_KV_HEADS)], axis=1)
    q_ref[...] = q.astype(q_ref.dtype)
    k_ref[...] = k.astype(k_ref.dtype)
    v_ref[...] = v.astype(v_ref.dtype)
    xr_ref[...] = p[:, ATTN_WIDTH + 2 * KV_WIDTH:ATTN_WIDTH + 2 * KV_WIDTH + RNN_WIDTH]
    xg_ref[...] = p[:, ATTN_WIDTH + 2 * KV_WIDTH + RNN_WIDTH:]
    if not rope:
        for r in range(ROW_TILE // SEQ):
            for i in range(N_KV_HEADS):
                ck_ref[r, 0, :, i, :] = k[r * SEQ:(r + 1) * SEQ, _head_cols(i)]
                cv_ref[r, 0, :, i, :] = v[r * SEQ:(r + 1) * SEQ, _head_cols(i)]


def _ab_in(x, mod_l, w_bf16, group0, rows_per_group, rope_tables):
    n = x.shape[0]
    rope = rope_tables is not None
    row = lambda width: pl.BlockSpec((ROW_TILE, width), lambda i: (i, 0))
    in_specs = [row(D_MODEL),
                pl.BlockSpec((None, 6, D_MODEL), _group_map(group0, rows_per_group)),
                pl.BlockSpec((D_MODEL, AB_IN_WIDTH), lambda i: (0, 0))]
    args = [x, mod_l, w_bf16]
    out_specs = [row(ATTN_WIDTH), row(KV_WIDTH), row(KV_WIDTH), row(RNN_WIDTH), row(RNN_WIDTH)]
    out_shape = [jax.ShapeDtypeStruct((n, ATTN_WIDTH), BF16),
                 jax.ShapeDtypeStruct((n, KV_WIDTH), BF16),
                 jax.ShapeDtypeStruct((n, KV_WIDTH), BF16),
                 jax.ShapeDtypeStruct((n, RNN_WIDTH), F32),
                 jax.ShapeDtypeStruct((n, RNN_WIDTH), F32)]
    if rope:
        tiles_per_seq = DEC_SEQ // ROW_TILE
        in_specs += [pl.BlockSpec((ROW_TILE, HEAD_DIM), lambda i: (i % tiles_per_seq, 0))] * 2
        args += list(rope_tables)
    else:
        req = ROW_TILE // SEQ
        cache = pl.BlockSpec((req, 1, SEQ, N_KV_HEADS, HEAD_DIM), lambda i: (i, 0, 0, 0, 0))
        out_specs += [cache, cache]
        out_shape += [jax.ShapeDtypeStruct((n // SEQ, 1, SEQ, N_KV_HEADS, HEAD_DIM), F32)] * 2
    return pl.pallas_call(
        functools.partial(_ab_in_kernel, rope=rope),
        grid=(n // ROW_TILE,),
        in_specs=in_specs,
        out_specs=out_specs,
        out_shape=out_shape,
        compiler_params=_params(1),
        name="ab_in_rope" if rope else "ab_in",
    )(*args)


def _rope_tables():
    rows = DEC_SEQ // GRID_W
    row = jnp.repeat(jnp.arange(rows, dtype=F32), GRID_W)
    col = jnp.tile(jnp.arange(GRID_W, dtype=F32), rows)
    n_freq = HEAD_DIM // 4
    freqs = ROPE_BASE ** (-jnp.arange(n_freq, dtype=F32) / n_freq)
    ang = jnp.concatenate([row[:, None] * freqs, col[:, None] * freqs], axis=-1)
    cos = jnp.repeat(jnp.cos(ang), 2, axis=-1)
    sin = jnp.repeat(jnp.sin(ang), 2, axis=-1)
    sign = jnp.tile(jnp.array([-1.0, 1.0], F32), HEAD_DIM // 2)
    return cos, sin * sign


def _sink_attention_head(s_list, v_list, sink):
    m = sink
    for s in s_list:
        m = jnp.maximum(m, jnp.max(s, axis=-1, keepdims=True))
    p_list = [jnp.exp(s - m) for s in s_list]
    denom = jnp.exp(sink - m)
    for p in p_list:
        denom = denom + jnp.sum(p, axis=-1, keepdims=True)
    out = None
    for p, v in zip(p_list, v_list):
        o = _dot(p.astype(BF16), v)
        out = o if out is None else out + o
    return out * (1.0 / denom)


def _head_cols(h):
    return slice(h * HEAD_DIM, (h + 1) * HEAD_DIM)


def _ctx_attn_kernel(sink_ref, q_ref, k_ref, v_ref, o_ref):
    def scores(item):
        r, h = item
        rows = slice(r * SEQ, (r + 1) * SEQ)
        kh = k_ref[rows, _head_cols(h // Q_PER_KV)].astype(BF16)
        return _dot_nt(q_ref[rows, _head_cols(h)], kh) * ATTN_SCALE

    def finish(item, s):
        r, h = item
        rows = slice(r * SEQ, (r + 1) * SEQ)
        vh = v_ref[rows, _head_cols(h // Q_PER_KV)].astype(BF16)
        o_ref[rows, _head_cols(h)] = _sink_attention_head([s], [vh], sink_ref[h]).astype(o_ref.dtype)

    items = [(r, h) for r in range(REQ_TILE) for h in range(N_Q_HEADS)]
    _software_pipeline(items, scores, finish)


def _ctx_attention(q, k, v, sink):
    seq = lambda width: pl.BlockSpec((REQ_TILE * SEQ, width), lambda b: (b, 0))
    return pl.pallas_call(
        _ctx_attn_kernel,
        grid=(BATCH // REQ_TILE,),
        in_specs=[pl.BlockSpec(memory_space=pltpu.SMEM), seq(ATTN_WIDTH), seq(KV_WIDTH), seq(KV_WIDTH)],
        out_specs=seq(ATTN_WIDTH),
        out_shape=jax.ShapeDtypeStruct((N_PROMPT, ATTN_WIDTH), BF16),
        compiler_params=_params(1),
        name="context_attention",
    )(sink, q, k, v)


LAT_Q = 256


def _lat_attn_kernel(sink_ref, q_ref, kp_ref, kc_ref, kn_ref, vp_ref, vc_ref, vn_ref,
                     kx_ref, vx_ref, o_ref):
    n = pl.program_id(1)
    nb = pl.num_programs(1)
    q = q_ref[...]
    kw = jnp.concatenate([kp_ref[...], kc_ref[...], kn_ref[...]], axis=0)
    vw = jnp.concatenate([vp_ref[...], vc_ref[...], vn_ref[...]], axis=0)
    kx = kx_ref[...].astype(BF16)
    vx = vx_ref[...].astype(BF16)
    n_key = LAT_Q + 2 * WINDOW
    qi = lax.broadcasted_iota(I32, (LAT_Q, n_key), 0)
    kj = lax.broadcasted_iota(I32, (LAT_Q, n_key), 1)
    rel = kj - qi
    valid = (rel >= 0) & (rel <= 2 * WINDOW)
    valid = valid & ((n > 0) | (kj >= WINDOW)) & ((n < nb - 1) | (kj < LAT_Q + WINDOW))
    def scores(h):
        sl = _head_cols(h // Q_PER_KV)
        qh = q[:, _head_cols(h)]
        s_ctx = _dot_nt(qh, kx[:, sl]) * ATTN_SCALE
        s_win = jnp.where(valid, _dot_nt(qh, kw[:, sl]) * ATTN_SCALE, NEG_INF)
        return s_ctx, s_win

    def finish(h, s):
        sl = _head_cols(h // Q_PER_KV)
        o_ref[:, _head_cols(h)] = _sink_attention_head(list(s), [vx[:, sl], vw[:, sl]],
                                                       sink_ref[h]).astype(o_ref.dtype)

    _software_pipeline(range(N_Q_HEADS), scores, finish)


def _lat_attention(q, k, v, k_ctx, v_ctx, sink):
    nb = DEC_SEQ // LAT_Q
    nw = DEC_SEQ // WINDOW
    per = LAT_Q // WINDOW
    cur = lambda b, n: (b * nb + n, 0)
    prev = lambda b, n: (b * nw + jnp.maximum(n * per - 1, 0), 0)
    nxt = lambda b, n: (b * nw + jnp.minimum((n + 1) * per, nw - 1), 0)
    tile = lambda width: pl.BlockSpec((LAT_Q, width), cur)
    edge = lambda imap: pl.BlockSpec((WINDOW, KV_WIDTH), imap)
    ctx = pl.BlockSpec((PAST_LEN, KV_WIDTH), lambda b, n: (b, 0))
    return pl.pallas_call(
        _lat_attn_kernel,
        grid=(DEC_BATCH, nb),
        in_specs=[pl.BlockSpec(memory_space=pltpu.SMEM), tile(ATTN_WIDTH),
                  edge(prev), tile(KV_WIDTH), edge(nxt),
                  edge(prev), tile(KV_WIDTH), edge(nxt), ctx, ctx],
        out_specs=tile(ATTN_WIDTH),
        out_shape=jax.ShapeDtypeStruct((N_SAMPLE, ATTN_WIDTH), BF16),
        compiler_params=_params(2),
        name="latent_attention",
    )(sink, q, k, k, k, v, v, v, k_ctx, v_ctx)


RNN_CHUNK = 256


def _rglru_kernel(xr_ref, xg_ref, cw_ref, cb_ref, wa_ref, ba_ref, wx_ref, bx_ref, lam_ref,
                  h0f_ref, h0b_ref, y_ref, hfl_ref, hbf_ref, xc_s, af_s, ab_s, uf_s, ub_s, *, rows, seq_len):
    n_seq = rows // seq_len
    n_chunk = rows // RNN_CHUNK
    cw = cw_ref[...]
    cb = cb_ref[...]
    zeros_halo = jnp.zeros((SUBLANES, RNN_WIDTH), F32)
    row8 = lax.broadcasted_iota(I32, (SUBLANES, RNN_WIDTH), 0)

    def conv_chunk(c):
        r0 = c * RNN_CHUNK
        first = r0 % seq_len == 0
        last = (r0 + RNN_CHUNK) % seq_len == 0
        before = zeros_halo if first else xr_ref[r0 - SUBLANES:r0, :]
        after = zeros_halo if last else xr_ref[r0 + RNN_CHUNK:r0 + RNN_CHUNK + SUBLANES, :]
        win = jnp.concatenate([before, xr_ref[r0:r0 + RNN_CHUNK, :], after], axis=0)
        xc = cb
        n_win = RNN_CHUNK + 2 * SUBLANES
        for i in range(CONV_W):
            shift = (CONV_PAD_LEFT - i) % n_win
            rolled = win if shift == 0 else pltpu.roll(win, shift, 0)
            xc = xc + rolled[SUBLANES:SUBLANES + RNN_CHUNK, :] * cw[i:i + 1, :]
        return xc

    def group_scan(a, u, reverse):
        for k in (1, 2, 4):
            if reverse:
                shift, ok = SUBLANES - k, row8 < SUBLANES - k
            else:
                shift, ok = k, row8 >= k
            a_nb = jnp.where(ok, pltpu.roll(a, shift, 0), 1.0)
            u_nb = jnp.where(ok, pltpu.roll(u, shift, 0), 0.0)
            u = a * u_nb + u
            a = a * a_nb
        return a, u

    for c in range(n_chunk):
        xc_s[c * RNN_CHUNK:(c + 1) * RNN_CHUNK, :] = conv_chunk(c)

    for d, (a_s, u_s) in enumerate(((af_s, uf_s), (ab_s, ub_s))):
        neg = -lam_ref[d:d + 1, :]
        softplus = jnp.maximum(neg, 0.0) + jnp.log1p(jnp.exp(-jnp.abs(neg)))
        decay = -LRU_C * softplus
        wa = wa_ref[d]
        wx = wx_ref[d]
        ba = ba_ref[d:d + 1, :]
        bx = bx_ref[d:d + 1, :]
        for c in range(n_chunk):
            xc = xc_s[c * RNN_CHUNK:(c + 1) * RNN_CHUNK, :]
            xcb = xc.astype(BF16)
            r = _sigmoid(_dot(xcb, wa) + ba)
            i = _sigmoid(_dot(xcb, wx) + bx)
            log_a = r * decay
            a = jnp.exp(log_a)
            a_s[c * RNN_CHUNK:(c + 1) * RNN_CHUNK, :] = a
            one_minus_a2 = -jnp.tanh(log_a) * (a * a + 1.0)
            u_s[c * RNN_CHUNK:(c + 1) * RNN_CHUNK, :] = jnp.sqrt(one_minus_a2) * (i * xc)

    n_group = seq_len // SUBLANES

    def body(g, carries):
        fwd, bwd = carries
        new_f, new_b = [], []
        for s in range(n_seq):
            rf = pl.multiple_of(s * seq_len + g * SUBLANES, SUBLANES)
            a, u = group_scan(af_s[pl.ds(rf, SUBLANES), :], uf_s[pl.ds(rf, SUBLANES), :], False)
            h = u + a * fwd[s]
            uf_s[pl.ds(rf, SUBLANES), :] = h
            new_f.append(h[SUBLANES - 1:SUBLANES, :])
            rb = pl.multiple_of(s * seq_len + (n_group - 1 - g) * SUBLANES, SUBLANES)
            a, u = group_scan(ab_s[pl.ds(rb, SUBLANES), :], ub_s[pl.ds(rb, SUBLANES), :], True)
            h = u + a * bwd[s]
            ub_s[pl.ds(rb, SUBLANES), :] = h
            new_b.append(h[0:1, :])
        return tuple(new_f), tuple(new_b)

    init = (tuple(h0f_ref[s:s + 1, :] for s in range(n_seq)), tuple(h0b_ref[s:s + 1, :] for s in range(n_seq)))
    last_f, first_b = lax.fori_loop(0, n_group, body, init)

    for s in range(n_seq):
        hfl_ref[s:s + 1, :] = last_f[s]
        hbf_ref[s:s + 1, :] = first_b[s]
    for c in range(n_chunk):
        sl = slice(c * RNN_CHUNK, (c + 1) * RNN_CHUNK)
        y_ref[sl, :] = ((uf_s[sl, :] + ub_s[sl, :]) * _gelu_tanh(xg_ref[sl, :])).astype(y_ref.dtype)


def _rglru(xr, xg, conv_w, conv_b, wa, ba, wx, bx, lam, h0f, h0b, rows, seq_len):
    n = xr.shape[0]
    n_seq = rows // seq_len
    row = pl.BlockSpec((rows, RNN_WIDTH), lambda i: (i, 0))
    full = lambda shape: pl.BlockSpec(shape, lambda i: (0,) * len(shape))
    state = pl.BlockSpec((None, n_seq, RNN_WIDTH), lambda i: (i, 0, 0))
    state_shape = jax.ShapeDtypeStruct((n // rows, n_seq, RNN_WIDTH), F32)
    return pl.pallas_call(
        functools.partial(_rglru_kernel, rows=rows, seq_len=seq_len),
        grid=(n // rows,),
        in_specs=[row, row, full((CONV_W, RNN_WIDTH)), full((1, RNN_WIDTH)),
                  full((2, RNN_WIDTH, RNN_WIDTH)), full((2, RNN_WIDTH)),
                  full((2, RNN_WIDTH, RNN_WIDTH)), full((2, RNN_WIDTH)), full((2, RNN_WIDTH)),
                  state, state],
        out_specs=[row, state, state],
        out_shape=[jax.ShapeDtypeStruct((n, RNN_WIDTH), BF16), state_shape, state_shape],
        scratch_shapes=[pltpu.VMEM((rows, RNN_WIDTH), F32)] * 5,
        compiler_params=_params(1),
        name="rglru_%d" % seq_len,
    )(xr, xg, conv_w, conv_b, wa, ba, wx, bx, lam, h0f, h0b)


def _block_diag_dense(w):
    eye = jnp.eye(RNN_BLOCKS, dtype=w.dtype)
    dense = w[:, :, :, None, :] * eye[None, :, None, :, None]
    return dense.reshape(2, RNN_WIDTH, RNN_WIDTH).astype(BF16)


SUB_TILE = 256
SUB_TILES = tuple(slice(s, s + SUB_TILE) for s in range(0, ROW_TILE, SUB_TILE))


def _residual_router(rows, x, o, m, lg_ref, lb_ref, rw_ref, x1_ref, xm_ref, aff_ref):
    x1 = _layer_norm(ALPHA * x + m[2:3] * o, lg_ref[...], lb_ref[...])
    x1_ref[rows, :] = x1
    xm = (x1 * (1.0 + m[4:5]) + m[3:4]).astype(BF16)
    xm_ref[rows, :] = xm
    lgt = _dot(xm, rw_ref[...])
    lane = lax.broadcasted_iota(I32, lgt.shape, 1)
    lgt = jnp.where(lane < N_EXPERTS, lgt, NEG_INF)
    ex = jnp.exp(lgt - jnp.max(lgt, axis=-1, keepdims=True))
    aff = ex / jnp.sum(ex, axis=-1, keepdims=True)
    aff_ref[rows, :] = aff[:, :N_EXPERTS]


def _mix_out_kernel(att_ref, rnn_ref, x_ref, mod_ref, w_ref, lg_ref, lb_ref, rw_ref,
                    x1_ref, xm_ref, aff_ref):
    m = mod_ref[...]

    def project(rows):
        return _dot(att_ref[rows, :], w_ref[:ATTN_WIDTH, :]) + _dot(rnn_ref[rows, :], w_ref[ATTN_WIDTH:, :])

    def finish(rows, o):
        _residual_router(rows, x_ref[rows, :], o, m, lg_ref, lb_ref, rw_ref, x1_ref, xm_ref, aff_ref)

    _software_pipeline(SUB_TILES, project, finish)


def _sgu_out_kernel(u_ref, v_ref, x_ref, mod_ref, spw_ref, spb_ref, w_ref, lg_ref, lb_ref, rw_ref,
                    x1_ref, xm_ref, aff_ref, mixed_s):
    m = mod_ref[...]

    def gate_and_project(rows):
        for c in range(rows.start, rows.stop, CHUNK):
            for g in range(SGU_GROUPS):
                cols = slice(g * SGU_GROUP_W, (g + 1) * SGU_GROUP_W)
                mixed_s[c:c + CHUNK, cols] = _dot(spw_ref[g], v_ref[c:c + CHUNK, cols]) + spb_ref[:, cols]
        gated = (u_ref[rows, :] * mixed_s[rows, :]).astype(BF16)
        return _dot(gated, w_ref[...])

    def finish(rows, o):
        _residual_router(rows, x_ref[rows, :], o, m, lg_ref, lb_ref, rw_ref, x1_ref, xm_ref, aff_ref)

    _software_pipeline(SUB_TILES, gate_and_project, finish)


def _mixer_out(kernel, name, acts, x, mod_l, weights, ln_g, ln_b, router_w, group0, rows_per_group,
               scratch=()):
    n = x.shape[0]
    row = lambda width: pl.BlockSpec((ROW_TILE, width), lambda i: (i, 0))
    full = lambda a: pl.BlockSpec(a.shape, lambda i: (0,) * a.ndim)
    return pl.pallas_call(
        kernel,
        grid=(n // ROW_TILE,),
        in_specs=([row(a.shape[1]) for a in acts]
                  + [row(D_MODEL), pl.BlockSpec((None, 6, D_MODEL), _group_map(group0, rows_per_group))]
                  + [full(w) for w in weights] + [full(ln_g), full(ln_b), full(router_w)]),
        out_specs=[row(D_MODEL), row(D_MODEL), row(N_EXPERTS)],
        out_shape=[jax.ShapeDtypeStruct((n, D_MODEL), F32),
                   jax.ShapeDtypeStruct((n, D_MODEL), BF16),
                   jax.ShapeDtypeStruct((n, N_EXPERTS), F32)],
        scratch_shapes=list(scratch),
        compiler_params=_params(1),
        name=name,
    )(*acts, x, mod_l, *weights, ln_g, ln_b, router_w)


def _sgu_in_kernel(x_ref, mod_ref, w_ref, b_ref, lg_ref, lb_ref, u_ref, v_ref):
    m = mod_ref[...]

    def project(rows):
        h = x_ref[rows, :] * (1.0 + m[1:2]) + m[0:1]
        return _dot(h.astype(BF16), w_ref[...])

    def finish(rows, p):
        p = _gelu_tanh(p + b_ref[...])
        u_ref[rows, :] = p[:, :SGU_WIDTH]
        v_ref[rows, :] = _layer_norm(p[:, SGU_WIDTH:], lg_ref[...], lb_ref[...]).astype(v_ref.dtype)

    _software_pipeline(SUB_TILES, project, finish)


def _sgu_in(x, mod_l, w_bf16, b, ln_g, ln_b, group0, rows_per_group):
    n = x.shape[0]
    row = lambda width: pl.BlockSpec((ROW_TILE, width), lambda i: (i, 0))
    full = lambda a: pl.BlockSpec(a.shape, lambda i: (0,) * a.ndim)
    return pl.pallas_call(
        _sgu_in_kernel,
        grid=(n // ROW_TILE,),
        in_specs=[row(D_MODEL), pl.BlockSpec((None, 6, D_MODEL), _group_map(group0, rows_per_group)),
                  full(w_bf16), full(b), full(ln_g), full(ln_b)],
        out_specs=[row(SGU_WIDTH), row(SGU_WIDTH)],
        out_shape=[jax.ShapeDtypeStruct((n, SGU_WIDTH), F32),
                   jax.ShapeDtypeStruct((n, SGU_WIDTH), BF16)],
        compiler_params=_params(1),
        name="sgu_in",
    )(x, mod_l, w_bf16, b, ln_g, ln_b)


def _route_kernel(aff_ref, tri_ref, posm_ref, posr_ref, affr_ref, blk_ref, *, tokens, cap):
    aff = aff_ref[...]
    n_col = aff.shape[1]
    n_blk = tokens // PREFIX_BLOCK

    def bisect(_, lo_hi):
        lo, hi = lo_hi
        mid = lo + ((hi - lo) >> 1)
        cnt = jnp.sum(jnp.where(aff >= pltpu.bitcast(mid, F32), 1.0, 0.0), axis=0, keepdims=True)
        ge = cnt >= cap
        return jnp.where(ge, mid, lo), jnp.where(ge, hi, mid)

    lo0 = jnp.zeros((1, n_col), I32)
    hi0 = jnp.full((1, n_col), 0x7F800000, I32)
    thr_bits, _ = lax.fori_loop(0, 31, bisect, (lo0, hi0))
    thr = pltpu.bitcast(thr_bits, F32)
    above = pltpu.bitcast(thr_bits + 1, F32)

    tri = tri_ref[...]

    def excl_prefix(x01):
        outs, before, inside = [], [], []
        off = jnp.zeros((1, n_col), F32)
        for blk in range(n_blk):
            xb = x01[blk * PREFIX_BLOCK:(blk + 1) * PREFIX_BLOCK, :]
            outs.append(_dot(tri, xb.astype(BF16)) + off)
            cnt = jnp.sum(xb, axis=0, keepdims=True)
            before.append(off)
            inside.append(cnt)
            off = off + cnt
        return (outs[0] if n_blk == 1 else jnp.concatenate(outs, axis=0)), before + inside

    gt = jnp.where(aff >= above, 1.0, 0.0)
    eq = jnp.where(aff >= thr, 1.0, 0.0) - gt
    need = cap - jnp.sum(gt, axis=0, keepdims=True)
    eq_rank, _ = excl_prefix(eq)
    sel = gt + jnp.where(eq_rank < need, eq, 0.0)
    pos, blk_rows = excl_prefix(sel)
    posm = jnp.where(sel > 0.0, pos, -1.0)
    posm_ref[...] = posm
    posr_ref[...] = posm.T
    affr_ref[...] = aff.T
    for i, row in enumerate(blk_rows):
        blk_ref[i:i + 1, :] = row


def _route(aff_t, tri, tokens, cap):
    n_col = aff_t.shape[1]
    n_blk = tokens // PREFIX_BLOCK
    whole = lambda shape: pl.BlockSpec(shape, lambda i: (0, 0))
    return pl.pallas_call(
        functools.partial(_route_kernel, tokens=tokens, cap=cap),
        grid=(1,),
        in_specs=[whole((tokens, n_col)), whole((PREFIX_BLOCK, PREFIX_BLOCK))],
        out_specs=[whole((tokens, n_col)), whole((n_col, tokens)), whole((n_col, tokens)),
                   whole((2 * n_blk, n_col))],
        out_shape=[jax.ShapeDtypeStruct((tokens, n_col), F32),
                   jax.ShapeDtypeStruct((n_col, tokens), F32),
                   jax.ShapeDtypeStruct((n_col, tokens), F32),
                   jax.ShapeDtypeStruct((2 * n_blk, n_col), F32)],
        compiler_params=_params(1),
        name="route_%d" % tokens,
    )(aff_t, tri)


def _route_requests(aff, tri, n_req, tokens, cap):
    n_col = n_req * N_EXPERTS
    n_blk = tokens // PREFIX_BLOCK
    aff_t = aff.reshape(n_req, tokens, N_EXPERTS).transpose(1, 0, 2).reshape(tokens, n_col)
    pad = (-n_col) % LANES
    if pad:
        aff_t = jnp.pad(aff_t, ((0, 0), (0, pad)))
    posm, posr, affr, blk = _route(aff_t, tri, tokens, cap)
    posm = posm[:, :n_col].reshape(tokens, n_req, N_EXPERTS).transpose(1, 0, 2)
    blk = blk[:, :n_col].astype(I32).reshape(2, n_blk, n_req, N_EXPERTS).transpose(0, 2, 1, 3)
    expert_major = lambda a: a[:n_col].reshape(n_req, N_EXPERTS, tokens)
    return (posm.reshape(n_req * tokens, N_EXPERTS), expert_major(posr), expert_major(affr), blk[0], blk[1])


def _slot_gate(hit, aff_row):
    g = jnp.sum(jnp.where(hit, aff_row, 0.0), axis=-1, keepdims=True)
    return jnp.broadcast_to(g, (hit.shape[0], LANES))


def _gather_prompt_kernel(x_ref, posr_ref, affr_ref, xg_ref, g_ref):
    slot = lax.broadcasted_iota(I32, (CAP_P, SEQ), 0).astype(F32)
    for r in range(REQ_TILE):
        tokens = slice(r * SEQ, (r + 1) * SEQ)
        slots = slice(r * CAP_P, (r + 1) * CAP_P)
        posr = posr_ref[r]
        affr = affr_ref[r]
        hits = [posr[e:e + 1, :] == slot for e in range(N_EXPERTS)]
        onehot = jnp.concatenate([jnp.where(h, 1.0, 0.0) for h in hits], axis=0).astype(BF16)
        xg = _dot(onehot, x_ref[tokens, :])
        for e in range(N_EXPERTS):
            xg_ref[e, slots, :] = xg[e * CAP_P:(e + 1) * CAP_P, :].astype(xg_ref.dtype)
            g_ref[e, slots, :] = _slot_gate(hits[e], affr[e:e + 1, :])


def _gather_prompt(xm, posr, affr):
    return pl.pallas_call(
        _gather_prompt_kernel,
        grid=(BATCH // REQ_TILE,),
        in_specs=[pl.BlockSpec((REQ_TILE * SEQ, D_MODEL), lambda b: (b, 0)),
                  pl.BlockSpec((REQ_TILE, N_EXPERTS, SEQ), lambda b: (b, 0, 0)),
                  pl.BlockSpec((REQ_TILE, N_EXPERTS, SEQ), lambda b: (b, 0, 0))],
        out_specs=[pl.BlockSpec((N_EXPERTS, REQ_TILE * CAP_P, D_MODEL), lambda b: (0, b, 0)),
                   pl.BlockSpec((N_EXPERTS, REQ_TILE * CAP_P, LANES), lambda b: (0, b, 0))],
        out_shape=[jax.ShapeDtypeStruct((N_EXPERTS, SLOTS_P, D_MODEL), BF16),
                   jax.ShapeDtypeStruct((N_EXPERTS, SLOTS_P, LANES), F32)],
        compiler_params=_params(1),
        name="gather_prompt",
    )(xm, posr, affr)


TOKEN_BLOCK = PREFIX_BLOCK
N_TOKEN_BLOCKS = DEC_SEQ // TOKEN_BLOCK
SLOT_ALIGN = 16
WINDOW_SLOTS = 64
LAST_WINDOW = CAP_S - WINDOW_SLOTS
ALL_WINDOWS = N_EXPERTS * WINDOW_SLOTS


def _window_plan(before, inside):
    start = (before // SLOT_ALIGN) * SLOT_ALIGN
    span = before - start + inside
    passes = jnp.where(inside > 0, (span + WINDOW_SLOTS - 1) // WINDOW_SLOTS, 0)
    return start.reshape(-1), jnp.max(passes, axis=-1).reshape(-1)


def _window_bounds(start_ref, step, expert, k):
    lower = start_ref[step * N_EXPERTS + expert] + WINDOW_SLOTS * k
    begin = pl.multiple_of(jnp.minimum(lower, LAST_WINDOW), SLOT_ALIGN)
    return lower, begin


BLOCKS_PER_STEP = 2
STEP_TOKENS = BLOCKS_PER_STEP * TOKEN_BLOCK
STEPS_PER_REQ = N_TOKEN_BLOCKS // BLOCKS_PER_STEP


def _gather_sample_kernel(start_ref, passes_ref, x_ref, posr_ref, affr_ref, xg_ref, g_ref):
    t = pl.program_id(1)
    first_block = (pl.program_id(0) * STEPS_PER_REQ + t) * BLOCKS_PER_STEP

    @pl.when(t == 0)
    def _():
        xg_ref[...] = jnp.zeros_like(xg_ref)
        g_ref[...] = jnp.zeros_like(g_ref)

    row = lax.broadcasted_iota(I32, (WINDOW_SLOTS, TOKEN_BLOCK), 0).astype(F32)

    def pick(j, k):
        tokens = slice(j * TOKEN_BLOCK, (j + 1) * TOKEN_BLOCK)
        hits, begins = [], []
        for e in range(N_EXPERTS):
            lower, begin = _window_bounds(start_ref, first_block + j, e, k)
            pe = posr_ref[e:e + 1, tokens]
            hits.append((pe - begin.astype(F32) == row) & (pe >= lower.astype(F32)))
            begins.append(begin)
        onehot = jnp.concatenate([jnp.where(h, 1.0, 0.0) for h in hits], axis=0).astype(BF16)
        return _dot(onehot, x_ref[tokens, :]), hits, begins

    def place(j, picked):
        rows_x, hits, begins = picked
        tokens = slice(j * TOKEN_BLOCK, (j + 1) * TOKEN_BLOCK)
        for e in range(N_EXPERTS):
            win = pl.ds(begins[e], WINDOW_SLOTS)
            xg_ref[e, win, :] += rows_x[e * WINDOW_SLOTS:(e + 1) * WINDOW_SLOTS, :].astype(xg_ref.dtype)
            g_ref[e, win, :] += _slot_gate(hits[e], affr_ref[e:e + 1, tokens])

    _software_pipeline(range(BLOCKS_PER_STEP), lambda j: pick(j, 0), place)

    for j in range(BLOCKS_PER_STEP):
        def extra_pass(k, carry, j=j):
            place(j, pick(j, k))
            return carry

        lax.fori_loop(1, passes_ref[first_block + j], extra_pass, 0)


def _gather_sample(xm, posr, affr, starts, passes):
    expert_rows = pl.BlockSpec((None, N_EXPERTS, STEP_TOKENS), lambda b, t, s, p: (b, 0, t))
    grid_spec = pltpu.PrefetchScalarGridSpec(
        num_scalar_prefetch=2,
        grid=(DEC_BATCH, STEPS_PER_REQ),
        in_specs=[pl.BlockSpec((STEP_TOKENS, D_MODEL), lambda b, t, s, p: (b * STEPS_PER_REQ + t, 0)),
                  expert_rows, expert_rows],
        out_specs=[pl.BlockSpec((N_EXPERTS, CAP_S, D_MODEL), lambda b, t, s, p: (0, b, 0)),
                   pl.BlockSpec((N_EXPERTS, CAP_S, LANES), lambda b, t, s, p: (0, b, 0))],
    )
    return pl.pallas_call(
        _gather_sample_kernel,
        grid_spec=grid_spec,
        out_shape=[jax.ShapeDtypeStruct((N_EXPERTS, SLOTS_S, D_MODEL), BF16),
                   jax.ShapeDtypeStruct((N_EXPERTS, SLOTS_S, LANES), F32)],
        compiler_params=_params(2),
        name="gather_sample",
    )(starts, passes, xm, posr, affr)


FF_TILE = 512
SLOT_CHUNK = 512


def _ffn_kernel(xp_ref, xs_ref, gp_ref, gs_ref, w1_ref, w3_ref, w2_ref, yp_ref, ys_ref, acc_ref):
    f = pl.program_id(1)
    n_f = pl.num_programs(1)
    @pl.when(f == 0)
    def _():
        acc_ref[...] = jnp.zeros_like(acc_ref)

    w1 = w1_ref[...].astype(BF16)
    w3 = w3_ref[...].astype(BF16)
    w2 = w2_ref[...].astype(BF16)

    def up_project(chunk):
        x_ref, c, _ = chunk
        x = x_ref[c * SLOT_CHUNK:(c + 1) * SLOT_CHUNK, :]
        return _dot(x, w1), _dot(x, w3)

    def down_project(chunk, h):
        _, c, base = chunk
        h1, h3 = h
        hid = (h1 * jax.nn.sigmoid(h1) * h3).astype(BF16)
        rows = slice(base + c * SLOT_CHUNK, base + (c + 1) * SLOT_CHUNK)
        acc_ref[rows, :] += _dot(hid, w2)

    chunks = [(x_ref, c, base) for x_ref, n_slot, base in ((xp_ref, SLOTS_P, 0), (xs_ref, SLOTS_S, SLOTS_P))
              for c in range(n_slot // SLOT_CHUNK)]
    _software_pipeline(chunks, up_project, down_project)

    @pl.when(f == n_f - 1)
    def _():
        for y_ref, g_ref, n_slot, base in ((yp_ref, gp_ref, SLOTS_P, 0), (ys_ref, gs_ref, SLOTS_S, SLOTS_P)):
            for c in range(n_slot // SLOT_CHUNK):
                rows = slice(c * SLOT_CHUNK, (c + 1) * SLOT_CHUNK)
                gate = jnp.concatenate([g_ref[rows, :]] * (D_MODEL // LANES), axis=1)
                y_ref[rows, :] = (acc_ref[base + c * SLOT_CHUNK:base + (c + 1) * SLOT_CHUNK, :]
                                  * gate).astype(y_ref.dtype)


def _expert_ffn(xgp, xgs, gp, gs, w1, w3, w2, layer):
    slots = lambda n, width: pl.BlockSpec((None, n, width), lambda e, f: (e, 0, 0))
    return pl.pallas_call(
        _ffn_kernel,
        grid=(N_EXPERTS, EXPERT_FF // FF_TILE),
        in_specs=[slots(SLOTS_P, D_MODEL), slots(SLOTS_S, D_MODEL), slots(SLOTS_P, LANES), slots(SLOTS_S, LANES),
                  pl.BlockSpec((None, None, D_MODEL, FF_TILE), lambda e, f: (layer, e, 0, f)),
                  pl.BlockSpec((None, None, D_MODEL, FF_TILE), lambda e, f: (layer, e, 0, f)),
                  pl.BlockSpec((None, None, FF_TILE, D_MODEL), lambda e, f: (layer, e, f, 0))],
        out_specs=[slots(SLOTS_P, D_MODEL), slots(SLOTS_S, D_MODEL)],
        out_shape=[jax.ShapeDtypeStruct((N_EXPERTS, SLOTS_P, D_MODEL), BF16),
                   jax.ShapeDtypeStruct((N_EXPERTS, SLOTS_S, D_MODEL), BF16)],
        scratch_shapes=[pltpu.VMEM((SLOTS_P + SLOTS_S, D_MODEL), F32)],
        compiler_params=_params(2),
        name="expert_ffn",
    )(xgp, xgs, gp, gs, w1, w3, w2)


def _combine_prompt_kernel(posm_ref, y_ref, x_ref, mod_ref, exp_ref, lg_ref, lb_ref, o_ref):
    m = mod_ref[...]
    slot = (lax.broadcasted_iota(I32, (SEQ, N_EXPERTS * CAP_P), 1) & (CAP_P - 1)).astype(F32)
    def scatter(r):
        tokens = slice(r * SEQ, (r + 1) * SEQ)
        spread = _dot(posm_ref[tokens, :].astype(BF16), exp_ref[...])
        onehot = jnp.where(spread == slot, 1.0, 0.0).astype(BF16)
        y = jnp.concatenate([y_ref[e, r * CAP_P:(r + 1) * CAP_P, :] for e in range(N_EXPERTS)], axis=0)
        return _dot(onehot, y)

    def finish(r, f):
        tokens = slice(r * SEQ, (r + 1) * SEQ)
        o_ref[tokens, :] = _layer_norm(ALPHA * x_ref[tokens, :] + m[5:6] * f, lg_ref[...], lb_ref[...])

    _software_pipeline(range(REQ_TILE), scatter, finish)


def _combine_prompt(posm, y, x1, mod_l, expand, ln_g, ln_b):
    full = lambda a: pl.BlockSpec(a.shape, lambda b: (0,) * a.ndim)
    return pl.pallas_call(
        _combine_prompt_kernel,
        grid=(BATCH // REQ_TILE,),
        in_specs=[pl.BlockSpec((REQ_TILE * SEQ, N_EXPERTS), lambda b: (b, 0)),
                  pl.BlockSpec((N_EXPERTS, REQ_TILE * CAP_P, D_MODEL), lambda b: (0, b, 0)),
                  pl.BlockSpec((REQ_TILE * SEQ, D_MODEL), lambda b: (b, 0)),
                  pl.BlockSpec((None, 6, D_MODEL), lambda b: (0, 0, 0)),
                  full(expand), full(ln_g), full(ln_b)],
        out_specs=pl.BlockSpec((REQ_TILE * SEQ, D_MODEL), lambda b: (b, 0)),
        out_shape=jax.ShapeDtypeStruct((N_PROMPT, D_MODEL), F32),
        compiler_params=_params(1),
        name="combine_prompt",
    )(posm, y, x1, mod_l, expand, ln_g, ln_b)


def _combine_sample_kernel(start_ref, passes_ref, posm_ref, y_ref, x_ref, mod_ref, exp_ref, off_ref,
                           lg_ref, lb_ref, o_ref):
    first_block = (pl.program_id(0) * STEPS_PER_REQ + pl.program_id(1)) * BLOCKS_PER_STEP
    exp_bf16 = exp_ref[...]
    offset = off_ref[...]
    m = mod_ref[...]

    def tokens_of(j):
        return slice(j * TOKEN_BLOCK, (j + 1) * TOKEN_BLOCK)

    def one_pass(j, k):
        spread = _dot(posm_ref[tokens_of(j), :].astype(BF16), exp_bf16)
        begin_row = jnp.zeros((1, ALL_WINDOWS), F32)
        lower_row = jnp.zeros((1, ALL_WINDOWS), F32)
        windows = []
        for e in range(N_EXPERTS):
            lower, begin = _window_bounds(start_ref, first_block + j, e, k)
            lanes_e = exp_bf16[e:e + 1, :].astype(F32)
            begin_row = begin_row + begin.astype(F32) * lanes_e
            lower_row = lower_row + lower.astype(F32) * lanes_e
            windows.append(y_ref[e, pl.ds(begin, WINDOW_SLOTS), :])
        hit = (spread - begin_row == offset) & (spread >= lower_row)
        onehot = jnp.where(hit, 1.0, 0.0).astype(BF16)
        return _dot(onehot, jnp.concatenate(windows, axis=0))

    def finish(j, f):
        rows = tokens_of(j)
        o_ref[rows, :] = _layer_norm(ALPHA * x_ref[rows, :] + m[5:6] * f, lg_ref[...], lb_ref[...])

    _software_pipeline(range(BLOCKS_PER_STEP), lambda j: one_pass(j, 0), finish)

    for j in range(BLOCKS_PER_STEP):
        n_pass = passes_ref[first_block + j]

        @pl.when(n_pass > 1)
        def _(j=j, n_pass=n_pass):
            f = lax.fori_loop(1, n_pass, lambda k, f: f + one_pass(j, k), one_pass(j, 0))
            finish(j, f)


def _combine_sample(posm, y, x1, mod_l, ln_g, ln_b, starts, passes):
    lane = jnp.arange(ALL_WINDOWS)
    expand = ((lane[None, :] // WINDOW_SLOTS) == jnp.arange(N_EXPERTS)[:, None]).astype(BF16)
    offset = (lane % WINDOW_SLOTS).astype(F32).reshape(1, ALL_WINDOWS)
    row = lambda width: pl.BlockSpec((STEP_TOKENS, width), lambda b, t, s, p: (b * STEPS_PER_REQ + t, 0))
    full = lambda a: pl.BlockSpec(a.shape, lambda b, t, s, p: (0,) * a.ndim)
    grid_spec = pltpu.PrefetchScalarGridSpec(
        num_scalar_prefetch=2,
        grid=(DEC_BATCH, STEPS_PER_REQ),
        in_specs=[row(N_EXPERTS),
                  pl.BlockSpec((N_EXPERTS, CAP_S, D_MODEL), lambda b, t, s, p: (0, b, 0)),
                  row(D_MODEL),
                  pl.BlockSpec((None, 6, D_MODEL), lambda b, t, s, p: (1 + b, 0, 0)),
                  full(expand), full(offset), full(ln_g), full(ln_b)],
        out_specs=row(D_MODEL),
    )
    return pl.pallas_call(
        _combine_sample_kernel,
        grid_spec=grid_spec,
        out_shape=jax.ShapeDtypeStruct((N_SAMPLE, D_MODEL), F32),
        compiler_params=_params(2),
        name="combine_sample",
    )(starts, passes, posm, y, x1, mod_l, expand, offset, ln_g, ln_b)


def _moe(prompt, sample, mod_l, w1, w3, w2, layer, ln_g, ln_b, tri, expand):
    x1_p, xm_p, aff_p = prompt
    x1_s, xm_s, aff_s = sample
    posm_p, posr_p, affr_p, _, _ = _route_requests(aff_p, tri, BATCH, SEQ, CAP_P)
    posm_s, posr_s, affr_s, before_s, inside_s = _route_requests(aff_s, tri, DEC_BATCH, DEC_SEQ, CAP_S)
    starts, passes = _window_plan(before_s, inside_s)
    xg_p, g_p = _gather_prompt(xm_p, posr_p, affr_p)
    xg_s, g_s = _gather_sample(xm_s, posr_s, affr_s, starts, passes)
    y_p, y_s = _expert_ffn(xg_p, xg_s, g_p, g_s, w1, w3, w2, layer)
    x2_p = _combine_prompt(posm_p, y_p, x1_p, mod_l, expand, ln_g, ln_b)
    x2_s = _combine_sample(posm_s, y_s, x1_s, mod_l, ln_g, ln_b, starts, passes)
    return x2_p, x2_s


def kernel(x_prompt, x_sample, cache_k, cache_v, state_rglru, c, c_ctx, mod_w, mod_b, ln_mix_g, ln_mix_b, ln_ffn_g, ln_ffn_b, ab_in_w, attn_sink, rnn_conv_w, rnn_conv_b, lru_wa, lru_ba, lru_wx, lru_bx, lru_lambda, ab_out_w, sgu_in_w, sgu_in_b, sgu_ln_g, sgu_ln_b, sgu_spatial_w, sgu_spatial_b, sgu_out_w, router_w, moe_w1, moe_w3, moe_w2):
    xp = x_prompt.reshape(N_PROMPT, D_MODEL)
    xs = x_sample.reshape(N_SAMPLE, D_MODEL)

    cvec = jnp.concatenate([c_ctx[None], c, jnp.zeros((SUBLANES - 1 - DEC_BATCH, D_MODEL), F32)], axis=0)
    mod = _modulation(cvec, mod_w, mod_b).reshape(DEPTH, SUBLANES, 6, D_MODEL)

    idx = jnp.arange(PREFIX_BLOCK)
    tri = (idx[None, :] < idx[:, None]).astype(BF16)
    lane = jnp.arange(N_EXPERTS * CAP_P)
    expand = ((lane[None, :] // CAP_P) == jnp.arange(N_EXPERTS)[:, None]).astype(BF16)
    rope_tables = _rope_tables()
    vec = lambda a: a.reshape(1, -1)
    router_pad = lambda w: jnp.pad(w, ((0, 0), (0, LANES - N_EXPERTS))).astype(BF16)

    new_k = new_v = new_state = None
    for l in range(DEPTH):
        e = l // 2
        mod_l = mod[l]
        rw = router_pad(router_w[l])
        lg, lb = vec(ln_mix_g[l]), vec(ln_mix_b[l])
        if l % 2 == 0:
            w_in = ab_in_w[e].astype(BF16)
            w_out = ab_out_w[e].astype(BF16)
            wa = _block_diag_dense(lru_wa[e])
            wx = _block_diag_dense(lru_wx[e])
            rnn_w = (rnn_conv_w[e], vec(rnn_conv_b[e]), wa, lru_ba[e], wx, lru_bx[e], lru_lambda[e])

            q, k, v, xr, xg, new_k, new_v = _ab_in(xp, mod_l, w_in, 0, N_PROMPT, None)
            att = _ctx_attention(q, k, v, attn_sink[e])
            rows = 8 * SEQ
            zeros = jnp.zeros((N_PROMPT // rows, rows // SEQ, RNN_WIDTH), F32)
            rnn, hf_last, hb_first = _rglru(xr, xg, *rnn_w, zeros, zeros, rows, SEQ)
            new_state = jnp.stack([hf_last.reshape(BATCH, RNN_WIDTH), hb_first.reshape(BATCH, RNN_WIDTH)], axis=1)
            prompt = _mixer_out(_mix_out_kernel, "mix_out", (att, rnn), xp, mod_l, (w_out,),
                                lg, lb, rw, 0, N_PROMPT)

            q, k, v, xr, xg = _ab_in(xs, mod_l, w_in, 1, DEC_SEQ, rope_tables)
            att = _lat_attention(q, k, v, cache_k[:, e].reshape(DEC_BATCH * PAST_LEN, KV_WIDTH),
                                 cache_v[:, e].reshape(DEC_BATCH * PAST_LEN, KV_WIDTH), attn_sink[e])
            h0 = state_rglru[:, e]
            rnn, _, _ = _rglru(xr, xg, *rnn_w, h0[:, 0][:, None, :], h0[:, 1][:, None, :], DEC_SEQ, DEC_SEQ)
            sample = _mixer_out(_mix_out_kernel, "mix_out", (att, rnn), xs, mod_l, (w_out,),
                                lg, lb, rw, 1, DEC_SEQ)
        else:
            w_in = sgu_in_w[e].astype(BF16)
            w_out = sgu_out_w[e].astype(BF16)
            spw = sgu_spatial_w[e].astype(BF16)
            spb = jnp.repeat(sgu_spatial_b[e].T, SGU_GROUP_W, axis=1)
            sgu_args = (w_in, vec(sgu_in_b[e]), vec(sgu_ln_g[e]), vec(sgu_ln_b[e]))
            scratch = (pltpu.VMEM((ROW_TILE, SGU_WIDTH), F32),)
            u, vn = _sgu_in(xp, mod_l, *sgu_args, 0, N_PROMPT)
            prompt = _mixer_out(_sgu_out_kernel, "sgu_out", (u, vn), xp, mod_l, (spw, spb, w_out),
                                lg, lb, rw, 0, N_PROMPT, scratch)
            u, vn = _sgu_in(xs, mod_l, *sgu_args, 1, DEC_SEQ)
            sample = _mixer_out(_sgu_out_kernel, "sgu_out", (u, vn), xs, mod_l, (spw, spb, w_out),
                                lg, lb, rw, 1, DEC_SEQ, scratch)

        xp, xs = _moe(prompt, sample, mod_l, moe_w1, moe_w3, moe_w2, l,
                      vec(ln_ffn_g[l]), vec(ln_ffn_b[l]), tri, expand)

    return (xp.reshape(BATCH, SEQ, D_MODEL),
            xs.reshape(DEC_BATCH, DEC_SEQ, D_MODEL),
            new_k,
            new_v,
            new_state.reshape(BATCH, 1, 2, RNN_WIDTH))
```

```python
import functools

import jax
import jax.numpy as jnp
from jax import lax
from jax.experimental import pallas as pl
from jax.experimental.pallas import tpu as pltpu

F32 = jnp.float32
BF16 = jnp.bfloat16
I32 = jnp.int32

D_MODEL = 1024
BATCH = 32
SEQ = 256
DEPTH = 2
DEC_BATCH = 4
DEC_SEQ = 2048
PAST_LEN = 256
GRID_W = 64
HEAD_DIM = 128
N_Q_HEADS = 4
N_KV_HEADS = 2
Q_PER_KV = N_Q_HEADS // N_KV_HEADS
ATTN_WIDTH = N_Q_HEADS * HEAD_DIM
KV_WIDTH = N_KV_HEADS * HEAD_DIM
WINDOW = 128
BLOCK = 128
ATTN_SCALE = HEAD_DIM ** -0.5
ROPE_BASE = 10000.0
NEG_INF = -1e30
RNN_WIDTH = 512
RNN_BLOCKS = 8
RNN_BLOCK_W = RNN_WIDTH // RNN_BLOCKS
CONV_W = 4
CONV_PAD_LEFT = 2
LRU_C = 8.0
AB_IN_WIDTH = ATTN_WIDTH + 2 * KV_WIDTH + 2 * RNN_WIDTH
CHUNK = 128
SGU_WIDTH = D_MODEL
SGU_GROUPS = 8
SGU_GROUP_W = SGU_WIDTH // SGU_GROUPS
N_EXPERTS = 16
EXPERT_FF = 2048
EC_FACTOR = 2
ALPHA = (2 * DEPTH) ** 0.25
LN_EPS = 1e-6

N_PROMPT = BATCH * SEQ
N_SAMPLE = DEC_BATCH * DEC_SEQ
CAP_P = EC_FACTOR * SEQ // N_EXPERTS
CAP_S = EC_FACTOR * DEC_SEQ // N_EXPERTS
SLOTS_P = BATCH * CAP_P
SLOTS_S = DEC_BATCH * CAP_S

LANES = 128
SUBLANES = 8
ROW_TILE = 1024
REQ_TILE = 4
PREFIX_BLOCK = 256
VMEM_LIMIT = 56 * 1024 * 1024


def _params(n_axes=1):
    return pltpu.CompilerParams(dimension_semantics=("arbitrary",) * n_axes,
                                vmem_limit_bytes=VMEM_LIMIT)


def _layer_norm(x, g, b):
    mu = jnp.mean(x, axis=-1, keepdims=True)
    xc = x - mu
    var = jnp.mean(xc * xc, axis=-1, keepdims=True)
    return xc * lax.rsqrt(var + LN_EPS) * g + b


_LOG2_E = 1.4426950408889634
_GELU_K0 = -2.0 * 0.7978845608028654 * _LOG2_E
_GELU_K1 = _GELU_K0 * 0.044715


def _gelu_tanh(x):
    return x / (1.0 + jnp.exp2(x * (_GELU_K0 + _GELU_K1 * (x * x))))


def _sigmoid(z):
    return 0.5 + 0.5 * jnp.tanh(0.5 * z)


def _dot(a, b):
    return jnp.dot(a, b, preferred_element_type=F32)


def _dot_nt(a, b):
    return lax.dot_general(a, b, (((1,), (1,)), ((), ())), preferred_element_type=F32)


def _software_pipeline(items, first_stage, *later_stages):
    items = list(items)
    stages = (first_stage,) + later_stages
    carried = {}
    for t in range(len(items) + len(stages) - 1):
        for s, stage in enumerate(stages):
            i = t - s
            if 0 <= i < len(items):
                carried[i] = stage(items[i]) if s == 0 else stage(items[i], carried[i])


def _mod_kernel(c_ref, w_ref, b_ref, o_ref):
    c = c_ref[...]
    s = c * jax.nn.sigmoid(c)
    o_ref[...] = _dot(s.astype(BF16), w_ref[...].astype(BF16)) + b_ref[...]


def _modulation(cvec8, mod_w, mod_b):
    n_col = 6 * D_MODEL // D_MODEL
    return pl.pallas_call(
        _mod_kernel,
        grid=(DEPTH, n_col),
        in_specs=[pl.BlockSpec((SUBLANES, D_MODEL), lambda l, j: (0, 0)),
                  pl.BlockSpec((None, D_MODEL, D_MODEL), lambda l, j: (l, 0, j)),
                  pl.BlockSpec((None, 1, D_MODEL), lambda l, j: (l, 0, j))],
        out_specs=pl.BlockSpec((None, SUBLANES, D_MODEL), lambda l, j: (l, 0, j)),
        out_shape=jax.ShapeDtypeStruct((DEPTH, SUBLANES, 6 * D_MODEL), F32),
        compiler_params=_params(2),
        name="adaln_modulation",
    )(cvec8, mod_w, mod_b.reshape(DEPTH, 1, 6 * D_MODEL))


def _group_map(group0, rows_per_group):
    tiles_per_group = rows_per_group // ROW_TILE
    return lambda i: (group0 + i // tiles_per_group, 0, 0)


def _rope(t, cos, sin_signed):
    lane = lax.broadcasted_iota(I32, t.shape, 1)
    swapped = jnp.where((lane & 1) == 0, pltpu.roll(t, HEAD_DIM - 1, 1), pltpu.roll(t, 1, 1))
    return t * cos + swapped * sin_signed


def _ab_in_kernel(*refs, rope):
    if rope:
        x_ref, mod_ref, w_ref, cos_ref, sin_ref, q_ref, k_ref, v_ref, xr_ref, xg_ref = refs
    else:
        x_ref, mod_ref, w_ref, q_ref, k_ref, v_ref, xr_ref, xg_ref, ck_ref, cv_ref = refs
    m = mod_ref[...]
    h = x_ref[...] * (1.0 + m[1:2]) + m[0:1]
    p = _dot(h.astype(BF16), w_ref[...])
    q = p[:, :ATTN_WIDTH]
    k = p[:, ATTN_WIDTH:ATTN_WIDTH + KV_WIDTH]
    v = p[:, ATTN_WIDTH + KV_WIDTH:ATTN_WIDTH + 2 * KV_WIDTH]
    if rope:
        cos = cos_ref[...]
        sin = sin_ref[...]
        q = jnp.concatenate([_rope(q[:, i * HEAD_DIM:(i + 1) * HEAD_DIM], cos, sin)
                             for i in range(N_Q_HEADS)], axis=1)
        k = jnp.concatenate([_rope(k[:, i * HEAD_DIM:(i + 1) * HEAD_DIM], cos, sin)
                             for i in range(N_KV_HEADS)], axis=1)
    q_ref[...] = q.astype(q_ref.dtype)
    k_ref[...] = k.astype(k_ref.dtype)
    v_ref[...] = v.astype(v_ref.dtype)
    xr_ref[...] = p[:, ATTN_WIDTH + 2 * KV_WIDTH:ATTN_WIDTH + 2 * KV_WIDTH + RNN_WIDTH]
    xg_ref[...] = p[:, ATTN_WIDTH + 2 * KV_WIDTH + RNN_WIDTH:]
    if not rope:
        for r in range(ROW_TILE // SEQ):
            for i in range(N_KV_HEADS):
                ck_ref[r, 0, :, i, :] = k[r * SEQ:(r + 1) * SEQ, _head_cols(i)]
                cv_ref[r, 0, :, i, :] = v[r * SEQ:(r + 1) * SEQ, _head_cols(i)]


def _ab_in(x, mod_l, w_bf16, group0, rows_per_group, rope_tables):
    n = x.shape[0]
    rope = rope_tables is not None
    row = lambda width: pl.BlockSpec((ROW_TILE, width), lambda i: (i, 0))
    in_specs = [row(D_MODEL),
                pl.BlockSpec((None, 6, D_MODEL), _group_map(group0, rows_per_group)),
                pl.BlockSpec((D_MODEL, AB_IN_WIDTH), lambda i: (0, 0))]
    args = [x, mod_l, w_bf16]
    out_specs = [row(ATTN_WIDTH), row(KV_WIDTH), row(KV_WIDTH), row(RNN_WIDTH), row(RNN_WIDTH)]
    out_shape = [jax.ShapeDtypeStruct((n, ATTN_WIDTH), BF16),
                 jax.ShapeDtypeStruct((n, KV_WIDTH), BF16),
                 jax.ShapeDtypeStruct((n, KV_WIDTH), BF16),
                 jax.ShapeDtypeStruct((n, RNN_WIDTH), F32),
                 jax.ShapeDtypeStruct((n, RNN_WIDTH), F32)]
    if rope:
        tiles_per_seq = DEC_SEQ // ROW_TILE
        in_specs += [pl.BlockSpec((ROW_TILE, HEAD_DIM), lambda i: (i % tiles_per_seq, 0))] * 2
        args += list(rope_tables)
    else:
        req = ROW_TILE // SEQ
        cache = pl.BlockSpec((req, 1, SEQ, N_KV_HEADS, HEAD_DIM), lambda i: (i, 0, 0, 0, 0))
        out_specs += [cache, cache]
        out_shape += [jax.ShapeDtypeStruct((n // SEQ, 1, SEQ, N_KV_HEADS, HEAD_DIM), F32)] * 2
    return pl.pallas_call(
        functools.partial(_ab_in_kernel, rope=rope),
        grid=(n // ROW_TILE,),
        in_specs=in_specs,
        out_specs=out_specs,
        out_shape=out_shape,
        compiler_params=_params(1),
        name="ab_in_rope" if rope else "ab_in",
    )(*args)


def _rope_tables():
    rows = DEC_SEQ // GRID_W
    row = jnp.repeat(jnp.arange(rows, dtype=F32), GRID_W)
    col = jnp.tile(jnp.arange(GRID_W, dtype=F32), rows)
    n_freq = HEAD_DIM // 4
    freqs = ROPE_BASE ** (-jnp.arange(n_freq, dtype=F32) / n_freq)
    ang = jnp.concatenate([row[:, None] * freqs, col[:, None] * freqs], axis=-1)
    cos = jnp.repeat(jnp.cos(ang), 2, axis=-1)
    sin = jnp.repeat(jnp.sin(ang), 2, axis=-1)
    sign = jnp.tile(jnp.array([-1.0, 1.0], F32), HEAD_DIM // 2)
    return cos, sin * sign


def _sink_attention_head(s_list, v_list, sink):
    m = sink
    for s in s_list:
        m = jnp.maximum(m, jnp.max(s, axis=-1, keepdims=True))
    p_list = [jnp.exp(s - m) for s in s_list]
    denom = jnp.exp(sink - m)
    for p in p_list:
        denom = denom + jnp.sum(p, axis=-1, keepdims=True)
    out = None
    for p, v in zip(p_list, v_list):
        o = _dot(p.astype(BF16), v)
        out = o if out is None else out + o
    return out * (1.0 / denom)


def _head_cols(h):
    return slice(h * HEAD_DIM, (h + 1) * HEAD_DIM)


def _ctx_attn_kernel(sink_ref, q_ref, k_ref, v_ref, o_ref):
    def scores(item):
        r, h = item
        rows = slice(r * SEQ, (r + 1) * SEQ)
        kh = k_ref[rows, _head_cols(h // Q_PER_KV)].astype(BF16)
        return _dot_nt(q_ref[rows, _head_cols(h)], kh) * ATTN_SCALE

    def finish(item, s):
        r, h = item
        rows = slice(r * SEQ, (r + 1) * SEQ)
        vh = v_ref[rows, _head_cols(h // Q_PER_KV)].astype(BF16)
        o_ref[rows, _head_cols(h)] = _sink_attention_head([s], [vh], sink_ref[h]).astype(o_ref.dtype)

    items = [(r, h) for r in range(REQ_TILE) for h in range(N_Q_HEADS)]
    _software_pipeline(items, scores, finish)


def _ctx_attention(q, k, v, sink):
    seq = lambda width: pl.BlockSpec((REQ_TILE * SEQ, width), lambda b: (b, 0))
    return pl.pallas_call(
        _ctx_attn_kernel,
        grid=(BATCH // REQ_TILE,),
        in_specs=[pl.BlockSpec(memory_space=pltpu.SMEM), seq(ATTN_WIDTH), seq(KV_WIDTH), seq(KV_WIDTH)],
        out_specs=seq(ATTN_WIDTH),
        out_shape=jax.ShapeDtypeStruct((N_PROMPT, ATTN_WIDTH), BF16),
        compiler_params=_params(1),
        name="context_attention",
    )(sink, q, k, v)


LAT_Q = 256


def _lat_attn_kernel(sink_ref, q_ref, kp_ref, kc_ref, kn_ref, vp_ref, vc_ref, vn_ref,
                     kx_ref, vx_ref, o_ref):
    n = pl.program_id(1)
    nb = pl.num_programs(1)
    q = q_ref[...]
    kw = jnp.concatenate([kp_ref[...], kc_ref[...], kn_ref[...]], axis=0)
    vw = jnp.concatenate([vp_ref[...], vc_ref[...], vn_ref[...]], axis=0)
    kx = kx_ref[...].astype(BF16)
    vx = vx_ref[...].astype(BF16)
    n_key = LAT_Q + 2 * WINDOW
    qi = lax.broadcasted_iota(I32, (LAT_Q, n_key), 0)
    kj = lax.broadcasted_iota(I32, (LAT_Q, n_key), 1)
    rel = kj - qi
    valid = (rel >= 0) & (rel <= 2 * WINDOW)
    valid = valid & ((n > 0) | (kj >= WINDOW)) & ((n < nb - 1) | (kj < LAT_Q + WINDOW))
    def scores(h):
        sl = _head_cols(h // Q_PER_KV)
        qh = q[:, _head_cols(h)]
        s_ctx = _dot_nt(qh, kx[:, sl]) * ATTN_SCALE
        s_win = jnp.where(valid, _dot_nt(qh, kw[:, sl]) * ATTN_SCALE, NEG_INF)
        return s_ctx, s_win

    def finish(h, s):
        sl = _head_cols(h // Q_PER_KV)
        o_ref[:, _head_cols(h)] = _sink_attention_head(list(s), [vx[:, sl], vw[:, sl]],
                                                       sink_ref[h]).astype(o_ref.dtype)

    _software_pipeline(range(N_Q_HEADS), scores, finish)


def _lat_attention(q, k, v, k_ctx, v_ctx, sink):
    nb = DEC_SEQ // LAT_Q
    nw = DEC_SEQ // WINDOW
    per = LAT_Q // WINDOW
    cur = lambda b, n: (b * nb + n, 0)
    prev = lambda b, n: (b * nw + jnp.maximum(n * per - 1, 0), 0)
    nxt = lambda b, n: (b * nw + jnp.minimum((n + 1) * per, nw - 1), 0)
    tile = lambda width: pl.BlockSpec((LAT_Q, width), cur)
    edge = lambda imap: pl.BlockSpec((WINDOW, KV_WIDTH), imap)
    ctx = pl.BlockSpec((PAST_LEN, KV_WIDTH), lambda b, n: (b, 0))
    return pl.pallas_call(
        _lat_attn_kernel,
        grid=(DEC_BATCH, nb),
        in_specs=[pl.BlockSpec(memory_space=pltpu.SMEM), tile(ATTN_WIDTH),
                  edge(prev), tile(KV_WIDTH), edge(nxt),
                  edge(prev), tile(KV_WIDTH), edge(nxt), ctx, ctx],
        out_specs=tile(ATTN_WIDTH),
        out_shape=jax.ShapeDtypeStruct((N_SAMPLE, ATTN_WIDTH), BF16),
        compiler_params=_params(2),
        name="latent_attention",
    )(sink, q, k, k, k, v, v, v, k_ctx, v_ctx)


RNN_CHUNK = 256


def _rglru_kernel(xr_ref, xg_ref, cw_ref, cb_ref, wa_ref, ba_ref, wx_ref, bx_ref, lam_ref,
                  h0f_ref, h0b_ref, y_ref, hfl_ref, hbf_ref, xc_s, af_s, ab_s, uf_s, ub_s, *, rows, seq_len):
    n_seq = rows // seq_len
    n_chunk = rows // RNN_CHUNK
    cw = cw_ref[...]
    cb = cb_ref[...]
    zeros_halo = jnp.zeros((SUBLANES, RNN_WIDTH), F32)
    row8 = lax.broadcasted_iota(I32, (SUBLANES, RNN_WIDTH), 0)

    def conv_chunk(c):
        r0 = c * RNN_CHUNK
        first = r0 % seq_len == 0
        last = (r0 + RNN_CHUNK) % seq_len == 0
        before = zeros_halo if first else xr_ref[r0 - SUBLANES:r0, :]
        after = zeros_halo if last else xr_ref[r0 + RNN_CHUNK:r0 + RNN_CHUNK + SUBLANES, :]
        win = jnp.concatenate([before, xr_ref[r0:r0 + RNN_CHUNK, :], after], axis=0)
        xc = cb
        n_win = RNN_CHUNK + 2 * SUBLANES
        for i in range(CONV_W):
            shift = (CONV_PAD_LEFT - i) % n_win
            rolled = win if shift == 0 else pltpu.roll(win, shift, 0)
            xc = xc + rolled[SUBLANES:SUBLANES + RNN_CHUNK, :] * cw[i:i + 1, :]
        return xc

    def group_scan(a, u, reverse):
        for k in (1, 2, 4):
            if reverse:
                shift, ok = SUBLANES - k, row8 < SUBLANES - k
            else:
                shift, ok = k, row8 >= k
            a_nb = jnp.where(ok, pltpu.roll(a, shift, 0), 1.0)
            u_nb = jnp.where(ok, pltpu.roll(u, shift, 0), 0.0)
            u = a * u_nb + u
            a = a * a_nb
        return a, u

    for c in range(n_chunk):
        xc_s[c * RNN_CHUNK:(c + 1) * RNN_CHUNK, :] = conv_chunk(c)

    for d, (a_s, u_s) in enumerate(((af_s, uf_s), (ab_s, ub_s))):
        neg = -lam_ref[d:d + 1, :]
        softplus = jnp.maximum(neg, 0.0) + jnp.log1p(jnp.exp(-jnp.abs(neg)))
        decay = -LRU_C * softplus
        wa = wa_ref[d]
        wx = wx_ref[d]
        ba = ba_ref[d:d + 1, :]
        bx = bx_ref[d:d + 1, :]
        for c in range(n_chunk):
            xc = xc_s[c * RNN_CHUNK:(c + 1) * RNN_CHUNK, :]
            xcb = xc.astype(BF16)
            r = _sigmoid(_dot(xcb, wa) + ba)
            i = _sigmoid(_dot(xcb, wx) + bx)
            log_a = r * decay
            a = jnp.exp(log_a)
            a_s[c * RNN_CHUNK:(c + 1) * RNN_CHUNK, :] = a
            one_minus_a2 = -jnp.tanh(log_a) * (a * a + 1.0)
            u_s[c * RNN_CHUNK:(c + 1) * RNN_CHUNK, :] = jnp.sqrt(one_minus_a2) * (i * xc)

    n_group = seq_len // SUBLANES

    def body(g, carries):
        fwd, bwd = carries
        new_f, new_b = [], []
        for s in range(n_seq):
            rf = pl.multiple_of(s * seq_len + g * SUBLANES, SUBLANES)
            a, u = group_scan(af_s[pl.ds(rf, SUBLANES), :], uf_s[pl.ds(rf, SUBLANES), :], False)
            h = u + a * fwd[s]
            uf_s[pl.ds(rf, SUBLANES), :] = h
            new_f.append(h[SUBLANES - 1:SUBLANES, :])
            rb = pl.multiple_of(s * seq_len + (n_group - 1 - g) * SUBLANES, SUBLANES)
            a, u = group_scan(ab_s[pl.ds(rb, SUBLANES), :], ub_s[pl.ds(rb, SUBLANES), :], True)
            h = u + a * bwd[s]
            ub_s[pl.ds(rb, SUBLANES), :] = h
            new_b.append(h[0:1, :])
        return tuple(new_f), tuple(new_b)

    init = (tuple(h0f_ref[s:s + 1, :] for s in range(n_seq)), tuple(h0b_ref[s:s + 1, :] for s in range(n_seq)))
    last_f, first_b = lax.fori_loop(0, n_group, body, init)

    for s in range(n_seq):
        hfl_ref[s:s + 1, :] = last_f[s]
        hbf_ref[s:s + 1, :] = first_b[s]
    for c in range(n_chunk):
        sl = slice(c * RNN_CHUNK, (c + 1) * RNN_CHUNK)
        y_ref[sl, :] = ((uf_s[sl, :] + ub_s[sl, :]) * _gelu_tanh(xg_ref[sl, :])).astype(y_ref.dtype)


def _rglru(xr, xg, conv_w, conv_b, wa, ba, wx, bx, lam, h0f, h0b, rows, seq_len):
    n = xr.shape[0]
    n_seq = rows // seq_len
    row = pl.BlockSpec((rows, RNN_WIDTH), lambda i: (i, 0))
    full = lambda shape: pl.BlockSpec(shape, lambda i: (0,) * len(shape))
    state = pl.BlockSpec((None, n_seq, RNN_WIDTH), lambda i: (i, 0, 0))
    state_shape = jax.ShapeDtypeStruct((n // rows, n_seq, RNN_WIDTH), F32)
    return pl.pallas_call(
        functools.partial(_rglru_kernel, rows=rows, seq_len=seq_len),
        grid=(n // rows,),
        in_specs=[row, row, full((CONV_W, RNN_WIDTH)), full((1, RNN_WIDTH)),
                  full((2, RNN_WIDTH, RNN_WIDTH)), full((2, RNN_WIDTH)),
                  full((2, RNN_WIDTH, RNN_WIDTH)), full((2, RNN_WIDTH)), full((2, RNN_WIDTH)),
                  state, state],
        out_specs=[row, state, state],
        out_shape=[jax.ShapeDtypeStruct((n, RNN_WIDTH), BF16), state_shape, state_shape],
        scratch_shapes=[pltpu.VMEM((rows, RNN_WIDTH), F32)] * 5,
        compiler_params=_params(1),
        name="rglru_%d" % seq_len,
    )(xr, xg, conv_w, conv_b, wa, ba, wx, bx, lam, h0f, h0b)


def _block_diag_dense(w):
    eye = jnp.eye(RNN_BLOCKS, dtype=w.dtype)
    dense = w[:, :, :, None, :] * eye[None, :, None, :, None]
    return dense.reshape(2, RNN_WIDTH, RNN_WIDTH).astype(BF16)


SUB_TILE = 256
SUB_TILES = tuple(slice(s, s + SUB_TILE) for s in range(0, ROW_TILE, SUB_TILE))
SGU_TILES = SUB_TILES


def _residual_router(rows, x, o, m, lg_ref, lb_ref, rw_ref, x1_ref, xm_ref, aff_ref):
    x1 = _layer_norm(ALPHA * x + m[2:3] * o, lg_ref[...], lb_ref[...])
    x1_ref[rows, :] = x1
    xm = (x1 * (1.0 + m[4:5]) + m[3:4]).astype(BF16)
    xm_ref[rows, :] = xm
    lgt = _dot(xm, rw_ref[...])
    lane = lax.broadcasted_iota(I32, lgt.shape, 1)
    lgt = jnp.where(lane < N_EXPERTS, lgt, NEG_INF)
    ex = jnp.exp(lgt - jnp.max(lgt, axis=-1, keepdims=True))
    aff = ex / jnp.sum(ex, axis=-1, keepdims=True)
    aff_ref[rows, :] = aff[:, :N_EXPERTS]


def _mix_out_kernel(att_ref, rnn_ref, x_ref, mod_ref, w_ref, lg_ref, lb_ref, rw_ref,
                    x1_ref, xm_ref, aff_ref):
    m = mod_ref[...]

    def project(rows):
        return _dot(att_ref[rows, :], w_ref[:ATTN_WIDTH, :]) + _dot(rnn_ref[rows, :], w_ref[ATTN_WIDTH:, :])

    def finish(rows, o):
        _residual_router(rows, x_ref[rows, :], o, m, lg_ref, lb_ref, rw_ref, x1_ref, xm_ref, aff_ref)

    _software_pipeline(SUB_TILES, project, finish)


def _sgu_kernel(x_ref, mod_ref, win_ref, bin_ref, sg_ref, sb_ref, spw_ref, spb_ref, w_ref, lg_ref, lb_ref, rw_ref,
                x1_ref, xm_ref, aff_ref, v_s, mixed_s):
    m = mod_ref[...]

    def project(rows):
        h = x_ref[rows, :] * (1.0 + m[1:2]) + m[0:1]
        return _dot(h.astype(BF16), win_ref[...])

    def gate(rows, p):
        p = _gelu_tanh(p + bin_ref[...])
        v_s[rows, :] = _layer_norm(p[:, SGU_WIDTH:], sg_ref[...], sb_ref[...]).astype(v_s.dtype)
        for c in range(rows.start, rows.stop, CHUNK):
            for g in range(SGU_GROUPS):
                cols = slice(g * SGU_GROUP_W, (g + 1) * SGU_GROUP_W)
                mixed_s[c:c + CHUNK, cols] = _dot(spw_ref[g], v_s[c:c + CHUNK, cols]) + spb_ref[:, cols]
        gated = (p[:, :SGU_WIDTH] * mixed_s[rows, :]).astype(BF16)
        return _dot(gated, w_ref[...])

    def finish(rows, o):
        _residual_router(rows, x_ref[rows, :], o, m, lg_ref, lb_ref, rw_ref, x1_ref, xm_ref, aff_ref)

    _software_pipeline(SGU_TILES, project, gate, finish)


def _mixer_out(kernel, name, acts, x, mod_l, weights, ln_g, ln_b, router_w, group0, rows_per_group,
               scratch=()):
    n = x.shape[0]
    row = lambda width: pl.BlockSpec((ROW_TILE, width), lambda i: (i, 0))
    full = lambda a: pl.BlockSpec(a.shape, lambda i: (0,) * a.ndim)
    return pl.pallas_call(
        kernel,
        grid=(n // ROW_TILE,),
        in_specs=([row(a.shape[1]) for a in acts]
                  + [row(D_MODEL), pl.BlockSpec((None, 6, D_MODEL), _group_map(group0, rows_per_group))]
                  + [full(w) for w in weights] + [full(ln_g), full(ln_b), full(router_w)]),
        out_specs=[row(D_MODEL), row(D_MODEL), row(N_EXPERTS)],
        out_shape=[jax.ShapeDtypeStruct((n, D_MODEL), F32),
                   jax.ShapeDtypeStruct((n, D_MODEL), BF16),
                   jax.ShapeDtypeStruct((n, N_EXPERTS), F32)],
        scratch_shapes=list(scratch),
        compiler_params=_params(1),
        name=name,
    )(*acts, x, mod_l, *weights, ln_g, ln_b, router_w)


def _route_kernel(aff_ref, tri_ref, posm_ref, posr_ref, affr_ref, blk_ref, *, tokens, cap):
    aff = aff_ref[...]
    n_col = aff.shape[1]
    n_blk = tokens // PREFIX_BLOCK

    def bisect(_, lo_hi):
        lo, hi = lo_hi
        mid = lo + ((hi - lo) >> 1)
        cnt = jnp.sum(jnp.where(aff >= pltpu.bitcast(mid, F32), 1.0, 0.0), axis=0, keepdims=True)
        ge = cnt >= cap
        return jnp.where(ge, mid, lo), jnp.where(ge, hi, mid)

    lo0 = jnp.zeros((1, n_col), I32)
    hi0 = jnp.full((1, n_col), 0x7F800000, I32)
    thr_bits, _ = lax.fori_loop(0, 31, bisect, (lo0, hi0))
    thr = pltpu.bitcast(thr_bits, F32)
    above = pltpu.bitcast(thr_bits + 1, F32)

    tri = tri_ref[...]

    def excl_prefix(x01):
        outs, before, inside = [], [], []
        off = jnp.zeros((1, n_col), F32)
        for blk in range(n_blk):
            xb = x01[blk * PREFIX_BLOCK:(blk + 1) * PREFIX_BLOCK, :]
            outs.append(_dot(tri, xb.astype(BF16)) + off)
            cnt = jnp.sum(xb, axis=0, keepdims=True)
            before.append(off)
            inside.append(cnt)
            off = off + cnt
        return (outs[0] if n_blk == 1 else jnp.concatenate(outs, axis=0)), before + inside

    gt = jnp.where(aff >= above, 1.0, 0.0)
    eq = jnp.where(aff >= thr, 1.0, 0.0) - gt
    need = cap - jnp.sum(gt, axis=0, keepdims=True)
    eq_rank, _ = excl_prefix(eq)
    sel = gt + jnp.where(eq_rank < need, eq, 0.0)
    pos, blk_rows = excl_prefix(sel)
    posm = jnp.where(sel > 0.0, pos, -1.0)
    posm_ref[...] = posm
    posr_ref[...] = posm.T
    affr_ref[...] = aff.T
    for i, row in enumerate(blk_rows):
        blk_ref[i:i + 1, :] = row


def _route(aff_t, tri, tokens, cap):
    n_col = aff_t.shape[1]
    n_blk = tokens // PREFIX_BLOCK
    whole = lambda shape: pl.BlockSpec(shape, lambda i: (0, 0))
    return pl.pallas_call(
        functools.partial(_route_kernel, tokens=tokens, cap=cap),
        grid=(1,),
        in_specs=[whole((tokens, n_col)), whole((PREFIX_BLOCK, PREFIX_BLOCK))],
        out_specs=[whole((tokens, n_col)), whole((n_col, tokens)), whole((n_col, tokens)),
                   whole((2 * n_blk, n_col))],
        out_shape=[jax.ShapeDtypeStruct((tokens, n_col), F32),
                   jax.ShapeDtypeStruct((n_col, tokens), F32),
                   jax.ShapeDtypeStruct((n_col, tokens), F32),
                   jax.ShapeDtypeStruct((2 * n_blk, n_col), F32)],
        compiler_params=_params(1),
        name="route_%d" % tokens,
    )(aff_t, tri)


def _route_requests(aff, tri, n_req, tokens, cap):
    n_col = n_req * N_EXPERTS
    n_blk = tokens // PREFIX_BLOCK
    aff_t = aff.reshape(n_req, tokens, N_EXPERTS).transpose(1, 0, 2).reshape(tokens, n_col)
    pad = (-n_col) % LANES
    if pad:
        aff_t = jnp.pad(aff_t, ((0, 0), (0, pad)))
    posm, posr, affr, blk = _route(aff_t, tri, tokens, cap)
    posm = posm[:, :n_col].reshape(tokens, n_req, N_EXPERTS).transpose(1, 0, 2)
    blk = blk[:, :n_col].astype(I32).reshape(2, n_blk, n_req, N_EXPERTS).transpose(0, 2, 1, 3)
    expert_major = lambda a: a[:n_col].reshape(n_req, N_EXPERTS, tokens)
    return (posm.reshape(n_req * tokens, N_EXPERTS), expert_major(posr), expert_major(affr), blk[0], blk[1])


def _slot_gate(hit, aff_row):
    g = jnp.sum(jnp.where(hit, aff_row, 0.0), axis=-1, keepdims=True)
    return jnp.broadcast_to(g, (hit.shape[0], LANES))


def _gather_prompt_kernel(x_ref, posr_ref, affr_ref, xg_ref, g_ref):
    slot = lax.broadcasted_iota(I32, (CAP_P, SEQ), 0).astype(F32)
    for r in range(REQ_TILE):
        tokens = slice(r * SEQ, (r + 1) * SEQ)
        slots = slice(r * CAP_P, (r + 1) * CAP_P)
        posr = posr_ref[r]
        affr = affr_ref[r]
        hits = [posr[e:e + 1, :] == slot for e in range(N_EXPERTS)]
        onehot = jnp.concatenate([jnp.where(h, 1.0, 0.0) for h in hits], axis=0).astype(BF16)
        xg = _dot(onehot, x_ref[tokens, :])
        for e in range(N_EXPERTS):
            xg_ref[e, slots, :] = xg[e * CAP_P:(e + 1) * CAP_P, :].astype(xg_ref.dtype)
            g_ref[e, slots, :] = _slot_gate(hits[e], affr[e:e + 1, :])


def _gather_prompt(xm, posr, affr):
    return pl.pallas_call(
        _gather_prompt_kernel,
        grid=(BATCH // REQ_TILE,),
        in_specs=[pl.BlockSpec((REQ_TILE * SEQ, D_MODEL), lambda b: (b, 0)),
                  pl.BlockSpec((REQ_TILE, N_EXPERTS, SEQ), lambda b: (b, 0, 0)),
                  pl.BlockSpec((REQ_TILE, N_EXPERTS, SEQ), lambda b: (b, 0, 0))],
        out_specs=[pl.BlockSpec((N_EXPERTS, REQ_TILE * CAP_P, D_MODEL), lambda b: (0, b, 0)),
                   pl.BlockSpec((N_EXPERTS, REQ_TILE * CAP_P, LANES), lambda b: (0, b, 0))],
        out_shape=[jax.ShapeDtypeStruct((N_EXPERTS, SLOTS_P, D_MODEL), BF16),
                   jax.ShapeDtypeStruct((N_EXPERTS, SLOTS_P, LANES), F32)],
        compiler_params=_params(1),
        name="gather_prompt",
    )(xm, posr, affr)


TOKEN_BLOCK = PREFIX_BLOCK
N_TOKEN_BLOCKS = DEC_SEQ // TOKEN_BLOCK
SLOT_ALIGN = 16
WINDOW_SLOTS = 64
LAST_WINDOW = CAP_S - WINDOW_SLOTS
ALL_WINDOWS = N_EXPERTS * WINDOW_SLOTS


def _window_plan(before, inside):
    start = (before // SLOT_ALIGN) * SLOT_ALIGN
    span = before - start + inside
    passes = jnp.where(inside > 0, (span + WINDOW_SLOTS - 1) // WINDOW_SLOTS, 0)
    return start.reshape(-1), jnp.max(passes, axis=-1).reshape(-1)


def _window_bounds(start_ref, step, expert, k):
    lower = start_ref[step * N_EXPERTS + expert] + WINDOW_SLOTS * k
    begin = pl.multiple_of(jnp.minimum(lower, LAST_WINDOW), SLOT_ALIGN)
    return lower, begin


BLOCKS_PER_STEP = 2
STEP_TOKENS = BLOCKS_PER_STEP * TOKEN_BLOCK
STEPS_PER_REQ = N_TOKEN_BLOCKS // BLOCKS_PER_STEP


def _gather_sample_kernel(start_ref, passes_ref, x_ref, posr_ref, affr_ref, xg_ref, g_ref):
    t = pl.program_id(1)
    first_block = (pl.program_id(0) * STEPS_PER_REQ + t) * BLOCKS_PER_STEP

    @pl.when(t == 0)
    def _():
        xg_ref[...] = jnp.zeros_like(xg_ref)
        g_ref[...] = jnp.zeros_like(g_ref)

    row = lax.broadcasted_iota(I32, (WINDOW_SLOTS, TOKEN_BLOCK), 0).astype(F32)

    def pick(j, k):
        tokens = slice(j * TOKEN_BLOCK, (j + 1) * TOKEN_BLOCK)
        hits, begins = [], []
        for e in range(N_EXPERTS):
            lower, begin = _window_bounds(start_ref, first_block + j, e, k)
            pe = posr_ref[e:e + 1, tokens]
            hits.append((pe - begin.astype(F32) == row) & (pe >= lower.astype(F32)))
            begins.append(begin)
        onehot = jnp.concatenate([jnp.where(h, 1.0, 0.0) for h in hits], axis=0).astype(BF16)
        return _dot(onehot, x_ref[tokens, :]), hits, begins

    def place(j, picked):
        rows_x, hits, begins = picked
        tokens = slice(j * TOKEN_BLOCK, (j + 1) * TOKEN_BLOCK)
        for e in range(N_EXPERTS):
            win = pl.ds(begins[e], WINDOW_SLOTS)
            xg_ref[e, win, :] += rows_x[e * WINDOW_SLOTS:(e + 1) * WINDOW_SLOTS, :].astype(xg_ref.dtype)
            g_ref[e, win, :] += _slot_gate(hits[e], affr_ref[e:e + 1, tokens])

    _software_pipeline(range(BLOCKS_PER_STEP), lambda j: pick(j, 0), place)

    for j in range(BLOCKS_PER_STEP):
        def extra_pass(k, carry, j=j):
            place(j, pick(j, k))
            return carry

        lax.fori_loop(1, passes_ref[first_block + j], extra_pass, 0)


def _gather_sample(xm, posr, affr, starts, passes):
    expert_rows = pl.BlockSpec((None, N_EXPERTS, STEP_TOKENS), lambda b, t, s, p: (b, 0, t))
    grid_spec = pltpu.PrefetchScalarGridSpec(
        num_scalar_prefetch=2,
        grid=(DEC_BATCH, STEPS_PER_REQ),
        in_specs=[pl.BlockSpec((STEP_TOKENS, D_MODEL), lambda b, t, s, p: (b * STEPS_PER_REQ + t, 0)),
                  expert_rows, expert_rows],
        out_specs=[pl.BlockSpec((N_EXPERTS, CAP_S, D_MODEL), lambda b, t, s, p: (0, b, 0)),
                   pl.BlockSpec((N_EXPERTS, CAP_S, LANES), lambda b, t, s, p: (0, b, 0))],
    )
    return pl.pallas_call(
        _gather_sample_kernel,
        grid_spec=grid_spec,
        out_shape=[jax.ShapeDtypeStruct((N_EXPERTS, SLOTS_S, D_MODEL), BF16),
                   jax.ShapeDtypeStruct((N_EXPERTS, SLOTS_S, LANES), F32)],
        compiler_params=_params(2),
        name="gather_sample",
    )(starts, passes, xm, posr, affr)


FF_TILE = 512
SLOT_CHUNK = 512


def _ffn_kernel(xp_ref, xs_ref, gp_ref, gs_ref, w1_ref, w3_ref, w2_ref, yp_ref, ys_ref, acc_ref):
    f = pl.program_id(1)
    n_f = pl.num_programs(1)
    @pl.when(f == 0)
    def _():
        acc_ref[...] = jnp.zeros_like(acc_ref)

    w1 = w1_ref[...].astype(BF16)
    w3 = w3_ref[...].astype(BF16)
    w2 = w2_ref[...].astype(BF16)

    def up_project(chunk):
        x_ref, c, _ = chunk
        x = x_ref[c * SLOT_CHUNK:(c + 1) * SLOT_CHUNK, :]
        return _dot(x, w1), _dot(x, w3)

    def down_project(chunk, h):
        _, c, base = chunk
        h1, h3 = h
        hid = (h1 * jax.nn.sigmoid(h1) * h3).astype(BF16)
        rows = slice(base + c * SLOT_CHUNK, base + (c + 1) * SLOT_CHUNK)
        acc_ref[rows, :] += _dot(hid, w2)

    chunks = [(x_ref, c, base) for x_ref, n_slot, base in ((xp_ref, SLOTS_P, 0), (xs_ref, SLOTS_S, SLOTS_P))
              for c in range(n_slot // SLOT_CHUNK)]
    _software_pipeline(chunks, up_project, down_project)

    @pl.when(f == n_f - 1)
    def _():
        for y_ref, g_ref, n_slot, base in ((yp_ref, gp_ref, SLOTS_P, 0), (ys_ref, gs_ref, SLOTS_S, SLOTS_P)):
            for c in range(n_slot // SLOT_CHUNK):
                rows = slice(c * SLOT_CHUNK, (c + 1) * SLOT_CHUNK)
                gate = jnp.concatenate([g_ref[rows, :]] * (D_MODEL // LANES), axis=1)
                y_ref[rows, :] = (acc_ref[base + c * SLOT_CHUNK:base + (c + 1) * SLOT_CHUNK, :]
                                  * gate).astype(y_ref.dtype)


def _expert_ffn(xgp, xgs, gp, gs, w1, w3, w2, layer):
    slots = lambda n, width: pl.BlockSpec((None, n, width), lambda e, f: (e, 0, 0))
    return pl.pallas_call(
        _ffn_kernel,
        grid=(N_EXPERTS, EXPERT_FF // FF_TILE),
        in_specs=[slots(SLOTS_P, D_MODEL), slots(SLOTS_S, D_MODEL), slots(SLOTS_P, LANES), slots(SLOTS_S, LANES),
                  pl.BlockSpec((None, None, D_MODEL, FF_TILE), lambda e, f: (layer, e, 0, f)),
                  pl.BlockSpec((None, None, D_MODEL, FF_TILE), lambda e, f: (layer, e, 0, f)),
                  pl.BlockSpec((None, None, FF_TILE, D_MODEL), lambda e, f: (layer, e, f, 0))],
        out_specs=[slots(SLOTS_P, D_MODEL), slots(SLOTS_S, D_MODEL)],
        out_shape=[jax.ShapeDtypeStruct((N_EXPERTS, SLOTS_P, D_MODEL), BF16),
                   jax.ShapeDtypeStruct((N_EXPERTS, SLOTS_S, D_MODEL), BF16)],
        scratch_shapes=[pltpu.VMEM((SLOTS_P + SLOTS_S, D_MODEL), F32)],
        compiler_params=_params(2),
        name="expert_ffn",
    )(xgp, xgs, gp, gs, w1, w3, w2)


def _combine_prompt_kernel(posm_ref, y_ref, x_ref, mod_ref, exp_ref, lg_ref, lb_ref, o_ref):
    m = mod_ref[...]
    slot = (lax.broadcasted_iota(I32, (SEQ, N_EXPERTS * CAP_P), 1) & (CAP_P - 1)).astype(F32)
    def scatter(r):
        tokens = slice(r * SEQ, (r + 1) * SEQ)
        spread = _dot(posm_ref[tokens, :].astype(BF16), exp_ref[...])
        onehot = jnp.where(spread == slot, 1.0, 0.0).astype(BF16)
        y = jnp.concatenate([y_ref[e, r * CAP_P:(r + 1) * CAP_P, :] for e in range(N_EXPERTS)], axis=0)
        return _dot(onehot, y)

    def finish(r, f):
        tokens = slice(r * SEQ, (r + 1) * SEQ)
        o_ref[tokens, :] = _layer_norm(ALPHA * x_ref[tokens, :] + m[5:6] * f, lg_ref[...], lb_ref[...])

    _software_pipeline(range(REQ_TILE), scatter, finish)


def _combine_prompt(posm, y, x1, mod_l, expand, ln_g, ln_b):
    full = lambda a: pl.BlockSpec(a.shape, lambda b: (0,) * a.ndim)
    return pl.pallas_call(
        _combine_prompt_kernel,
        grid=(BATCH // REQ_TILE,),
        in_specs=[pl.BlockSpec((REQ_TILE * SEQ, N_EXPERTS), lambda b: (b, 0)),
                  pl.BlockSpec((N_EXPERTS, REQ_TILE * CAP_P, D_MODEL), lambda b: (0, b, 0)),
                  pl.BlockSpec((REQ_TILE * SEQ, D_MODEL), lambda b: (b, 0)),
                  pl.BlockSpec((None, 6, D_MODEL), lambda b: (0, 0, 0)),
                  full(expand), full(ln_g), full(ln_b)],
        out_specs=pl.BlockSpec((REQ_TILE * SEQ, D_MODEL), lambda b: (b, 0)),
        out_shape=jax.ShapeDtypeStruct((N_PROMPT, D_MODEL), F32),
        compiler_params=_params(1),
        name="combine_prompt",
    )(posm, y, x1, mod_l, expand, ln_g, ln_b)


def _combine_sample_kernel(start_ref, passes_ref, posm_ref, y_ref, x_ref, mod_ref, exp_ref, off_ref,
                           lg_ref, lb_ref, o_ref):
    first_block = (pl.program_id(0) * STEPS_PER_REQ + pl.program_id(1)) * BLOCKS_PER_STEP
    exp_bf16 = exp_ref[...]
    offset = off_ref[...]
    m = mod_ref[...]

    def tokens_of(j):
        return slice(j * TOKEN_BLOCK, (j + 1) * TOKEN_BLOCK)

    def one_pass(j, k):
        spread = _dot(posm_ref[tokens_of(j), :].astype(BF16), exp_bf16)
        begin_row = jnp.zeros((1, ALL_WINDOWS), F32)
        lower_row = jnp.zeros((1, ALL_WINDOWS), F32)
        windows = []
        for e in range(N_EXPERTS):
            lower, begin = _window_bounds(start_ref, first_block + j, e, k)
            lanes_e = exp_bf16[e:e + 1, :].astype(F32)
            begin_row = begin_row + begin.astype(F32) * lanes_e
            lower_row = lower_row + lower.astype(F32) * lanes_e
            windows.append(y_ref[e, pl.ds(begin, WINDOW_SLOTS), :])
        hit = (spread - begin_row == offset) & (spread >= lower_row)
        onehot = jnp.where(hit, 1.0, 0.0).astype(BF16)
        return _dot(onehot, jnp.concatenate(windows, axis=0))

    def finish(j, f):
        rows = tokens_of(j)
        o_ref[rows, :] = _layer_norm(ALPHA * x_ref[rows, :] + m[5:6] * f, lg_ref[...], lb_ref[...])

    _software_pipeline(range(BLOCKS_PER_STEP), lambda j: one_pass(j, 0), finish)

    for j in range(BLOCKS_PER_STEP):
        n_pass = passes_ref[first_block + j]

        @pl.when(n_pass > 1)
        def _(j=j, n_pass=n_pass):
            f = lax.fori_loop(1, n_pass, lambda k, f: f + one_pass(j, k), one_pass(j, 0))
            finish(j, f)


def _combine_sample(posm, y, x1, mod_l, ln_g, ln_b, starts, passes):
    lane = jnp.arange(ALL_WINDOWS)
    expand = ((lane[None, :] // WINDOW_SLOTS) == jnp.arange(N_EXPERTS)[:, None]).astype(BF16)
    offset = (lane % WINDOW_SLOTS).astype(F32).reshape(1, ALL_WINDOWS)
    row = lambda width: pl.BlockSpec((STEP_TOKENS, width), lambda b, t, s, p: (b * STEPS_PER_REQ + t, 0))
    full = lambda a: pl.BlockSpec(a.shape, lambda b, t, s, p: (0,) * a.ndim)
    grid_spec = pltpu.PrefetchScalarGridSpec(
        num_scalar_prefetch=2,
        grid=(DEC_BATCH, STEPS_PER_REQ),
        in_specs=[row(N_EXPERTS),
                  pl.BlockSpec((N_EXPERTS, CAP_S, D_MODEL), lambda b, t, s, p: (0, b, 0)),
                  row(D_MODEL),
                  pl.BlockSpec((None, 6, D_MODEL), lambda b, t, s, p: (1 + b, 0, 0)),
                  full(expand), full(offset), full(ln_g), full(ln_b)],
        out_specs=row(D_MODEL),
    )
    return pl.pallas_call(
        _combine_sample_kernel,
        grid_spec=grid_spec,
        out_shape=jax.ShapeDtypeStruct((N_SAMPLE, D_MODEL), F32),
        compiler_params=_params(2),
        name="combine_sample",
    )(starts, passes, posm, y, x1, mod_l, expand, offset, ln_g, ln_b)


def _moe(prompt, sample, mod_l, w1, w3, w2, layer, ln_g, ln_b, tri, expand):
    x1_p, xm_p, aff_p = prompt
    x1_s, xm_s, aff_s = sample
    posm_p, posr_p, affr_p, _, _ = _route_requests(aff_p, tri, BATCH, SEQ, CAP_P)
    posm_s, posr_s, affr_s, before_s, inside_s = _route_requests(aff_s, tri, DEC_BATCH, DEC_SEQ, CAP_S)
    starts, passes = _window_plan(before_s, inside_s)
    xg_p, g_p = _gather_prompt(xm_p, posr_p, affr_p)
    xg_s, g_s = _gather_sample(xm_s, posr_s, affr_s, starts, passes)
    y_p, y_s = _expert_ffn(xg_p, xg_s, g_p, g_s, w1, w3, w2, layer)
    x2_p = _combine_prompt(posm_p, y_p, x1_p, mod_l, expand, ln_g, ln_b)
    x2_s = _combine_sample(posm_s, y_s, x1_s, mod_l, ln_g, ln_b, starts, passes)
    return x2_p, x2_s


def kernel(x_prompt, x_sample, cache_k, cache_v, state_rglru, c, c_ctx, mod_w, mod_b, ln_mix_g, ln_mix_b, ln_ffn_g, ln_ffn_b, ab_in_w, attn_sink, rnn_conv_w, rnn_conv_b, lru_wa, lru_ba, lru_wx, lru_bx, lru_lambda, ab_out_w, sgu_in_w, sgu_in_b, sgu_ln_g, sgu_ln_b, sgu_spatial_w, sgu_spatial_b, sgu_out_w, router_w, moe_w1, moe_w3, moe_w2):
    xp = x_prompt.reshape(N_PROMPT, D_MODEL)
    xs = x_sample.reshape(N_SAMPLE, D_MODEL)

    cvec = jnp.concatenate([c_ctx[None], c, jnp.zeros((SUBLANES - 1 - DEC_BATCH, D_MODEL), F32)], axis=0)
    mod = _modulation(cvec, mod_w, mod_b).reshape(DEPTH, SUBLANES, 6, D_MODEL)

    idx = jnp.arange(PREFIX_BLOCK)
    tri = (idx[None, :] < idx[:, None]).astype(BF16)
    lane = jnp.arange(N_EXPERTS * CAP_P)
    expand = ((lane[None, :] // CAP_P) == jnp.arange(N_EXPERTS)[:, None]).astype(BF16)
    rope_tables = _rope_tables()
    vec = lambda a: a.reshape(1, -1)
    router_pad = lambda w: jnp.pad(w, ((0, 0), (0, LANES - N_EXPERTS))).astype(BF16)

    new_k = new_v = new_state = None
    for l in range(DEPTH):
        e = l // 2
        mod_l = mod[l]
        rw = router_pad(router_w[l])
        lg, lb = vec(ln_mix_g[l]), vec(ln_mix_b[l])
        if l % 2 == 0:
            w_in = ab_in_w[e].astype(BF16)
            w_out = ab_out_w[e].astype(BF16)
            wa = _block_diag_dense(lru_wa[e])
            wx = _block_diag_dense(lru_wx[e])
            rnn_w = (rnn_conv_w[e], vec(rnn_conv_b[e]), wa, lru_ba[e], wx, lru_bx[e], lru_lambda[e])

            q, k, v, xr, xg, new_k, new_v = _ab_in(xp, mod_l, w_in, 0, N_PROMPT, None)
            att = _ctx_attention(q, k, v, attn_sink[e])
            rows = 8 * SEQ
            zeros = jnp.zeros((N_PROMPT // rows, rows // SEQ, RNN_WIDTH), F32)
            rnn, hf_last, hb_first = _rglru(xr, xg, *rnn_w, zeros, zeros, rows, SEQ)
            new_state = jnp.stack([hf_last.reshape(BATCH, RNN_WIDTH), hb_first.reshape(BATCH, RNN_WIDTH)], axis=1)
            prompt = _mixer_out(_mix_out_kernel, "mix_out", (att, rnn), xp, mod_l, (w_out,),
                                lg, lb, rw, 0, N_PROMPT)

            q, k, v, xr, xg = _ab_in(xs, mod_l, w_in, 1, DEC_SEQ, rope_tables)
            att = _lat_attention(q, k, v, cache_k[:, e].reshape(DEC_BATCH * PAST_LEN, KV_WIDTH),
                                 cache_v[:, e].reshape(DEC_BATCH * PAST_LEN, KV_WIDTH), attn_sink[e])
            h0 = state_rglru[:, e]
            rnn, _, _ = _rglru(xr, xg, *rnn_w, h0[:, 0][:, None, :], h0[:, 1][:, None, :], DEC_SEQ, DEC_SEQ)
            sample = _mixer_out(_mix_out_kernel, "mix_out", (att, rnn), xs, mod_l, (w_out,),
                                lg, lb, rw, 1, DEC_SEQ)
        else:
            w_in = sgu_in_w[e].astype(BF16)
            w_out = sgu_out_w[e].astype(BF16)
            spw = sgu_spatial_w[e].astype(BF16)
            spb = jnp.repeat(sgu_spatial_b[e].T, SGU_GROUP_W, axis=1)
            sgu_w = (w_in, vec(sgu_in_b[e]), vec(sgu_ln_g[e]), vec(sgu_ln_b[e]), spw, spb, w_out)
            scratch = (pltpu.VMEM((ROW_TILE, SGU_WIDTH), BF16), pltpu.VMEM((ROW_TILE, SGU_WIDTH), F32))
            prompt = _mixer_out(_sgu_kernel, "sgu", (), xp, mod_l, sgu_w, lg, lb, rw, 0, N_PROMPT, scratch)
            sample = _mixer_out(_sgu_kernel, "sgu", (), xs, mod_l, sgu_w, lg, lb, rw, 1, DEC_SEQ, scratch)

        xp, xs = _moe(prompt, sample, mod_l, moe_w1, moe_w3, moe_w2, l,
                      vec(ln_ffn_g[l]), vec(ln_ffn_b[l]), tri, expand)

    return (xp.reshape(BATCH, SEQ, D_MODEL),
            xs.reshape(DEC_BATCH, DEC_SEQ, D_MODEL),
            new_k,
            new_v,
            new_state.reshape(BATCH, 1, 2, RNN_WIDTH))
```

```python
import functools

import jax
import jax.numpy as jnp
from jax import lax
from jax.experimental import pallas as pl
from jax.experimental.pallas import tpu as pltpu

F32 = jnp.float32
BF16 = jnp.bfloat16
I32 = jnp.int32

D_MODEL = 1024
BATCH = 32
SEQ = 256
DEPTH = 2
DEC_BATCH = 4
DEC_SEQ = 2048
PAST_LEN = 256
GRID_W = 64
HEAD_DIM = 128
N_Q_HEADS = 4
N_KV_HEADS = 2
Q_PER_KV = N_Q_HEADS // N_KV_HEADS
ATTN_WIDTH = N_Q_HEADS * HEAD_DIM
KV_WIDTH = N_KV_HEADS * HEAD_DIM
WINDOW = 128
BLOCK = 128
ATTN_SCALE = HEAD_DIM ** -0.5
ROPE_BASE = 10000.0
NEG_INF = -1e30
RNN_WIDTH = 512
RNN_BLOCKS = 8
RNN_BLOCK_W = RNN_WIDTH // RNN_BLOCKS
CONV_W = 4
CONV_PAD_LEFT = 2
LRU_C = 8.0
AB_IN_WIDTH = ATTN_WIDTH + 2 * KV_WIDTH + 2 * RNN_WIDTH
CHUNK = 128
SGU_WIDTH = D_MODEL
SGU_GROUPS = 8
SGU_GROUP_W = SGU_WIDTH // SGU_GROUPS
N_EXPERTS = 16
EXPERT_FF = 2048
EC_FACTOR = 2
ALPHA = (2 * DEPTH) ** 0.25
LN_EPS = 1e-6

N_PROMPT = BATCH * SEQ
N_SAMPLE = DEC_BATCH * DEC_SEQ
CAP_P = EC_FACTOR * SEQ // N_EXPERTS
CAP_S = EC_FACTOR * DEC_SEQ // N_EXPERTS
SLOTS_P = BATCH * CAP_P
SLOTS_S = DEC_BATCH * CAP_S

LANES = 128
SUBLANES = 8
ROW_TILE = 1024
REQ_TILE = 4
PREFIX_BLOCK = 256
VMEM_LIMIT = 56 * 1024 * 1024


def _params(n_axes=1):
    return pltpu.CompilerParams(dimension_semantics=("arbitrary",) * n_axes,
                                vmem_limit_bytes=VMEM_LIMIT)


def _layer_norm(x, g, b):
    mu = jnp.mean(x, axis=-1, keepdims=True)
    xc = x - mu
    var = jnp.mean(xc * xc, axis=-1, keepdims=True)
    return xc * lax.rsqrt(var + LN_EPS) * g + b


_LOG2_E = 1.4426950408889634
_GELU_K0 = -2.0 * 0.7978845608028654 * _LOG2_E
_GELU_K1 = _GELU_K0 * 0.044715


def _gelu_tanh(x):
    return x / (1.0 + jnp.exp2(x * (_GELU_K0 + _GELU_K1 * (x * x))))


def _sigmoid(z):
    return 0.5 + 0.5 * jnp.tanh(0.5 * z)


def _dot(a, b):
    return jnp.dot(a, b, preferred_element_type=F32)


def _dot_nt(a, b):
    return lax.dot_general(a, b, (((1,), (1,)), ((), ())), preferred_element_type=F32)


def _software_pipeline(items, first_stage, *later_stages):
    items = list(items)
    stages = (first_stage,) + later_stages
    carried = {}
    for t in range(len(items) + len(stages) - 1):
        for s, stage in enumerate(stages):
            i = t - s
            if 0 <= i < len(items):
                carried[i] = stage(items[i]) if s == 0 else stage(items[i], carried[i])


def _mod_kernel(c_ref, w_ref, b_ref, o_ref):
    c = c_ref[...]
    s = c * jax.nn.sigmoid(c)
    o_ref[...] = _dot(s.astype(BF16), w_ref[...].astype(BF16)) + b_ref[...]


def _modulation(cvec8, mod_w, mod_b):
    n_col = 6 * D_MODEL // D_MODEL
    return pl.pallas_call(
        _mod_kernel,
        grid=(DEPTH, n_col),
        in_specs=[pl.BlockSpec((SUBLANES, D_MODEL), lambda l, j: (0, 0)),
                  pl.BlockSpec((None, D_MODEL, D_MODEL), lambda l, j: (l, 0, j)),
                  pl.BlockSpec((None, 1, D_MODEL), lambda l, j: (l, 0, j))],
        out_specs=pl.BlockSpec((None, SUBLANES, D_MODEL), lambda l, j: (l, 0, j)),
        out_shape=jax.ShapeDtypeStruct((DEPTH, SUBLANES, 6 * D_MODEL), F32),
        compiler_params=_params(2),
        name="adaln_modulation",
    )(cvec8, mod_w, mod_b.reshape(DEPTH, 1, 6 * D_MODEL))


def _group_map(group0, rows_per_group):
    tiles_per_group = rows_per_group // ROW_TILE
    return lambda i: (group0 + i // tiles_per_group, 0, 0)


def _rope(t, cos, sin_signed):
    lane = lax.broadcasted_iota(I32, t.shape, 1)
    swapped = jnp.where((lane & 1) == 0, pltpu.roll(t, HEAD_DIM - 1, 1), pltpu.roll(t, 1, 1))
    return t * cos + swapped * sin_signed


def _ab_in_kernel(*refs, rope):
    if rope:
        x_ref, mod_ref, w_ref, cos_ref, sin_ref, q_ref, k_ref, v_ref, xr_ref, xg_ref = refs
    else:
        x_ref, mod_ref, w_ref, q_ref, k_ref, v_ref, xr_ref, xg_ref, ck_ref, cv_ref = refs
    m = mod_ref[...]
    h = x_ref[...] * (1.0 + m[1:2]) + m[0:1]
    p = _dot(h.astype(BF16), w_ref[...])
    q = p[:, :ATTN_WIDTH] * ATTN_SCALE
    k = p[:, ATTN_WIDTH:ATTN_WIDTH + KV_WIDTH]
    v = p[:, ATTN_WIDTH + KV_WIDTH:ATTN_WIDTH + 2 * KV_WIDTH]
    if rope:
        cos = cos_ref[...]
        sin = sin_ref[...]
        q = jnp.concatenate([_rope(q[:, i * HEAD_DIM:(i + 1) * HEAD_DIM], cos, sin)
                             for i in range(N_Q_HEADS)], axis=1)
        k = jnp.concatenate([_rope(k[:, i * HEAD_DIM:(i + 1) * HEAD_DIM], cos, sin)
                             for i in range(N_KV_HEADS)], axis=1)
    q_ref[...] = q.astype(q_ref.dtype)
    k_ref[...] = k.astype(k_ref.dtype)
    v_ref[...] = v.astype(v_ref.dtype)
    xr_ref[...] = p[:, ATTN_WIDTH + 2 * KV_WIDTH:ATTN_WIDTH + 2 * KV_WIDTH + RNN_WIDTH]
    xg_ref[...] = p[:, ATTN_WIDTH + 2 * KV_WIDTH + RNN_WIDTH:]
    if not rope:
        for r in range(ROW_TILE // SEQ):
            for i in range(N_KV_HEADS):
                ck_ref[r, 0, :, i, :] = k[r * SEQ:(r + 1) * SEQ, _head_cols(i)]
                cv_ref[r, 0, :, i, :] = v[r * SEQ:(r + 1) * SEQ, _head_cols(i)]


def _ab_in(x, mod_l, w_bf16, group0, rows_per_group, rope_tables):
    n = x.shape[0]
    rope = rope_tables is not None
    row = lambda width: pl.BlockSpec((ROW_TILE, width), lambda i: (i, 0))
    in_specs = [row(D_MODEL),
                pl.BlockSpec((None, 6, D_MODEL), _group_map(group0, rows_per_group)),
                pl.BlockSpec((D_MODEL, AB_IN_WIDTH), lambda i: (0, 0))]
    args = [x, mod_l, w_bf16]
    out_specs = [row(ATTN_WIDTH), row(KV_WIDTH), row(KV_WIDTH), row(RNN_WIDTH), row(RNN_WIDTH)]
    out_shape = [jax.ShapeDtypeStruct((n, ATTN_WIDTH), BF16),
                 jax.ShapeDtypeStruct((n, KV_WIDTH), BF16),
                 jax.ShapeDtypeStruct((n, KV_WIDTH), BF16),
                 jax.ShapeDtypeStruct((n, RNN_WIDTH), F32),
                 jax.ShapeDtypeStruct((n, RNN_WIDTH), F32)]
    if rope:
        tiles_per_seq = DEC_SEQ // ROW_TILE
        in_specs += [pl.BlockSpec((ROW_TILE, HEAD_DIM), lambda i: (i % tiles_per_seq, 0))] * 2
        args += list(rope_tables)
    else:
        req = ROW_TILE // SEQ
        cache = pl.BlockSpec((req, 1, SEQ, N_KV_HEADS, HEAD_DIM), lambda i: (i, 0, 0, 0, 0))
        out_specs += [cache, cache]
        out_shape += [jax.ShapeDtypeStruct((n // SEQ, 1, SEQ, N_KV_HEADS, HEAD_DIM), F32)] * 2
    return pl.pallas_call(
        functools.partial(_ab_in_kernel, rope=rope),
        grid=(n // ROW_TILE,),
        in_specs=in_specs,
        out_specs=out_specs,
        out_shape=out_shape,
        compiler_params=_params(1),
        name="ab_in_rope" if rope else "ab_in",
    )(*args)


def _rope_tables():
    rows = DEC_SEQ // GRID_W
    row = jnp.repeat(jnp.arange(rows, dtype=F32), GRID_W)
    col = jnp.tile(jnp.arange(GRID_W, dtype=F32), rows)
    n_freq = HEAD_DIM // 4
    freqs = ROPE_BASE ** (-jnp.arange(n_freq, dtype=F32) / n_freq)
    ang = jnp.concatenate([row[:, None] * freqs, col[:, None] * freqs], axis=-1)
    cos = jnp.repeat(jnp.cos(ang), 2, axis=-1)
    sin = jnp.repeat(jnp.sin(ang), 2, axis=-1)
    sign = jnp.tile(jnp.array([-1.0, 1.0], F32), HEAD_DIM // 2)
    return cos, sin * sign


def _sink_attention_head(s_list, v_list, sink):
    m = sink
    for s in s_list:
        m = jnp.maximum(m, jnp.max(s, axis=-1, keepdims=True))
    p_list = [jnp.exp(s - m) for s in s_list]
    denom = jnp.exp(sink - m)
    for p in p_list:
        denom = denom + jnp.sum(p, axis=-1, keepdims=True)
    out = None
    for p, v in zip(p_list, v_list):
        o = _dot(p.astype(BF16), v)
        out = o if out is None else out + o
    return out * (1.0 / denom)


def _head_cols(h):
    return slice(h * HEAD_DIM, (h + 1) * HEAD_DIM)


def _ctx_attn_kernel(sink_ref, q_ref, k_ref, v_ref, o_ref):
    def scores(item):
        r, h = item
        rows = slice(r * SEQ, (r + 1) * SEQ)
        kh = k_ref[rows, _head_cols(h // Q_PER_KV)].astype(BF16)
        return _dot_nt(q_ref[rows, _head_cols(h)], kh)

    def finish(item, s):
        r, h = item
        rows = slice(r * SEQ, (r + 1) * SEQ)
        vh = v_ref[rows, _head_cols(h // Q_PER_KV)].astype(BF16)
        o_ref[rows, _head_cols(h)] = _sink_attention_head([s], [vh], sink_ref[h]).astype(o_ref.dtype)

    items = [(r, h) for r in range(REQ_TILE) for h in range(N_Q_HEADS)]
    _software_pipeline(items, scores, finish)


def _ctx_attention(q, k, v, sink):
    seq = lambda width: pl.BlockSpec((REQ_TILE * SEQ, width), lambda b: (b, 0))
    return pl.pallas_call(
        _ctx_attn_kernel,
        grid=(BATCH // REQ_TILE,),
        in_specs=[pl.BlockSpec(memory_space=pltpu.SMEM), seq(ATTN_WIDTH), seq(KV_WIDTH), seq(KV_WIDTH)],
        out_specs=seq(ATTN_WIDTH),
        out_shape=jax.ShapeDtypeStruct((N_PROMPT, ATTN_WIDTH), BF16),
        compiler_params=_params(1),
        name="context_attention",
    )(sink, q, k, v)


LAT_Q = 256


def _lat_attn_kernel(sink_ref, q_ref, kp_ref, kc_ref, kn_ref, vp_ref, vc_ref, vn_ref,
                     kx_ref, vx_ref, o_ref):
    n = pl.program_id(1)
    nb = pl.num_programs(1)
    q = q_ref[...]
    kw = jnp.concatenate([kp_ref[...], kc_ref[...], kn_ref[...]], axis=0)
    vw = jnp.concatenate([vp_ref[...], vc_ref[...], vn_ref[...]], axis=0)
    kx = kx_ref[...].astype(BF16)
    vx = vx_ref[...].astype(BF16)
    n_key = LAT_Q + 2 * WINDOW
    qi = lax.broadcasted_iota(I32, (LAT_Q, n_key), 0)
    kj = lax.broadcasted_iota(I32, (LAT_Q, n_key), 1)
    rel = kj - qi
    valid = (rel >= 0) & (rel <= 2 * WINDOW)
    valid = valid & ((n > 0) | (kj >= WINDOW)) & ((n < nb - 1) | (kj < LAT_Q + WINDOW))
    def scores(h):
        sl = _head_cols(h // Q_PER_KV)
        qh = q[:, _head_cols(h)]
        s_ctx = _dot_nt(qh, kx[:, sl])
        s_win = jnp.where(valid, _dot_nt(qh, kw[:, sl]), NEG_INF)
        return s_ctx, s_win

    def finish(h, s):
        sl = _head_cols(h // Q_PER_KV)
        o_ref[:, _head_cols(h)] = _sink_attention_head(list(s), [vx[:, sl], vw[:, sl]],
                                                       sink_ref[h]).astype(o_ref.dtype)

    _software_pipeline(range(N_Q_HEADS), scores, finish)


def _lat_attention(q, k, v, k_ctx, v_ctx, sink):
    nb = DEC_SEQ // LAT_Q
    nw = DEC_SEQ // WINDOW
    per = LAT_Q // WINDOW
    cur = lambda b, n: (b * nb + n, 0)
    prev = lambda b, n: (b * nw + jnp.maximum(n * per - 1, 0), 0)
    nxt = lambda b, n: (b * nw + jnp.minimum((n + 1) * per, nw - 1), 0)
    tile = lambda width: pl.BlockSpec((LAT_Q, width), cur)
    edge = lambda imap: pl.BlockSpec((WINDOW, KV_WIDTH), imap)
    ctx = pl.BlockSpec((PAST_LEN, KV_WIDTH), lambda b, n: (b, 0))
    return pl.pallas_call(
        _lat_attn_kernel,
        grid=(DEC_BATCH, nb),
        in_specs=[pl.BlockSpec(memory_space=pltpu.SMEM), tile(ATTN_WIDTH),
                  edge(prev), tile(KV_WIDTH), edge(nxt),
                  edge(prev), tile(KV_WIDTH), edge(nxt), ctx, ctx],
        out_specs=tile(ATTN_WIDTH),
        out_shape=jax.ShapeDtypeStruct((N_SAMPLE, ATTN_WIDTH), BF16),
        compiler_params=_params(2),
        name="latent_attention",
    )(sink, q, k, k, k, v, v, v, k_ctx, v_ctx)


RNN_CHUNK = 256


def _rglru_kernel(xr_ref, xg_ref, cw_ref, cb_ref, wa_ref, ba_ref, wx_ref, bx_ref, lam_ref,
                  h0f_ref, h0b_ref, y_ref, hfl_ref, hbf_ref, xc_s, af_s, ab_s, uf_s, ub_s, *, rows, seq_len):
    n_seq = rows // seq_len
    n_chunk = rows // RNN_CHUNK
    cw = cw_ref[...]
    cb = cb_ref[...]
    zeros_halo = jnp.zeros((SUBLANES, RNN_WIDTH), F32)
    row8 = lax.broadcasted_iota(I32, (SUBLANES, RNN_WIDTH), 0)

    def conv_chunk(c):
        r0 = c * RNN_CHUNK
        first = r0 % seq_len == 0
        last = (r0 + RNN_CHUNK) % seq_len == 0
        before = zeros_halo if first else xr_ref[r0 - SUBLANES:r0, :]
        after = zeros_halo if last else xr_ref[r0 + RNN_CHUNK:r0 + RNN_CHUNK + SUBLANES, :]
        win = jnp.concatenate([before, xr_ref[r0:r0 + RNN_CHUNK, :], after], axis=0)
        xc = cb
        n_win = RNN_CHUNK + 2 * SUBLANES
        for i in range(CONV_W):
            shift = (CONV_PAD_LEFT - i) % n_win
            rolled = win if shift == 0 else pltpu.roll(win, shift, 0)
            xc = xc + rolled[SUBLANES:SUBLANES + RNN_CHUNK, :] * cw[i:i + 1, :]
        return xc

    def group_scan(a, u, reverse):
        for k in (1, 2, 4):
            if reverse:
                shift, ok = SUBLANES - k, row8 < SUBLANES - k
            else:
                shift, ok = k, row8 >= k
            a_nb = jnp.where(ok, pltpu.roll(a, shift, 0), 1.0)
            u_nb = jnp.where(ok, pltpu.roll(u, shift, 0), 0.0)
            u = a * u_nb + u
            a = a * a_nb
        return a, u

    for c in range(n_chunk):
        xc_s[c * RNN_CHUNK:(c + 1) * RNN_CHUNK, :] = conv_chunk(c)

    for d, (a_s, u_s) in enumerate(((af_s, uf_s), (ab_s, ub_s))):
        neg = -lam_ref[d:d + 1, :]
        softplus = jnp.maximum(neg, 0.0) + jnp.log1p(jnp.exp(-jnp.abs(neg)))
        decay = -LRU_C * softplus
        wa = wa_ref[d]
        wx = wx_ref[d]
        ba = ba_ref[d:d + 1, :]
        bx = bx_ref[d:d + 1, :]
        for c in range(n_chunk):
            xc = xc_s[c * RNN_CHUNK:(c + 1) * RNN_CHUNK, :]
            xcb = xc.astype(BF16)
            r = _sigmoid(_dot(xcb, wa) + ba)
            i = _sigmoid(_dot(xcb, wx) + bx)
            log_a = r * decay
            a = jnp.exp(log_a)
            a_s[c * RNN_CHUNK:(c + 1) * RNN_CHUNK, :] = a
            one_minus_a2 = -jnp.tanh(log_a) * (a * a + 1.0)
            u_s[c * RNN_CHUNK:(c + 1) * RNN_CHUNK, :] = jnp.sqrt(one_minus_a2) * (i * xc)

    n_group = seq_len // SUBLANES

    def body(g, carries):
        fwd, bwd = carries
        new_f, new_b = [], []
        for s in range(n_seq):
            rf = pl.multiple_of(s * seq_len + g * SUBLANES, SUBLANES)
            a, u = group_scan(af_s[pl.ds(rf, SUBLANES), :], uf_s[pl.ds(rf, SUBLANES), :], False)
            h = u + a * fwd[s]
            uf_s[pl.ds(rf, SUBLANES), :] = h
            new_f.append(h[SUBLANES - 1:SUBLANES, :])
            rb = pl.multiple_of(s * seq_len + (n_group - 1 - g) * SUBLANES, SUBLANES)
            a, u = group_scan(ab_s[pl.ds(rb, SUBLANES), :], ub_s[pl.ds(rb, SUBLANES), :], True)
            h = u + a * bwd[s]
            ub_s[pl.ds(rb, SUBLANES), :] = h
            new_b.append(h[0:1, :])
        return tuple(new_f), tuple(new_b)

    init = (tuple(h0f_ref[s:s + 1, :] for s in range(n_seq)), tuple(h0b_ref[s:s + 1, :] for s in range(n_seq)))
    last_f, first_b = lax.fori_loop(0, n_group, body, init)

    for s in range(n_seq):
        hfl_ref[s:s + 1, :] = last_f[s]
        hbf_ref[s:s + 1, :] = first_b[s]
    for c in range(n_chunk):
        sl = slice(c * RNN_CHUNK, (c + 1) * RNN_CHUNK)
        y_ref[sl, :] = ((uf_s[sl, :] + ub_s[sl, :]) * _gelu_tanh(xg_ref[sl, :])).astype(y_ref.dtype)


def _rglru(xr, xg, conv_w, conv_b, wa, ba, wx, bx, lam, h0f, h0b, rows, seq_len):
    n = xr.shape[0]
    n_seq = rows // seq_len
    row = pl.BlockSpec((rows, RNN_WIDTH), lambda i: (i, 0))
    full = lambda shape: pl.BlockSpec(shape, lambda i: (0,) * len(shape))
    state = pl.BlockSpec((None, n_seq, RNN_WIDTH), lambda i: (i, 0, 0))
    state_shape = jax.ShapeDtypeStruct((n // rows, n_seq, RNN_WIDTH), F32)
    return pl.pallas_call(
        functools.partial(_rglru_kernel, rows=rows, seq_len=seq_len),
        grid=(n // rows,),
        in_specs=[row, row, full((CONV_W, RNN_WIDTH)), full((1, RNN_WIDTH)),
                  full((2, RNN_WIDTH, RNN_WIDTH)), full((2, RNN_WIDTH)),
                  full((2, RNN_WIDTH, RNN_WIDTH)), full((2, RNN_WIDTH)), full((2, RNN_WIDTH)),
                  state, state],
        out_specs=[row, state, state],
        out_shape=[jax.ShapeDtypeStruct((n, RNN_WIDTH), BF16), state_shape, state_shape],
        scratch_shapes=[pltpu.VMEM((rows, RNN_WIDTH), F32)] * 5,
        compiler_params=_params(1),
        name="rglru_%d" % seq_len,
    )(xr, xg, conv_w, conv_b, wa, ba, wx, bx, lam, h0f, h0b)


def _block_diag_dense(w):
    eye = jnp.eye(RNN_BLOCKS, dtype=w.dtype)
    dense = w[:, :, :, None, :] * eye[None, :, None, :, None]
    return dense.reshape(2, RNN_WIDTH, RNN_WIDTH).astype(BF16)


SUB_TILE = 256
SUB_TILES = tuple(slice(s, s + SUB_TILE) for s in range(0, ROW_TILE, SUB_TILE))
SGU_TILES = SUB_TILES


def _residual_router(rows, x, o, m, lg_ref, lb_ref, rw_ref, x1_ref, xm_ref, aff_ref):
    x1 = _layer_norm(ALPHA * x + m[2:3] * o, lg_ref[...], lb_ref[...])
    x1_ref[rows, :] = x1
    xm = (x1 * (1.0 + m[4:5]) + m[3:4]).astype(BF16)
    xm_ref[rows, :] = xm
    lgt = _dot(xm, rw_ref[...])
    lane = lax.broadcasted_iota(I32, lgt.shape, 1)
    lgt = jnp.where(lane < N_EXPERTS, lgt, NEG_INF)
    ex = jnp.exp(lgt - jnp.max(lgt, axis=-1, keepdims=True))
    aff = ex / jnp.sum(ex, axis=-1, keepdims=True)
    aff_ref[rows, :] = aff[:, :N_EXPERTS]


def _mix_out_kernel(att_ref, rnn_ref, x_ref, mod_ref, w_ref, lg_ref, lb_ref, rw_ref,
                    x1_ref, xm_ref, aff_ref):
    m = mod_ref[...]

    def project(rows):
        return _dot(att_ref[rows, :], w_ref[:ATTN_WIDTH, :]) + _dot(rnn_ref[rows, :], w_ref[ATTN_WIDTH:, :])

    def finish(rows, o):
        _residual_router(rows, x_ref[rows, :], o, m, lg_ref, lb_ref, rw_ref, x1_ref, xm_ref, aff_ref)

    _software_pipeline(SUB_TILES, project, finish)


def _sgu_kernel(x_ref, mod_ref, win_ref, bin_ref, sg_ref, sb_ref, spw_ref, spb_ref, w_ref, lg_ref, lb_ref, rw_ref,
                x1_ref, xm_ref, aff_ref, v_s, mixed_s):
    m = mod_ref[...]

    def project(rows):
        h = x_ref[rows, :] * (1.0 + m[1:2]) + m[0:1]
        return _dot(h.astype(BF16), win_ref[...])

    def gate(rows, p):
        p = _gelu_tanh(p + bin_ref[...])
        v_s[rows, :] = _layer_norm(p[:, SGU_WIDTH:], sg_ref[...], sb_ref[...]).astype(v_s.dtype)
        for c in range(rows.start, rows.stop, CHUNK):
            for g in range(SGU_GROUPS):
                cols = slice(g * SGU_GROUP_W, (g + 1) * SGU_GROUP_W)
                mixed_s[c:c + CHUNK, cols] = _dot(spw_ref[g], v_s[c:c + CHUNK, cols]) + spb_ref[:, cols]
        gated = (p[:, :SGU_WIDTH] * mixed_s[rows, :]).astype(BF16)
        return _dot(gated, w_ref[...])

    def finish(rows, o):
        _residual_router(rows, x_ref[rows, :], o, m, lg_ref, lb_ref, rw_ref, x1_ref, xm_ref, aff_ref)

    _software_pipeline(SGU_TILES, project, gate, finish)


def _mixer_out(kernel, name, acts, x, mod_l, weights, ln_g, ln_b, router_w, group0, rows_per_group,
               scratch=()):
    n = x.shape[0]
    row = lambda width: pl.BlockSpec((ROW_TILE, width), lambda i: (i, 0))
    full = lambda a: pl.BlockSpec(a.shape, lambda i: (0,) * a.ndim)
    return pl.pallas_call(
        kernel,
        grid=(n // ROW_TILE,),
        in_specs=([row(a.shape[1]) for a in acts]
                  + [row(D_MODEL), pl.BlockSpec((None, 6, D_MODEL), _group_map(group0, rows_per_group))]
                  + [full(w) for w in weights] + [full(ln_g), full(ln_b), full(router_w)]),
        out_specs=[row(D_MODEL), row(D_MODEL), row(N_EXPERTS)],
        out_shape=[jax.ShapeDtypeStruct((n, D_MODEL), F32),
                   jax.ShapeDtypeStruct((n, D_MODEL), BF16),
                   jax.ShapeDtypeStruct((n, N_EXPERTS), F32)],
        scratch_shapes=list(scratch),
        compiler_params=_params(1),
        name=name,
    )(*acts, x, mod_l, *weights, ln_g, ln_b, router_w)


def _route_kernel(*refs, n_req, tokens, cap, place_in_kernel):
    n_blk = tokens // PREFIX_BLOCK
    if place_in_kernel:
        aff_ref, tri_ref, place_ref, posm_ref, posr_ref, affr_ref, blk_ref = refs
        n_col = place_ref.shape[2]
        aff = jnp.zeros((tokens, n_col), F32)
        for b in range(n_req):
            piece = aff_ref[b * tokens:(b + 1) * tokens, :]
            hi = piece.astype(BF16)
            rem = piece - hi.astype(F32)
            mid = rem.astype(BF16)
            low = (rem - mid.astype(F32)).astype(BF16)
            place = place_ref[b]
            aff = aff + ((_dot(hi, place) + _dot(mid, place)) + _dot(low, place))
    else:
        aff_ref, tri_ref, posm_ref, posr_ref, affr_ref, blk_ref = refs
        aff = aff_ref[...]
        n_col = aff.shape[1]

    def bisect(_, lo_hi):
        lo, hi = lo_hi
        mid = lo + ((hi - lo) >> 1)
        cnt = jnp.sum(jnp.where(aff >= pltpu.bitcast(mid, F32), 1.0, 0.0), axis=0, keepdims=True)
        ge = cnt >= cap
        return jnp.where(ge, mid, lo), jnp.where(ge, hi, mid)

    lo0 = jnp.zeros((1, n_col), I32)
    hi0 = jnp.full((1, n_col), 0x7F800000, I32)
    thr_bits, _ = lax.fori_loop(0, 31, bisect, (lo0, hi0))
    thr = pltpu.bitcast(thr_bits, F32)
    above = pltpu.bitcast(thr_bits + 1, F32)

    tri = tri_ref[...]

    def excl_prefix(x01):
        outs, before, inside = [], [], []
        off = jnp.zeros((1, n_col), F32)
        for blk in range(n_blk):
            xb = x01[blk * PREFIX_BLOCK:(blk + 1) * PREFIX_BLOCK, :]
            outs.append(_dot(tri, xb.astype(BF16)) + off)
            cnt = jnp.sum(xb, axis=0, keepdims=True)
            before.append(off)
            inside.append(cnt)
            off = off + cnt
        return (outs[0] if n_blk == 1 else jnp.concatenate(outs, axis=0)), before + inside

    gt = jnp.where(aff >= above, 1.0, 0.0)
    eq = jnp.where(aff >= thr, 1.0, 0.0) - gt
    need = cap - jnp.sum(gt, axis=0, keepdims=True)
    eq_rank, _ = excl_prefix(eq)
    sel = gt + jnp.where(eq_rank < need, eq, 0.0)
    pos, blk_rows = excl_prefix(sel)
    posm = jnp.where(sel > 0.0, pos, -1.0)
    posr_ref[...] = posm.T
    affr_ref[...] = aff.T
    for i, row in enumerate(blk_rows):
        blk_ref[i:i + 1, :] = row
    if place_in_kernel:
        posm_bf16 = posm.astype(BF16)
        for b in range(n_req):
            posm_ref[b * tokens:(b + 1) * tokens, :] = _dot_nt(posm_bf16, place_ref[b])
    else:
        posm_ref[...] = posm


PLACE_IN_KERNEL_MAX_REQ = 8


def _route_requests(aff, tri, n_req, tokens, cap):
    n_real = n_req * N_EXPERTS
    n_col = -(-n_real // LANES) * LANES
    n_blk = tokens // PREFIX_BLOCK
    place_in_kernel = n_req <= PLACE_IN_KERNEL_MAX_REQ
    whole = lambda shape: pl.BlockSpec(shape, lambda i: (0,) * len(shape))
    if place_in_kernel:
        col = jnp.arange(n_col)
        place = (col[None, None, :] == (jnp.arange(n_req)[:, None, None] * N_EXPERTS
                                        + jnp.arange(N_EXPERTS)[None, :, None])).astype(BF16)
        args = (aff, tri, place)
        token_major = (n_req * tokens, N_EXPERTS)
        in_specs = [whole(token_major), whole((PREFIX_BLOCK, PREFIX_BLOCK)), whole(place.shape)]
    else:
        aff_t = aff.reshape(n_req, tokens, N_EXPERTS).transpose(1, 0, 2).reshape(tokens, n_real)
        args = (jnp.pad(aff_t, ((0, 0), (0, n_col - n_real))), tri)
        token_major = (tokens, n_col)
        in_specs = [whole(token_major), whole((PREFIX_BLOCK, PREFIX_BLOCK))]
    posm, posr, affr, blk = pl.pallas_call(
        functools.partial(_route_kernel, n_req=n_req, tokens=tokens, cap=cap, place_in_kernel=place_in_kernel),
        grid=(1,),
        in_specs=in_specs,
        out_specs=[whole(token_major), whole((n_col, tokens)), whole((n_col, tokens)), whole((2 * n_blk, n_col))],
        out_shape=[jax.ShapeDtypeStruct(token_major, F32),
                   jax.ShapeDtypeStruct((n_col, tokens), F32),
                   jax.ShapeDtypeStruct((n_col, tokens), F32),
                   jax.ShapeDtypeStruct((2 * n_blk, n_col), F32)],
        compiler_params=_params(1),
        name="route_%d" % tokens,
    )(*args)
    if not place_in_kernel:
        posm = posm[:, :n_real].reshape(tokens, n_req, N_EXPERTS).transpose(1, 0, 2).reshape(n_req * tokens, N_EXPERTS)
    blk = blk[:, :n_real].astype(I32).reshape(2, n_blk, n_req, N_EXPERTS).transpose(0, 2, 1, 3)
    expert_major = lambda a: a.reshape(n_col // N_EXPERTS, N_EXPERTS, tokens)
    return posm, expert_major(posr), expert_major(affr), blk[0], blk[1]


def _slot_gate(hit, aff_row):
    g = jnp.sum(jnp.where(hit, aff_row, 0.0), axis=-1, keepdims=True)
    return jnp.broadcast_to(g, (hit.shape[0], LANES))


def _gather_prompt_kernel(x_ref, posr_ref, affr_ref, xg_ref, g_ref):
    slot = lax.broadcasted_iota(I32, (CAP_P, SEQ), 0).astype(F32)
    for r in range(REQ_TILE):
        tokens = slice(r * SEQ, (r + 1) * SEQ)
        slots = slice(r * CAP_P, (r + 1) * CAP_P)
        posr = posr_ref[r]
        affr = affr_ref[r]
        hits = [posr[e:e + 1, :] == slot for e in range(N_EXPERTS)]
        onehot = jnp.concatenate([jnp.where(h, 1.0, 0.0) for h in hits], axis=0).astype(BF16)
        xg = _dot(onehot, x_ref[tokens, :])
        for e in range(N_EXPERTS):
            xg_ref[e, slots, :] = xg[e * CAP_P:(e + 1) * CAP_P, :].astype(xg_ref.dtype)
            g_ref[e, slots, :] = _slot_gate(hits[e], affr[e:e + 1, :])


def _gather_prompt(xm, posr, affr):
    return pl.pallas_call(
        _gather_prompt_kernel,
        grid=(BATCH // REQ_TILE,),
        in_specs=[pl.BlockSpec((REQ_TILE * SEQ, D_MODEL), lambda b: (b, 0)),
                  pl.BlockSpec((REQ_TILE, N_EXPERTS, SEQ), lambda b: (b, 0, 0)),
                  pl.BlockSpec((REQ_TILE, N_EXPERTS, SEQ), lambda b: (b, 0, 0))],
        out_specs=[pl.BlockSpec((N_EXPERTS, REQ_TILE * CAP_P, D_MODEL), lambda b: (0, b, 0)),
                   pl.BlockSpec((N_EXPERTS, REQ_TILE * CAP_P, LANES), lambda b: (0, b, 0))],
        out_shape=[jax.ShapeDtypeStruct((N_EXPERTS, SLOTS_P, D_MODEL), BF16),
                   jax.ShapeDtypeStruct((N_EXPERTS, SLOTS_P, LANES), F32)],
        compiler_params=_params(1),
        name="gather_prompt",
    )(xm, posr, affr)


TOKEN_BLOCK = PREFIX_BLOCK
N_TOKEN_BLOCKS = DEC_SEQ // TOKEN_BLOCK
SLOT_ALIGN = 16
WINDOW_SLOTS = 64
LAST_WINDOW = CAP_S - WINDOW_SLOTS
ALL_WINDOWS = N_EXPERTS * WINDOW_SLOTS


def _window_plan(before, inside):
    start = (before // SLOT_ALIGN) * SLOT_ALIGN
    span = before - start + inside
    passes = jnp.where(inside > 0, (span + WINDOW_SLOTS - 1) // WINDOW_SLOTS, 0)
    return start.reshape(-1), jnp.max(passes, axis=-1).reshape(-1)


def _window_bounds(start_ref, step, expert, k):
    lower = start_ref[step * N_EXPERTS + expert] + WINDOW_SLOTS * k
    begin = pl.multiple_of(jnp.minimum(lower, LAST_WINDOW), SLOT_ALIGN)
    return lower, begin


BLOCKS_PER_STEP = 2
STEP_TOKENS = BLOCKS_PER_STEP * TOKEN_BLOCK
STEPS_PER_REQ = N_TOKEN_BLOCKS // BLOCKS_PER_STEP


def _gather_sample_kernel(start_ref, passes_ref, x_ref, posr_ref, affr_ref, xg_ref, g_ref):
    t = pl.program_id(1)
    first_block = (pl.program_id(0) * STEPS_PER_REQ + t) * BLOCKS_PER_STEP

    @pl.when(t == 0)
    def _():
        xg_ref[...] = jnp.zeros_like(xg_ref)
        g_ref[...] = jnp.zeros_like(g_ref)

    row = lax.broadcasted_iota(I32, (WINDOW_SLOTS, TOKEN_BLOCK), 0).astype(F32)

    def pick(j, k):
        tokens = slice(j * TOKEN_BLOCK, (j + 1) * TOKEN_BLOCK)
        hits, begins = [], []
        for e in range(N_EXPERTS):
            lower, begin = _window_bounds(start_ref, first_block + j, e, k)
            pe = posr_ref[e:e + 1, tokens]
            hits.append((pe - begin.astype(F32) == row) & (pe >= lower.astype(F32)))
            begins.append(begin)
        onehot = jnp.concatenate([jnp.where(h, 1.0, 0.0) for h in hits], axis=0).astype(BF16)
        return _dot(onehot, x_ref[tokens, :]), hits, begins

    def place(j, picked):
        rows_x, hits, begins = picked
        tokens = slice(j * TOKEN_BLOCK, (j + 1) * TOKEN_BLOCK)
        for e in range(N_EXPERTS):
            win = pl.ds(begins[e], WINDOW_SLOTS)
            xg_ref[e, win, :] += rows_x[e * WINDOW_SLOTS:(e + 1) * WINDOW_SLOTS, :].astype(xg_ref.dtype)
            g_ref[e, win, :] += _slot_gate(hits[e], affr_ref[e:e + 1, tokens])

    _software_pipeline(range(BLOCKS_PER_STEP), lambda j: pick(j, 0), place)

    for j in range(BLOCKS_PER_STEP):
        def extra_pass(k, carry, j=j):
            place(j, pick(j, k))
            return carry

        lax.fori_loop(1, passes_ref[first_block + j], extra_pass, 0)


def _gather_sample(xm, posr, affr, starts, passes):
    expert_rows = pl.BlockSpec((None, N_EXPERTS, STEP_TOKENS), lambda b, t, s, p: (b, 0, t))
    grid_spec = pltpu.PrefetchScalarGridSpec(
        num_scalar_prefetch=2,
        grid=(DEC_BATCH, STEPS_PER_REQ),
        in_specs=[pl.BlockSpec((STEP_TOKENS, D_MODEL), lambda b, t, s, p: (b * STEPS_PER_REQ + t, 0)),
                  expert_rows, expert_rows],
        out_specs=[pl.BlockSpec((N_EXPERTS, CAP_S, D_MODEL), lambda b, t, s, p: (0, b, 0)),
                   pl.BlockSpec((N_EXPERTS, CAP_S, LANES), lambda b, t, s, p: (0, b, 0))],
    )
    return pl.pallas_call(
        _gather_sample_kernel,
        grid_spec=grid_spec,
        out_shape=[jax.ShapeDtypeStruct((N_EXPERTS, SLOTS_S, D_MODEL), BF16),
                   jax.ShapeDtypeStruct((N_EXPERTS, SLOTS_S, LANES), F32)],
        compiler_params=_params(2),
        name="gather_sample",
    )(starts, passes, xm, posr, affr)


FF_TILE = 512
SLOT_CHUNK = 1024


def _ffn_kernel(xp_ref, xs_ref, gp_ref, gs_ref, w1_ref, w3_ref, w2_ref, yp_ref, ys_ref, acc_ref):
    f = pl.program_id(1)
    n_f = pl.num_programs(1)
    @pl.when(f == 0)
    def _():
        acc_ref[...] = jnp.zeros_like(acc_ref)

    w1 = w1_ref[...].astype(BF16)
    w3 = w3_ref[...].astype(BF16)
    w2 = w2_ref[...].astype(BF16)

    def up_project(chunk):
        x_ref, c, _ = chunk
        x = x_ref[c * SLOT_CHUNK:(c + 1) * SLOT_CHUNK, :]
        return _dot(x, w1), _dot(x, w3)

    def down_project(chunk, h):
        _, c, base = chunk
        h1, h3 = h
        hid = (h1 * jax.nn.sigmoid(h1) * h3).astype(BF16)
        rows = slice(base + c * SLOT_CHUNK, base + (c + 1) * SLOT_CHUNK)
        acc_ref[rows, :] += _dot(hid, w2)

    chunks = [(x_ref, c, base) for x_ref, n_slot, base in ((xp_ref, SLOTS_P, 0), (xs_ref, SLOTS_S, SLOTS_P))
              for c in range(n_slot // SLOT_CHUNK)]
    _software_pipeline(chunks, up_project, down_project)

    @pl.when(f == n_f - 1)
    def _():
        for y_ref, g_ref, n_slot, base in ((yp_ref, gp_ref, SLOTS_P, 0), (ys_ref, gs_ref, SLOTS_S, SLOTS_P)):
            for c in range(n_slot // SLOT_CHUNK):
                rows = slice(c * SLOT_CHUNK, (c + 1) * SLOT_CHUNK)
                gate = jnp.concatenate([g_ref[rows, :]] * (D_MODEL // LANES), axis=1)
                y_ref[rows, :] = (acc_ref[base + c * SLOT_CHUNK:base + (c + 1) * SLOT_CHUNK, :]
                                  * gate).astype(y_ref.dtype)


def _expert_ffn(xgp, xgs, gp, gs, w1, w3, w2, layer):
    slots = lambda n, width: pl.BlockSpec((None, n, width), lambda e, f: (e, 0, 0))
    return pl.pallas_call(
        _ffn_kernel,
        grid=(N_EXPERTS, EXPERT_FF // FF_TILE),
        in_specs=[slots(SLOTS_P, D_MODEL), slots(SLOTS_S, D_MODEL), slots(SLOTS_P, LANES), slots(SLOTS_S, LANES),
                  pl.BlockSpec((None, None, D_MODEL, FF_TILE), lambda e, f: (layer, e, 0, f)),
                  pl.BlockSpec((None, None, D_MODEL, FF_TILE), lambda e, f: (layer, e, 0, f)),
                  pl.BlockSpec((None, None, FF_TILE, D_MODEL), lambda e, f: (layer, e, f, 0))],
        out_specs=[slots(SLOTS_P, D_MODEL), slots(SLOTS_S, D_MODEL)],
        out_shape=[jax.ShapeDtypeStruct((N_EXPERTS, SLOTS_P, D_MODEL), BF16),
                   jax.ShapeDtypeStruct((N_EXPERTS, SLOTS_S, D_MODEL), BF16)],
        scratch_shapes=[pltpu.VMEM((SLOTS_P + SLOTS_S, D_MODEL), F32)],
        compiler_params=_params(2),
        name="expert_ffn",
    )(xgp, xgs, gp, gs, w1, w3, w2)


def _combine_prompt_kernel(posm_ref, y_ref, x_ref, mod_ref, exp_ref, lg_ref, lb_ref, o_ref):
    m = mod_ref[...]
    slot = (lax.broadcasted_iota(I32, (SEQ, N_EXPERTS * CAP_P), 1) & (CAP_P - 1)).astype(F32)
    def scatter(r):
        tokens = slice(r * SEQ, (r + 1) * SEQ)
        spread = _dot(posm_ref[tokens, :].astype(BF16), exp_ref[...])
        onehot = jnp.where(spread == slot, 1.0, 0.0).astype(BF16)
        y = jnp.concatenate([y_ref[e, r * CAP_P:(r + 1) * CAP_P, :] for e in range(N_EXPERTS)], axis=0)
        return _dot(onehot, y)

    def finish(r, f):
        tokens = slice(r * SEQ, (r + 1) * SEQ)
        o_ref[tokens, :] = _layer_norm(ALPHA * x_ref[tokens, :] + m[5:6] * f, lg_ref[...], lb_ref[...])

    _software_pipeline(range(REQ_TILE), scatter, finish)


def _combine_prompt(posm, y, x1, mod_l, expand, ln_g, ln_b):
    full = lambda a: pl.BlockSpec(a.shape, lambda b: (0,) * a.ndim)
    return pl.pallas_call(
        _combine_prompt_kernel,
        grid=(BATCH // REQ_TILE,),
        in_specs=[pl.BlockSpec((REQ_TILE * SEQ, N_EXPERTS), lambda b: (b, 0)),
                  pl.BlockSpec((N_EXPERTS, REQ_TILE * CAP_P, D_MODEL), lambda b: (0, b, 0)),
                  pl.BlockSpec((REQ_TILE * SEQ, D_MODEL), lambda b: (b, 0)),
                  pl.BlockSpec((None, 6, D_MODEL), lambda b: (0, 0, 0)),
                  full(expand), full(ln_g), full(ln_b)],
        out_specs=pl.BlockSpec((REQ_TILE * SEQ, D_MODEL), lambda b: (b, 0)),
        out_shape=jax.ShapeDtypeStruct((N_PROMPT, D_MODEL), F32),
        compiler_params=_params(1),
        name="combine_prompt",
    )(posm, y, x1, mod_l, expand, ln_g, ln_b)


def _combine_sample_kernel(start_ref, passes_ref, posm_ref, y_ref, x_ref, mod_ref, exp_ref, off_ref,
                           lg_ref, lb_ref, o_ref):
    first_block = (pl.program_id(0) * STEPS_PER_REQ + pl.program_id(1)) * BLOCKS_PER_STEP
    exp_bf16 = exp_ref[...]
    offset = off_ref[...]
    m = mod_ref[...]

    def tokens_of(j):
        return slice(j * TOKEN_BLOCK, (j + 1) * TOKEN_BLOCK)

    def one_pass(j, k):
        spread = _dot(posm_ref[tokens_of(j), :].astype(BF16), exp_bf16)
        begin_row = jnp.zeros((1, ALL_WINDOWS), F32)
        lower_row = jnp.zeros((1, ALL_WINDOWS), F32)
        windows = []
        for e in range(N_EXPERTS):
            lower, begin = _window_bounds(start_ref, first_block + j, e, k)
            lanes_e = exp_bf16[e:e + 1, :].astype(F32)
            begin_row = begin_row + begin.astype(F32) * lanes_e
            lower_row = lower_row + lower.astype(F32) * lanes_e
            windows.append(y_ref[e, pl.ds(begin, WINDOW_SLOTS), :])
        hit = (spread - begin_row == offset) & (spread >= lower_row)
        onehot = jnp.where(hit, 1.0, 0.0).astype(BF16)
        return _dot(onehot, jnp.concatenate(windows, axis=0))

    def finish(j, f):
        rows = tokens_of(j)
        o_ref[rows, :] = _layer_norm(ALPHA * x_ref[rows, :] + m[5:6] * f, lg_ref[...], lb_ref[...])

    _software_pipeline(range(BLOCKS_PER_STEP), lambda j: one_pass(j, 0), finish)

    for j in range(BLOCKS_PER_STEP):
        n_pass = passes_ref[first_block + j]

        @pl.when(n_pass > 1)
        def _(j=j, n_pass=n_pass):
            f = lax.fori_loop(1, n_pass, lambda k, f: f + one_pass(j, k), one_pass(j, 0))
            finish(j, f)


def _combine_sample(posm, y, x1, mod_l, ln_g, ln_b, starts, passes):
    lane = jnp.arange(ALL_WINDOWS)
    expand = ((lane[None, :] // WINDOW_SLOTS) == jnp.arange(N_EXPERTS)[:, None]).astype(BF16)
    offset = (lane % WINDOW_SLOTS).astype(F32).reshape(1, ALL_WINDOWS)
    row = lambda width: pl.BlockSpec((STEP_TOKENS, width), lambda b, t, s, p: (b * STEPS_PER_REQ + t, 0))
    full = lambda a: pl.BlockSpec(a.shape, lambda b, t, s, p: (0,) * a.ndim)
    grid_spec = pltpu.PrefetchScalarGridSpec(
        num_scalar_prefetch=2,
        grid=(DEC_BATCH, STEPS_PER_REQ),
        in_specs=[row(N_EXPERTS),
                  pl.BlockSpec((N_EXPERTS, CAP_S, D_MODEL), lambda b, t, s, p: (0, b, 0)),
                  row(D_MODEL),
                  pl.BlockSpec((None, 6, D_MODEL), lambda b, t, s, p: (1 + b, 0, 0)),
                  full(expand), full(offset), full(ln_g), full(ln_b)],
        out_specs=row(D_MODEL),
    )
    return pl.pallas_call(
        _combine_sample_kernel,
        grid_spec=grid_spec,
        out_shape=jax.ShapeDtypeStruct((N_SAMPLE, D_MODEL), F32),
        compiler_params=_params(2),
        name="combine_sample",
    )(starts, passes, posm, y, x1, mod_l, expand, offset, ln_g, ln_b)


def _moe(prompt, sample, mod_l, w1, w3, w2, layer, ln_g, ln_b, tri, expand):
    x1_p, xm_p, aff_p = prompt
    x1_s, xm_s, aff_s = sample
    posm_p, posr_p, affr_p, _, _ = _route_requests(aff_p, tri, BATCH, SEQ, CAP_P)
    posm_s, posr_s, affr_s, before_s, inside_s = _route_requests(aff_s, tri, DEC_BATCH, DEC_SEQ, CAP_S)
    starts, passes = _window_plan(before_s, inside_s)
    xg_p, g_p = _gather_prompt(xm_p, posr_p, affr_p)
    xg_s, g_s = _gather_sample(xm_s, posr_s, affr_s, starts, passes)
    y_p, y_s = _expert_ffn(xg_p, xg_s, g_p, g_s, w1, w3, w2, layer)
    x2_p = _combine_prompt(posm_p, y_p, x1_p, mod_l, expand, ln_g, ln_b)
    x2_s = _combine_sample(posm_s, y_s, x1_s, mod_l, ln_g, ln_b, starts, passes)
    return x2_p, x2_s


def kernel(x_prompt, x_sample, cache_k, cache_v, state_rglru, c, c_ctx, mod_w, mod_b, ln_mix_g, ln_mix_b, ln_ffn_g, ln_ffn_b, ab_in_w, attn_sink, rnn_conv_w, rnn_conv_b, lru_wa, lru_ba, lru_wx, lru_bx, lru_lambda, ab_out_w, sgu_in_w, sgu_in_b, sgu_ln_g, sgu_ln_b, sgu_spatial_w, sgu_spatial_b, sgu_out_w, router_w, moe_w1, moe_w3, moe_w2):
    xp = x_prompt.reshape(N_PROMPT, D_MODEL)
    xs = x_sample.reshape(N_SAMPLE, D_MODEL)

    cvec = jnp.concatenate([c_ctx[None], c, jnp.zeros((SUBLANES - 1 - DEC_BATCH, D_MODEL), F32)], axis=0)
    mod = _modulation(cvec, mod_w, mod_b).reshape(DEPTH, SUBLANES, 6, D_MODEL)

    idx = jnp.arange(PREFIX_BLOCK)
    tri = (idx[None, :] < idx[:, None]).astype(BF16)
    lane = jnp.arange(N_EXPERTS * CAP_P)
    expand = ((lane[None, :] // CAP_P) == jnp.arange(N_EXPERTS)[:, None]).astype(BF16)
    rope_tables = _rope_tables()
    vec = lambda a: a.reshape(1, -1)
    router_pad = lambda w: jnp.pad(w, ((0, 0), (0, LANES - N_EXPERTS))).astype(BF16)

    new_k = new_v = new_state = None
    for l in range(DEPTH):
        e = l // 2
        mod_l = mod[l]
        rw = router_pad(router_w[l])
        lg, lb = vec(ln_mix_g[l]), vec(ln_mix_b[l])
        if l % 2 == 0:
            w_in = ab_in_w[e].astype(BF16)
            w_out = ab_out_w[e].astype(BF16)
            wa = _block_diag_dense(lru_wa[e])
            wx = _block_diag_dense(lru_wx[e])
            rnn_w = (rnn_conv_w[e], vec(rnn_conv_b[e]), wa, lru_ba[e], wx, lru_bx[e], lru_lambda[e])

            q, k, v, xr, xg, new_k, new_v = _ab_in(xp, mod_l, w_in, 0, N_PROMPT, None)
            att = _ctx_attention(q, k, v, attn_sink[e])
            rows = 8 * SEQ
            zeros = jnp.zeros((N_PROMPT // rows, rows // SEQ, RNN_WIDTH), F32)
            rnn, hf_last, hb_first = _rglru(xr, xg, *rnn_w, zeros, zeros, rows, SEQ)
            new_state = jnp.stack([hf_last.reshape(BATCH, RNN_WIDTH), hb_first.reshape(BATCH, RNN_WIDTH)], axis=1)
            prompt = _mixer_out(_mix_out_kernel, "mix_out", (att, rnn), xp, mod_l, (w_out,),
                                lg, lb, rw, 0, N_PROMPT)

            q, k, v, xr, xg = _ab_in(xs, mod_l, w_in, 1, DEC_SEQ, rope_tables)
            att = _lat_attention(q, k, v, cache_k[:, e].reshape(DEC_BATCH * PAST_LEN, KV_WIDTH),
                                 cache_v[:, e].reshape(DEC_BATCH * PAST_LEN, KV_WIDTH), attn_sink[e])
            h0 = state_rglru[:, e]
            rnn, _, _ = _rglru(xr, xg, *rnn_w, h0[:, 0][:, None, :], h0[:, 1][:, None, :], DEC_SEQ, DEC_SEQ)
            sample = _mixer_out(_mix_out_kernel, "mix_out", (att, rnn), xs, mod_l, (w_out,),
                                lg, lb, rw, 1, DEC_SEQ)
        else:
            w_in = sgu_in_w[e].astype(BF16)
            w_out = sgu_out_w[e].astype(BF16)
            spw = sgu_spatial_w[e].astype(BF16)
            spb = jnp.repeat(sgu_spatial_b[e].T, SGU_GROUP_W, axis=1)
            sgu_w = (w_in, vec(sgu_in_b[e]), vec(sgu_ln_g[e]), vec(sgu_ln_b[e]), spw, spb, w_out)
            scratch = (pltpu.VMEM((ROW_TILE, SGU_WIDTH), BF16), pltpu.VMEM((ROW_TILE, SGU_WIDTH), F32))
            prompt = _mixer_out(_sgu_kernel, "sgu", (), xp, mod_l, sgu_w, lg, lb, rw, 0, N_PROMPT, scratch)
            sample = _mixer_out(_sgu_kernel, "sgu", (), xs, mod_l, sgu_w, lg, lb, rw, 1, DEC_SEQ, scratch)

        xp, xs = _moe(prompt, sample, mod_l, moe_w1, moe_w3, moe_w2, l,
                      vec(ln_ffn_g[l]), vec(ln_ffn_b[l]), tri, expand)

    return (xp.reshape(BATCH, SEQ, D_MODEL),
            xs.reshape(DEC_BATCH, DEC_SEQ, D_MODEL),
            new_k,
            new_v,
            new_state.reshape(BATCH, 1, 2, RNN_WIDTH))
```

```python
import functools

import jax
import jax.numpy as jnp
from jax import lax
from jax.experimental import pallas as pl
from jax.experimental.pallas import tpu as pltpu

F32 = jnp.float32
BF16 = jnp.bfloat16
I32 = jnp.int32

D_MODEL = 1024
BATCH = 32
SEQ = 256
DEPTH = 2
DEC_BATCH = 4
DEC_SEQ = 2048
PAST_LEN = 256
GRID_W = 64
HEAD_DIM = 128
N_Q_HEADS = 4
N_KV_HEADS = 2
Q_PER_KV = N_Q_HEADS // N_KV_HEADS
ATTN_WIDTH = N_Q_HEADS * HEAD_DIM
KV_WIDTH = N_KV_HEADS * HEAD_DIM
WINDOW = 128
BLOCK = 128
ATTN_SCALE = HEAD_DIM ** -0.5
ROPE_BASE = 10000.0
NEG_INF = -1e30
RNN_WIDTH = 512
RNN_BLOCKS = 8
RNN_BLOCK_W = RNN_WIDTH // RNN_BLOCKS
CONV_W = 4
CONV_PAD_LEFT = 2
LRU_C = 8.0
AB_IN_WIDTH = ATTN_WIDTH + 2 * KV_WIDTH + 2 * RNN_WIDTH
CHUNK = 128
SGU_WIDTH = D_MODEL
SGU_GROUPS = 8
SGU_GROUP_W = SGU_WIDTH // SGU_GROUPS
N_EXPERTS = 16
EXPERT_FF = 2048
EC_FACTOR = 2
ALPHA = (2 * DEPTH) ** 0.25
LN_EPS = 1e-6

N_PROMPT = BATCH * SEQ
N_SAMPLE = DEC_BATCH * DEC_SEQ
CAP_P = EC_FACTOR * SEQ // N_EXPERTS
CAP_S = EC_FACTOR * DEC_SEQ // N_EXPERTS
SLOTS_P = BATCH * CAP_P
SLOTS_S = DEC_BATCH * CAP_S

LANES = 128
SUBLANES = 8
ROW_TILE = 1024
REQ_TILE = 4
PREFIX_BLOCK = 256
VMEM_LIMIT = 56 * 1024 * 1024


def _params(n_axes=1):
    return pltpu.CompilerParams(dimension_semantics=("arbitrary",) * n_axes,
                                vmem_limit_bytes=VMEM_LIMIT)


def _layer_norm(x, g, b):
    mu = jnp.mean(x, axis=-1, keepdims=True)
    xc = x - mu
    var = jnp.mean(xc * xc, axis=-1, keepdims=True)
    return xc * lax.rsqrt(var + LN_EPS) * g + b


_LOG2_E = 1.4426950408889634
_GELU_K0 = -2.0 * 0.7978845608028654 * _LOG2_E
_GELU_K1 = _GELU_K0 * 0.044715


def _gelu_tanh(x):
    return x / (1.0 + jnp.exp2(x * (_GELU_K0 + _GELU_K1 * (x * x))))


def _sigmoid(z):
    return 0.5 + 0.5 * jnp.tanh(0.5 * z)


def _dot(a, b):
    return jnp.dot(a, b, preferred_element_type=F32)


def _dot_nt(a, b):
    return lax.dot_general(a, b, (((1,), (1,)), ((), ())), preferred_element_type=F32)


def _software_pipeline(items, first_stage, *later_stages):
    items = list(items)
    stages = (first_stage,) + later_stages
    carried = {}
    for t in range(len(items) + len(stages) - 1):
        for s, stage in enumerate(stages):
            i = t - s
            if 0 <= i < len(items):
                carried[i] = stage(items[i]) if s == 0 else stage(items[i], carried[i])


def _mod_kernel(c_ref, w_ref, b_ref, o_ref):
    c = c_ref[...]
    s = c * jax.nn.sigmoid(c)
    o_ref[...] = _dot(s.astype(BF16), w_ref[...].astype(BF16)) + b_ref[...]


def _modulation(cvec8, mod_w, mod_b):
    n_col = 6 * D_MODEL // D_MODEL
    return pl.pallas_call(
        _mod_kernel,
        grid=(DEPTH, n_col),
        in_specs=[pl.BlockSpec((SUBLANES, D_MODEL), lambda l, j: (0, 0)),
                  pl.BlockSpec((None, D_MODEL, D_MODEL), lambda l, j: (l, 0, j)),
                  pl.BlockSpec((None, 1, D_MODEL), lambda l, j: (l, 0, j))],
        out_specs=pl.BlockSpec((None, SUBLANES, D_MODEL), lambda l, j: (l, 0, j)),
        out_shape=jax.ShapeDtypeStruct((DEPTH, SUBLANES, 6 * D_MODEL), F32),
        compiler_params=_params(2),
        name="adaln_modulation",
    )(cvec8, mod_w, mod_b.reshape(DEPTH, 1, 6 * D_MODEL))


def _group_map(group0, rows_per_group):
    tiles_per_group = rows_per_group // ROW_TILE
    return lambda i: (group0 + i // tiles_per_group, 0, 0)


def _rope(t, cos, sin_signed):
    lane = lax.broadcasted_iota(I32, t.shape, 1)
    swapped = jnp.where((lane & 1) == 0, pltpu.roll(t, HEAD_DIM - 1, 1), pltpu.roll(t, 1, 1))
    return t * cos + swapped * sin_signed


def _ab_in_kernel(*refs, rope):
    if rope:
        x_ref, mod_ref, w_ref, cos_ref, sin_ref, q_ref, k_ref, v_ref, xr_ref, xg_ref = refs
    else:
        x_ref, mod_ref, w_ref, q_ref, k_ref, v_ref, xr_ref, xg_ref, ck_ref, cv_ref = refs
    m = mod_ref[...]
    h = x_ref[...] * (1.0 + m[1:2]) + m[0:1]
    p = _dot(h.astype(BF16), w_ref[...])
    q = p[:, :ATTN_WIDTH] * ATTN_SCALE
    k = p[:, ATTN_WIDTH:ATTN_WIDTH + KV_WIDTH]
    v = p[:, ATTN_WIDTH + KV_WIDTH:ATTN_WIDTH + 2 * KV_WIDTH]
    if rope:
        cos = cos_ref[...]
        sin = sin_ref[...]
        q = jnp.concatenate([_rope(q[:, i * HEAD_DIM:(i + 1) * HEAD_DIM], cos, sin)
                             for i in range(N_Q_HEADS)], axis=1)
        k = jnp.concatenate([_rope(k[:, i * HEAD_DIM:(i + 1) * HEAD_DIM], cos, sin)
                             for i in range(N_KV_HEADS)], axis=1)
    q_ref[...] = q.astype(q_ref.dtype)
    k_ref[...] = k.astype(k_ref.dtype)
    v_ref[...] = v.astype(v_ref.dtype)
    xr_ref[...] = p[:, ATTN_WIDTH + 2 * KV_WIDTH:ATTN_WIDTH + 2 * KV_WIDTH + RNN_WIDTH]
    xg_ref[...] = p[:, ATTN_WIDTH + 2 * KV_WIDTH + RNN_WIDTH:]
    if not rope:
        for r in range(ROW_TILE // SEQ):
            for i in range(N_KV_HEADS):
                ck_ref[r, 0, :, i, :] = k[r * SEQ:(r + 1) * SEQ, _head_cols(i)]
                cv_ref[r, 0, :, i, :] = v[r * SEQ:(r + 1) * SEQ, _head_cols(i)]


def _ab_in(x, mod_l, w_bf16, group0, rows_per_group, rope_tables):
    n = x.shape[0]
    rope = rope_tables is not None
    row = lambda width: pl.BlockSpec((ROW_TILE, width), lambda i: (i, 0))
    in_specs = [row(D_MODEL),
                pl.BlockSpec((None, 6, D_MODEL), _group_map(group0, rows_per_group)),
                pl.BlockSpec((D_MODEL, AB_IN_WIDTH), lambda i: (0, 0))]
    args = [x, mod_l, w_bf16]
    out_specs = [row(ATTN_WIDTH), row(KV_WIDTH), row(KV_WIDTH), row(RNN_WIDTH), row(RNN_WIDTH)]
    out_shape = [jax.ShapeDtypeStruct((n, ATTN_WIDTH), BF16),
                 jax.ShapeDtypeStruct((n, KV_WIDTH), BF16),
                 jax.ShapeDtypeStruct((n, KV_WIDTH), BF16),
                 jax.ShapeDtypeStruct((n, RNN_WIDTH), F32),
                 jax.ShapeDtypeStruct((n, RNN_WIDTH), F32)]
    if rope:
        tiles_per_seq = DEC_SEQ // ROW_TILE
        in_specs += [pl.BlockSpec((ROW_TILE, HEAD_DIM), lambda i: (i % tiles_per_seq, 0))] * 2
        args += list(rope_tables)
    else:
        req = ROW_TILE // SEQ
        cache = pl.BlockSpec((req, 1, SEQ, N_KV_HEADS, HEAD_DIM), lambda i: (i, 0, 0, 0, 0))
        out_specs += [cache, cache]
        out_shape += [jax.ShapeDtypeStruct((n // SEQ, 1, SEQ, N_KV_HEADS, HEAD_DIM), F32)] * 2
    return pl.pallas_call(
        functools.partial(_ab_in_kernel, rope=rope),
        grid=(n // ROW_TILE,),
        in_specs=in_specs,
        out_specs=out_specs,
        out_shape=out_shape,
        compiler_params=_params(1),
        name="ab_in_rope" if rope else "ab_in",
    )(*args)


def _rope_tables():
    rows = DEC_SEQ // GRID_W
    row = jnp.repeat(jnp.arange(rows, dtype=F32), GRID_W)
    col = jnp.tile(jnp.arange(GRID_W, dtype=F32), rows)
    n_freq = HEAD_DIM // 4
    freqs = ROPE_BASE ** (-jnp.arange(n_freq, dtype=F32) / n_freq)
    ang = jnp.concatenate([row[:, None] * freqs, col[:, None] * freqs], axis=-1)
    cos = jnp.repeat(jnp.cos(ang), 2, axis=-1)
    sin = jnp.repeat(jnp.sin(ang), 2, axis=-1)
    sign = jnp.tile(jnp.array([-1.0, 1.0], F32), HEAD_DIM // 2)
    return cos, sin * sign


def _sink_attention_head(s_list, v_list, sink):
    m = sink
    for s in s_list:
        m = jnp.maximum(m, jnp.max(s, axis=-1, keepdims=True))
    p_list = [jnp.exp(s - m) for s in s_list]
    denom = jnp.exp(sink - m)
    for p in p_list:
        denom = denom + jnp.sum(p, axis=-1, keepdims=True)
    out = None
    for p, v in zip(p_list, v_list):
        o = _dot(p.astype(BF16), v)
        out = o if out is None else out + o
    return out * (1.0 / denom)


def _head_cols(h):
    return slice(h * HEAD_DIM, (h + 1) * HEAD_DIM)


def _ctx_attn_kernel(sink_ref, q_ref, k_ref, v_ref, o_ref):
    def scores(item):
        r, h = item
        rows = slice(r * SEQ, (r + 1) * SEQ)
        kh = k_ref[rows, _head_cols(h // Q_PER_KV)].astype(BF16)
        return _dot_nt(q_ref[rows, _head_cols(h)], kh)

    def finish(item, s):
        r, h = item
        rows = slice(r * SEQ, (r + 1) * SEQ)
        vh = v_ref[rows, _head_cols(h // Q_PER_KV)].astype(BF16)
        o_ref[rows, _head_cols(h)] = _sink_attention_head([s], [vh], sink_ref[h]).astype(o_ref.dtype)

    items = [(r, h) for r in range(REQ_TILE) for h in range(N_Q_HEADS)]
    _software_pipeline(items, scores, finish)


def _ctx_attention(q, k, v, sink):
    seq = lambda width: pl.BlockSpec((REQ_TILE * SEQ, width), lambda b: (b, 0))
    return pl.pallas_call(
        _ctx_attn_kernel,
        grid=(BATCH // REQ_TILE,),
        in_specs=[pl.BlockSpec(memory_space=pltpu.SMEM), seq(ATTN_WIDTH), seq(KV_WIDTH), seq(KV_WIDTH)],
        out_specs=seq(ATTN_WIDTH),
        out_shape=jax.ShapeDtypeStruct((N_PROMPT, ATTN_WIDTH), BF16),
        compiler_params=_params(1),
        name="context_attention",
    )(sink, q, k, v)


LAT_Q = 256


def _lat_attn_kernel(sink_ref, q_ref, kp_ref, kc_ref, kn_ref, vp_ref, vc_ref, vn_ref,
                     kx_ref, vx_ref, o_ref):
    n = pl.program_id(1)
    nb = pl.num_programs(1)
    q = q_ref[...]
    kw = jnp.concatenate([kp_ref[...], kc_ref[...], kn_ref[...]], axis=0)
    vw = jnp.concatenate([vp_ref[...], vc_ref[...], vn_ref[...]], axis=0)
    kx = kx_ref[...].astype(BF16)
    vx = vx_ref[...].astype(BF16)
    n_key = LAT_Q + 2 * WINDOW
    qi = lax.broadcasted_iota(I32, (LAT_Q, n_key), 0)
    kj = lax.broadcasted_iota(I32, (LAT_Q, n_key), 1)
    rel = kj - qi
    valid = (rel >= 0) & (rel <= 2 * WINDOW)
    valid = valid & ((n > 0) | (kj >= WINDOW)) & ((n < nb - 1) | (kj < LAT_Q + WINDOW))
    def scores(h):
        sl = _head_cols(h // Q_PER_KV)
        qh = q[:, _head_cols(h)]
        s_ctx = _dot_nt(qh, kx[:, sl])
        s_win = jnp.where(valid, _dot_nt(qh, kw[:, sl]), NEG_INF)
        return s_ctx, s_win

    def finish(h, s):
        sl = _head_cols(h // Q_PER_KV)
        o_ref[:, _head_cols(h)] = _sink_attention_head(list(s), [vx[:, sl], vw[:, sl]],
                                                       sink_ref[h]).astype(o_ref.dtype)

    _software_pipeline(range(N_Q_HEADS), scores, finish)


def _lat_attention(q, k, v, k_ctx, v_ctx, sink):
    nb = DEC_SEQ // LAT_Q
    nw = DEC_SEQ // WINDOW
    per = LAT_Q // WINDOW
    cur = lambda b, n: (b * nb + n, 0)
    prev = lambda b, n: (b * nw + jnp.maximum(n * per - 1, 0), 0)
    nxt = lambda b, n: (b * nw + jnp.minimum((n + 1) * per, nw - 1), 0)
    tile = lambda width: pl.BlockSpec((LAT_Q, width), cur)
    edge = lambda imap: pl.BlockSpec((WINDOW, KV_WIDTH), imap)
    ctx = pl.BlockSpec((PAST_LEN, KV_WIDTH), lambda b, n: (b, 0))
    return pl.pallas_call(
        _lat_attn_kernel,
        grid=(DEC_BATCH, nb),
        in_specs=[pl.BlockSpec(memory_space=pltpu.SMEM), tile(ATTN_WIDTH),
                  edge(prev), tile(KV_WIDTH), edge(nxt),
                  edge(prev), tile(KV_WIDTH), edge(nxt), ctx, ctx],
        out_specs=tile(ATTN_WIDTH),
        out_shape=jax.ShapeDtypeStruct((N_SAMPLE, ATTN_WIDTH), BF16),
        compiler_params=_params(2),
        name="latent_attention",
    )(sink, q, k, k, k, v, v, v, k_ctx, v_ctx)


RNN_CHUNK = 256


def _rglru_kernel(xr_ref, xg_ref, cw_ref, cb_ref, wa_ref, ba_ref, wx_ref, bx_ref, lam_ref,
                  h0f_ref, h0b_ref, y_ref, hfl_ref, hbf_ref, xc_s, af_s, ab_s, uf_s, ub_s, *, rows, seq_len):
    n_seq = rows // seq_len
    n_chunk = rows // RNN_CHUNK
    cw = cw_ref[...]
    cb = cb_ref[...]
    zeros_halo = jnp.zeros((SUBLANES, RNN_WIDTH), F32)
    row8 = lax.broadcasted_iota(I32, (SUBLANES, RNN_WIDTH), 0)

    def conv_chunk(c):
        r0 = c * RNN_CHUNK
        first = r0 % seq_len == 0
        last = (r0 + RNN_CHUNK) % seq_len == 0
        before = zeros_halo if first else xr_ref[r0 - SUBLANES:r0, :]
        after = zeros_halo if last else xr_ref[r0 + RNN_CHUNK:r0 + RNN_CHUNK + SUBLANES, :]
        win = jnp.concatenate([before, xr_ref[r0:r0 + RNN_CHUNK, :], after], axis=0)
        xc = cb
        n_win = RNN_CHUNK + 2 * SUBLANES
        for i in range(CONV_W):
            shift = (CONV_PAD_LEFT - i) % n_win
            rolled = win if shift == 0 else pltpu.roll(win, shift, 0)
            xc = xc + rolled[SUBLANES:SUBLANES + RNN_CHUNK, :] * cw[i:i + 1, :]
        return xc

    def group_scan(a, u, reverse):
        for k in (1, 2, 4):
            if reverse:
                shift, ok = SUBLANES - k, row8 < SUBLANES - k
            else:
                shift, ok = k, row8 >= k
            a_nb = jnp.where(ok, pltpu.roll(a, shift, 0), 1.0)
            u_nb = jnp.where(ok, pltpu.roll(u, shift, 0), 0.0)
            u = a * u_nb + u
            a = a * a_nb
        return a, u

    for c in range(n_chunk):
        xc_s[c * RNN_CHUNK:(c + 1) * RNN_CHUNK, :] = conv_chunk(c)

    for d, (a_s, u_s) in enumerate(((af_s, uf_s), (ab_s, ub_s))):
        neg = -lam_ref[d:d + 1, :]
        softplus = jnp.maximum(neg, 0.0) + jnp.log1p(jnp.exp(-jnp.abs(neg)))
        decay = -LRU_C * softplus
        wa = wa_ref[d]
        wx = wx_ref[d]
        ba = ba_ref[d:d + 1, :]
        bx = bx_ref[d:d + 1, :]
        for c in range(n_chunk):
            xc = xc_s[c * RNN_CHUNK:(c + 1) * RNN_CHUNK, :]
            xcb = xc.astype(BF16)
            r = _sigmoid(_dot(xcb, wa) + ba)
            i = _sigmoid(_dot(xcb, wx) + bx)
            log_a = r * decay
            a = jnp.exp(log_a)
            a_s[c * RNN_CHUNK:(c + 1) * RNN_CHUNK, :] = a
            one_minus_a2 = -jnp.tanh(log_a) * (a * a + 1.0)
            u_s[c * RNN_CHUNK:(c + 1) * RNN_CHUNK, :] = jnp.sqrt(one_minus_a2) * (i * xc)

    n_group = seq_len // SUBLANES

    def body(g, carries):
        fwd, bwd = carries
        new_f, new_b = [], []
        for s in range(n_seq):
            rf = pl.multiple_of(s * seq_len + g * SUBLANES, SUBLANES)
            a, u = group_scan(af_s[pl.ds(rf, SUBLANES), :], uf_s[pl.ds(rf, SUBLANES), :], False)
            h = u + a * fwd[s]
            uf_s[pl.ds(rf, SUBLANES), :] = h
            new_f.append(h[SUBLANES - 1:SUBLANES, :])
            rb = pl.multiple_of(s * seq_len + (n_group - 1 - g) * SUBLANES, SUBLANES)
            a, u = group_scan(ab_s[pl.ds(rb, SUBLANES), :], ub_s[pl.ds(rb, SUBLANES), :], True)
            h = u + a * bwd[s]
            ub_s[pl.ds(rb, SUBLANES), :] = h
            new_b.append(h[0:1, :])
        return tuple(new_f), tuple(new_b)

    init = (tuple(h0f_ref[s:s + 1, :] for s in range(n_seq)), tuple(h0b_ref[s:s + 1, :] for s in range(n_seq)))
    last_f, first_b = lax.fori_loop(0, n_group, body, init)

    for s in range(n_seq):
        hfl_ref[s:s + 1, :] = last_f[s]
        hbf_ref[s:s + 1, :] = first_b[s]
    for c in range(n_chunk):
        sl = slice(c * RNN_CHUNK, (c + 1) * RNN_CHUNK)
        y_ref[sl, :] = ((uf_s[sl, :] + ub_s[sl, :]) * _gelu_tanh(xg_ref[sl, :])).astype(y_ref.dtype)


def _rglru(xr, xg, conv_w, conv_b, wa, ba, wx, bx, lam, h0f, h0b, rows, seq_len):
    n = xr.shape[0]
    n_seq = rows // seq_len
    row = pl.BlockSpec((rows, RNN_WIDTH), lambda i: (i, 0))
    full = lambda shape: pl.BlockSpec(shape, lambda i: (0,) * len(shape))
    state = pl.BlockSpec((None, n_seq, RNN_WIDTH), lambda i: (i, 0, 0))
    state_shape = jax.ShapeDtypeStruct((n // rows, n_seq, RNN_WIDTH), F32)
    return pl.pallas_call(
        functools.partial(_rglru_kernel, rows=rows, seq_len=seq_len),
        grid=(n // rows,),
        in_specs=[row, row, full((CONV_W, RNN_WIDTH)), full((1, RNN_WIDTH)),
                  full((2, RNN_WIDTH, RNN_WIDTH)), full((2, RNN_WIDTH)),
                  full((2, RNN_WIDTH, RNN_WIDTH)), full((2, RNN_WIDTH)), full((2, RNN_WIDTH)),
                  state, state],
        out_specs=[row, state, state],
        out_shape=[jax.ShapeDtypeStruct((n, RNN_WIDTH), BF16), state_shape, state_shape],
        scratch_shapes=[pltpu.VMEM((rows, RNN_WIDTH), F32)] * 5,
        compiler_params=_params(1),
        name="rglru_%d" % seq_len,
    )(xr, xg, conv_w, conv_b, wa, ba, wx, bx, lam, h0f, h0b)


def _block_diag_dense(w):
    eye = jnp.eye(RNN_BLOCKS, dtype=w.dtype)
    dense = w[:, :, :, None, :] * eye[None, :, None, :, None]
    return dense.reshape(2, RNN_WIDTH, RNN_WIDTH).astype(BF16)


SUB_TILE = 256
SUB_TILES = tuple(slice(s, s + SUB_TILE) for s in range(0, ROW_TILE, SUB_TILE))
SGU_TILE = 512
SGU_TILES = tuple(slice(s, s + SGU_TILE) for s in range(0, ROW_TILE, SGU_TILE))


def _residual_router(rows, x, o, m, lg_ref, lb_ref, rw_ref, x1_ref, xm_ref, aff_ref):
    x1 = _layer_norm(ALPHA * x + m[2:3] * o, lg_ref[...], lb_ref[...])
    x1_ref[rows, :] = x1
    xm = (x1 * (1.0 + m[4:5]) + m[3:4]).astype(BF16)
    xm_ref[rows, :] = xm
    lgt = _dot(xm, rw_ref[...])
    lane = lax.broadcasted_iota(I32, lgt.shape, 1)
    lgt = jnp.where(lane < N_EXPERTS, lgt, NEG_INF)
    ex = jnp.exp(lgt - jnp.max(lgt, axis=-1, keepdims=True))
    aff = ex / jnp.sum(ex, axis=-1, keepdims=True)
    aff_ref[rows, :] = aff[:, :N_EXPERTS]


def _mix_out_kernel(att_ref, rnn_ref, x_ref, mod_ref, w_ref, lg_ref, lb_ref, rw_ref,
                    x1_ref, xm_ref, aff_ref):
    m = mod_ref[...]

    def project(rows):
        return _dot(att_ref[rows, :], w_ref[:ATTN_WIDTH, :]) + _dot(rnn_ref[rows, :], w_ref[ATTN_WIDTH:, :])

    def finish(rows, o):
        _residual_router(rows, x_ref[rows, :], o, m, lg_ref, lb_ref, rw_ref, x1_ref, xm_ref, aff_ref)

    _software_pipeline(SUB_TILES, project, finish)


def _sgu_kernel(x_ref, mod_ref, win_ref, bin_ref, sg_ref, sb_ref, spw_ref, spb_ref, w_ref, lg_ref, lb_ref, rw_ref,
                x1_ref, xm_ref, aff_ref, v_s, gated_s):
    m = mod_ref[...]

    def project(rows):
        h = x_ref[rows, :] * (1.0 + m[1:2]) + m[0:1]
        return _dot(h.astype(BF16), win_ref[...])

    def gate(rows, p):
        p = _gelu_tanh(p + bin_ref[...])
        v_s[rows, :] = _layer_norm(p[:, SGU_WIDTH:], sg_ref[...], sb_ref[...]).astype(v_s.dtype)
        for c in range(rows.start, rows.stop, CHUNK):
            local = slice(c - rows.start, c - rows.start + CHUNK)
            for g in range(SGU_GROUPS):
                cols = slice(g * SGU_GROUP_W, (g + 1) * SGU_GROUP_W)
                mixed = _dot(spw_ref[g], v_s[c:c + CHUNK, cols]) + spb_ref[:, cols]
                gated_s[c:c + CHUNK, cols] = (p[local, cols] * mixed).astype(gated_s.dtype)
        return _dot(gated_s[rows, :], w_ref[...])

    def finish(rows, o):
        _residual_router(rows, x_ref[rows, :], o, m, lg_ref, lb_ref, rw_ref, x1_ref, xm_ref, aff_ref)

    _software_pipeline(SGU_TILES, project, gate, finish)


def _mixer_out(kernel, name, acts, x, mod_l, weights, ln_g, ln_b, router_w, group0, rows_per_group,
               scratch=()):
    n = x.shape[0]
    row = lambda width: pl.BlockSpec((ROW_TILE, width), lambda i: (i, 0))
    full = lambda a: pl.BlockSpec(a.shape, lambda i: (0,) * a.ndim)
    return pl.pallas_call(
        kernel,
        grid=(n // ROW_TILE,),
        in_specs=([row(a.shape[1]) for a in acts]
                  + [row(D_MODEL), pl.BlockSpec((None, 6, D_MODEL), _group_map(group0, rows_per_group))]
                  + [full(w) for w in weights] + [full(ln_g), full(ln_b), full(router_w)]),
        out_specs=[row(D_MODEL), row(D_MODEL), row(N_EXPERTS)],
        out_shape=[jax.ShapeDtypeStruct((n, D_MODEL), F32),
                   jax.ShapeDtypeStruct((n, D_MODEL), BF16),
                   jax.ShapeDtypeStruct((n, N_EXPERTS), F32)],
        scratch_shapes=list(scratch),
        compiler_params=_params(1),
        name=name,
    )(*acts, x, mod_l, *weights, ln_g, ln_b, router_w)


def _route_kernel(*refs, n_req, tokens, cap, place_in_kernel):
    n_blk = tokens // PREFIX_BLOCK
    if place_in_kernel:
        aff_ref, tri_ref, place_ref, posm_ref, posr_ref, affr_ref, blk_ref = refs
        n_col = place_ref.shape[2]
        aff = jnp.zeros((tokens, n_col), F32)
        for b in range(n_req):
            piece = aff_ref[b * tokens:(b + 1) * tokens, :]
            hi = piece.astype(BF16)
            rem = piece - hi.astype(F32)
            mid = rem.astype(BF16)
            low = (rem - mid.astype(F32)).astype(BF16)
            place = place_ref[b]
            aff = aff + ((_dot(hi, place) + _dot(mid, place)) + _dot(low, place))
    else:
        aff_ref, tri_ref, posm_ref, posr_ref, affr_ref, blk_ref = refs
        aff = aff_ref[...]
        n_col = aff.shape[1]

    def bisect(_, lo_hi):
        lo, hi = lo_hi
        mid = lo + ((hi - lo) >> 1)
        cnt = jnp.sum(jnp.where(aff >= pltpu.bitcast(mid, F32), 1.0, 0.0), axis=0, keepdims=True)
        ge = cnt >= cap
        return jnp.where(ge, mid, lo), jnp.where(ge, hi, mid)

    lo0 = jnp.zeros((1, n_col), I32)
    hi0 = jnp.full((1, n_col), 0x7F800000, I32)
    thr_bits, _ = lax.fori_loop(0, 31, bisect, (lo0, hi0))
    thr = pltpu.bitcast(thr_bits, F32)
    above = pltpu.bitcast(thr_bits + 1, F32)

    tri = tri_ref[...]

    def excl_prefix(x01):
        outs, before, inside = [], [], []
        off = jnp.zeros((1, n_col), F32)
        for blk in range(n_blk):
            xb = x01[blk * PREFIX_BLOCK:(blk + 1) * PREFIX_BLOCK, :]
            outs.append(_dot(tri, xb.astype(BF16)) + off)
            cnt = jnp.sum(xb, axis=0, keepdims=True)
            before.append(off)
            inside.append(cnt)
            off = off + cnt
        return (outs[0] if n_blk == 1 else jnp.concatenate(outs, axis=0)), before + inside

    gt = jnp.where(aff >= above, 1.0, 0.0)
    eq = jnp.where(aff >= thr, 1.0, 0.0) - gt
    need = cap - jnp.sum(gt, axis=0, keepdims=True)
    eq_rank, _ = excl_prefix(eq)
    sel = gt + jnp.where(eq_rank < need, eq, 0.0)
    pos, blk_rows = excl_prefix(sel)
    posm = jnp.where(sel > 0.0, pos, -1.0)
    posr_ref[...] = posm.T
    affr_ref[...] = aff.T
    for i, row in enumerate(blk_rows):
        blk_ref[i:i + 1, :] = row
    if place_in_kernel:
        posm_bf16 = posm.astype(BF16)
        for b in range(n_req):
            posm_ref[b * tokens:(b + 1) * tokens, :] = _dot_nt(posm_bf16, place_ref[b])
    else:
        posm_ref[...] = posm


PLACE_IN_KERNEL_MAX_REQ = 8


def _route_requests(aff, tri, n_req, tokens, cap):
    n_real = n_req * N_EXPERTS
    n_col = -(-n_real // LANES) * LANES
    n_blk = tokens // PREFIX_BLOCK
    place_in_kernel = n_req <= PLACE_IN_KERNEL_MAX_REQ
    whole = lambda shape: pl.BlockSpec(shape, lambda i: (0,) * len(shape))
    if place_in_kernel:
        col = jnp.arange(n_col)
        place = (col[None, None, :] == (jnp.arange(n_req)[:, None, None] * N_EXPERTS
                                        + jnp.arange(N_EXPERTS)[None, :, None])).astype(BF16)
        args = (aff, tri, place)
        token_major = (n_req * tokens, N_EXPERTS)
        in_specs = [whole(token_major), whole((PREFIX_BLOCK, PREFIX_BLOCK)), whole(place.shape)]
    else:
        aff_t = aff.reshape(n_req, tokens, N_EXPERTS).transpose(1, 0, 2).reshape(tokens, n_real)
        args = (jnp.pad(aff_t, ((0, 0), (0, n_col - n_real))), tri)
        token_major = (tokens, n_col)
        in_specs = [whole(token_major), whole((PREFIX_BLOCK, PREFIX_BLOCK))]
    posm, posr, affr, blk = pl.pallas_call(
        functools.partial(_route_kernel, n_req=n_req, tokens=tokens, cap=cap, place_in_kernel=place_in_kernel),
        grid=(1,),
        in_specs=in_specs,
        out_specs=[whole(token_major), whole((n_col, tokens)), whole((n_col, tokens)), whole((2 * n_blk, n_col))],
        out_shape=[jax.ShapeDtypeStruct(token_major, F32),
                   jax.ShapeDtypeStruct((n_col, tokens), F32),
                   jax.ShapeDtypeStruct((n_col, tokens), F32),
                   jax.ShapeDtypeStruct((2 * n_blk, n_col), F32)],
        compiler_params=_params(1),
        name="route_%d" % tokens,
    )(*args)
    if not place_in_kernel:
        posm = posm[:, :n_real].reshape(tokens, n_req, N_EXPERTS).transpose(1, 0, 2).reshape(n_req * tokens, N_EXPERTS)
    blk = blk[:, :n_real].astype(I32).reshape(2, n_blk, n_req, N_EXPERTS).transpose(0, 2, 1, 3)
    expert_major = lambda a: a.reshape(n_col // N_EXPERTS, N_EXPERTS, tokens)
    return posm, expert_major(posr), expert_major(affr), blk[0], blk[1]


def _slot_gate(hit, aff_row):
    g = jnp.sum(jnp.where(hit, aff_row, 0.0), axis=-1, keepdims=True)
    return jnp.broadcast_to(g, (hit.shape[0], LANES))


def _gather_prompt_kernel(x_ref, posr_ref, affr_ref, xg_ref, g_ref):
    slot = lax.broadcasted_iota(I32, (CAP_P, SEQ), 0).astype(F32)
    for r in range(REQ_TILE):
        tokens = slice(r * SEQ, (r + 1) * SEQ)
        slots = slice(r * CAP_P, (r + 1) * CAP_P)
        posr = posr_ref[r]
        affr = affr_ref[r]
        hits = [posr[e:e + 1, :] == slot for e in range(N_EXPERTS)]
        onehot = jnp.concatenate([jnp.where(h, 1.0, 0.0) for h in hits], axis=0).astype(BF16)
        xg = _dot(onehot, x_ref[tokens, :])
        for e in range(N_EXPERTS):
            xg_ref[e, slots, :] = xg[e * CAP_P:(e + 1) * CAP_P, :].astype(xg_ref.dtype)
            g_ref[e, slots, :] = _slot_gate(hits[e], affr[e:e + 1, :])


def _gather_prompt(xm, posr, affr):
    return pl.pallas_call(
        _gather_prompt_kernel,
        grid=(BATCH // REQ_TILE,),
        in_specs=[pl.BlockSpec((REQ_TILE * SEQ, D_MODEL), lambda b: (b, 0)),
                  pl.BlockSpec((REQ_TILE, N_EXPERTS, SEQ), lambda b: (b, 0, 0)),
                  pl.BlockSpec((REQ_TILE, N_EXPERTS, SEQ), lambda b: (b, 0, 0))],
        out_specs=[pl.BlockSpec((N_EXPERTS, REQ_TILE * CAP_P, D_MODEL), lambda b: (0, b, 0)),
                   pl.BlockSpec((N_EXPERTS, REQ_TILE * CAP_P, LANES), lambda b: (0, b, 0))],
        out_shape=[jax.ShapeDtypeStruct((N_EXPERTS, SLOTS_P, D_MODEL), BF16),
                   jax.ShapeDtypeStruct((N_EXPERTS, SLOTS_P, LANES), F32)],
        compiler_params=_params(1),
        name="gather_prompt",
    )(xm, posr, affr)


TOKEN_BLOCK = PREFIX_BLOCK
N_TOKEN_BLOCKS = DEC_SEQ // TOKEN_BLOCK
SLOT_ALIGN = 16
WINDOW_SLOTS = 64
LAST_WINDOW = CAP_S - WINDOW_SLOTS
ALL_WINDOWS = N_EXPERTS * WINDOW_SLOTS


def _window_plan(before, inside):
    start = (before // SLOT_ALIGN) * SLOT_ALIGN
    span = before - start + inside
    passes = jnp.where(inside > 0, (span + WINDOW_SLOTS - 1) // WINDOW_SLOTS, 0)
    return start.reshape(-1), jnp.max(passes, axis=-1).reshape(-1)


def _window_bounds(start_ref, step, expert, k):
    lower = start_ref[step * N_EXPERTS + expert] + WINDOW_SLOTS * k
    begin = pl.multiple_of(jnp.minimum(lower, LAST_WINDOW), SLOT_ALIGN)
    return lower, begin


BLOCKS_PER_STEP = 2
STEP_TOKENS = BLOCKS_PER_STEP * TOKEN_BLOCK
STEPS_PER_REQ = N_TOKEN_BLOCKS // BLOCKS_PER_STEP


def _gather_sample_kernel(start_ref, passes_ref, x_ref, posr_ref, affr_ref, xg_ref, g_ref):
    t = pl.program_id(1)
    first_block = (pl.program_id(0) * STEPS_PER_REQ + t) * BLOCKS_PER_STEP

    @pl.when(t == 0)
    def _():
        xg_ref[...] = jnp.zeros_like(xg_ref)
        g_ref[...] = jnp.zeros_like(g_ref)

    row = lax.broadcasted_iota(I32, (WINDOW_SLOTS, TOKEN_BLOCK), 0).astype(F32)

    def pick(j, k):
        tokens = slice(j * TOKEN_BLOCK, (j + 1) * TOKEN_BLOCK)
        hits, begins = [], []
        for e in range(N_EXPERTS):
            lower, begin = _window_bounds(start_ref, first_block + j, e, k)
            pe = posr_ref[e:e + 1, tokens]
            hits.append((pe - begin.astype(F32) == row) & (pe >= lower.astype(F32)))
            begins.append(begin)
        onehot = jnp.concatenate([jnp.where(h, 1.0, 0.0) for h in hits], axis=0).astype(BF16)
        return _dot(onehot, x_ref[tokens, :]), hits, begins

    def place(j, picked):
        rows_x, hits, begins = picked
        tokens = slice(j * TOKEN_BLOCK, (j + 1) * TOKEN_BLOCK)
        for e in range(N_EXPERTS):
            win = pl.ds(begins[e], WINDOW_SLOTS)
            xg_ref[e, win, :] += rows_x[e * WINDOW_SLOTS:(e + 1) * WINDOW_SLOTS, :].astype(xg_ref.dtype)
            g_ref[e, win, :] += _slot_gate(hits[e], affr_ref[e:e + 1, tokens])

    _software_pipeline(range(BLOCKS_PER_STEP), lambda j: pick(j, 0), place)

    for j in range(BLOCKS_PER_STEP):
        def extra_pass(k, carry, j=j):
            place(j, pick(j, k))
            return carry

        lax.fori_loop(1, passes_ref[first_block + j], extra_pass, 0)


def _gather_sample(xm, posr, affr, starts, passes):
    expert_rows = pl.BlockSpec((None, N_EXPERTS, STEP_TOKENS), lambda b, t, s, p: (b, 0, t))
    grid_spec = pltpu.PrefetchScalarGridSpec(
        num_scalar_prefetch=2,
        grid=(DEC_BATCH, STEPS_PER_REQ),
        in_specs=[pl.BlockSpec((STEP_TOKENS, D_MODEL), lambda b, t, s, p: (b * STEPS_PER_REQ + t, 0)),
                  expert_rows, expert_rows],
        out_specs=[pl.BlockSpec((N_EXPERTS, CAP_S, D_MODEL), lambda b, t, s, p: (0, b, 0)),
                   pl.BlockSpec((N_EXPERTS, CAP_S, LANES), lambda b, t, s, p: (0, b, 0))],
    )
    return pl.pallas_call(
        _gather_sample_kernel,
        grid_spec=grid_spec,
        out_shape=[jax.ShapeDtypeStruct((N_EXPERTS, SLOTS_S, D_MODEL), BF16),
                   jax.ShapeDtypeStruct((N_EXPERTS, SLOTS_S, LANES), F32)],
        compiler_params=_params(2),
        name="gather_sample",
    )(starts, passes, xm, posr, affr)


FF_TILE = 512
SLOT_CHUNK = 1024


def _ffn_kernel(xp_ref, xs_ref, gp_ref, gs_ref, w1_ref, w3_ref, w2_ref, yp_ref, ys_ref, acc_ref):
    f = pl.program_id(1)
    n_f = pl.num_programs(1)
    @pl.when(f == 0)
    def _():
        acc_ref[...] = jnp.zeros_like(acc_ref)

    w1 = w1_ref[...].astype(BF16)
    w3 = w3_ref[...].astype(BF16)
    w2 = w2_ref[...].astype(BF16)

    def up_project(chunk):
        x_ref, c, _ = chunk
        x = x_ref[c * SLOT_CHUNK:(c + 1) * SLOT_CHUNK, :]
        return _dot(x, w1), _dot(x, w3)

    def down_project(chunk, h):
        _, c, base = chunk
        h1, h3 = h
        hid = (h1 * jax.nn.sigmoid(h1) * h3).astype(BF16)
        rows = slice(base + c * SLOT_CHUNK, base + (c + 1) * SLOT_CHUNK)
        acc_ref[rows, :] += _dot(hid, w2)

    chunks = [(x_ref, c, base) for x_ref, n_slot, base in ((xp_ref, SLOTS_P, 0), (xs_ref, SLOTS_S, SLOTS_P))
              for c in range(n_slot // SLOT_CHUNK)]
    _software_pipeline(chunks, up_project, down_project)

    @pl.when(f == n_f - 1)
    def _():
        for y_ref, g_ref, n_slot, base in ((yp_ref, gp_ref, SLOTS_P, 0), (ys_ref, gs_ref, SLOTS_S, SLOTS_P)):
            for c in range(n_slot // SLOT_CHUNK):
                rows = slice(c * SLOT_CHUNK, (c + 1) * SLOT_CHUNK)
                gate = jnp.concatenate([g_ref[rows, :]] * (D_MODEL // LANES), axis=1)
                y_ref[rows, :] = (acc_ref[base + c * SLOT_CHUNK:base + (c + 1) * SLOT_CHUNK, :]
                                  * gate).astype(y_ref.dtype)


def _expert_ffn(xgp, xgs, gp, gs, w1, w3, w2, layer):
    slots = lambda n, width: pl.BlockSpec((None, n, width), lambda e, f: (e, 0, 0))
    return pl.pallas_call(
        _ffn_kernel,
        grid=(N_EXPERTS, EXPERT_FF // FF_TILE),
        in_specs=[slots(SLOTS_P, D_MODEL), slots(SLOTS_S, D_MODEL), slots(SLOTS_P, LANES), slots(SLOTS_S, LANES),
                  pl.BlockSpec((None, None, D_MODEL, FF_TILE), lambda e, f: (layer, e, 0, f)),
                  pl.BlockSpec((None, None, D_MODEL, FF_TILE), lambda e, f: (layer, e, 0, f)),
                  pl.BlockSpec((None, None, FF_TILE, D_MODEL), lambda e, f: (layer, e, f, 0))],
        out_specs=[slots(SLOTS_P, D_MODEL), slots(SLOTS_S, D_MODEL)],
        out_shape=[jax.ShapeDtypeStruct((N_EXPERTS, SLOTS_P, D_MODEL), BF16),
                   jax.ShapeDtypeStruct((N_EXPERTS, SLOTS_S, D_MODEL), BF16)],
        scratch_shapes=[pltpu.VMEM((SLOTS_P + SLOTS_S, D_MODEL), F32)],
        compiler_params=_params(2),
        name="expert_ffn",
    )(xgp, xgs, gp, gs, w1, w3, w2)


def _combine_prompt_kernel(posm_ref, y_ref, x_ref, mod_ref, exp_ref, lg_ref, lb_ref, o_ref):
    m = mod_ref[...]
    slot = (lax.broadcasted_iota(I32, (SEQ, N_EXPERTS * CAP_P), 1) & (CAP_P - 1)).astype(F32)
    def scatter(r):
        tokens = slice(r * SEQ, (r + 1) * SEQ)
        spread = _dot(posm_ref[tokens, :].astype(BF16), exp_ref[...])
        onehot = jnp.where(spread == slot, 1.0, 0.0).astype(BF16)
        y = jnp.concatenate([y_ref[e, r * CAP_P:(r + 1) * CAP_P, :] for e in range(N_EXPERTS)], axis=0)
        return _dot(onehot, y)

    def finish(r, f):
        tokens = slice(r * SEQ, (r + 1) * SEQ)
        o_ref[tokens, :] = _layer_norm(ALPHA * x_ref[tokens, :] + m[5:6] * f, lg_ref[...], lb_ref[...])

    _software_pipeline(range(REQ_TILE), scatter, finish)


def _combine_prompt(posm, y, x1, mod_l, expand, ln_g, ln_b):
    full = lambda a: pl.BlockSpec(a.shape, lambda b: (0,) * a.ndim)
    return pl.pallas_call(
        _combine_prompt_kernel,
        grid=(BATCH // REQ_TILE,),
        in_specs=[pl.BlockSpec((REQ_TILE * SEQ, N_EXPERTS), lambda b: (b, 0)),
                  pl.BlockSpec((N_EXPERTS, REQ_TILE * CAP_P, D_MODEL), lambda b: (0, b, 0)),
                  pl.BlockSpec((REQ_TILE * SEQ, D_MODEL), lambda b: (b, 0)),
                  pl.BlockSpec((None, 6, D_MODEL), lambda b: (0, 0, 0)),
                  full(expand), full(ln_g), full(ln_b)],
        out_specs=pl.BlockSpec((REQ_TILE * SEQ, D_MODEL), lambda b: (b, 0)),
        out_shape=jax.ShapeDtypeStruct((N_PROMPT, D_MODEL), F32),
        compiler_params=_params(1),
        name="combine_prompt",
    )(posm, y, x1, mod_l, expand, ln_g, ln_b)


def _combine_sample_kernel(start_ref, passes_ref, posm_ref, y_ref, x_ref, mod_ref, exp_ref, off_ref,
                           lg_ref, lb_ref, o_ref):
    first_block = (pl.program_id(0) * STEPS_PER_REQ + pl.program_id(1)) * BLOCKS_PER_STEP
    exp_bf16 = exp_ref[...]
    offset = off_ref[...]
    m = mod_ref[...]

    def tokens_of(j):
        return slice(j * TOKEN_BLOCK, (j + 1) * TOKEN_BLOCK)

    def one_pass(j, k):
        spread = _dot(posm_ref[tokens_of(j), :].astype(BF16), exp_bf16)
        begin_row = jnp.zeros((1, ALL_WINDOWS), F32)
        lower_row = jnp.zeros((1, ALL_WINDOWS), F32)
        windows = []
        for e in range(N_EXPERTS):
            lower, begin = _window_bounds(start_ref, first_block + j, e, k)
            lanes_e = exp_bf16[e:e + 1, :].astype(F32)
            begin_row = begin_row + begin.astype(F32) * lanes_e
            lower_row = lower_row + lower.astype(F32) * lanes_e
            windows.append(y_ref[e, pl.ds(begin, WINDOW_SLOTS), :])
        hit = (spread - begin_row == offset) & (spread >= lower_row)
        onehot = jnp.where(hit, 1.0, 0.0).astype(BF16)
        return _dot(onehot, jnp.concatenate(windows, axis=0))

    def finish(j, f):
        rows = tokens_of(j)
        o_ref[rows, :] = _layer_norm(ALPHA * x_ref[rows, :] + m[5:6] * f, lg_ref[...], lb_ref[...])

    _software_pipeline(range(BLOCKS_PER_STEP), lambda j: one_pass(j, 0), finish)

    for j in range(BLOCKS_PER_STEP):
        n_pass = passes_ref[first_block + j]

        @pl.when(n_pass > 1)
        def _(j=j, n_pass=n_pass):
            f = lax.fori_loop(1, n_pass, lambda k, f: f + one_pass(j, k), one_pass(j, 0))
            finish(j, f)


def _combine_sample(posm, y, x1, mod_l, ln_g, ln_b, starts, passes):
    lane = jnp.arange(ALL_WINDOWS)
    expand = ((lane[None, :] // WINDOW_SLOTS) == jnp.arange(N_EXPERTS)[:, None]).astype(BF16)
    offset = (lane % WINDOW_SLOTS).astype(F32).reshape(1, ALL_WINDOWS)
    row = lambda width: pl.BlockSpec((STEP_TOKENS, width), lambda b, t, s, p: (b * STEPS_PER_REQ + t, 0))
    full = lambda a: pl.BlockSpec(a.shape, lambda b, t, s, p: (0,) * a.ndim)
    grid_spec = pltpu.PrefetchScalarGridSpec(
        num_scalar_prefetch=2,
        grid=(DEC_BATCH, STEPS_PER_REQ),
        in_specs=[row(N_EXPERTS),
                  pl.BlockSpec((N_EXPERTS, CAP_S, D_MODEL), lambda b, t, s, p: (0, b, 0)),
                  row(D_MODEL),
                  pl.BlockSpec((None, 6, D_MODEL), lambda b, t, s, p: (1 + b, 0, 0)),
                  full(expand), full(offset), full(ln_g), full(ln_b)],
        out_specs=row(D_MODEL),
    )
    return pl.pallas_call(
        _combine_sample_kernel,
        grid_spec=grid_spec,
        out_shape=jax.ShapeDtypeStruct((N_SAMPLE, D_MODEL), F32),
        compiler_params=_params(2),
        name="combine_sample",
    )(starts, passes, posm, y, x1, mod_l, expand, offset, ln_g, ln_b)


def _moe(prompt, sample, mod_l, w1, w3, w2, layer, ln_g, ln_b, tri, expand):
    x1_p, xm_p, aff_p = prompt
    x1_s, xm_s, aff_s = sample
    posm_p, posr_p, affr_p, _, _ = _route_requests(aff_p, tri, BATCH, SEQ, CAP_P)
    posm_s, posr_s, affr_s, before_s, inside_s = _route_requests(aff_s, tri, DEC_BATCH, DEC_SEQ, CAP_S)
    starts, passes = _window_plan(before_s, inside_s)
    xg_p, g_p = _gather_prompt(xm_p, posr_p, affr_p)
    xg_s, g_s = _gather_sample(xm_s, posr_s, affr_s, starts, passes)
    y_p, y_s = _expert_ffn(xg_p, xg_s, g_p, g_s, w1, w3, w2, layer)
    x2_p = _combine_prompt(posm_p, y_p, x1_p, mod_l, expand, ln_g, ln_b)
    x2_s = _combine_sample(posm_s, y_s, x1_s, mod_l, ln_g, ln_b, starts, passes)
    return x2_p, x2_s


def kernel(x_prompt, x_sample, cache_k, cache_v, state_rglru, c, c_ctx, mod_w, mod_b, ln_mix_g, ln_mix_b, ln_ffn_g, ln_ffn_b, ab_in_w, attn_sink, rnn_conv_w, rnn_conv_b, lru_wa, lru_ba, lru_wx, lru_bx, lru_lambda, ab_out_w, sgu_in_w, sgu_in_b, sgu_ln_g, sgu_ln_b, sgu_spatial_w, sgu_spatial_b, sgu_out_w, router_w, moe_w1, moe_w3, moe_w2):
    xp = x_prompt.reshape(N_PROMPT, D_MODEL)
    xs = x_sample.reshape(N_SAMPLE, D_MODEL)

    cvec = jnp.concatenate([c_ctx[None], c, jnp.zeros((SUBLANES - 1 - DEC_BATCH, D_MODEL), F32)], axis=0)
    mod = _modulation(cvec, mod_w, mod_b).reshape(DEPTH, SUBLANES, 6, D_MODEL)

    idx = jnp.arange(PREFIX_BLOCK)
    tri = (idx[None, :] < idx[:, None]).astype(BF16)
    lane = jnp.arange(N_EXPERTS * CAP_P)
    expand = ((lane[None, :] // CAP_P) == jnp.arange(N_EXPERTS)[:, None]).astype(BF16)
    rope_tables = _rope_tables()
    vec = lambda a: a.reshape(1, -1)
    router_pad = lambda w: jnp.pad(w, ((0, 0), (0, LANES - N_EXPERTS))).astype(BF16)

    new_k = new_v = new_state = None
    for l in range(DEPTH):
        e = l // 2
        mod_l = mod[l]
        rw = router_pad(router_w[l])
        lg, lb = vec(ln_mix_g[l]), vec(ln_mix_b[l])
        if l % 2 == 0:
            w_in = ab_in_w[e].astype(BF16)
            w_out = ab_out_w[e].astype(BF16)
            wa = _block_diag_dense(lru_wa[e])
            wx = _block_diag_dense(lru_wx[e])
            rnn_w = (rnn_conv_w[e], vec(rnn_conv_b[e]), wa, lru_ba[e], wx, lru_bx[e], lru_lambda[e])

            q, k, v, xr, xg, new_k, new_v = _ab_in(xp, mod_l, w_in, 0, N_PROMPT, None)
            att = _ctx_attention(q, k, v, attn_sink[e])
            rows = 8 * SEQ
            zeros = jnp.zeros((N_PROMPT // rows, rows // SEQ, RNN_WIDTH), F32)
            rnn, hf_last, hb_first = _rglru(xr, xg, *rnn_w, zeros, zeros, rows, SEQ)
            new_state = jnp.stack([hf_last.reshape(BATCH, RNN_WIDTH), hb_first.reshape(BATCH, RNN_WIDTH)], axis=1)
            prompt = _mixer_out(_mix_out_kernel, "mix_out", (att, rnn), xp, mod_l, (w_out,),
                                lg, lb, rw, 0, N_PROMPT)

            q, k, v, xr, xg = _ab_in(xs, mod_l, w_in, 1, DEC_SEQ, rope_tables)
            att = _lat_attention(q, k, v, cache_k[:, e].reshape(DEC_BATCH * PAST_LEN, KV_WIDTH),
                                 cache_v[:, e].reshape(DEC_BATCH * PAST_LEN, KV_WIDTH), attn_sink[e])
            h0 = state_rglru[:, e]
            rnn, _, _ = _rglru(xr, xg, *rnn_w, h0[:, 0][:, None, :], h0[:, 1][:, None, :], DEC_SEQ, DEC_SEQ)
            sample = _mixer_out(_mix_out_kernel, "mix_out", (att, rnn), xs, mod_l, (w_out,),
                                lg, lb, rw, 1, DEC_SEQ)
        else:
            w_in = sgu_in_w[e].astype(BF16)
            w_out = sgu_out_w[e].astype(BF16)
            spw = sgu_spatial_w[e].astype(BF16)
            spb = jnp.repeat(sgu_spatial_b[e].T, SGU_GROUP_W, axis=1)
            sgu_w = (w_in, vec(sgu_in_b[e]), vec(sgu_ln_g[e]), vec(sgu_ln_b[e]), spw, spb, w_out)
            scratch = (pltpu.VMEM((ROW_TILE, SGU_WIDTH), BF16), pltpu.VMEM((ROW_TILE, SGU_WIDTH), BF16))
            prompt = _mixer_out(_sgu_kernel, "sgu", (), xp, mod_l, sgu_w, lg, lb, rw, 0, N_PROMPT, scratch)
            sample = _mixer_out(_sgu_kernel, "sgu", (), xs, mod_l, sgu_w, lg, lb, rw, 1, DEC_SEQ, scratch)

        xp, xs = _moe(prompt, sample, mod_l, moe_w1, moe_w3, moe_w2, l,
                      vec(ln_ffn_g[l]), vec(ln_ffn_b[l]), tri, expand)

    return (xp.reshape(BATCH, SEQ, D_MODEL),
            xs.reshape(DEC_BATCH, DEC_SEQ, D_MODEL),
            new_k,
            new_v,
            new_state.reshape(BATCH, 1, 2, RNN_WIDTH))
```

```python
import functools

import jax
import jax.numpy as jnp
from jax import lax
from jax.experimental import pallas as pl
from jax.experimental.pallas import tpu as pltpu

F32 = jnp.float32
BF16 = jnp.bfloat16
I32 = jnp.int32

D_MODEL = 1024
BATCH = 32
SEQ = 256
DEPTH = 2
DEC_BATCH = 4
DEC_SEQ = 2048
PAST_LEN = 256
GRID_W = 64
HEAD_DIM = 128
N_Q_HEADS = 4
N_KV_HEADS = 2
Q_PER_KV = N_Q_HEADS // N_KV_HEADS
ATTN_WIDTH = N_Q_HEADS * HEAD_DIM
KV_WIDTH = N_KV_HEADS * HEAD_DIM
WINDOW = 128
BLOCK = 128
ATTN_SCALE = HEAD_DIM ** -0.5
ROPE_BASE = 10000.0
NEG_INF = -1e30
RNN_WIDTH = 512
RNN_BLOCKS = 8
RNN_BLOCK_W = RNN_WIDTH // RNN_BLOCKS
CONV_W = 4
CONV_PAD_LEFT = 2
LRU_C = 8.0
AB_IN_WIDTH = ATTN_WIDTH + 2 * KV_WIDTH + 2 * RNN_WIDTH
CHUNK = 128
SGU_WIDTH = D_MODEL
SGU_GROUPS = 8
SGU_GROUP_W = SGU_WIDTH // SGU_GROUPS
N_EXPERTS = 16
EXPERT_FF = 2048
EC_FACTOR = 2
ALPHA = (2 * DEPTH) ** 0.25
LN_EPS = 1e-6

N_PROMPT = BATCH * SEQ
N_SAMPLE = DEC_BATCH * DEC_SEQ
CAP_P = EC_FACTOR * SEQ // N_EXPERTS
CAP_S = EC_FACTOR * DEC_SEQ // N_EXPERTS
SLOTS_P = BATCH * CAP_P
SLOTS_S = DEC_BATCH * CAP_S

LANES = 128
SUBLANES = 8
ROW_TILE = 1024
REQ_TILE = 4
PREFIX_BLOCK = 256
VMEM_LIMIT = 56 * 1024 * 1024


def _params(n_axes=1):
    return pltpu.CompilerParams(dimension_semantics=("arbitrary",) * n_axes,
                                vmem_limit_bytes=VMEM_LIMIT)


def _layer_norm(x, g, b):
    mu = jnp.mean(x, axis=-1, keepdims=True)
    xc = x - mu
    var = jnp.mean(xc * xc, axis=-1, keepdims=True)
    return xc * lax.rsqrt(var + LN_EPS) * g + b


_LOG2_E = 1.4426950408889634
_GELU_K0 = -2.0 * 0.7978845608028654 * _LOG2_E
_GELU_K1 = _GELU_K0 * 0.044715


def _gelu_tanh(x):
    return x / (1.0 + jnp.exp2(x * (_GELU_K0 + _GELU_K1 * (x * x))))


def _sigmoid(z):
    return 0.5 + 0.5 * jnp.tanh(0.5 * z)


def _dot(a, b):
    return jnp.dot(a, b, preferred_element_type=F32)


def _dot_nt(a, b):
    return lax.dot_general(a, b, (((1,), (1,)), ((), ())), preferred_element_type=F32)


def _software_pipeline(items, first_stage, *later_stages):
    items = list(items)
    stages = (first_stage,) + later_stages
    carried = {}
    for t in range(len(items) + len(stages) - 1):
        for s, stage in enumerate(stages):
            i = t - s
            if 0 <= i < len(items):
                carried[i] = stage(items[i]) if s == 0 else stage(items[i], carried[i])


def _mod_kernel(c_ref, w_ref, b_ref, o_ref):
    c = c_ref[...]
    s = c * jax.nn.sigmoid(c)
    o_ref[...] = _dot(s.astype(BF16), w_ref[...].astype(BF16)) + b_ref[...]


def _modulation(cvec8, mod_w, mod_b):
    n_col = 6 * D_MODEL // D_MODEL
    return pl.pallas_call(
        _mod_kernel,
        grid=(DEPTH, n_col),
        in_specs=[pl.BlockSpec((SUBLANES, D_MODEL), lambda l, j: (0, 0)),
                  pl.BlockSpec((None, D_MODEL, D_MODEL), lambda l, j: (l, 0, j)),
                  pl.BlockSpec((None, 1, D_MODEL), lambda l, j: (l, 0, j))],
        out_specs=pl.BlockSpec((None, SUBLANES, D_MODEL), lambda l, j: (l, 0, j)),
        out_shape=jax.ShapeDtypeStruct((DEPTH, SUBLANES, 6 * D_MODEL), F32),
        compiler_params=_params(2),
        name="adaln_modulation",
    )(cvec8, mod_w, mod_b.reshape(DEPTH, 1, 6 * D_MODEL))


def _group_map(group0, rows_per_group):
    tiles_per_group = rows_per_group // ROW_TILE
    return lambda i: (group0 + i // tiles_per_group, 0, 0)


def _rope(t, cos, sin_signed):
    lane = lax.broadcasted_iota(I32, t.shape, 1)
    swapped = jnp.where((lane & 1) == 0, pltpu.roll(t, HEAD_DIM - 1, 1), pltpu.roll(t, 1, 1))
    return t * cos + swapped * sin_signed


def _ab_in_kernel(*refs, rope):
    if rope:
        x_ref, mod_ref, w_ref, cos_ref, sin_ref, q_ref, k_ref, v_ref, xr_ref, xg_ref = refs
    else:
        x_ref, mod_ref, w_ref, q_ref, k_ref, v_ref, xr_ref, xg_ref, ck_ref, cv_ref = refs
    m = mod_ref[...]

    def project(rows):
        h = x_ref[rows, :] * (1.0 + m[1:2]) + m[0:1]
        return _dot(h.astype(BF16), w_ref[...])

    def finish(rows, p):
        q = p[:, :ATTN_WIDTH] * ATTN_SCALE
        k = p[:, ATTN_WIDTH:ATTN_WIDTH + KV_WIDTH]
        v = p[:, ATTN_WIDTH + KV_WIDTH:ATTN_WIDTH + 2 * KV_WIDTH]
        if rope:
            cos = cos_ref[rows, :]
            sin = sin_ref[rows, :]
            q = jnp.concatenate([_rope(q[:, _head_cols(i)], cos, sin) for i in range(N_Q_HEADS)], axis=1)
            k = jnp.concatenate([_rope(k[:, _head_cols(i)], cos, sin) for i in range(N_KV_HEADS)], axis=1)
        q_ref[rows, :] = q.astype(q_ref.dtype)
        k_ref[rows, :] = k.astype(k_ref.dtype)
        v_ref[rows, :] = v.astype(v_ref.dtype)
        xr_ref[rows, :] = p[:, ATTN_WIDTH + 2 * KV_WIDTH:ATTN_WIDTH + 2 * KV_WIDTH + RNN_WIDTH]
        xg_ref[rows, :] = _gelu_tanh(p[:, ATTN_WIDTH + 2 * KV_WIDTH + RNN_WIDTH:])
        if not rope:
            r = rows.start // SEQ
            for i in range(N_KV_HEADS):
                ck_ref[r, 0, :, i, :] = k[:, _head_cols(i)]
                cv_ref[r, 0, :, i, :] = v[:, _head_cols(i)]

    _software_pipeline(SUB_TILES, project, finish)


def _ab_in(x, mod_l, w_bf16, group0, rows_per_group, rope_tables):
    n = x.shape[0]
    rope = rope_tables is not None
    assert SUB_TILE == SEQ, "the context-cache writes assume one request per sub-tile"
    row = lambda width: pl.BlockSpec((ROW_TILE, width), lambda i: (i, 0))
    in_specs = [row(D_MODEL),
                pl.BlockSpec((None, 6, D_MODEL), _group_map(group0, rows_per_group)),
                pl.BlockSpec((D_MODEL, AB_IN_WIDTH), lambda i: (0, 0))]
    args = [x, mod_l, w_bf16]
    out_specs = [row(ATTN_WIDTH), row(KV_WIDTH), row(KV_WIDTH), row(RNN_WIDTH), row(RNN_WIDTH)]
    out_shape = [jax.ShapeDtypeStruct((n, ATTN_WIDTH), BF16),
                 jax.ShapeDtypeStruct((n, KV_WIDTH), BF16),
                 jax.ShapeDtypeStruct((n, KV_WIDTH), BF16),
                 jax.ShapeDtypeStruct((n, RNN_WIDTH), F32),
                 jax.ShapeDtypeStruct((n, RNN_WIDTH), F32)]
    if rope:
        tiles_per_seq = DEC_SEQ // ROW_TILE
        in_specs += [pl.BlockSpec((ROW_TILE, HEAD_DIM), lambda i: (i % tiles_per_seq, 0))] * 2
        args += list(rope_tables)
    else:
        req = ROW_TILE // SEQ
        cache = pl.BlockSpec((req, 1, SEQ, N_KV_HEADS, HEAD_DIM), lambda i: (i, 0, 0, 0, 0))
        out_specs += [cache, cache]
        out_shape += [jax.ShapeDtypeStruct((n // SEQ, 1, SEQ, N_KV_HEADS, HEAD_DIM), F32)] * 2
    return pl.pallas_call(
        functools.partial(_ab_in_kernel, rope=rope),
        grid=(n // ROW_TILE,),
        in_specs=in_specs,
        out_specs=out_specs,
        out_shape=out_shape,
        compiler_params=_params(1),
        name="ab_in_rope" if rope else "ab_in",
    )(*args)


def _rope_tables():
    rows = DEC_SEQ // GRID_W
    row = jnp.repeat(jnp.arange(rows, dtype=F32), GRID_W)
    col = jnp.tile(jnp.arange(GRID_W, dtype=F32), rows)
    n_freq = HEAD_DIM // 4
    freqs = ROPE_BASE ** (-jnp.arange(n_freq, dtype=F32) / n_freq)
    ang = jnp.concatenate([row[:, None] * freqs, col[:, None] * freqs], axis=-1)
    cos = jnp.repeat(jnp.cos(ang), 2, axis=-1)
    sin = jnp.repeat(jnp.sin(ang), 2, axis=-1)
    sign = jnp.tile(jnp.array([-1.0, 1.0], F32), HEAD_DIM // 2)
    return cos, sin * sign


def _sink_attention_head(s_list, v_list, sink):
    m = sink
    for s in s_list:
        m = jnp.maximum(m, jnp.max(s, axis=-1, keepdims=True))
    p_list = [jnp.exp(s - m) for s in s_list]
    denom = jnp.exp(sink - m)
    for p in p_list:
        denom = denom + jnp.sum(p, axis=-1, keepdims=True)
    out = None
    for p, v in zip(p_list, v_list):
        o = _dot(p.astype(BF16), v)
        out = o if out is None else out + o
    return out * (1.0 / denom)


def _head_cols(h):
    return slice(h * HEAD_DIM, (h + 1) * HEAD_DIM)


def _ctx_attn_kernel(sink_ref, q_ref, k_ref, v_ref, o_ref):
    def scores(item):
        r, h = item
        rows = slice(r * SEQ, (r + 1) * SEQ)
        kh = k_ref[rows, _head_cols(h // Q_PER_KV)].astype(BF16)
        return _dot_nt(q_ref[rows, _head_cols(h)], kh)

    def finish(item, s):
        r, h = item
        rows = slice(r * SEQ, (r + 1) * SEQ)
        vh = v_ref[rows, _head_cols(h // Q_PER_KV)].astype(BF16)
        o_ref[rows, _head_cols(h)] = _sink_attention_head([s], [vh], sink_ref[h]).astype(o_ref.dtype)

    items = [(r, h) for r in range(REQ_TILE) for h in range(N_Q_HEADS)]
    _software_pipeline(items, scores, finish)


def _ctx_attention(q, k, v, sink):
    seq = lambda width: pl.BlockSpec((REQ_TILE * SEQ, width), lambda b: (b, 0))
    return pl.pallas_call(
        _ctx_attn_kernel,
        grid=(BATCH // REQ_TILE,),
        in_specs=[pl.BlockSpec(memory_space=pltpu.SMEM), seq(ATTN_WIDTH), seq(KV_WIDTH), seq(KV_WIDTH)],
        out_specs=seq(ATTN_WIDTH),
        out_shape=jax.ShapeDtypeStruct((N_PROMPT, ATTN_WIDTH), BF16),
        compiler_params=_params(1),
        name="context_attention",
    )(sink, q, k, v)


LAT_Q = 256


def _lat_attn_kernel(sink_ref, q_ref, kp_ref, kc_ref, kn_ref, vp_ref, vc_ref, vn_ref,
                     kx_ref, vx_ref, o_ref):
    n = pl.program_id(1)
    nb = pl.num_programs(1)
    q = q_ref[...]
    kw = jnp.concatenate([kp_ref[...], kc_ref[...], kn_ref[...]], axis=0)
    vw = jnp.concatenate([vp_ref[...], vc_ref[...], vn_ref[...]], axis=0)
    kx = kx_ref[...].astype(BF16)
    vx = vx_ref[...].astype(BF16)
    n_key = LAT_Q + 2 * WINDOW
    qi = lax.broadcasted_iota(I32, (LAT_Q, n_key), 0)
    kj = lax.broadcasted_iota(I32, (LAT_Q, n_key), 1)
    rel = kj - qi
    valid = (rel >= 0) & (rel <= 2 * WINDOW)
    valid = valid & ((n > 0) | (kj >= WINDOW)) & ((n < nb - 1) | (kj < LAT_Q + WINDOW))
    def scores(h):
        sl = _head_cols(h // Q_PER_KV)
        qh = q[:, _head_cols(h)]
        s_ctx = _dot_nt(qh, kx[:, sl])
        s_win = jnp.where(valid, _dot_nt(qh, kw[:, sl]), NEG_INF)
        return s_ctx, s_win

    def finish(h, s):
        sl = _head_cols(h // Q_PER_KV)
        o_ref[:, _head_cols(h)] = _sink_attention_head(list(s), [vx[:, sl], vw[:, sl]],
                                                       sink_ref[h]).astype(o_ref.dtype)

    _software_pipeline(range(N_Q_HEADS), scores, finish)


def _lat_attention(q, k, v, k_ctx, v_ctx, sink):
    nb = DEC_SEQ // LAT_Q
    nw = DEC_SEQ // WINDOW
    per = LAT_Q // WINDOW
    cur = lambda b, n: (b * nb + n, 0)
    prev = lambda b, n: (b * nw + jnp.maximum(n * per - 1, 0), 0)
    nxt = lambda b, n: (b * nw + jnp.minimum((n + 1) * per, nw - 1), 0)
    tile = lambda width: pl.BlockSpec((LAT_Q, width), cur)
    edge = lambda imap: pl.BlockSpec((WINDOW, KV_WIDTH), imap)
    ctx = pl.BlockSpec((PAST_LEN, KV_WIDTH), lambda b, n: (b, 0))
    return pl.pallas_call(
        _lat_attn_kernel,
        grid=(DEC_BATCH, nb),
        in_specs=[pl.BlockSpec(memory_space=pltpu.SMEM), tile(ATTN_WIDTH),
                  edge(prev), tile(KV_WIDTH), edge(nxt),
                  edge(prev), tile(KV_WIDTH), edge(nxt), ctx, ctx],
        out_specs=tile(ATTN_WIDTH),
        out_shape=jax.ShapeDtypeStruct((N_SAMPLE, ATTN_WIDTH), BF16),
        compiler_params=_params(2),
        name="latent_attention",
    )(sink, q, k, k, k, v, v, v, k_ctx, v_ctx)


RNN_CHUNK = 256


def _rglru_kernel(xr_ref, xg_ref, cw_ref, cb_ref, wa_ref, ba_ref, wx_ref, bx_ref, lam_ref,
                  h0f_ref, h0b_ref, y_ref, hfl_ref, hbf_ref, xc_s, af_s, ab_s, uf_s, ub_s, *, rows, seq_len):
    n_seq = rows // seq_len
    n_chunk = rows // RNN_CHUNK
    cw = cw_ref[...]
    cb = cb_ref[...]
    zeros_halo = jnp.zeros((SUBLANES, RNN_WIDTH), F32)
    row8 = lax.broadcasted_iota(I32, (SUBLANES, RNN_WIDTH), 0)

    def conv_chunk(c):
        r0 = c * RNN_CHUNK
        first = r0 % seq_len == 0
        last = (r0 + RNN_CHUNK) % seq_len == 0
        before = zeros_halo if first else xr_ref[r0 - SUBLANES:r0, :]
        after = zeros_halo if last else xr_ref[r0 + RNN_CHUNK:r0 + RNN_CHUNK + SUBLANES, :]
        win = jnp.concatenate([before, xr_ref[r0:r0 + RNN_CHUNK, :], after], axis=0)
        xc = cb
        n_win = RNN_CHUNK + 2 * SUBLANES
        for i in range(CONV_W):
            shift = (CONV_PAD_LEFT - i) % n_win
            rolled = win if shift == 0 else pltpu.roll(win, shift, 0)
            xc = xc + rolled[SUBLANES:SUBLANES + RNN_CHUNK, :] * cw[i:i + 1, :]
        return xc

    def group_scan(a, u, reverse):
        for k in (1, 2, 4):
            if reverse:
                shift, ok = SUBLANES - k, row8 < SUBLANES - k
            else:
                shift, ok = k, row8 >= k
            a_nb = jnp.where(ok, pltpu.roll(a, shift, 0), 1.0)
            u_nb = jnp.where(ok, pltpu.roll(u, shift, 0), 0.0)
            u = a * u_nb + u
            a = a * a_nb
        return a, u

    for c in range(n_chunk):
        xc_s[c * RNN_CHUNK:(c + 1) * RNN_CHUNK, :] = conv_chunk(c)

    for d, (a_s, u_s) in enumerate(((af_s, uf_s), (ab_s, ub_s))):
        neg = -lam_ref[d:d + 1, :]
        softplus = jnp.maximum(neg, 0.0) + jnp.log1p(jnp.exp(-jnp.abs(neg)))
        decay = -LRU_C * softplus
        wa = wa_ref[d]
        wx = wx_ref[d]
        ba = ba_ref[d:d + 1, :]
        bx = bx_ref[d:d + 1, :]
        for c in range(n_chunk):
            xc = xc_s[c * RNN_CHUNK:(c + 1) * RNN_CHUNK, :]
            xcb = xc.astype(BF16)
            r = _sigmoid(_dot(xcb, wa) + ba)
            i = _sigmoid(_dot(xcb, wx) + bx)
            log_a = r * decay
            a = jnp.exp(log_a)
            a_s[c * RNN_CHUNK:(c + 1) * RNN_CHUNK, :] = a
            one_minus_a2 = -jnp.tanh(log_a) * (a * a + 1.0)
            u_s[c * RNN_CHUNK:(c + 1) * RNN_CHUNK, :] = jnp.sqrt(one_minus_a2) * (i * xc)

    n_group = seq_len // SUBLANES

    def body(g, carries):
        fwd, bwd = carries
        new_f, new_b = [], []
        for s in range(n_seq):
            rf = pl.multiple_of(s * seq_len + g * SUBLANES, SUBLANES)
            a, u = group_scan(af_s[pl.ds(rf, SUBLANES), :], uf_s[pl.ds(rf, SUBLANES), :], False)
            h = u + a * fwd[s]
            uf_s[pl.ds(rf, SUBLANES), :] = h
            new_f.append(h[SUBLANES - 1:SUBLANES, :])
            rb = pl.multiple_of(s * seq_len + (n_group - 1 - g) * SUBLANES, SUBLANES)
            a, u = group_scan(ab_s[pl.ds(rb, SUBLANES), :], ub_s[pl.ds(rb, SUBLANES), :], True)
            h = u + a * bwd[s]
            ub_s[pl.ds(rb, SUBLANES), :] = h
            new_b.append(h[0:1, :])
        return tuple(new_f), tuple(new_b)

    init = (tuple(h0f_ref[s:s + 1, :] for s in range(n_seq)), tuple(h0b_ref[s:s + 1, :] for s in range(n_seq)))
    last_f, first_b = lax.fori_loop(0, n_group, body, init)

    for s in range(n_seq):
        hfl_ref[s:s + 1, :] = last_f[s]
        hbf_ref[s:s + 1, :] = first_b[s]
    for c in range(n_chunk):
        sl = slice(c * RNN_CHUNK, (c + 1) * RNN_CHUNK)
        y_ref[sl, :] = ((uf_s[sl, :] + ub_s[sl, :]) * xg_ref[sl, :]).astype(y_ref.dtype)


def _rglru(xr, xg, conv_w, conv_b, wa, ba, wx, bx, lam, h0f, h0b, rows, seq_len):
    n = xr.shape[0]
    n_seq = rows // seq_len
    row = pl.BlockSpec((rows, RNN_WIDTH), lambda i: (i, 0))
    full = lambda shape: pl.BlockSpec(shape, lambda i: (0,) * len(shape))
    state = pl.BlockSpec((None, n_seq, RNN_WIDTH), lambda i: (i, 0, 0))
    state_shape = jax.ShapeDtypeStruct((n // rows, n_seq, RNN_WIDTH), F32)
    return pl.pallas_call(
        functools.partial(_rglru_kernel, rows=rows, seq_len=seq_len),
        grid=(n // rows,),
        in_specs=[row, row, full((CONV_W, RNN_WIDTH)), full((1, RNN_WIDTH)),
                  full((2, RNN_WIDTH, RNN_WIDTH)), full((2, RNN_WIDTH)),
                  full((2, RNN_WIDTH, RNN_WIDTH)), full((2, RNN_WIDTH)), full((2, RNN_WIDTH)),
                  state, state],
        out_specs=[row, state, state],
        out_shape=[jax.ShapeDtypeStruct((n, RNN_WIDTH), BF16), state_shape, state_shape],
        scratch_shapes=[pltpu.VMEM((rows, RNN_WIDTH), F32)] * 5,
        compiler_params=_params(1),
        name="rglru_%d" % seq_len,
    )(xr, xg, conv_w, conv_b, wa, ba, wx, bx, lam, h0f, h0b)


def _block_diag_dense(w):
    eye = jnp.eye(RNN_BLOCKS, dtype=w.dtype)
    dense = w[:, :, :, None, :] * eye[None, :, None, :, None]
    return dense.reshape(2, RNN_WIDTH, RNN_WIDTH).astype(BF16)


SUB_TILE = 256
SUB_TILES = tuple(slice(s, s + SUB_TILE) for s in range(0, ROW_TILE, SUB_TILE))
SGU_TILE = 512
SGU_TILES = tuple(slice(s, s + SGU_TILE) for s in range(0, ROW_TILE, SGU_TILE))


def _residual_router(rows, x, o, m, lg_ref, lb_ref, rw_ref, x1_ref, xm_ref, aff_ref):
    x1 = _layer_norm(ALPHA * x + m[2:3] * o, lg_ref[...], lb_ref[...])
    x1_ref[rows, :] = x1
    xm = (x1 * (1.0 + m[4:5]) + m[3:4]).astype(BF16)
    xm_ref[rows, :] = xm
    lgt = _dot(xm, rw_ref[...])
    lane = lax.broadcasted_iota(I32, lgt.shape, 1)
    lgt = jnp.where(lane < N_EXPERTS, lgt, NEG_INF)
    ex = jnp.exp(lgt - jnp.max(lgt, axis=-1, keepdims=True))
    aff = ex / jnp.sum(ex, axis=-1, keepdims=True)
    aff_ref[rows, :] = aff[:, :N_EXPERTS]


def _mix_out_kernel(att_ref, rnn_ref, x_ref, mod_ref, w_ref, lg_ref, lb_ref, rw_ref,
                    x1_ref, xm_ref, aff_ref):
    m = mod_ref[...]

    def project(rows):
        return _dot(att_ref[rows, :], w_ref[:ATTN_WIDTH, :]) + _dot(rnn_ref[rows, :], w_ref[ATTN_WIDTH:, :])

    def finish(rows, o):
        _residual_router(rows, x_ref[rows, :], o, m, lg_ref, lb_ref, rw_ref, x1_ref, xm_ref, aff_ref)

    _software_pipeline(SUB_TILES, project, finish)


def _sgu_kernel(x_ref, mod_ref, win_ref, bin_ref, sg_ref, sb_ref, spw_ref, spb_ref, w_ref, lg_ref, lb_ref, rw_ref,
                x1_ref, xm_ref, aff_ref, v_s, gated_s):
    m = mod_ref[...]

    def project(rows):
        h = x_ref[rows, :] * (1.0 + m[1:2]) + m[0:1]
        return _dot(h.astype(BF16), win_ref[...])

    def gate(rows, p):
        p = _gelu_tanh(p + bin_ref[...])
        v_s[rows, :] = _layer_norm(p[:, SGU_WIDTH:], sg_ref[...], sb_ref[...]).astype(v_s.dtype)
        for c in range(rows.start, rows.stop, CHUNK):
            local = slice(c - rows.start, c - rows.start + CHUNK)
            for g in range(SGU_GROUPS):
                cols = slice(g * SGU_GROUP_W, (g + 1) * SGU_GROUP_W)
                mixed = _dot(spw_ref[g], v_s[c:c + CHUNK, cols]) + spb_ref[:, cols]
                gated_s[c:c + CHUNK, cols] = (p[local, cols] * mixed).astype(gated_s.dtype)
        return _dot(gated_s[rows, :], w_ref[...])

    def finish(rows, o):
        _residual_router(rows, x_ref[rows, :], o, m, lg_ref, lb_ref, rw_ref, x1_ref, xm_ref, aff_ref)

    _software_pipeline(SGU_TILES, project, gate, finish)


def _mixer_out(kernel, name, acts, x, mod_l, weights, ln_g, ln_b, router_w, group0, rows_per_group,
               scratch=()):
    n = x.shape[0]
    row = lambda width: pl.BlockSpec((ROW_TILE, width), lambda i: (i, 0))
    full = lambda a: pl.BlockSpec(a.shape, lambda i: (0,) * a.ndim)
    return pl.pallas_call(
        kernel,
        grid=(n // ROW_TILE,),
        in_specs=([row(a.shape[1]) for a in acts]
                  + [row(D_MODEL), pl.BlockSpec((None, 6, D_MODEL), _group_map(group0, rows_per_group))]
                  + [full(w) for w in weights] + [full(ln_g), full(ln_b), full(router_w)]),
        out_specs=[row(D_MODEL), row(D_MODEL), row(N_EXPERTS)],
        out_shape=[jax.ShapeDtypeStruct((n, D_MODEL), F32),
                   jax.ShapeDtypeStruct((n, D_MODEL), BF16),
                   jax.ShapeDtypeStruct((n, N_EXPERTS), F32)],
        scratch_shapes=list(scratch),
        compiler_params=_params(1),
        name=name,
    )(*acts, x, mod_l, *weights, ln_g, ln_b, router_w)


def _route_kernel(*refs, n_req, tokens, cap, place_in_kernel):
    n_blk = tokens // PREFIX_BLOCK
    if place_in_kernel:
        aff_ref, tri_ref, place_ref, posm_ref, posr_ref, affr_ref, blk_ref = refs
        n_col = place_ref.shape[2]
        aff = jnp.zeros((tokens, n_col), F32)
        for b in range(n_req):
            piece = aff_ref[b * tokens:(b + 1) * tokens, :]
            hi = piece.astype(BF16)
            rem = piece - hi.astype(F32)
            mid = rem.astype(BF16)
            low = (rem - mid.astype(F32)).astype(BF16)
            place = place_ref[b]
            aff = aff + ((_dot(hi, place) + _dot(mid, place)) + _dot(low, place))
    else:
        aff_ref, tri_ref, posm_ref, posr_ref, affr_ref, blk_ref = refs
        aff = aff_ref[...]
        n_col = aff.shape[1]

    def bisect(_, lo_hi):
        lo, hi = lo_hi
        mid = lo + ((hi - lo) >> 1)
        cnt = jnp.sum(jnp.where(aff >= pltpu.bitcast(mid, F32), 1.0, 0.0), axis=0, keepdims=True)
        ge = cnt >= cap
        return jnp.where(ge, mid, lo), jnp.where(ge, hi, mid)

    lo0 = jnp.zeros((1, n_col), I32)
    hi0 = jnp.full((1, n_col), 0x7F800000, I32)
    thr_bits, _ = lax.fori_loop(0, 31, bisect, (lo0, hi0))
    thr = pltpu.bitcast(thr_bits, F32)
    above = pltpu.bitcast(thr_bits + 1, F32)

    tri = tri_ref[...]

    def excl_prefix(x01):
        outs, before, inside = [], [], []
        off = jnp.zeros((1, n_col), F32)
        for blk in range(n_blk):
            xb = x01[blk * PREFIX_BLOCK:(blk + 1) * PREFIX_BLOCK, :]
            outs.append(_dot(tri, xb.astype(BF16)) + off)
            cnt = jnp.sum(xb, axis=0, keepdims=True)
            before.append(off)
            inside.append(cnt)
            off = off + cnt
        return (outs[0] if n_blk == 1 else jnp.concatenate(outs, axis=0)), before + inside

    gt = jnp.where(aff >= above, 1.0, 0.0)
    eq = jnp.where(aff >= thr, 1.0, 0.0) - gt
    need = cap - jnp.sum(gt, axis=0, keepdims=True)
    eq_rank, _ = excl_prefix(eq)
    sel = gt + jnp.where(eq_rank < need, eq, 0.0)
    pos, blk_rows = excl_prefix(sel)
    posm = jnp.where(sel > 0.0, pos, -1.0)
    posr_ref[...] = posm.T
    affr_ref[...] = aff.T
    for i, row in enumerate(blk_rows):
        blk_ref[i:i + 1, :] = row
    if place_in_kernel:
        posm_bf16 = posm.astype(BF16)
        for b in range(n_req):
            posm_ref[b * tokens:(b + 1) * tokens, :] = _dot_nt(posm_bf16, place_ref[b])
    else:
        posm_ref[...] = posm


PLACE_IN_KERNEL_MAX_REQ = 8


def _route_requests(aff, tri, n_req, tokens, cap):
    n_real = n_req * N_EXPERTS
    n_col = -(-n_real // LANES) * LANES
    n_blk = tokens // PREFIX_BLOCK
    place_in_kernel = n_req <= PLACE_IN_KERNEL_MAX_REQ
    whole = lambda shape: pl.BlockSpec(shape, lambda i: (0,) * len(shape))
    if place_in_kernel:
        col = jnp.arange(n_col)
        place = (col[None, None, :] == (jnp.arange(n_req)[:, None, None] * N_EXPERTS
                                        + jnp.arange(N_EXPERTS)[None, :, None])).astype(BF16)
        args = (aff, tri, place)
        token_major = (n_req * tokens, N_EXPERTS)
        in_specs = [whole(token_major), whole((PREFIX_BLOCK, PREFIX_BLOCK)), whole(place.shape)]
    else:
        aff_t = aff.reshape(n_req, tokens, N_EXPERTS).transpose(1, 0, 2).reshape(tokens, n_real)
        args = (jnp.pad(aff_t, ((0, 0), (0, n_col - n_real))), tri)
        token_major = (tokens, n_col)
        in_specs = [whole(token_major), whole((PREFIX_BLOCK, PREFIX_BLOCK))]
    posm, posr, affr, blk = pl.pallas_call(
        functools.partial(_route_kernel, n_req=n_req, tokens=tokens, cap=cap, place_in_kernel=place_in_kernel),
        grid=(1,),
        in_specs=in_specs,
        out_specs=[whole(token_major), whole((n_col, tokens)), whole((n_col, tokens)), whole((2 * n_blk, n_col))],
        out_shape=[jax.ShapeDtypeStruct(token_major, F32),
                   jax.ShapeDtypeStruct((n_col, tokens), F32),
                   jax.ShapeDtypeStruct((n_col, tokens), F32),
                   jax.ShapeDtypeStruct((2 * n_blk, n_col), F32)],
        compiler_params=_params(1),
        name="route_%d" % tokens,
    )(*args)
    if not place_in_kernel:
        posm = posm[:, :n_real].reshape(tokens, n_req, N_EXPERTS).transpose(1, 0, 2).reshape(n_req * tokens, N_EXPERTS)
    blk = blk[:, :n_real].astype(I32).reshape(2, n_blk, n_req, N_EXPERTS).transpose(0, 2, 1, 3)
    expert_major = lambda a: a.reshape(n_col // N_EXPERTS, N_EXPERTS, tokens)
    return posm, expert_major(posr), expert_major(affr), blk[0], blk[1]


def _slot_gate(hit, aff_row):
    g = jnp.sum(jnp.where(hit, aff_row, 0.0), axis=-1, keepdims=True)
    return jnp.broadcast_to(g, (hit.shape[0], LANES))


def _gather_prompt_kernel(x_ref, posr_ref, affr_ref, xg_ref, g_ref):
    slot = lax.broadcasted_iota(I32, (CAP_P, SEQ), 0).astype(F32)
    for r in range(REQ_TILE):
        tokens = slice(r * SEQ, (r + 1) * SEQ)
        slots = slice(r * CAP_P, (r + 1) * CAP_P)
        posr = posr_ref[r]
        affr = affr_ref[r]
        hits = [posr[e:e + 1, :] == slot for e in range(N_EXPERTS)]
        onehot = jnp.concatenate([jnp.where(h, 1.0, 0.0) for h in hits], axis=0).astype(BF16)
        xg = _dot(onehot, x_ref[tokens, :])
        for e in range(N_EXPERTS):
            xg_ref[e, slots, :] = xg[e * CAP_P:(e + 1) * CAP_P, :].astype(xg_ref.dtype)
            g_ref[e, slots, :] = _slot_gate(hits[e], affr[e:e + 1, :])


def _gather_prompt(xm, posr, affr):
    return pl.pallas_call(
        _gather_prompt_kernel,
        grid=(BATCH // REQ_TILE,),
        in_specs=[pl.BlockSpec((REQ_TILE * SEQ, D_MODEL), lambda b: (b, 0)),
                  pl.BlockSpec((REQ_TILE, N_EXPERTS, SEQ), lambda b: (b, 0, 0)),
                  pl.BlockSpec((REQ_TILE, N_EXPERTS, SEQ), lambda b: (b, 0, 0))],
        out_specs=[pl.BlockSpec((N_EXPERTS, REQ_TILE * CAP_P, D_MODEL), lambda b: (0, b, 0)),
                   pl.BlockSpec((N_EXPERTS, REQ_TILE * CAP_P, LANES), lambda b: (0, b, 0))],
        out_shape=[jax.ShapeDtypeStruct((N_EXPERTS, SLOTS_P, D_MODEL), BF16),
                   jax.ShapeDtypeStruct((N_EXPERTS, SLOTS_P, LANES), F32)],
        compiler_params=_params(1),
        name="gather_prompt",
    )(xm, posr, affr)


TOKEN_BLOCK = PREFIX_BLOCK
N_TOKEN_BLOCKS = DEC_SEQ // TOKEN_BLOCK
SLOT_ALIGN = 16
WINDOW_SLOTS = 64
LAST_WINDOW = CAP_S - WINDOW_SLOTS
ALL_WINDOWS = N_EXPERTS * WINDOW_SLOTS


def _window_plan(before, inside):
    start = (before // SLOT_ALIGN) * SLOT_ALIGN
    span = before - start + inside
    passes = jnp.where(inside > 0, (span + WINDOW_SLOTS - 1) // WINDOW_SLOTS, 0)
    return start.reshape(-1), jnp.max(passes, axis=-1).reshape(-1)


def _window_bounds(start_ref, step, expert, k):
    lower = start_ref[step * N_EXPERTS + expert] + WINDOW_SLOTS * k
    begin = pl.multiple_of(jnp.minimum(lower, LAST_WINDOW), SLOT_ALIGN)
    return lower, begin


BLOCKS_PER_STEP = 2
STEP_TOKENS = BLOCKS_PER_STEP * TOKEN_BLOCK
STEPS_PER_REQ = N_TOKEN_BLOCKS // BLOCKS_PER_STEP


def _gather_sample_kernel(start_ref, passes_ref, x_ref, posr_ref, affr_ref, xg_ref, g_ref):
    t = pl.program_id(1)
    first_block = (pl.program_id(0) * STEPS_PER_REQ + t) * BLOCKS_PER_STEP

    @pl.when(t == 0)
    def _():
        xg_ref[...] = jnp.zeros_like(xg_ref)
        g_ref[...] = jnp.zeros_like(g_ref)

    row = lax.broadcasted_iota(I32, (WINDOW_SLOTS, TOKEN_BLOCK), 0).astype(F32)

    def pick(j, k):
        tokens = slice(j * TOKEN_BLOCK, (j + 1) * TOKEN_BLOCK)
        hits, begins = [], []
        for e in range(N_EXPERTS):
            lower, begin = _window_bounds(start_ref, first_block + j, e, k)
            pe = posr_ref[e:e + 1, tokens]
            hits.append((pe - begin.astype(F32) == row) & (pe >= lower.astype(F32)))
            begins.append(begin)
        onehot = jnp.concatenate([jnp.where(h, 1.0, 0.0) for h in hits], axis=0).astype(BF16)
        return _dot(onehot, x_ref[tokens, :]), hits, begins

    def place(j, picked):
        rows_x, hits, begins = picked
        tokens = slice(j * TOKEN_BLOCK, (j + 1) * TOKEN_BLOCK)
        for e in range(N_EXPERTS):
            win = pl.ds(begins[e], WINDOW_SLOTS)
            xg_ref[e, win, :] += rows_x[e * WINDOW_SLOTS:(e + 1) * WINDOW_SLOTS, :].astype(xg_ref.dtype)
            g_ref[e, win, :] += _slot_gate(hits[e], affr_ref[e:e + 1, tokens])

    _software_pipeline(range(BLOCKS_PER_STEP), lambda j: pick(j, 0), place)

    for j in range(BLOCKS_PER_STEP):
        def extra_pass(k, carry, j=j):
            place(j, pick(j, k))
            return carry

        lax.fori_loop(1, passes_ref[first_block + j], extra_pass, 0)


def _gather_sample(xm, posr, affr, starts, passes):
    expert_rows = pl.BlockSpec((None, N_EXPERTS, STEP_TOKENS), lambda b, t, s, p: (b, 0, t))
    grid_spec = pltpu.PrefetchScalarGridSpec(
        num_scalar_prefetch=2,
        grid=(DEC_BATCH, STEPS_PER_REQ),
        in_specs=[pl.BlockSpec((STEP_TOKENS, D_MODEL), lambda b, t, s, p: (b * STEPS_PER_REQ + t, 0)),
                  expert_rows, expert_rows],
        out_specs=[pl.BlockSpec((N_EXPERTS, CAP_S, D_MODEL), lambda b, t, s, p: (0, b, 0)),
                   pl.BlockSpec((N_EXPERTS, CAP_S, LANES), lambda b, t, s, p: (0, b, 0))],
    )
    return pl.pallas_call(
        _gather_sample_kernel,
        grid_spec=grid_spec,
        out_shape=[jax.ShapeDtypeStruct((N_EXPERTS, SLOTS_S, D_MODEL), BF16),
                   jax.ShapeDtypeStruct((N_EXPERTS, SLOTS_S, LANES), F32)],
        compiler_params=_params(2),
        name="gather_sample",
    )(starts, passes, xm, posr, affr)


FF_TILE = 512
SLOT_CHUNK = 1024


def _ffn_kernel(xp_ref, xs_ref, gp_ref, gs_ref, w1_ref, w3_ref, w2_ref, yp_ref, ys_ref, acc_ref):
    f = pl.program_id(1)
    n_f = pl.num_programs(1)
    @pl.when(f == 0)
    def _():
        acc_ref[...] = jnp.zeros_like(acc_ref)

    w1 = w1_ref[...].astype(BF16)
    w3 = w3_ref[...].astype(BF16)
    w2 = w2_ref[...].astype(BF16)

    def up_project(chunk):
        x_ref, c, _ = chunk
        x = x_ref[c * SLOT_CHUNK:(c + 1) * SLOT_CHUNK, :]
        return _dot(x, w1), _dot(x, w3)

    def down_project(chunk, h):
        _, c, base = chunk
        h1, h3 = h
        hid = (h1 * jax.nn.sigmoid(h1) * h3).astype(BF16)
        rows = slice(base + c * SLOT_CHUNK, base + (c + 1) * SLOT_CHUNK)
        acc_ref[rows, :] += _dot(hid, w2)

    chunks = [(x_ref, c, base) for x_ref, n_slot, base in ((xp_ref, SLOTS_P, 0), (xs_ref, SLOTS_S, SLOTS_P))
              for c in range(n_slot // SLOT_CHUNK)]
    _software_pipeline(chunks, up_project, down_project)

    @pl.when(f == n_f - 1)
    def _():
        for y_ref, g_ref, n_slot, base in ((yp_ref, gp_ref, SLOTS_P, 0), (ys_ref, gs_ref, SLOTS_S, SLOTS_P)):
            for c in range(n_slot // SLOT_CHUNK):
                rows = slice(c * SLOT_CHUNK, (c + 1) * SLOT_CHUNK)
                gate = jnp.concatenate([g_ref[rows, :]] * (D_MODEL // LANES), axis=1)
                y_ref[rows, :] = (acc_ref[base + c * SLOT_CHUNK:base + (c + 1) * SLOT_CHUNK, :]
                                  * gate).astype(y_ref.dtype)


def _expert_ffn(xgp, xgs, gp, gs, w1, w3, w2, layer):
    slots = lambda n, width: pl.BlockSpec((None, n, width), lambda e, f: (e, 0, 0))
    return pl.pallas_call(
        _ffn_kernel,
        grid=(N_EXPERTS, EXPERT_FF // FF_TILE),
        in_specs=[slots(SLOTS_P, D_MODEL), slots(SLOTS_S, D_MODEL), slots(SLOTS_P, LANES), slots(SLOTS_S, LANES),
                  pl.BlockSpec((None, None, D_MODEL, FF_TILE), lambda e, f: (layer, e, 0, f)),
                  pl.BlockSpec((None, None, D_MODEL, FF_TILE), lambda e, f: (layer, e, 0, f)),
                  pl.BlockSpec((None, None, FF_TILE, D_MODEL), lambda e, f: (layer, e, f, 0))],
        out_specs=[slots(SLOTS_P, D_MODEL), slots(SLOTS_S, D_MODEL)],
        out_shape=[jax.ShapeDtypeStruct((N_EXPERTS, SLOTS_P, D_MODEL), BF16),
                   jax.ShapeDtypeStruct((N_EXPERTS, SLOTS_S, D_MODEL), BF16)],
        scratch_shapes=[pltpu.VMEM((SLOTS_P + SLOTS_S, D_MODEL), F32)],
        compiler_params=_params(2),
        name="expert_ffn",
    )(xgp, xgs, gp, gs, w1, w3, w2)


def _combine_prompt_kernel(posm_ref, y_ref, x_ref, mod_ref, exp_ref, lg_ref, lb_ref, o_ref):
    m = mod_ref[...]
    slot = (lax.broadcasted_iota(I32, (SEQ, N_EXPERTS * CAP_P), 1) & (CAP_P - 1)).astype(F32)
    def scatter(r):
        tokens = slice(r * SEQ, (r + 1) * SEQ)
        spread = _dot(posm_ref[tokens, :].astype(BF16), exp_ref[...])
        onehot = jnp.where(spread == slot, 1.0, 0.0).astype(BF16)
        y = jnp.concatenate([y_ref[e, r * CAP_P:(r + 1) * CAP_P, :] for e in range(N_EXPERTS)], axis=0)
        return _dot(onehot, y)

    def finish(r, f):
        tokens = slice(r * SEQ, (r + 1) * SEQ)
        o_ref[tokens, :] = _layer_norm(ALPHA * x_ref[tokens, :] + m[5:6] * f, lg_ref[...], lb_ref[...])

    _software_pipeline(range(REQ_TILE), scatter, finish)


def _combine_prompt(posm, y, x1, mod_l, expand, ln_g, ln_b):
    full = lambda a: pl.BlockSpec(a.shape, lambda b: (0,) * a.ndim)
    return pl.pallas_call(
        _combine_prompt_kernel,
        grid=(BATCH // REQ_TILE,),
        in_specs=[pl.BlockSpec((REQ_TILE * SEQ, N_EXPERTS), lambda b: (b, 0)),
                  pl.BlockSpec((N_EXPERTS, REQ_TILE * CAP_P, D_MODEL), lambda b: (0, b, 0)),
                  pl.BlockSpec((REQ_TILE * SEQ, D_MODEL), lambda b: (b, 0)),
                  pl.BlockSpec((None, 6, D_MODEL), lambda b: (0, 0, 0)),
                  full(expand), full(ln_g), full(ln_b)],
        out_specs=pl.BlockSpec((REQ_TILE * SEQ, D_MODEL), lambda b: (b, 0)),
        out_shape=jax.ShapeDtypeStruct((N_PROMPT, D_MODEL), F32),
        compiler_params=_params(1),
        name="combine_prompt",
    )(posm, y, x1, mod_l, expand, ln_g, ln_b)


def _combine_sample_kernel(start_ref, passes_ref, posm_ref, y_ref, x_ref, mod_ref, exp_ref, off_ref,
                           lg_ref, lb_ref, o_ref):
    first_block = (pl.program_id(0) * STEPS_PER_REQ + pl.program_id(1)) * BLOCKS_PER_STEP
    exp_bf16 = exp_ref[...]
    offset = off_ref[...]
    m = mod_ref[...]

    def tokens_of(j):
        return slice(j * TOKEN_BLOCK, (j + 1) * TOKEN_BLOCK)

    def one_pass(j, k):
        spread = _dot(posm_ref[tokens_of(j), :].astype(BF16), exp_bf16)
        begin_row = jnp.zeros((1, ALL_WINDOWS), F32)
        lower_row = jnp.zeros((1, ALL_WINDOWS), F32)
        windows = []
        for e in range(N_EXPERTS):
            lower, begin = _window_bounds(start_ref, first_block + j, e, k)
            lanes_e = exp_bf16[e:e + 1, :].astype(F32)
            begin_row = begin_row + begin.astype(F32) * lanes_e
            lower_row = lower_row + lower.astype(F32) * lanes_e
            windows.append(y_ref[e, pl.ds(begin, WINDOW_SLOTS), :])
        hit = (spread - begin_row == offset) & (spread >= lower_row)
        onehot = jnp.where(hit, 1.0, 0.0).astype(BF16)
        return _dot(onehot, jnp.concatenate(windows, axis=0))

    def finish(j, f):
        rows = tokens_of(j)
        o_ref[rows, :] = _layer_norm(ALPHA * x_ref[rows, :] + m[5:6] * f, lg_ref[...], lb_ref[...])

    _software_pipeline(range(BLOCKS_PER_STEP), lambda j: one_pass(j, 0), finish)

    for j in range(BLOCKS_PER_STEP):
        n_pass = passes_ref[first_block + j]

        @pl.when(n_pass > 1)
        def _(j=j, n_pass=n_pass):
            f = lax.fori_loop(1, n_pass, lambda k, f: f + one_pass(j, k), one_pass(j, 0))
            finish(j, f)


def _combine_sample(posm, y, x1, mod_l, ln_g, ln_b, starts, passes):
    lane = jnp.arange(ALL_WINDOWS)
    expand = ((lane[None, :] // WINDOW_SLOTS) == jnp.arange(N_EXPERTS)[:, None]).astype(BF16)
    offset = (lane % WINDOW_SLOTS).astype(F32).reshape(1, ALL_WINDOWS)
    row = lambda width: pl.BlockSpec((STEP_TOKENS, width), lambda b, t, s, p: (b * STEPS_PER_REQ + t, 0))
    full = lambda a: pl.BlockSpec(a.shape, lambda b, t, s, p: (0,) * a.ndim)
    grid_spec = pltpu.PrefetchScalarGridSpec(
        num_scalar_prefetch=2,
        grid=(DEC_BATCH, STEPS_PER_REQ),
        in_specs=[row(N_EXPERTS),
                  pl.BlockSpec((N_EXPERTS, CAP_S, D_MODEL), lambda b, t, s, p: (0, b, 0)),
                  row(D_MODEL),
                  pl.BlockSpec((None, 6, D_MODEL), lambda b, t, s, p: (1 + b, 0, 0)),
                  full(expand), full(offset), full(ln_g), full(ln_b)],
        out_specs=row(D_MODEL),
    )
    return pl.pallas_call(
        _combine_sample_kernel,
        grid_spec=grid_spec,
        out_shape=jax.ShapeDtypeStruct((N_SAMPLE, D_MODEL), F32),
        compiler_params=_params(2),
        name="combine_sample",
    )(starts, passes, posm, y, x1, mod_l, expand, offset, ln_g, ln_b)


def _moe(prompt, sample, mod_l, w1, w3, w2, layer, ln_g, ln_b, tri, expand):
    x1_p, xm_p, aff_p = prompt
    x1_s, xm_s, aff_s = sample
    posm_p, posr_p, affr_p, _, _ = _route_requests(aff_p, tri, BATCH, SEQ, CAP_P)
    posm_s, posr_s, affr_s, before_s, inside_s = _route_requests(aff_s, tri, DEC_BATCH, DEC_SEQ, CAP_S)
    starts, passes = _window_plan(before_s, inside_s)
    xg_p, g_p = _gather_prompt(xm_p, posr_p, affr_p)
    xg_s, g_s = _gather_sample(xm_s, posr_s, affr_s, starts, passes)
    y_p, y_s = _expert_ffn(xg_p, xg_s, g_p, g_s, w1, w3, w2, layer)
    x2_p = _combine_prompt(posm_p, y_p, x1_p, mod_l, expand, ln_g, ln_b)
    x2_s = _combine_sample(posm_s, y_s, x1_s, mod_l, ln_g, ln_b, starts, passes)
    return x2_p, x2_s


def kernel(x_prompt, x_sample, cache_k, cache_v, state_rglru, c, c_ctx, mod_w, mod_b, ln_mix_g, ln_mix_b, ln_ffn_g, ln_ffn_b, ab_in_w, attn_sink, rnn_conv_w, rnn_conv_b, lru_wa, lru_ba, lru_wx, lru_bx, lru_lambda, ab_out_w, sgu_in_w, sgu_in_b, sgu_ln_g, sgu_ln_b, sgu_spatial_w, sgu_spatial_b, sgu_out_w, router_w, moe_w1, moe_w3, moe_w2):
    xp = x_prompt.reshape(N_PROMPT, D_MODEL)
    xs = x_sample.reshape(N_SAMPLE, D_MODEL)

    cvec = jnp.concatenate([c_ctx[None], c, jnp.zeros((SUBLANES - 1 - DEC_BATCH, D_MODEL), F32)], axis=0)
    mod = _modulation(cvec, mod_w, mod_b).reshape(DEPTH, SUBLANES, 6, D_MODEL)

    idx = jnp.arange(PREFIX_BLOCK)
    tri = (idx[None, :] < idx[:, None]).astype(BF16)
    lane = jnp.arange(N_EXPERTS * CAP_P)
    expand = ((lane[None, :] // CAP_P) == jnp.arange(N_EXPERTS)[:, None]).astype(BF16)
    rope_tables = _rope_tables()
    vec = lambda a: a.reshape(1, -1)
    router_pad = lambda w: jnp.pad(w, ((0, 0), (0, LANES - N_EXPERTS))).astype(BF16)

    new_k = new_v = new_state = None
    for l in range(DEPTH):
        e = l // 2
        mod_l = mod[l]
        rw = router_pad(router_w[l])
        lg, lb = vec(ln_mix_g[l]), vec(ln_mix_b[l])
        if l % 2 == 0:
            w_in = ab_in_w[e].astype(BF16)
            w_out = ab_out_w[e].astype(BF16)
            wa = _block_diag_dense(lru_wa[e])
            wx = _block_diag_dense(lru_wx[e])
            rnn_w = (rnn_conv_w[e], vec(rnn_conv_b[e]), wa, lru_ba[e], wx, lru_bx[e], lru_lambda[e])

            q, k, v, xr, xg, new_k, new_v = _ab_in(xp, mod_l, w_in, 0, N_PROMPT, None)
            att = _ctx_attention(q, k, v, attn_sink[e])
            rows = 8 * SEQ
            zeros = jnp.zeros((N_PROMPT // rows, rows // SEQ, RNN_WIDTH), F32)
            rnn, hf_last, hb_first = _rglru(xr, xg, *rnn_w, zeros, zeros, rows, SEQ)
            new_state = jnp.stack([hf_last.reshape(BATCH, RNN_WIDTH), hb_first.reshape(BATCH, RNN_WIDTH)], axis=1)
            prompt = _mixer_out(_mix_out_kernel, "mix_out", (att, rnn), xp, mod_l, (w_out,),
                                lg, lb, rw, 0, N_PROMPT)

            q, k, v, xr, xg = _ab_in(xs, mod_l, w_in, 1, DEC_SEQ, rope_tables)
            att = _lat_attention(q, k, v, cache_k[:, e].reshape(DEC_BATCH * PAST_LEN, KV_WIDTH),
                                 cache_v[:, e].reshape(DEC_BATCH * PAST_LEN, KV_WIDTH), attn_sink[e])
            h0 = state_rglru[:, e]
            rnn, _, _ = _rglru(xr, xg, *rnn_w, h0[:, 0][:, None, :], h0[:, 1][:, None, :], DEC_SEQ, DEC_SEQ)
            sample = _mixer_out(_mix_out_kernel, "mix_out", (att, rnn), xs, mod_l, (w_out,),
                                lg, lb, rw, 1, DEC_SEQ)
        else:
            w_in = sgu_in_w[e].astype(BF16)
            w_out = sgu_out_w[e].astype(BF16)
            spw = sgu_spatial_w[e].astype(BF16)
            spb = jnp.repeat(sgu_spatial_b[e].T, SGU_GROUP_W, axis=1)
            sgu_w = (w_in, vec(sgu_in_b[e]), vec(sgu_ln_g[e]), vec(sgu_ln_b[e]), spw, spb, w_out)
            scratch = (pltpu.VMEM((ROW_TILE, SGU_WIDTH), BF16), pltpu.VMEM((ROW_TILE, SGU_WIDTH), BF16))
            prompt = _mixer_out(_sgu_kernel, "sgu", (), xp, mod_l, sgu_w, lg, lb, rw, 0, N_PROMPT, scratch)
            sample = _mixer_out(_sgu_kernel, "sgu", (), xs, mod_l, sgu_w, lg, lb, rw, 1, DEC_SEQ, scratch)

        xp, xs = _moe(prompt, sample, mod_l, moe_w1, moe_w3, moe_w2, l,
                      vec(ln_ffn_g[l]), vec(ln_ffn_b[l]), tri, expand)

    return (xp.reshape(BATCH, SEQ, D_MODEL),
            xs.reshape(DEC_BATCH, DEC_SEQ, D_MODEL),
            new_k,
            new_v,
            new_state.reshape(BATCH, 1, 2, RNN_WIDTH))
```

```python
import functools

import numpy as np
import jax
import jax.numpy as jnp
from jax import lax
from jax.experimental import pallas as pl
from jax.experimental.pallas import tpu as pltpu

F32 = jnp.float32
BF16 = jnp.bfloat16
I32 = jnp.int32

D_MODEL = 1024
BATCH = 32
SEQ = 256
DEPTH = 2
DEC_BATCH = 4
DEC_SEQ = 2048
PAST_LEN = 256
GRID_W = 64
HEAD_DIM = 128
N_Q_HEADS = 4
N_KV_HEADS = 2
Q_PER_KV = N_Q_HEADS // N_KV_HEADS
ATTN_WIDTH = N_Q_HEADS * HEAD_DIM
KV_WIDTH = N_KV_HEADS * HEAD_DIM
WINDOW = 128
BLOCK = 128
ATTN_SCALE = HEAD_DIM ** -0.5
ROPE_BASE = 10000.0
NEG_INF = -1e30
RNN_WIDTH = 512
RNN_BLOCKS = 8
RNN_BLOCK_W = RNN_WIDTH // RNN_BLOCKS
CONV_W = 4
CONV_PAD_LEFT = 2
LRU_C = 8.0
AB_IN_WIDTH = ATTN_WIDTH + 2 * KV_WIDTH + 2 * RNN_WIDTH
CHUNK = 128
SGU_WIDTH = D_MODEL
SGU_GROUPS = 8
SGU_GROUP_W = SGU_WIDTH // SGU_GROUPS
N_EXPERTS = 16
EXPERT_FF = 2048
EC_FACTOR = 2
ALPHA = (2 * DEPTH) ** 0.25
LN_EPS = 1e-6

N_PROMPT = BATCH * SEQ
N_SAMPLE = DEC_BATCH * DEC_SEQ
CAP_P = EC_FACTOR * SEQ // N_EXPERTS
CAP_S = EC_FACTOR * DEC_SEQ // N_EXPERTS
SLOTS_P = BATCH * CAP_P
SLOTS_S = DEC_BATCH * CAP_S

LANES = 128
SUBLANES = 8
ROW_TILE = 1024
REQ_TILE = 4
PREFIX_BLOCK = 256
VMEM_LIMIT = 56 * 1024 * 1024


def _params(n_axes=1):
    return pltpu.CompilerParams(dimension_semantics=("arbitrary",) * n_axes,
                                vmem_limit_bytes=VMEM_LIMIT)


def _layer_norm(x, g, b):
    mu = jnp.mean(x, axis=-1, keepdims=True)
    xc = x - mu
    var = jnp.mean(xc * xc, axis=-1, keepdims=True)
    return xc * lax.rsqrt(var + LN_EPS) * g + b


_LOG2_E = 1.4426950408889634
_GELU_K0 = -2.0 * 0.7978845608028654 * _LOG2_E
_GELU_K1 = _GELU_K0 * 0.044715


def _gelu_tanh(x):
    return x / (1.0 + jnp.exp2(x * (_GELU_K0 + _GELU_K1 * (x * x))))


def _sigmoid(z):
    return 0.5 + 0.5 * jnp.tanh(0.5 * z)


def _dot(a, b):
    return jnp.dot(a, b, preferred_element_type=F32)


def _dot_nt(a, b):
    return lax.dot_general(a, b, (((1,), (1,)), ((), ())), preferred_element_type=F32)


def _software_pipeline(items, first_stage, *later_stages):
    items = list(items)
    stages = (first_stage,) + later_stages
    carried = {}
    for t in range(len(items) + len(stages) - 1):
        for s, stage in enumerate(stages):
            i = t - s
            if 0 <= i < len(items):
                carried[i] = stage(items[i]) if s == 0 else stage(items[i], carried[i])


def _mod_kernel(c_ref, w_ref, b_ref, o_ref):
    c = c_ref[...]
    s = c * jax.nn.sigmoid(c)
    o_ref[...] = _dot(s.astype(BF16), w_ref[...].astype(BF16)) + b_ref[...]


def _modulation(cvec8, mod_w, mod_b):
    n_col = 6 * D_MODEL // D_MODEL
    return pl.pallas_call(
        _mod_kernel,
        grid=(DEPTH, n_col),
        in_specs=[pl.BlockSpec((SUBLANES, D_MODEL), lambda l, j: (0, 0)),
                  pl.BlockSpec((None, D_MODEL, D_MODEL), lambda l, j: (l, 0, j)),
                  pl.BlockSpec((None, 1, D_MODEL), lambda l, j: (l, 0, j))],
        out_specs=pl.BlockSpec((None, SUBLANES, D_MODEL), lambda l, j: (l, 0, j)),
        out_shape=jax.ShapeDtypeStruct((DEPTH, SUBLANES, 6 * D_MODEL), F32),
        compiler_params=_params(2),
        name="adaln_modulation",
    )(cvec8, mod_w, mod_b.reshape(DEPTH, 1, 6 * D_MODEL))


def _group_map(group0, rows_per_group):
    tiles_per_group = rows_per_group // ROW_TILE
    return lambda i: (group0 + i // tiles_per_group, 0, 0)


def _rope(t, cos, sin_signed):
    lane = lax.broadcasted_iota(I32, t.shape, 1)
    swapped = jnp.where((lane & 1) == 0, pltpu.roll(t, HEAD_DIM - 1, 1), pltpu.roll(t, 1, 1))
    return t * cos + swapped * sin_signed


def _ab_in_kernel(*refs, rope):
    if rope:
        x_ref, mod_ref, w_ref, cos_ref, sin_ref, q_ref, k_ref, v_ref, xr_ref, xg_ref = refs
    else:
        x_ref, mod_ref, w_ref, q_ref, k_ref, v_ref, xr_ref, xg_ref, ck_ref, cv_ref = refs
    m = mod_ref[...]
    h = x_ref[...] * (1.0 + m[1:2]) + m[0:1]
    p = _dot(h.astype(BF16), w_ref[...])
    q = p[:, :ATTN_WIDTH] * (ATTN_SCALE * _LOG2_E)
    k = p[:, ATTN_WIDTH:ATTN_WIDTH + KV_WIDTH]
    v = p[:, ATTN_WIDTH + KV_WIDTH:ATTN_WIDTH + 2 * KV_WIDTH]
    if rope:
        cos = cos_ref[...]
        sin = sin_ref[...]
        q = jnp.concatenate([_rope(q[:, i * HEAD_DIM:(i + 1) * HEAD_DIM], cos, sin)
                             for i in range(N_Q_HEADS)], axis=1)
        k = jnp.concatenate([_rope(k[:, i * HEAD_DIM:(i + 1) * HEAD_DIM], cos, sin)
                             for i in range(N_KV_HEADS)], axis=1)
    q_ref[...] = q.astype(q_ref.dtype)
    k_ref[...] = k.astype(k_ref.dtype)
    v_ref[...] = v.astype(v_ref.dtype)
    xr_ref[...] = p[:, ATTN_WIDTH + 2 * KV_WIDTH:ATTN_WIDTH + 2 * KV_WIDTH + RNN_WIDTH]
    xg_ref[...] = p[:, ATTN_WIDTH + 2 * KV_WIDTH + RNN_WIDTH:]
    if not rope:
        for r in range(ROW_TILE // SEQ):
            for i in range(N_KV_HEADS):
                ck_ref[r, 0, :, i, :] = k[r * SEQ:(r + 1) * SEQ, _head_cols(i)]
                cv_ref[r, 0, :, i, :] = v[r * SEQ:(r + 1) * SEQ, _head_cols(i)]


def _ab_in(x, mod_l, w_bf16, group0, rows_per_group, rope_tables):
    n = x.shape[0]
    rope = rope_tables is not None
    row = lambda width: pl.BlockSpec((ROW_TILE, width), lambda i: (i, 0))
    in_specs = [row(D_MODEL),
                pl.BlockSpec((None, 6, D_MODEL), _group_map(group0, rows_per_group)),
                pl.BlockSpec((D_MODEL, AB_IN_WIDTH), lambda i: (0, 0))]
    args = [x, mod_l, w_bf16]
    out_specs = [row(ATTN_WIDTH), row(KV_WIDTH), row(KV_WIDTH), row(RNN_WIDTH), row(RNN_WIDTH)]
    out_shape = [jax.ShapeDtypeStruct((n, ATTN_WIDTH), BF16),
                 jax.ShapeDtypeStruct((n, KV_WIDTH), BF16),
                 jax.ShapeDtypeStruct((n, KV_WIDTH), BF16),
                 jax.ShapeDtypeStruct((n, RNN_WIDTH), F32),
                 jax.ShapeDtypeStruct((n, RNN_WIDTH), F32)]
    if rope:
        tiles_per_seq = DEC_SEQ // ROW_TILE
        in_specs += [pl.BlockSpec((ROW_TILE, HEAD_DIM), lambda i: (i % tiles_per_seq, 0))] * 2
        args += list(rope_tables)
    else:
        req = ROW_TILE // SEQ
        cache = pl.BlockSpec((req, 1, SEQ, N_KV_HEADS, HEAD_DIM), lambda i: (i, 0, 0, 0, 0))
        out_specs += [cache, cache]
        out_shape += [jax.ShapeDtypeStruct((n // SEQ, 1, SEQ, N_KV_HEADS, HEAD_DIM), F32)] * 2
    return pl.pallas_call(
        functools.partial(_ab_in_kernel, rope=rope),
        grid=(n // ROW_TILE,),
        in_specs=in_specs,
        out_specs=out_specs,
        out_shape=out_shape,
        compiler_params=_params(1),
        name="ab_in_rope" if rope else "ab_in",
    )(*args)


def _rope_tables():
    rows = DEC_SEQ // GRID_W
    row = np.repeat(np.arange(rows, dtype=np.float32), GRID_W)
    col = np.tile(np.arange(GRID_W, dtype=np.float32), rows)
    n_freq = HEAD_DIM // 4
    freqs = np.float32(ROPE_BASE) ** (-np.arange(n_freq, dtype=np.float32) / np.float32(n_freq))
    ang = np.concatenate([row[:, None] * freqs, col[:, None] * freqs], axis=-1)
    cos = np.repeat(np.cos(ang), 2, axis=-1)
    sin = np.repeat(np.sin(ang), 2, axis=-1)
    sign = np.tile(np.array([-1.0, 1.0], np.float32), HEAD_DIM // 2)
    return jnp.asarray(cos, F32), jnp.asarray(sin * sign, F32)


def _sink_attention_head(s_list, v_list, sink):
    sink = sink * _LOG2_E
    m = sink
    for s in s_list:
        m = jnp.maximum(m, jnp.max(s, axis=-1, keepdims=True))
    p_list = [jnp.exp2(s - m) for s in s_list]
    denom = jnp.exp2(sink - m)
    for p in p_list:
        denom = denom + jnp.sum(p, axis=-1, keepdims=True)
    out = None
    for p, v in zip(p_list, v_list):
        o = _dot(p.astype(BF16), v)
        out = o if out is None else out + o
    return out * (1.0 / denom)


def _head_cols(h):
    return slice(h * HEAD_DIM, (h + 1) * HEAD_DIM)


def _ctx_attn_kernel(sink_ref, q_ref, k_ref, v_ref, o_ref):
    def scores(item):
        r, h = item
        rows = slice(r * SEQ, (r + 1) * SEQ)
        kh = k_ref[rows, _head_cols(h // Q_PER_KV)].astype(BF16)
        return _dot_nt(q_ref[rows, _head_cols(h)], kh)

    def finish(item, s):
        r, h = item
        rows = slice(r * SEQ, (r + 1) * SEQ)
        vh = v_ref[rows, _head_cols(h // Q_PER_KV)].astype(BF16)
        o_ref[rows, _head_cols(h)] = _sink_attention_head([s], [vh], sink_ref[h]).astype(o_ref.dtype)

    items = [(r, h) for r in range(REQ_TILE) for h in range(N_Q_HEADS)]
    _software_pipeline(items, scores, finish)


def _ctx_attention(q, k, v, sink):
    seq = lambda width: pl.BlockSpec((REQ_TILE * SEQ, width), lambda b: (b, 0))
    return pl.pallas_call(
        _ctx_attn_kernel,
        grid=(BATCH // REQ_TILE,),
        in_specs=[pl.BlockSpec(memory_space=pltpu.SMEM), seq(ATTN_WIDTH), seq(KV_WIDTH), seq(KV_WIDTH)],
        out_specs=seq(ATTN_WIDTH),
        out_shape=jax.ShapeDtypeStruct((N_PROMPT, ATTN_WIDTH), BF16),
        compiler_params=_params(1),
        name="context_attention",
    )(sink, q, k, v)


LAT_Q = 256


def _lat_attn_kernel(sink_ref, q_ref, kp_ref, kc_ref, kn_ref, vp_ref, vc_ref, vn_ref,
                     kx_ref, vx_ref, o_ref):
    n = pl.program_id(1)
    nb = pl.num_programs(1)
    q = q_ref[...]
    kw = jnp.concatenate([kp_ref[...], kc_ref[...], kn_ref[...]], axis=0)
    vw = jnp.concatenate([vp_ref[...], vc_ref[...], vn_ref[...]], axis=0)
    kx = kx_ref[...].astype(BF16)
    vx = vx_ref[...].astype(BF16)
    n_key = LAT_Q + 2 * WINDOW
    qi = lax.broadcasted_iota(I32, (LAT_Q, n_key), 0)
    kj = lax.broadcasted_iota(I32, (LAT_Q, n_key), 1)
    rel = kj - qi
    valid = (rel >= 0) & (rel <= 2 * WINDOW)
    valid = valid & ((n > 0) | (kj >= WINDOW)) & ((n < nb - 1) | (kj < LAT_Q + WINDOW))
    def scores(h):
        sl = _head_cols(h // Q_PER_KV)
        qh = q[:, _head_cols(h)]
        s_ctx = _dot_nt(qh, kx[:, sl])
        s_win = jnp.where(valid, _dot_nt(qh, kw[:, sl]), NEG_INF)
        return s_ctx, s_win

    def finish(h, s):
        sl = _head_cols(h // Q_PER_KV)
        o_ref[:, _head_cols(h)] = _sink_attention_head(list(s), [vx[:, sl], vw[:, sl]],
                                                       sink_ref[h]).astype(o_ref.dtype)

    _software_pipeline(range(N_Q_HEADS), scores, finish)


def _lat_attention(q, k, v, k_ctx, v_ctx, sink):
    nb = DEC_SEQ // LAT_Q
    nw = DEC_SEQ // WINDOW
    per = LAT_Q // WINDOW
    cur = lambda b, n: (b * nb + n, 0)
    prev = lambda b, n: (b * nw + jnp.maximum(n * per - 1, 0), 0)
    nxt = lambda b, n: (b * nw + jnp.minimum((n + 1) * per, nw - 1), 0)
    tile = lambda width: pl.BlockSpec((LAT_Q, width), cur)
    edge = lambda imap: pl.BlockSpec((WINDOW, KV_WIDTH), imap)
    ctx = pl.BlockSpec((PAST_LEN, KV_WIDTH), lambda b, n: (b, 0))
    return pl.pallas_call(
        _lat_attn_kernel,
        grid=(DEC_BATCH, nb),
        in_specs=[pl.BlockSpec(memory_space=pltpu.SMEM), tile(ATTN_WIDTH),
                  edge(prev), tile(KV_WIDTH), edge(nxt),
                  edge(prev), tile(KV_WIDTH), edge(nxt), ctx, ctx],
        out_specs=tile(ATTN_WIDTH),
        out_shape=jax.ShapeDtypeStruct((N_SAMPLE, ATTN_WIDTH), BF16),
        compiler_params=_params(2),
        name="latent_attention",
    )(sink, q, k, k, k, v, v, v, k_ctx, v_ctx)


RNN_CHUNK = 256


def _rglru_kernel(xr_ref, xg_ref, cw_ref, cb_ref, wa_ref, ba_ref, wx_ref, bx_ref, lam_ref,
                  h0f_ref, h0b_ref, y_ref, hfl_ref, hbf_ref, xc_s, af_s, ab_s, uf_s, ub_s, *, rows, seq_len):
    n_seq = rows // seq_len
    n_chunk = rows // RNN_CHUNK
    cw = cw_ref[...]
    cb = cb_ref[...]
    zeros_halo = jnp.zeros((SUBLANES, RNN_WIDTH), F32)
    row8 = lax.broadcasted_iota(I32, (SUBLANES, RNN_WIDTH), 0)

    def conv_chunk(c):
        r0 = c * RNN_CHUNK
        first = r0 % seq_len == 0
        last = (r0 + RNN_CHUNK) % seq_len == 0
        before = zeros_halo if first else xr_ref[r0 - SUBLANES:r0, :]
        after = zeros_halo if last else xr_ref[r0 + RNN_CHUNK:r0 + RNN_CHUNK + SUBLANES, :]
        win = jnp.concatenate([before, xr_ref[r0:r0 + RNN_CHUNK, :], after], axis=0)
        xc = cb
        n_win = RNN_CHUNK + 2 * SUBLANES
        for i in range(CONV_W):
            shift = (CONV_PAD_LEFT - i) % n_win
            rolled = win if shift == 0 else pltpu.roll(win, shift, 0)
            xc = xc + rolled[SUBLANES:SUBLANES + RNN_CHUNK, :] * cw[i:i + 1, :]
        return xc

    def group_scan(a, u, reverse):
        for k in (1, 2, 4):
            if reverse:
                shift, ok = SUBLANES - k, row8 < SUBLANES - k
            else:
                shift, ok = k, row8 >= k
            a_nb = jnp.where(ok, pltpu.roll(a, shift, 0), 1.0)
            u_nb = jnp.where(ok, pltpu.roll(u, shift, 0), 0.0)
            u = a * u_nb + u
            a = a * a_nb
        return a, u

    for c in range(n_chunk):
        xc_s[c * RNN_CHUNK:(c + 1) * RNN_CHUNK, :] = conv_chunk(c)

    for d, (a_s, u_s) in enumerate(((af_s, uf_s), (ab_s, ub_s))):
        neg = -lam_ref[d:d + 1, :]
        softplus = jnp.maximum(neg, 0.0) + jnp.log1p(jnp.exp(-jnp.abs(neg)))
        decay = -LRU_C * softplus
        wa = wa_ref[d]
        wx = wx_ref[d]
        ba = ba_ref[d:d + 1, :]
        bx = bx_ref[d:d + 1, :]
        for c in range(n_chunk):
            xc = xc_s[c * RNN_CHUNK:(c + 1) * RNN_CHUNK, :]
            xcb = xc.astype(BF16)
            r = _sigmoid(_dot(xcb, wa) + ba)
            i = _sigmoid(_dot(xcb, wx) + bx)
            log_a = r * decay
            a = jnp.exp(log_a)
            a_s[c * RNN_CHUNK:(c + 1) * RNN_CHUNK, :] = a
            one_minus_a2 = -jnp.tanh(log_a) * (a * a + 1.0)
            u_s[c * RNN_CHUNK:(c + 1) * RNN_CHUNK, :] = jnp.sqrt(one_minus_a2) * (i * xc)

    n_group = seq_len // SUBLANES

    def body(g, carries):
        fwd, bwd = carries
        new_f, new_b = [], []
        for s in range(n_seq):
            rf = pl.multiple_of(s * seq_len + g * SUBLANES, SUBLANES)
            a, u = group_scan(af_s[pl.ds(rf, SUBLANES), :], uf_s[pl.ds(rf, SUBLANES), :], False)
            h = u + a * fwd[s]
            uf_s[pl.ds(rf, SUBLANES), :] = h
            new_f.append(h[SUBLANES - 1:SUBLANES, :])
            rb = pl.multiple_of(s * seq_len + (n_group - 1 - g) * SUBLANES, SUBLANES)
            a, u = group_scan(ab_s[pl.ds(rb, SUBLANES), :], ub_s[pl.ds(rb, SUBLANES), :], True)
            h = u + a * bwd[s]
            ub_s[pl.ds(rb, SUBLANES), :] = h
            new_b.append(h[0:1, :])
        return tuple(new_f), tuple(new_b)

    init = (tuple(h0f_ref[s:s + 1, :] for s in range(n_seq)), tuple(h0b_ref[s:s + 1, :] for s in range(n_seq)))
    last_f, first_b = lax.fori_loop(0, n_group, body, init)

    for s in range(n_seq):
        hfl_ref[s:s + 1, :] = last_f[s]
        hbf_ref[s:s + 1, :] = first_b[s]
    for c in range(n_chunk):
        sl = slice(c * RNN_CHUNK, (c + 1) * RNN_CHUNK)
        y_ref[sl, :] = ((uf_s[sl, :] + ub_s[sl, :]) * _gelu_tanh(xg_ref[sl, :])).astype(y_ref.dtype)


def _rglru(xr, xg, conv_w, conv_b, wa, ba, wx, bx, lam, h0f, h0b, rows, seq_len):
    n = xr.shape[0]
    n_seq = rows // seq_len
    row = pl.BlockSpec((rows, RNN_WIDTH), lambda i: (i, 0))
    full = lambda shape: pl.BlockSpec(shape, lambda i: (0,) * len(shape))
    state = pl.BlockSpec((None, n_seq, RNN_WIDTH), lambda i: (i, 0, 0))
    state_shape = jax.ShapeDtypeStruct((n // rows, n_seq, RNN_WIDTH), F32)
    return pl.pallas_call(
        functools.partial(_rglru_kernel, rows=rows, seq_len=seq_len),
        grid=(n // rows,),
        in_specs=[row, row, full((CONV_W, RNN_WIDTH)), full((1, RNN_WIDTH)),
                  full((2, RNN_WIDTH, RNN_WIDTH)), full((2, RNN_WIDTH)),
                  full((2, RNN_WIDTH, RNN_WIDTH)), full((2, RNN_WIDTH)), full((2, RNN_WIDTH)),
                  state, state],
        out_specs=[row, state, state],
        out_shape=[jax.ShapeDtypeStruct((n, RNN_WIDTH), BF16), state_shape, state_shape],
        scratch_shapes=[pltpu.VMEM((rows, RNN_WIDTH), F32)] * 5,
        compiler_params=_params(1),
        name="rglru_%d" % seq_len,
    )(xr, xg, conv_w, conv_b, wa, ba, wx, bx, lam, h0f, h0b)


def _block_diag_dense(w):
    eye = jnp.eye(RNN_BLOCKS, dtype=w.dtype)
    dense = w[:, :, :, None, :] * eye[None, :, None, :, None]
    return dense.reshape(2, RNN_WIDTH, RNN_WIDTH).astype(BF16)


SUB_TILE = 256
SUB_TILES = tuple(slice(s, s + SUB_TILE) for s in range(0, ROW_TILE, SUB_TILE))
SGU_TILE = 512
SGU_TILES = tuple(slice(s, s + SGU_TILE) for s in range(0, ROW_TILE, SGU_TILE))


def _residual_router(rows, x, o, m, lg_ref, lb_ref, rw_ref, x1_ref, xm_ref, aff_ref):
    x1 = _layer_norm(ALPHA * x + m[2:3] * o, lg_ref[...], lb_ref[...])
    x1_ref[rows, :] = x1
    xm = (x1 * (1.0 + m[4:5]) + m[3:4]).astype(BF16)
    xm_ref[rows, :] = xm
    lgt = _dot(xm, rw_ref[...])
    lane = lax.broadcasted_iota(I32, lgt.shape, 1)
    lgt = jnp.where(lane < N_EXPERTS, lgt, NEG_INF)
    ex = jnp.exp(lgt - jnp.max(lgt, axis=-1, keepdims=True))
    aff = ex / jnp.sum(ex, axis=-1, keepdims=True)
    aff_ref[rows, :] = aff[:, :N_EXPERTS]


def _mix_out_kernel(att_ref, rnn_ref, x_ref, mod_ref, w_ref, lg_ref, lb_ref, rw_ref,
                    x1_ref, xm_ref, aff_ref):
    m = mod_ref[...]

    def project(rows):
        return _dot(att_ref[rows, :], w_ref[:ATTN_WIDTH, :]) + _dot(rnn_ref[rows, :], w_ref[ATTN_WIDTH:, :])

    def finish(rows, o):
        _residual_router(rows, x_ref[rows, :], o, m, lg_ref, lb_ref, rw_ref, x1_ref, xm_ref, aff_ref)

    _software_pipeline(SUB_TILES, project, finish)


def _sgu_kernel(x_ref, mod_ref, win_ref, bin_ref, sg_ref, sb_ref, spw_ref, spb_ref, w_ref, lg_ref, lb_ref, rw_ref,
                x1_ref, xm_ref, aff_ref, v_s, gated_s):
    m = mod_ref[...]

    def project(rows):
        h = x_ref[rows, :] * (1.0 + m[1:2]) + m[0:1]
        return _dot(h.astype(BF16), win_ref[...])

    def gate(rows, p):
        p = _gelu_tanh(p + bin_ref[...])
        v_s[rows, :] = _layer_norm(p[:, SGU_WIDTH:], sg_ref[...], sb_ref[...]).astype(v_s.dtype)
        for c in range(rows.start, rows.stop, CHUNK):
            local = slice(c - rows.start, c - rows.start + CHUNK)
            for g in range(SGU_GROUPS):
                cols = slice(g * SGU_GROUP_W, (g + 1) * SGU_GROUP_W)
                mixed = _dot(spw_ref[g], v_s[c:c + CHUNK, cols]) + spb_ref[:, cols]
                gated_s[c:c + CHUNK, cols] = (p[local, cols] * mixed).astype(gated_s.dtype)
        return _dot(gated_s[rows, :], w_ref[...])

    def finish(rows, o):
        _residual_router(rows, x_ref[rows, :], o, m, lg_ref, lb_ref, rw_ref, x1_ref, xm_ref, aff_ref)

    _software_pipeline(SGU_TILES, project, gate, finish)


def _mixer_out(kernel, name, acts, x, mod_l, weights, ln_g, ln_b, router_w, group0, rows_per_group,
               scratch=()):
    n = x.shape[0]
    row = lambda width: pl.BlockSpec((ROW_TILE, width), lambda i: (i, 0))
    full = lambda a: pl.BlockSpec(a.shape, lambda i: (0,) * a.ndim)
    return pl.pallas_call(
        kernel,
        grid=(n // ROW_TILE,),
        in_specs=([row(a.shape[1]) for a in acts]
                  + [row(D_MODEL), pl.BlockSpec((None, 6, D_MODEL), _group_map(group0, rows_per_group))]
                  + [full(w) for w in weights] + [full(ln_g), full(ln_b), full(router_w)]),
        out_specs=[row(D_MODEL), row(D_MODEL), row(N_EXPERTS)],
        out_shape=[jax.ShapeDtypeStruct((n, D_MODEL), F32),
                   jax.ShapeDtypeStruct((n, D_MODEL), BF16),
                   jax.ShapeDtypeStruct((n, N_EXPERTS), F32)],
        scratch_shapes=list(scratch),
        compiler_params=_params(1),
        name=name,
    )(*acts, x, mod_l, *weights, ln_g, ln_b, router_w)


def _route_kernel(*refs, n_req, tokens, cap, place_in_kernel):
    n_blk = tokens // PREFIX_BLOCK
    if place_in_kernel:
        aff_ref, tri_ref, place_ref, posm_ref, posr_ref, affr_ref, blk_ref = refs
        n_col = place_ref.shape[2]
        aff = jnp.zeros((tokens, n_col), F32)
        for b in range(n_req):
            piece = aff_ref[b * tokens:(b + 1) * tokens, :]
            hi = piece.astype(BF16)
            rem = piece - hi.astype(F32)
            mid = rem.astype(BF16)
            low = (rem - mid.astype(F32)).astype(BF16)
            place = place_ref[b]
            aff = aff + ((_dot(hi, place) + _dot(mid, place)) + _dot(low, place))
    else:
        aff_ref, tri_ref, posm_ref, posr_ref, affr_ref, blk_ref = refs
        aff = aff_ref[...]
        n_col = aff.shape[1]

    def bisect(_, lo_hi):
        lo, hi = lo_hi
        mid = lo + ((hi - lo) >> 1)
        cnt = jnp.sum(jnp.where(aff >= pltpu.bitcast(mid, F32), 1.0, 0.0), axis=0, keepdims=True)
        ge = cnt >= cap
        return jnp.where(ge, mid, lo), jnp.where(ge, hi, mid)

    lo0 = jnp.zeros((1, n_col), I32)
    hi0 = jnp.full((1, n_col), 0x7F800000, I32)
    thr_bits, _ = lax.fori_loop(0, 31, bisect, (lo0, hi0))
    thr = pltpu.bitcast(thr_bits, F32)
    above = pltpu.bitcast(thr_bits + 1, F32)

    tri = tri_ref[...]

    def excl_prefix(x01):
        outs, before, inside = [], [], []
        off = jnp.zeros((1, n_col), F32)
        for blk in range(n_blk):
            xb = x01[blk * PREFIX_BLOCK:(blk + 1) * PREFIX_BLOCK, :]
            outs.append(_dot(tri, xb.astype(BF16)) + off)
            cnt = jnp.sum(xb, axis=0, keepdims=True)
            before.append(off)
            inside.append(cnt)
            off = off + cnt
        return (outs[0] if n_blk == 1 else jnp.concatenate(outs, axis=0)), before + inside

    gt = jnp.where(aff >= above, 1.0, 0.0)
    eq = jnp.where(aff >= thr, 1.0, 0.0) - gt
    need = cap - jnp.sum(gt, axis=0, keepdims=True)
    eq_rank, _ = excl_prefix(eq)
    sel = gt + jnp.where(eq_rank < need, eq, 0.0)
    pos, blk_rows = excl_prefix(sel)
    posm = jnp.where(sel > 0.0, pos, -1.0)
    posr_ref[...] = posm.T
    affr_ref[...] = aff.T
    for i, row in enumerate(blk_rows):
        blk_ref[i:i + 1, :] = row
    if place_in_kernel:
        posm_bf16 = posm.astype(BF16)
        for b in range(n_req):
            posm_ref[b * tokens:(b + 1) * tokens, :] = _dot_nt(posm_bf16, place_ref[b])
    else:
        posm_ref[...] = posm


PLACE_IN_KERNEL_MAX_REQ = 8


def _route_requests(aff, tri, n_req, tokens, cap):
    n_real = n_req * N_EXPERTS
    n_col = -(-n_real // LANES) * LANES
    n_blk = tokens // PREFIX_BLOCK
    place_in_kernel = n_req <= PLACE_IN_KERNEL_MAX_REQ
    whole = lambda shape: pl.BlockSpec(shape, lambda i: (0,) * len(shape))
    if place_in_kernel:
        col = np.arange(n_col)
        place = jnp.asarray(col[None, None, :] == (np.arange(n_req)[:, None, None] * N_EXPERTS
                                                   + np.arange(N_EXPERTS)[None, :, None]), BF16)
        args = (aff, tri, place)
        token_major = (n_req * tokens, N_EXPERTS)
        in_specs = [whole(token_major), whole((PREFIX_BLOCK, PREFIX_BLOCK)), whole(place.shape)]
    else:
        aff_t = aff.reshape(n_req, tokens, N_EXPERTS).transpose(1, 0, 2).reshape(tokens, n_real)
        args = (jnp.pad(aff_t, ((0, 0), (0, n_col - n_real))), tri)
        token_major = (tokens, n_col)
        in_specs = [whole(token_major), whole((PREFIX_BLOCK, PREFIX_BLOCK))]
    posm, posr, affr, blk = pl.pallas_call(
        functools.partial(_route_kernel, n_req=n_req, tokens=tokens, cap=cap, place_in_kernel=place_in_kernel),
        grid=(1,),
        in_specs=in_specs,
        out_specs=[whole(token_major), whole((n_col, tokens)), whole((n_col, tokens)), whole((2 * n_blk, n_col))],
        out_shape=[jax.ShapeDtypeStruct(token_major, F32),
                   jax.ShapeDtypeStruct((n_col, tokens), F32),
                   jax.ShapeDtypeStruct((n_col, tokens), F32),
                   jax.ShapeDtypeStruct((2 * n_blk, n_col), F32)],
        compiler_params=_params(1),
        name="route_%d" % tokens,
    )(*args)
    if not place_in_kernel:
        posm = posm[:, :n_real].reshape(tokens, n_req, N_EXPERTS).transpose(1, 0, 2).reshape(n_req * tokens, N_EXPERTS)
    blk = blk[:, :n_real].astype(I32).reshape(2, n_blk, n_req, N_EXPERTS).transpose(0, 2, 1, 3)
    expert_major = lambda a: a.reshape(n_col // N_EXPERTS, N_EXPERTS, tokens)
    return posm, expert_major(posr), expert_major(affr), blk[0], blk[1]


def _slot_gate(hit, aff_row):
    g = jnp.sum(jnp.where(hit, aff_row, 0.0), axis=-1, keepdims=True)
    return jnp.broadcast_to(g, (hit.shape[0], LANES))


def _gather_prompt_kernel(x_ref, posr_ref, affr_ref, xg_ref, g_ref):
    slot = lax.broadcasted_iota(I32, (CAP_P, SEQ), 0).astype(F32)
    for r in range(REQ_TILE):
        tokens = slice(r * SEQ, (r + 1) * SEQ)
        slots = slice(r * CAP_P, (r + 1) * CAP_P)
        posr = posr_ref[r]
        affr = affr_ref[r]
        hits = [posr[e:e + 1, :] == slot for e in range(N_EXPERTS)]
        onehot = jnp.concatenate([jnp.where(h, 1.0, 0.0) for h in hits], axis=0).astype(BF16)
        xg = _dot(onehot, x_ref[tokens, :])
        for e in range(N_EXPERTS):
            xg_ref[e, slots, :] = xg[e * CAP_P:(e + 1) * CAP_P, :].astype(xg_ref.dtype)
            g_ref[e, slots, :] = _slot_gate(hits[e], affr[e:e + 1, :])


def _gather_prompt(xm, posr, affr):
    return pl.pallas_call(
        _gather_prompt_kernel,
        grid=(BATCH // REQ_TILE,),
        in_specs=[pl.BlockSpec((REQ_TILE * SEQ, D_MODEL), lambda b: (b, 0)),
                  pl.BlockSpec((REQ_TILE, N_EXPERTS, SEQ), lambda b: (b, 0, 0)),
                  pl.BlockSpec((REQ_TILE, N_EXPERTS, SEQ), lambda b: (b, 0, 0))],
        out_specs=[pl.BlockSpec((N_EXPERTS, REQ_TILE * CAP_P, D_MODEL), lambda b: (0, b, 0)),
                   pl.BlockSpec((N_EXPERTS, REQ_TILE * CAP_P, LANES), lambda b: (0, b, 0))],
        out_shape=[jax.ShapeDtypeStruct((N_EXPERTS, SLOTS_P, D_MODEL), BF16),
                   jax.ShapeDtypeStruct((N_EXPERTS, SLOTS_P, LANES), F32)],
        compiler_params=_params(1),
        name="gather_prompt",
    )(xm, posr, affr)


TOKEN_BLOCK = PREFIX_BLOCK
N_TOKEN_BLOCKS = DEC_SEQ // TOKEN_BLOCK
SLOT_ALIGN = 16
WINDOW_SLOTS = 64
LAST_WINDOW = CAP_S - WINDOW_SLOTS
ALL_WINDOWS = N_EXPERTS * WINDOW_SLOTS


def _window_plan(before, inside):
    start = (before // SLOT_ALIGN) * SLOT_ALIGN
    span = before - start + inside
    passes = jnp.where(inside > 0, (span + WINDOW_SLOTS - 1) // WINDOW_SLOTS, 0)
    return start.reshape(-1), jnp.max(passes, axis=-1).reshape(-1)


def _window_bounds(start_ref, step, expert, k):
    lower = start_ref[step * N_EXPERTS + expert] + WINDOW_SLOTS * k
    begin = pl.multiple_of(jnp.minimum(lower, LAST_WINDOW), SLOT_ALIGN)
    return lower, begin


BLOCKS_PER_STEP = 2
STEP_TOKENS = BLOCKS_PER_STEP * TOKEN_BLOCK
STEPS_PER_REQ = N_TOKEN_BLOCKS // BLOCKS_PER_STEP


def _gather_sample_kernel(start_ref, passes_ref, x_ref, posr_ref, affr_ref, xg_ref, g_ref):
    t = pl.program_id(1)
    first_block = (pl.program_id(0) * STEPS_PER_REQ + t) * BLOCKS_PER_STEP

    @pl.when(t == 0)
    def _():
        xg_ref[...] = jnp.zeros_like(xg_ref)
        g_ref[...] = jnp.zeros_like(g_ref)

    row = lax.broadcasted_iota(I32, (WINDOW_SLOTS, TOKEN_BLOCK), 0).astype(F32)

    def pick(j, k):
        tokens = slice(j * TOKEN_BLOCK, (j + 1) * TOKEN_BLOCK)
        hits, begins = [], []
        for e in range(N_EXPERTS):
            lower, begin = _window_bounds(start_ref, first_block + j, e, k)
            pe = posr_ref[e:e + 1, tokens]
            hits.append((pe - begin.astype(F32) == row) & (pe >= lower.astype(F32)))
            begins.append(begin)
        onehot = jnp.concatenate([jnp.where(h, 1.0, 0.0) for h in hits], axis=0).astype(BF16)
        return _dot(onehot, x_ref[tokens, :]), hits, begins

    def place(j, picked):
        rows_x, hits, begins = picked
        tokens = slice(j * TOKEN_BLOCK, (j + 1) * TOKEN_BLOCK)
        for e in range(N_EXPERTS):
            win = pl.ds(begins[e], WINDOW_SLOTS)
            xg_ref[e, win, :] += rows_x[e * WINDOW_SLOTS:(e + 1) * WINDOW_SLOTS, :].astype(xg_ref.dtype)
            g_ref[e, win, :] += _slot_gate(hits[e], affr_ref[e:e + 1, tokens])

    _software_pipeline(range(BLOCKS_PER_STEP), lambda j: pick(j, 0), place)

    for j in range(BLOCKS_PER_STEP):
        def extra_pass(k, carry, j=j):
            place(j, pick(j, k))
            return carry

        lax.fori_loop(1, passes_ref[first_block + j], extra_pass, 0)


def _gather_sample(xm, posr, affr, starts, passes):
    expert_rows = pl.BlockSpec((None, N_EXPERTS, STEP_TOKENS), lambda b, t, s, p: (b, 0, t))
    grid_spec = pltpu.PrefetchScalarGridSpec(
        num_scalar_prefetch=2,
        grid=(DEC_BATCH, STEPS_PER_REQ),
        in_specs=[pl.BlockSpec((STEP_TOKENS, D_MODEL), lambda b, t, s, p: (b * STEPS_PER_REQ + t, 0)),
                  expert_rows, expert_rows],
        out_specs=[pl.BlockSpec((N_EXPERTS, CAP_S, D_MODEL), lambda b, t, s, p: (0, b, 0)),
                   pl.BlockSpec((N_EXPERTS, CAP_S, LANES), lambda b, t, s, p: (0, b, 0))],
    )
    return pl.pallas_call(
        _gather_sample_kernel,
        grid_spec=grid_spec,
        out_shape=[jax.ShapeDtypeStruct((N_EXPERTS, SLOTS_S, D_MODEL), BF16),
                   jax.ShapeDtypeStruct((N_EXPERTS, SLOTS_S, LANES), F32)],
        compiler_params=_params(2),
        name="gather_sample",
    )(starts, passes, xm, posr, affr)


FF_TILE = 512
SLOT_CHUNK = 1024


def _ffn_kernel(xp_ref, xs_ref, gp_ref, gs_ref, w1_ref, w3_ref, w2_ref, yp_ref, ys_ref, acc_ref):
    f = pl.program_id(1)
    n_f = pl.num_programs(1)
    @pl.when(f == 0)
    def _():
        acc_ref[...] = jnp.zeros_like(acc_ref)

    cast = {}

    def weight(ref):
        if id(ref) not in cast:
            cast[id(ref)] = ref[...].astype(BF16)
        return cast[id(ref)]

    def up_project(chunk):
        x_ref, c, _ = chunk
        x = x_ref[c * SLOT_CHUNK:(c + 1) * SLOT_CHUNK, :]
        h1 = _dot(x, weight(w1_ref))
        return h1, _dot(x, weight(w3_ref))

    def down_project(chunk, h):
        _, c, base = chunk
        h1, h3 = h
        hid = (h1 * jax.nn.sigmoid(h1) * h3).astype(BF16)
        rows = slice(base + c * SLOT_CHUNK, base + (c + 1) * SLOT_CHUNK)
        acc_ref[rows, :] += _dot(hid, weight(w2_ref))

    chunks = [(x_ref, c, base) for x_ref, n_slot, base in ((xp_ref, SLOTS_P, 0), (xs_ref, SLOTS_S, SLOTS_P))
              for c in range(n_slot // SLOT_CHUNK)]
    _software_pipeline(chunks, up_project, down_project)

    @pl.when(f == n_f - 1)
    def _():
        for y_ref, g_ref, n_slot, base in ((yp_ref, gp_ref, SLOTS_P, 0), (ys_ref, gs_ref, SLOTS_S, SLOTS_P)):
            for c in range(n_slot // SLOT_CHUNK):
                rows = slice(c * SLOT_CHUNK, (c + 1) * SLOT_CHUNK)
                gate = jnp.concatenate([g_ref[rows, :]] * (D_MODEL // LANES), axis=1)
                y_ref[rows, :] = (acc_ref[base + c * SLOT_CHUNK:base + (c + 1) * SLOT_CHUNK, :]
                                  * gate).astype(y_ref.dtype)


def _expert_ffn(xgp, xgs, gp, gs, w1, w3, w2, layer):
    slots = lambda n, width: pl.BlockSpec((None, n, width), lambda e, f: (e, 0, 0))
    return pl.pallas_call(
        _ffn_kernel,
        grid=(N_EXPERTS, EXPERT_FF // FF_TILE),
        in_specs=[slots(SLOTS_P, D_MODEL), slots(SLOTS_S, D_MODEL), slots(SLOTS_P, LANES), slots(SLOTS_S, LANES),
                  pl.BlockSpec((None, None, D_MODEL, FF_TILE), lambda e, f: (layer, e, 0, f)),
                  pl.BlockSpec((None, None, D_MODEL, FF_TILE), lambda e, f: (layer, e, 0, f)),
                  pl.BlockSpec((None, None, FF_TILE, D_MODEL), lambda e, f: (layer, e, f, 0))],
        out_specs=[slots(SLOTS_P, D_MODEL), slots(SLOTS_S, D_MODEL)],
        out_shape=[jax.ShapeDtypeStruct((N_EXPERTS, SLOTS_P, D_MODEL), BF16),
                   jax.ShapeDtypeStruct((N_EXPERTS, SLOTS_S, D_MODEL), BF16)],
        scratch_shapes=[pltpu.VMEM((SLOTS_P + SLOTS_S, D_MODEL), F32)],
        compiler_params=_params(2),
        name="expert_ffn",
    )(xgp, xgs, gp, gs, w1, w3, w2)


def _combine_prompt_kernel(posm_ref, y_ref, x_ref, mod_ref, exp_ref, lg_ref, lb_ref, o_ref):
    m = mod_ref[...]
    slot = (lax.broadcasted_iota(I32, (SEQ, N_EXPERTS * CAP_P), 1) & (CAP_P - 1)).astype(F32)
    def scatter(r):
        tokens = slice(r * SEQ, (r + 1) * SEQ)
        spread = _dot(posm_ref[tokens, :].astype(BF16), exp_ref[...])
        onehot = jnp.where(spread == slot, 1.0, 0.0).astype(BF16)
        y = jnp.concatenate([y_ref[e, r * CAP_P:(r + 1) * CAP_P, :] for e in range(N_EXPERTS)], axis=0)
        return _dot(onehot, y)

    def finish(r, f):
        tokens = slice(r * SEQ, (r + 1) * SEQ)
        o_ref[tokens, :] = _layer_norm(ALPHA * x_ref[tokens, :] + m[5:6] * f, lg_ref[...], lb_ref[...])

    _software_pipeline(range(REQ_TILE), scatter, finish)


def _combine_prompt(posm, y, x1, mod_l, expand, ln_g, ln_b):
    full = lambda a: pl.BlockSpec(a.shape, lambda b: (0,) * a.ndim)
    return pl.pallas_call(
        _combine_prompt_kernel,
        grid=(BATCH // REQ_TILE,),
        in_specs=[pl.BlockSpec((REQ_TILE * SEQ, N_EXPERTS), lambda b: (b, 0)),
                  pl.BlockSpec((N_EXPERTS, REQ_TILE * CAP_P, D_MODEL), lambda b: (0, b, 0)),
                  pl.BlockSpec((REQ_TILE * SEQ, D_MODEL), lambda b: (b, 0)),
                  pl.BlockSpec((None, 6, D_MODEL), lambda b: (0, 0, 0)),
                  full(expand), full(ln_g), full(ln_b)],
        out_specs=pl.BlockSpec((REQ_TILE * SEQ, D_MODEL), lambda b: (b, 0)),
        out_shape=jax.ShapeDtypeStruct((N_PROMPT, D_MODEL), F32),
        compiler_params=_params(1),
        name="combine_prompt",
    )(posm, y, x1, mod_l, expand, ln_g, ln_b)


def _combine_sample_kernel(start_ref, passes_ref, posm_ref, y_ref, x_ref, mod_ref, exp_ref, off_ref,
                           lg_ref, lb_ref, o_ref):
    first_block = (pl.program_id(0) * STEPS_PER_REQ + pl.program_id(1)) * BLOCKS_PER_STEP
    exp_bf16 = exp_ref[...]
    offset = off_ref[...]
    m = mod_ref[...]

    def tokens_of(j):
        return slice(j * TOKEN_BLOCK, (j + 1) * TOKEN_BLOCK)

    def one_pass(j, k):
        spread = _dot(posm_ref[tokens_of(j), :].astype(BF16), exp_bf16)
        begin_row = jnp.zeros((1, ALL_WINDOWS), F32)
        lower_row = jnp.zeros((1, ALL_WINDOWS), F32)
        windows = []
        for e in range(N_EXPERTS):
            lower, begin = _window_bounds(start_ref, first_block + j, e, k)
            lanes_e = exp_bf16[e:e + 1, :].astype(F32)
            begin_row = begin_row + begin.astype(F32) * lanes_e
            lower_row = lower_row + lower.astype(F32) * lanes_e
            windows.append(y_ref[e, pl.ds(begin, WINDOW_SLOTS), :])
        hit = (spread - begin_row == offset) & (spread >= lower_row)
        onehot = jnp.where(hit, 1.0, 0.0).astype(BF16)
        return _dot(onehot, jnp.concatenate(windows, axis=0))

    def finish(j, f):
        rows = tokens_of(j)
        o_ref[rows, :] = _layer_norm(ALPHA * x_ref[rows, :] + m[5:6] * f, lg_ref[...], lb_ref[...])

    _software_pipeline(range(BLOCKS_PER_STEP), lambda j: one_pass(j, 0), finish)

    for j in range(BLOCKS_PER_STEP):
        n_pass = passes_ref[first_block + j]

        @pl.when(n_pass > 1)
        def _(j=j, n_pass=n_pass):
            f = lax.fori_loop(1, n_pass, lambda k, f: f + one_pass(j, k), one_pass(j, 0))
            finish(j, f)


def _combine_sample(posm, y, x1, mod_l, ln_g, ln_b, starts, passes):
    lane = np.arange(ALL_WINDOWS)
    expand = jnp.asarray((lane[None, :] // WINDOW_SLOTS) == np.arange(N_EXPERTS)[:, None], BF16)
    offset = jnp.asarray((lane % WINDOW_SLOTS).reshape(1, ALL_WINDOWS), F32)
    row = lambda width: pl.BlockSpec((STEP_TOKENS, width), lambda b, t, s, p: (b * STEPS_PER_REQ + t, 0))
    full = lambda a: pl.BlockSpec(a.shape, lambda b, t, s, p: (0,) * a.ndim)
    grid_spec = pltpu.PrefetchScalarGridSpec(
        num_scalar_prefetch=2,
        grid=(DEC_BATCH, STEPS_PER_REQ),
        in_specs=[row(N_EXPERTS),
                  pl.BlockSpec((N_EXPERTS, CAP_S, D_MODEL), lambda b, t, s, p: (0, b, 0)),
                  row(D_MODEL),
                  pl.BlockSpec((None, 6, D_MODEL), lambda b, t, s, p: (1 + b, 0, 0)),
                  full(expand), full(offset), full(ln_g), full(ln_b)],
        out_specs=row(D_MODEL),
    )
    return pl.pallas_call(
        _combine_sample_kernel,
        grid_spec=grid_spec,
        out_shape=jax.ShapeDtypeStruct((N_SAMPLE, D_MODEL), F32),
        compiler_params=_params(2),
        name="combine_sample",
    )(starts, passes, posm, y, x1, mod_l, expand, offset, ln_g, ln_b)


def _moe(prompt, sample, mod_l, w1, w3, w2, layer, ln_g, ln_b, tri, expand):
    x1_p, xm_p, aff_p = prompt
    x1_s, xm_s, aff_s = sample
    posm_p, posr_p, affr_p, _, _ = _route_requests(aff_p, tri, BATCH, SEQ, CAP_P)
    posm_s, posr_s, affr_s, before_s, inside_s = _route_requests(aff_s, tri, DEC_BATCH, DEC_SEQ, CAP_S)
    starts, passes = _window_plan(before_s, inside_s)
    xg_p, g_p = _gather_prompt(xm_p, posr_p, affr_p)
    xg_s, g_s = _gather_sample(xm_s, posr_s, affr_s, starts, passes)
    y_p, y_s = _expert_ffn(xg_p, xg_s, g_p, g_s, w1, w3, w2, layer)
    x2_p = _combine_prompt(posm_p, y_p, x1_p, mod_l, expand, ln_g, ln_b)
    x2_s = _combine_sample(posm_s, y_s, x1_s, mod_l, ln_g, ln_b, starts, passes)
    return x2_p, x2_s


def kernel(x_prompt, x_sample, cache_k, cache_v, state_rglru, c, c_ctx, mod_w, mod_b, ln_mix_g, ln_mix_b, ln_ffn_g, ln_ffn_b, ab_in_w, attn_sink, rnn_conv_w, rnn_conv_b, lru_wa, lru_ba, lru_wx, lru_bx, lru_lambda, ab_out_w, sgu_in_w, sgu_in_b, sgu_ln_g, sgu_ln_b, sgu_spatial_w, sgu_spatial_b, sgu_out_w, router_w, moe_w1, moe_w3, moe_w2):
    xp = x_prompt.reshape(N_PROMPT, D_MODEL)
    xs = x_sample.reshape(N_SAMPLE, D_MODEL)

    cvec = jnp.concatenate([c_ctx[None], c, jnp.zeros((SUBLANES - 1 - DEC_BATCH, D_MODEL), F32)], axis=0)
    mod = _modulation(cvec, mod_w, mod_b).reshape(DEPTH, SUBLANES, 6, D_MODEL)

    idx = np.arange(PREFIX_BLOCK)
    tri = jnp.asarray(idx[None, :] < idx[:, None], BF16)
    lane = np.arange(N_EXPERTS * CAP_P)
    expand = jnp.asarray((lane[None, :] // CAP_P) == np.arange(N_EXPERTS)[:, None], BF16)
    rope_tables = _rope_tables()
    vec = lambda a: a.reshape(1, -1)
    router_pad = lambda w: jnp.pad(w, ((0, 0), (0, LANES - N_EXPERTS))).astype(BF16)

    new_k = new_v = new_state = None
    for l in range(DEPTH):
        e = l // 2
        mod_l = mod[l]
        rw = router_pad(router_w[l])
        lg, lb = vec(ln_mix_g[l]), vec(ln_mix_b[l])
        if l % 2 == 0:
            w_in = ab_in_w[e].astype(BF16)
            w_out = ab_out_w[e].astype(BF16)
            wa = _block_diag_dense(lru_wa[e])
            wx = _block_diag_dense(lru_wx[e])
            rnn_w = (rnn_conv_w[e], vec(rnn_conv_b[e]), wa, lru_ba[e], wx, lru_bx[e], lru_lambda[e])

            q, k, v, xr, xg, new_k, new_v = _ab_in(xp, mod_l, w_in, 0, N_PROMPT, None)
            att = _ctx_attention(q, k, v, attn_sink[e])
            rows = 8 * SEQ
            zeros = jnp.zeros((N_PROMPT // rows, rows // SEQ, RNN_WIDTH), F32)
            rnn, hf_last, hb_first = _rglru(xr, xg, *rnn_w, zeros, zeros, rows, SEQ)
            new_state = jnp.stack([hf_last.reshape(BATCH, RNN_WIDTH), hb_first.reshape(BATCH, RNN_WIDTH)], axis=1)
            prompt = _mixer_out(_mix_out_kernel, "mix_out", (att, rnn), xp, mod_l, (w_out,),
                                lg, lb, rw, 0, N_PROMPT)

            q, k, v, xr, xg = _ab_in(xs, mod_l, w_in, 1, DEC_SEQ, rope_tables)
            att = _lat_attention(q, k, v, cache_k[:, e].reshape(DEC_BATCH * PAST_LEN, KV_WIDTH),
                                 cache_v[:, e].reshape(DEC_BATCH * PAST_LEN, KV_WIDTH), attn_sink[e])
            h0 = state_rglru[:, e]
            rnn, _, _ = _rglru(xr, xg, *rnn_w, h0[:, 0][:, None, :], h0[:, 1][:, None, :], DEC_SEQ, DEC_SEQ)
            sample = _mixer_out(_mix_out_kernel, "mix_out", (att, rnn), xs, mod_l, (w_out,),
                                lg, lb, rw, 1, DEC_SEQ)
        else:
            w_in = sgu_in_w[e].astype(BF16)
            w_out = sgu_out_w[e].astype(BF16)
            spw = sgu_spatial_w[e].astype(BF16)
            spb = jnp.repeat(sgu_spatial_b[e].T, SGU_GROUP_W, axis=1)
            sgu_w = (w_in, vec(sgu_in_b[e]), vec(sgu_ln_g[e]), vec(sgu_ln_b[e]), spw, spb, w_out)
            scratch = (pltpu.VMEM((ROW_TILE, SGU_WIDTH), BF16), pltpu.VMEM((ROW_TILE, SGU_WIDTH), BF16))
            prompt = _mixer_out(_sgu_kernel, "sgu", (), xp, mod_l, sgu_w, lg, lb, rw, 0, N_PROMPT, scratch)
            sample = _mixer_out(_sgu_kernel, "sgu", (), xs, mod_l, sgu_w, lg, lb, rw, 1, DEC_SEQ, scratch)

        xp, xs = _moe(prompt, sample, mod_l, moe_w1, moe_w3, moe_w2, l,
                      vec(ln_ffn_g[l]), vec(ln_ffn_b[l]), tri, expand)

    return (xp.reshape(BATCH, SEQ, D_MODEL),
            xs.reshape(DEC_BATCH, DEC_SEQ, D_MODEL),
            new_k,
            new_v,
            new_state.reshape(BATCH, 1, 2, RNN_WIDTH))
```

```python
import functools

import numpy as np
import jax
import jax.numpy as jnp
from jax import lax
from jax.experimental import pallas as pl
from jax.experimental.pallas import tpu as pltpu

F32 = jnp.float32
BF16 = jnp.bfloat16
I32 = jnp.int32

D_MODEL = 1024
BATCH = 32
SEQ = 256
DEPTH = 2
DEC_BATCH = 4
DEC_SEQ = 2048
PAST_LEN = 256
GRID_W = 64
HEAD_DIM = 128
N_Q_HEADS = 4
N_KV_HEADS = 2
Q_PER_KV = N_Q_HEADS // N_KV_HEADS
ATTN_WIDTH = N_Q_HEADS * HEAD_DIM
KV_WIDTH = N_KV_HEADS * HEAD_DIM
WINDOW = 128
BLOCK = 128
ATTN_SCALE = HEAD_DIM ** -0.5
ROPE_BASE = 10000.0
NEG_INF = -1e30
RNN_WIDTH = 512
RNN_BLOCKS = 8
RNN_BLOCK_W = RNN_WIDTH // RNN_BLOCKS
CONV_W = 4
CONV_PAD_LEFT = 2
LRU_C = 8.0
AB_IN_WIDTH = ATTN_WIDTH + 2 * KV_WIDTH + 2 * RNN_WIDTH
CHUNK = 128
SGU_WIDTH = D_MODEL
SGU_GROUPS = 8
SGU_GROUP_W = SGU_WIDTH // SGU_GROUPS
N_EXPERTS = 16
EXPERT_FF = 2048
EC_FACTOR = 2
ALPHA = (2 * DEPTH) ** 0.25
LN_EPS = 1e-6

N_PROMPT = BATCH * SEQ
N_SAMPLE = DEC_BATCH * DEC_SEQ
CAP_P = EC_FACTOR * SEQ // N_EXPERTS
CAP_S = EC_FACTOR * DEC_SEQ // N_EXPERTS
SLOTS_P = BATCH * CAP_P
SLOTS_S = DEC_BATCH * CAP_S

LANES = 128
SUBLANES = 8
ROW_TILE = 1024
REQ_TILE = 4
PREFIX_BLOCK = 256
VMEM_LIMIT = 56 * 1024 * 1024


def _params(n_axes=1):
    return pltpu.CompilerParams(dimension_semantics=("arbitrary",) * n_axes,
                                vmem_limit_bytes=VMEM_LIMIT)


def _layer_norm(x, g, b):
    mu = jnp.mean(x, axis=-1, keepdims=True)
    xc = x - mu
    var = jnp.mean(xc * xc, axis=-1, keepdims=True)
    return xc * lax.rsqrt(var + LN_EPS) * g + b


_LOG2_E = 1.4426950408889634
_GELU_K0 = -2.0 * 0.7978845608028654 * _LOG2_E
_GELU_K1 = _GELU_K0 * 0.044715


def _gelu_tanh(x):
    return x / (1.0 + jnp.exp2(x * (_GELU_K0 + _GELU_K1 * (x * x))))


def _sigmoid(z):
    return 0.5 + 0.5 * jnp.tanh(0.5 * z)


def _dot(a, b):
    return jnp.dot(a, b, preferred_element_type=F32)


def _dot_nt(a, b):
    return lax.dot_general(a, b, (((1,), (1,)), ((), ())), preferred_element_type=F32)


def _software_pipeline(items, first_stage, *later_stages):
    items = list(items)
    stages = (first_stage,) + later_stages
    carried = {}
    for t in range(len(items) + len(stages) - 1):
        for s, stage in enumerate(stages):
            i = t - s
            if 0 <= i < len(items):
                carried[i] = stage(items[i]) if s == 0 else stage(items[i], carried[i])


def _mod_kernel(c_ref, w_ref, b_ref, o_ref):
    c = c_ref[...]
    s = c * jax.nn.sigmoid(c)
    o_ref[...] = _dot(s.astype(BF16), w_ref[...].astype(BF16)) + b_ref[...]


MOD_COLS = 2 * D_MODEL


def _modulation(cvec8, mod_w, mod_b):
    return pl.pallas_call(
        _mod_kernel,
        grid=(DEPTH, 6 * D_MODEL // MOD_COLS),
        in_specs=[pl.BlockSpec((SUBLANES, D_MODEL), lambda l, j: (0, 0)),
                  pl.BlockSpec((None, D_MODEL, MOD_COLS), lambda l, j: (l, 0, j)),
                  pl.BlockSpec((None, 1, MOD_COLS), lambda l, j: (l, 0, j))],
        out_specs=pl.BlockSpec((None, SUBLANES, MOD_COLS), lambda l, j: (l, 0, j)),
        out_shape=jax.ShapeDtypeStruct((DEPTH, SUBLANES, 6 * D_MODEL), F32),
        compiler_params=_params(2),
        name="adaln_modulation",
    )(cvec8, mod_w, mod_b.reshape(DEPTH, 1, 6 * D_MODEL))


def _group_map(group0, rows_per_group):
    tiles_per_group = rows_per_group // ROW_TILE
    return lambda i: (group0 + i // tiles_per_group, 0, 0)


def _rope(t, cos, sin_signed):
    lane = lax.broadcasted_iota(I32, t.shape, 1)
    swapped = jnp.where((lane & 1) == 0, pltpu.roll(t, HEAD_DIM - 1, 1), pltpu.roll(t, 1, 1))
    return t * cos + swapped * sin_signed


def _ab_in_kernel(*refs, rope):
    if rope:
        x_ref, mod_ref, w_ref, cos_ref, sin_ref, q_ref, k_ref, v_ref, xr_ref, xg_ref = refs
    else:
        x_ref, mod_ref, w_ref, q_ref, k_ref, v_ref, xr_ref, xg_ref, ck_ref, cv_ref = refs
    m = mod_ref[...]
    h = x_ref[...] * (1.0 + m[1:2]) + m[0:1]
    p = _dot(h.astype(BF16), w_ref[...])
    q = p[:, :ATTN_WIDTH] * (ATTN_SCALE * _LOG2_E)
    k = p[:, ATTN_WIDTH:ATTN_WIDTH + KV_WIDTH]
    v = p[:, ATTN_WIDTH + KV_WIDTH:ATTN_WIDTH + 2 * KV_WIDTH]
    if rope:
        cos = cos_ref[...]
        sin = sin_ref[...]
        q = jnp.concatenate([_rope(q[:, i * HEAD_DIM:(i + 1) * HEAD_DIM], cos, sin)
                             for i in range(N_Q_HEADS)], axis=1)
        k = jnp.concatenate([_rope(k[:, i * HEAD_DIM:(i + 1) * HEAD_DIM], cos, sin)
                             for i in range(N_KV_HEADS)], axis=1)
    q_ref[...] = q.astype(q_ref.dtype)
    k_ref[...] = k.astype(k_ref.dtype)
    v_ref[...] = v.astype(v_ref.dtype)
    xr_ref[...] = p[:, ATTN_WIDTH + 2 * KV_WIDTH:ATTN_WIDTH + 2 * KV_WIDTH + RNN_WIDTH]
    xg_ref[...] = p[:, ATTN_WIDTH + 2 * KV_WIDTH + RNN_WIDTH:]
    if not rope:
        for r in range(ROW_TILE // SEQ):
            for i in range(N_KV_HEADS):
                ck_ref[r, 0, :, i, :] = k[r * SEQ:(r + 1) * SEQ, _head_cols(i)]
                cv_ref[r, 0, :, i, :] = v[r * SEQ:(r + 1) * SEQ, _head_cols(i)]


def _ab_in(x, mod_l, w_bf16, group0, rows_per_group, rope_tables):
    n = x.shape[0]
    rope = rope_tables is not None
    row = lambda width: pl.BlockSpec((ROW_TILE, width), lambda i: (i, 0))
    in_specs = [row(D_MODEL),
                pl.BlockSpec((None, 6, D_MODEL), _group_map(group0, rows_per_group)),
                pl.BlockSpec((D_MODEL, AB_IN_WIDTH), lambda i: (0, 0))]
    args = [x, mod_l, w_bf16]
    out_specs = [row(ATTN_WIDTH), row(KV_WIDTH), row(KV_WIDTH), row(RNN_WIDTH), row(RNN_WIDTH)]
    out_shape = [jax.ShapeDtypeStruct((n, ATTN_WIDTH), BF16),
                 jax.ShapeDtypeStruct((n, KV_WIDTH), BF16),
                 jax.ShapeDtypeStruct((n, KV_WIDTH), BF16),
                 jax.ShapeDtypeStruct((n, RNN_WIDTH), F32),
                 jax.ShapeDtypeStruct((n, RNN_WIDTH), F32)]
    if rope:
        tiles_per_seq = DEC_SEQ // ROW_TILE
        in_specs += [pl.BlockSpec((ROW_TILE, HEAD_DIM), lambda i: (i % tiles_per_seq, 0))] * 2
        args += list(rope_tables)
    else:
        req = ROW_TILE // SEQ
        cache = pl.BlockSpec((req, 1, SEQ, N_KV_HEADS, HEAD_DIM), lambda i: (i, 0, 0, 0, 0))
        out_specs += [cache, cache]
        out_shape += [jax.ShapeDtypeStruct((n // SEQ, 1, SEQ, N_KV_HEADS, HEAD_DIM), F32)] * 2
    return pl.pallas_call(
        functools.partial(_ab_in_kernel, rope=rope),
        grid=(n // ROW_TILE,),
        in_specs=in_specs,
        out_specs=out_specs,
        out_shape=out_shape,
        compiler_params=_params(1),
        name="ab_in_rope" if rope else "ab_in",
    )(*args)


def _rope_tables():
    rows = DEC_SEQ // GRID_W
    row = np.repeat(np.arange(rows, dtype=np.float32), GRID_W)
    col = np.tile(np.arange(GRID_W, dtype=np.float32), rows)
    n_freq = HEAD_DIM // 4
    freqs = np.float32(ROPE_BASE) ** (-np.arange(n_freq, dtype=np.float32) / np.float32(n_freq))
    ang = np.concatenate([row[:, None] * freqs, col[:, None] * freqs], axis=-1)
    cos = np.repeat(np.cos(ang), 2, axis=-1)
    sin = np.repeat(np.sin(ang), 2, axis=-1)
    sign = np.tile(np.array([-1.0, 1.0], np.float32), HEAD_DIM // 2)
    return jnp.asarray(cos, F32), jnp.asarray(sin * sign, F32)


def _sink_attention_head(s_list, v_list, sink):
    sink = sink * _LOG2_E
    m = sink
    for s in s_list:
        m = jnp.maximum(m, jnp.max(s, axis=-1, keepdims=True))
    p_list = [jnp.exp2(s - m) for s in s_list]
    denom = jnp.exp2(sink - m)
    for p in p_list:
        denom = denom + jnp.sum(p, axis=-1, keepdims=True)
    out = None
    for p, v in zip(p_list, v_list):
        o = _dot(p.astype(BF16), v)
        out = o if out is None else out + o
    return out * (1.0 / denom)


def _head_cols(h):
    return slice(h * HEAD_DIM, (h + 1) * HEAD_DIM)


def _ctx_attn_kernel(sink_ref, q_ref, k_ref, v_ref, o_ref):
    def scores(item):
        r, h = item
        rows = slice(r * SEQ, (r + 1) * SEQ)
        kh = k_ref[rows, _head_cols(h // Q_PER_KV)].astype(BF16)
        return _dot_nt(q_ref[rows, _head_cols(h)], kh)

    def finish(item, s):
        r, h = item
        rows = slice(r * SEQ, (r + 1) * SEQ)
        vh = v_ref[rows, _head_cols(h // Q_PER_KV)].astype(BF16)
        o_ref[rows, _head_cols(h)] = _sink_attention_head([s], [vh], sink_ref[h]).astype(o_ref.dtype)

    items = [(r, h) for r in range(REQ_TILE) for h in range(N_Q_HEADS)]
    _software_pipeline(items, scores, finish)


def _ctx_attention(q, k, v, sink):
    seq = lambda width: pl.BlockSpec((REQ_TILE * SEQ, width), lambda b: (b, 0))
    return pl.pallas_call(
        _ctx_attn_kernel,
        grid=(BATCH // REQ_TILE,),
        in_specs=[pl.BlockSpec(memory_space=pltpu.SMEM), seq(ATTN_WIDTH), seq(KV_WIDTH), seq(KV_WIDTH)],
        out_specs=seq(ATTN_WIDTH),
        out_shape=jax.ShapeDtypeStruct((N_PROMPT, ATTN_WIDTH), BF16),
        compiler_params=_params(1),
        name="context_attention",
    )(sink, q, k, v)


LAT_Q = 512


def _lat_attn_kernel(sink_ref, q_ref, kp_ref, kc_ref, kn_ref, vp_ref, vc_ref, vn_ref,
                     kx_ref, vx_ref, o_ref):
    n = pl.program_id(1)
    nb = pl.num_programs(1)
    q = q_ref[...]
    kw = jnp.concatenate([kp_ref[...], kc_ref[...], kn_ref[...]], axis=0)
    vw = jnp.concatenate([vp_ref[...], vc_ref[...], vn_ref[...]], axis=0)
    kx = kx_ref[...].astype(BF16)
    vx = vx_ref[...].astype(BF16)
    n_key = 3 * WINDOW
    qi = lax.broadcasted_iota(I32, (WINDOW, n_key), 0)
    kj = lax.broadcasted_iota(I32, (WINDOW, n_key), 1)
    rel = kj - qi
    in_band = (rel >= 0) & (rel <= 2 * WINDOW)

    def valid_keys(j):
        pos = kj + j * WINDOW
        return in_band & ((n > 0) | (pos >= WINDOW)) & ((n < nb - 1) | (pos < LAT_Q + WINDOW))

    valid = [valid_keys(j) for j in range(LAT_Q // WINDOW)]

    def scores(item):
        h, j = item
        sl = _head_cols(h // Q_PER_KV)
        qh = q[j * WINDOW:(j + 1) * WINDOW, _head_cols(h)]
        s_ctx = _dot_nt(qh, kx[:, sl])
        s_win = jnp.where(valid[j], _dot_nt(qh, kw[j * WINDOW:j * WINDOW + n_key, sl]), NEG_INF)
        return s_ctx, s_win

    def finish(item, s):
        h, j = item
        sl = _head_cols(h // Q_PER_KV)
        out = _sink_attention_head(list(s), [vx[:, sl], vw[j * WINDOW:j * WINDOW + n_key, sl]], sink_ref[h])
        o_ref[j * WINDOW:(j + 1) * WINDOW, _head_cols(h)] = out.astype(o_ref.dtype)

    _software_pipeline([(h, j) for h in range(N_Q_HEADS) for j in range(LAT_Q // WINDOW)], scores, finish)


def _lat_attention(q, k, v, k_ctx, v_ctx, sink):
    nb = DEC_SEQ // LAT_Q
    nw = DEC_SEQ // WINDOW
    per = LAT_Q // WINDOW
    cur = lambda b, n: (b * nb + n, 0)
    prev = lambda b, n: (b * nw + jnp.maximum(n * per - 1, 0), 0)
    nxt = lambda b, n: (b * nw + jnp.minimum((n + 1) * per, nw - 1), 0)
    tile = lambda width: pl.BlockSpec((LAT_Q, width), cur)
    edge = lambda imap: pl.BlockSpec((WINDOW, KV_WIDTH), imap)
    ctx = pl.BlockSpec((PAST_LEN, KV_WIDTH), lambda b, n: (b, 0))
    return pl.pallas_call(
        _lat_attn_kernel,
        grid=(DEC_BATCH, nb),
        in_specs=[pl.BlockSpec(memory_space=pltpu.SMEM), tile(ATTN_WIDTH),
                  edge(prev), tile(KV_WIDTH), edge(nxt),
                  edge(prev), tile(KV_WIDTH), edge(nxt), ctx, ctx],
        out_specs=tile(ATTN_WIDTH),
        out_shape=jax.ShapeDtypeStruct((N_SAMPLE, ATTN_WIDTH), BF16),
        compiler_params=_params(2),
        name="latent_attention",
    )(sink, q, k, k, k, v, v, v, k_ctx, v_ctx)


RNN_CHUNK = 256


def _rglru_kernel(xr_ref, xg_ref, cw_ref, cb_ref, wa_ref, ba_ref, wx_ref, bx_ref, lam_ref,
                  h0f_ref, h0b_ref, y_ref, hfl_ref, hbf_ref, xc_s, af_s, ab_s, uf_s, ub_s, *, rows, seq_len):
    n_seq = rows // seq_len
    n_chunk = rows // RNN_CHUNK
    cw = cw_ref[...]
    cb = cb_ref[...]
    zeros_halo = jnp.zeros((SUBLANES, RNN_WIDTH), F32)
    row8 = lax.broadcasted_iota(I32, (SUBLANES, RNN_WIDTH), 0)

    def conv_chunk(c):
        r0 = c * RNN_CHUNK
        first = r0 % seq_len == 0
        last = (r0 + RNN_CHUNK) % seq_len == 0
        before = zeros_halo if first else xr_ref[r0 - SUBLANES:r0, :]
        after = zeros_halo if last else xr_ref[r0 + RNN_CHUNK:r0 + RNN_CHUNK + SUBLANES, :]
        win = jnp.concatenate([before, xr_ref[r0:r0 + RNN_CHUNK, :], after], axis=0)
        xc = cb
        n_win = RNN_CHUNK + 2 * SUBLANES
        for i in range(CONV_W):
            shift = (CONV_PAD_LEFT - i) % n_win
            rolled = win if shift == 0 else pltpu.roll(win, shift, 0)
            xc = xc + rolled[SUBLANES:SUBLANES + RNN_CHUNK, :] * cw[i:i + 1, :]
        return xc

    def group_scan(a, u, reverse):
        for k in (1, 2, 4):
            if reverse:
                shift, ok = SUBLANES - k, row8 < SUBLANES - k
            else:
                shift, ok = k, row8 >= k
            a_nb = jnp.where(ok, pltpu.roll(a, shift, 0), 1.0)
            u_nb = jnp.where(ok, pltpu.roll(u, shift, 0), 0.0)
            u = a * u_nb + u
            a = a * a_nb
        return a, u

    for c in range(n_chunk):
        xc_s[c * RNN_CHUNK:(c + 1) * RNN_CHUNK, :] = conv_chunk(c)

    for d, (a_s, u_s) in enumerate(((af_s, uf_s), (ab_s, ub_s))):
        neg = -lam_ref[d:d + 1, :]
        softplus = jnp.maximum(neg, 0.0) + jnp.log1p(jnp.exp(-jnp.abs(neg)))
        decay = -LRU_C * softplus
        wa = wa_ref[d]
        wx = wx_ref[d]
        ba = ba_ref[d:d + 1, :]
        bx = bx_ref[d:d + 1, :]
        for c in range(n_chunk):
            xc = xc_s[c * RNN_CHUNK:(c + 1) * RNN_CHUNK, :]
            xcb = xc.astype(BF16)
            r = _sigmoid(_dot(xcb, wa) + ba)
            i = _sigmoid(_dot(xcb, wx) + bx)
            log_a = r * decay
            a = jnp.exp(log_a)
            a_s[c * RNN_CHUNK:(c + 1) * RNN_CHUNK, :] = a
            one_minus_a2 = -jnp.tanh(log_a) * (a * a + 1.0)
            u_s[c * RNN_CHUNK:(c + 1) * RNN_CHUNK, :] = jnp.sqrt(one_minus_a2) * (i * xc)

    n_group = seq_len // SUBLANES

    def body(g, carries):
        fwd, bwd = carries
        new_f, new_b = [], []
        for s in range(n_seq):
            rf = pl.multiple_of(s * seq_len + g * SUBLANES, SUBLANES)
            a, u = group_scan(af_s[pl.ds(rf, SUBLANES), :], uf_s[pl.ds(rf, SUBLANES), :], False)
            h = u + a * fwd[s]
            uf_s[pl.ds(rf, SUBLANES), :] = h
            new_f.append(h[SUBLANES - 1:SUBLANES, :])
            rb = pl.multiple_of(s * seq_len + (n_group - 1 - g) * SUBLANES, SUBLANES)
            a, u = group_scan(ab_s[pl.ds(rb, SUBLANES), :], ub_s[pl.ds(rb, SUBLANES), :], True)
            h = u + a * bwd[s]
            ub_s[pl.ds(rb, SUBLANES), :] = h
            new_b.append(h[0:1, :])
        return tuple(new_f), tuple(new_b)

    init = (tuple(h0f_ref[s:s + 1, :] for s in range(n_seq)), tuple(h0b_ref[s:s + 1, :] for s in range(n_seq)))
    last_f, first_b = lax.fori_loop(0, n_group, body, init)

    for s in range(n_seq):
        hfl_ref[s:s + 1, :] = last_f[s]
        hbf_ref[s:s + 1, :] = first_b[s]
    for c in range(n_chunk):
        sl = slice(c * RNN_CHUNK, (c + 1) * RNN_CHUNK)
        y_ref[sl, :] = ((uf_s[sl, :] + ub_s[sl, :]) * _gelu_tanh(xg_ref[sl, :])).astype(y_ref.dtype)


def _rglru(xr, xg, conv_w, conv_b, wa, ba, wx, bx, lam, h0f, h0b, rows, seq_len):
    n = xr.shape[0]
    n_seq = rows // seq_len
    row = pl.BlockSpec((rows, RNN_WIDTH), lambda i: (i, 0))
    full = lambda shape: pl.BlockSpec(shape, lambda i: (0,) * len(shape))
    state = pl.BlockSpec((None, n_seq, RNN_WIDTH), lambda i: (i, 0, 0))
    state_shape = jax.ShapeDtypeStruct((n // rows, n_seq, RNN_WIDTH), F32)
    return pl.pallas_call(
        functools.partial(_rglru_kernel, rows=rows, seq_len=seq_len),
        grid=(n // rows,),
        in_specs=[row, row, full((CONV_W, RNN_WIDTH)), full((1, RNN_WIDTH)),
                  full((2, RNN_WIDTH, RNN_WIDTH)), full((2, RNN_WIDTH)),
                  full((2, RNN_WIDTH, RNN_WIDTH)), full((2, RNN_WIDTH)), full((2, RNN_WIDTH)),
                  state, state],
        out_specs=[row, state, state],
        out_shape=[jax.ShapeDtypeStruct((n, RNN_WIDTH), BF16), state_shape, state_shape],
        scratch_shapes=[pltpu.VMEM((rows, RNN_WIDTH), F32)] * 5,
        compiler_params=_params(1),
        name="rglru_%d" % seq_len,
    )(xr, xg, conv_w, conv_b, wa, ba, wx, bx, lam, h0f, h0b)


def _block_diag_dense(w):
    eye = jnp.eye(RNN_BLOCKS, dtype=w.dtype)
    dense = w[:, :, :, None, :] * eye[None, :, None, :, None]
    return dense.reshape(2, RNN_WIDTH, RNN_WIDTH).astype(BF16)


SUB_TILE = 256
SUB_TILES = tuple(slice(s, s + SUB_TILE) for s in range(0, ROW_TILE, SUB_TILE))
SGU_TILE = 512
SGU_TILES = tuple(slice(s, s + SGU_TILE) for s in range(0, ROW_TILE, SGU_TILE))


def _residual_router(rows, x, o, m, lg_ref, lb_ref, rw_ref, x1_ref, xm_ref, aff_ref):
    x1 = _layer_norm(ALPHA * x + m[2:3] * o, lg_ref[...], lb_ref[...])
    x1_ref[rows, :] = x1
    xm = (x1 * (1.0 + m[4:5]) + m[3:4]).astype(BF16)
    xm_ref[rows, :] = xm
    lgt = _dot(xm, rw_ref[...])
    lane = lax.broadcasted_iota(I32, lgt.shape, 1)
    lgt = jnp.where(lane < N_EXPERTS, lgt, NEG_INF)
    ex = jnp.exp(lgt - jnp.max(lgt, axis=-1, keepdims=True))
    aff = ex / jnp.sum(ex, axis=-1, keepdims=True)
    aff_ref[rows, :] = aff[:, :N_EXPERTS]


def _mix_out_kernel(att_ref, rnn_ref, x_ref, mod_ref, w_ref, lg_ref, lb_ref, rw_ref,
                    x1_ref, xm_ref, aff_ref):
    m = mod_ref[...]

    def project(rows):
        return _dot(att_ref[rows, :], w_ref[:ATTN_WIDTH, :]) + _dot(rnn_ref[rows, :], w_ref[ATTN_WIDTH:, :])

    def finish(rows, o):
        _residual_router(rows, x_ref[rows, :], o, m, lg_ref, lb_ref, rw_ref, x1_ref, xm_ref, aff_ref)

    _software_pipeline(SUB_TILES, project, finish)


def _sgu_kernel(x_ref, mod_ref, win_ref, bin_ref, sg_ref, sb_ref, spw_ref, spb_ref, w_ref, lg_ref, lb_ref, rw_ref,
                x1_ref, xm_ref, aff_ref, v_s, gated_s):
    m = mod_ref[...]

    def project(rows):
        h = x_ref[rows, :] * (1.0 + m[1:2]) + m[0:1]
        return _dot(h.astype(BF16), win_ref[...])

    def gate(rows, p):
        p = _gelu_tanh(p + bin_ref[...])
        v_s[rows, :] = _layer_norm(p[:, SGU_WIDTH:], sg_ref[...], sb_ref[...]).astype(v_s.dtype)
        for c in range(rows.start, rows.stop, CHUNK):
            local = slice(c - rows.start, c - rows.start + CHUNK)
            for g in range(SGU_GROUPS):
                cols = slice(g * SGU_GROUP_W, (g + 1) * SGU_GROUP_W)
                mixed = _dot(spw_ref[g], v_s[c:c + CHUNK, cols]) + spb_ref[:, cols]
                gated_s[c:c + CHUNK, cols] = (p[local, cols] * mixed).astype(gated_s.dtype)
        return _dot(gated_s[rows, :], w_ref[...])

    def finish(rows, o):
        _residual_router(rows, x_ref[rows, :], o, m, lg_ref, lb_ref, rw_ref, x1_ref, xm_ref, aff_ref)

    _software_pipeline(SGU_TILES, project, gate, finish)


def _mixer_out(kernel, name, acts, x, mod_l, weights, ln_g, ln_b, router_w, group0, rows_per_group,
               scratch=()):
    n = x.shape[0]
    row = lambda width: pl.BlockSpec((ROW_TILE, width), lambda i: (i, 0))
    full = lambda a: pl.BlockSpec(a.shape, lambda i: (0,) * a.ndim)
    return pl.pallas_call(
        kernel,
        grid=(n // ROW_TILE,),
        in_specs=([row(a.shape[1]) for a in acts]
                  + [row(D_MODEL), pl.BlockSpec((None, 6, D_MODEL), _group_map(group0, rows_per_group))]
                  + [full(w) for w in weights] + [full(ln_g), full(ln_b), full(router_w)]),
        out_specs=[row(D_MODEL), row(D_MODEL), row(N_EXPERTS)],
        out_shape=[jax.ShapeDtypeStruct((n, D_MODEL), F32),
                   jax.ShapeDtypeStruct((n, D_MODEL), BF16),
                   jax.ShapeDtypeStruct((n, N_EXPERTS), F32)],
        scratch_shapes=list(scratch),
        compiler_params=_params(1),
        name=name,
    )(*acts, x, mod_l, *weights, ln_g, ln_b, router_w)


def _route_kernel(*refs, n_req, tokens, cap, place_in_kernel):
    n_blk = tokens // PREFIX_BLOCK
    if place_in_kernel:
        aff_ref, tri_ref, place_ref, posm_ref, posr_ref, affr_ref, blk_ref = refs
        n_col = place_ref.shape[2]
        aff = jnp.zeros((tokens, n_col), F32)
        for b in range(n_req):
            piece = aff_ref[b * tokens:(b + 1) * tokens, :]
            hi = piece.astype(BF16)
            rem = piece - hi.astype(F32)
            mid = rem.astype(BF16)
            low = (rem - mid.astype(F32)).astype(BF16)
            place = place_ref[b]
            aff = aff + ((_dot(hi, place) + _dot(mid, place)) + _dot(low, place))
    else:
        aff_ref, tri_ref, posm_ref, posr_ref, affr_ref, blk_ref = refs
        aff = aff_ref[...]
        n_col = aff.shape[1]

    def bisect(_, lo_hi):
        lo, hi = lo_hi
        mid = lo + ((hi - lo) >> 1)
        cnt = jnp.sum(jnp.where(aff >= pltpu.bitcast(mid, F32), 1.0, 0.0), axis=0, keepdims=True)
        ge = cnt >= cap
        return jnp.where(ge, mid, lo), jnp.where(ge, hi, mid)

    lo0 = jnp.zeros((1, n_col), I32)
    hi0 = jnp.full((1, n_col), 0x7F800000, I32)
    thr_bits, _ = lax.fori_loop(0, 31, bisect, (lo0, hi0))
    thr = pltpu.bitcast(thr_bits, F32)
    above = pltpu.bitcast(thr_bits + 1, F32)

    tri = tri_ref[...]

    def excl_prefix(x01):
        outs, before, inside = [], [], []
        off = jnp.zeros((1, n_col), F32)
        for blk in range(n_blk):
            xb = x01[blk * PREFIX_BLOCK:(blk + 1) * PREFIX_BLOCK, :]
            outs.append(_dot(tri, xb.astype(BF16)) + off)
            cnt = jnp.sum(xb, axis=0, keepdims=True)
            before.append(off)
            inside.append(cnt)
            off = off + cnt
        return (outs[0] if n_blk == 1 else jnp.concatenate(outs, axis=0)), before + inside

    gt = jnp.where(aff >= above, 1.0, 0.0)
    eq = jnp.where(aff >= thr, 1.0, 0.0) - gt
    need = cap - jnp.sum(gt, axis=0, keepdims=True)
    eq_rank, _ = excl_prefix(eq)
    sel = gt + jnp.where(eq_rank < need, eq, 0.0)
    pos, blk_rows = excl_prefix(sel)
    posm = jnp.where(sel > 0.0, pos, -1.0)
    posr_ref[...] = posm.T
    affr_ref[...] = aff.T
    for i, row in enumerate(blk_rows):
        blk_ref[i:i + 1, :] = row
    if place_in_kernel:
        posm_bf16 = posm.astype(BF16)
        for b in range(n_req):
            posm_ref[b * tokens:(b + 1) * tokens, :] = _dot_nt(posm_bf16, place_ref[b])
    else:
        posm_ref[...] = posm


PLACE_IN_KERNEL_MAX_REQ = 8


def _route_requests(aff, tri, n_req, tokens, cap):
    n_real = n_req * N_EXPERTS
    n_col = -(-n_real // LANES) * LANES
    n_blk = tokens // PREFIX_BLOCK
    place_in_kernel = n_req <= PLACE_IN_KERNEL_MAX_REQ
    whole = lambda shape: pl.BlockSpec(shape, lambda i: (0,) * len(shape))
    if place_in_kernel:
        col = np.arange(n_col)
        place = jnp.asarray(col[None, None, :] == (np.arange(n_req)[:, None, None] * N_EXPERTS
                                                   + np.arange(N_EXPERTS)[None, :, None]), BF16)
        args = (aff, tri, place)
        token_major = (n_req * tokens, N_EXPERTS)
        in_specs = [whole(token_major), whole((PREFIX_BLOCK, PREFIX_BLOCK)), whole(place.shape)]
    else:
        aff_t = aff.reshape(n_req, tokens, N_EXPERTS).transpose(1, 0, 2).reshape(tokens, n_real)
        args = (jnp.pad(aff_t, ((0, 0), (0, n_col - n_real))), tri)
        token_major = (tokens, n_col)
        in_specs = [whole(token_major), whole((PREFIX_BLOCK, PREFIX_BLOCK))]
    posm, posr, affr, blk = pl.pallas_call(
        functools.partial(_route_kernel, n_req=n_req, tokens=tokens, cap=cap, place_in_kernel=place_in_kernel),
        grid=(1,),
        in_specs=in_specs,
        out_specs=[whole(token_major), whole((n_col, tokens)), whole((n_col, tokens)), whole((2 * n_blk, n_col))],
        out_shape=[jax.ShapeDtypeStruct(token_major, F32),
                   jax.ShapeDtypeStruct((n_col, tokens), F32),
                   jax.ShapeDtypeStruct((n_col, tokens), F32),
                   jax.ShapeDtypeStruct((2 * n_blk, n_col), F32)],
        compiler_params=_params(1),
        name="route_%d" % tokens,
    )(*args)
    if not place_in_kernel:
        posm = posm[:, :n_real].reshape(tokens, n_req, N_EXPERTS).transpose(1, 0, 2).reshape(n_req * tokens, N_EXPERTS)
    blk = blk[:, :n_real].astype(I32).reshape(2, n_blk, n_req, N_EXPERTS).transpose(0, 2, 1, 3)
    expert_major = lambda a: a.reshape(n_col // N_EXPERTS, N_EXPERTS, tokens)
    return posm, expert_major(posr), expert_major(affr), blk[0], blk[1]


def _slot_gate(hit, aff_row):
    g = jnp.sum(jnp.where(hit, aff_row, 0.0), axis=-1, keepdims=True)
    return jnp.broadcast_to(g, (hit.shape[0], LANES))


def _gather_prompt_kernel(x_ref, posr_ref, affr_ref, xg_ref, g_ref):
    slot = lax.broadcasted_iota(I32, (CAP_P, SEQ), 0).astype(F32)
    for r in range(REQ_TILE):
        tokens = slice(r * SEQ, (r + 1) * SEQ)
        slots = slice(r * CAP_P, (r + 1) * CAP_P)
        posr = posr_ref[r]
        affr = affr_ref[r]
        hits = [posr[e:e + 1, :] == slot for e in range(N_EXPERTS)]
        onehot = jnp.concatenate([jnp.where(h, 1.0, 0.0) for h in hits], axis=0).astype(BF16)
        xg = _dot(onehot, x_ref[tokens, :])
        for e in range(N_EXPERTS):
            xg_ref[e, slots, :] = xg[e * CAP_P:(e + 1) * CAP_P, :].astype(xg_ref.dtype)
            g_ref[e, slots, :] = _slot_gate(hits[e], affr[e:e + 1, :])


def _gather_prompt(xm, posr, affr):
    return pl.pallas_call(
        _gather_prompt_kernel,
        grid=(BATCH // REQ_TILE,),
        in_specs=[pl.BlockSpec((REQ_TILE * SEQ, D_MODEL), lambda b: (b, 0)),
                  pl.BlockSpec((REQ_TILE, N_EXPERTS, SEQ), lambda b: (b, 0, 0)),
                  pl.BlockSpec((REQ_TILE, N_EXPERTS, SEQ), lambda b: (b, 0, 0))],
        out_specs=[pl.BlockSpec((N_EXPERTS, REQ_TILE * CAP_P, D_MODEL), lambda b: (0, b, 0)),
                   pl.BlockSpec((N_EXPERTS, REQ_TILE * CAP_P, LANES), lambda b: (0, b, 0))],
        out_shape=[jax.ShapeDtypeStruct((N_EXPERTS, SLOTS_P, D_MODEL), BF16),
                   jax.ShapeDtypeStruct((N_EXPERTS, SLOTS_P, LANES), F32)],
        compiler_params=_params(1),
        name="gather_prompt",
    )(xm, posr, affr)


TOKEN_BLOCK = PREFIX_BLOCK
N_TOKEN_BLOCKS = DEC_SEQ // TOKEN_BLOCK
SLOT_ALIGN = 16
WINDOW_SLOTS = 64
LAST_WINDOW = CAP_S - WINDOW_SLOTS
ALL_WINDOWS = N_EXPERTS * WINDOW_SLOTS


def _window_plan(before, inside):
    start = (before // SLOT_ALIGN) * SLOT_ALIGN
    span = before - start + inside
    passes = jnp.where(inside > 0, (span + WINDOW_SLOTS - 1) // WINDOW_SLOTS, 0)
    return start.reshape(-1), jnp.max(passes, axis=-1).reshape(-1)


def _window_bounds(start_ref, step, expert, k):
    lower = start_ref[step * N_EXPERTS + expert] + WINDOW_SLOTS * k
    begin = pl.multiple_of(jnp.minimum(lower, LAST_WINDOW), SLOT_ALIGN)
    return lower, begin


BLOCKS_PER_STEP = 2
STEP_TOKENS = BLOCKS_PER_STEP * TOKEN_BLOCK
STEPS_PER_REQ = N_TOKEN_BLOCKS // BLOCKS_PER_STEP


def _gather_sample_kernel(start_ref, passes_ref, x_ref, posr_ref, affr_ref, xg_ref, g_ref):
    t = pl.program_id(1)
    first_block = (pl.program_id(0) * STEPS_PER_REQ + t) * BLOCKS_PER_STEP

    @pl.when(t == 0)
    def _():
        xg_ref[...] = jnp.zeros_like(xg_ref)
        g_ref[...] = jnp.zeros_like(g_ref)

    row = lax.broadcasted_iota(I32, (WINDOW_SLOTS, TOKEN_BLOCK), 0).astype(F32)

    def pick(j, k):
        tokens = slice(j * TOKEN_BLOCK, (j + 1) * TOKEN_BLOCK)
        hits, begins = [], []
        for e in range(N_EXPERTS):
            lower, begin = _window_bounds(start_ref, first_block + j, e, k)
            pe = posr_ref[e:e + 1, tokens]
            hits.append((pe - begin.astype(F32) == row) & (pe >= lower.astype(F32)))
            begins.append(begin)
        onehot = jnp.concatenate([jnp.where(h, 1.0, 0.0) for h in hits], axis=0).astype(BF16)
        return _dot(onehot, x_ref[tokens, :]), hits, begins

    def place(j, picked):
        rows_x, hits, begins = picked
        tokens = slice(j * TOKEN_BLOCK, (j + 1) * TOKEN_BLOCK)
        for e in range(N_EXPERTS):
            win = pl.ds(begins[e], WINDOW_SLOTS)
            xg_ref[e, win, :] += rows_x[e * WINDOW_SLOTS:(e + 1) * WINDOW_SLOTS, :].astype(xg_ref.dtype)
            g_ref[e, win, :] += _slot_gate(hits[e], affr_ref[e:e + 1, tokens])

    _software_pipeline(range(BLOCKS_PER_STEP), lambda j: pick(j, 0), place)

    for j in range(BLOCKS_PER_STEP):
        def extra_pass(k, carry, j=j):
            place(j, pick(j, k))
            return carry

        lax.fori_loop(1, passes_ref[first_block + j], extra_pass, 0)


def _gather_sample(xm, posr, affr, starts, passes):
    expert_rows = pl.BlockSpec((None, N_EXPERTS, STEP_TOKENS), lambda b, t, s, p: (b, 0, t))
    grid_spec = pltpu.PrefetchScalarGridSpec(
        num_scalar_prefetch=2,
        grid=(DEC_BATCH, STEPS_PER_REQ),
        in_specs=[pl.BlockSpec((STEP_TOKENS, D_MODEL), lambda b, t, s, p: (b * STEPS_PER_REQ + t, 0)),
                  expert_rows, expert_rows],
        out_specs=[pl.BlockSpec((N_EXPERTS, CAP_S, D_MODEL), lambda b, t, s, p: (0, b, 0)),
                   pl.BlockSpec((N_EXPERTS, CAP_S, LANES), lambda b, t, s, p: (0, b, 0))],
    )
    return pl.pallas_call(
        _gather_sample_kernel,
        grid_spec=grid_spec,
        out_shape=[jax.ShapeDtypeStruct((N_EXPERTS, SLOTS_S, D_MODEL), BF16),
                   jax.ShapeDtypeStruct((N_EXPERTS, SLOTS_S, LANES), F32)],
        compiler_params=_params(2),
        name="gather_sample",
    )(starts, passes, xm, posr, affr)


FF_TILE = 512
SLOT_CHUNK = 1024


def _ffn_kernel(xp_ref, xs_ref, gp_ref, gs_ref, w1_ref, w3_ref, w2_ref, yp_ref, ys_ref, acc_ref):
    f = pl.program_id(1)
    n_f = pl.num_programs(1)
    @pl.when(f == 0)
    def _():
        acc_ref[...] = jnp.zeros_like(acc_ref)

    cast = {}

    def weight(ref):
        if id(ref) not in cast:
            cast[id(ref)] = ref[...].astype(BF16)
        return cast[id(ref)]

    def up_project(chunk):
        x_ref, c, _ = chunk
        x = x_ref[c * SLOT_CHUNK:(c + 1) * SLOT_CHUNK, :]
        h1 = _dot(x, weight(w1_ref))
        return h1, _dot(x, weight(w3_ref))

    def down_project(chunk, h):
        _, c, base = chunk
        h1, h3 = h
        hid = (h1 * jax.nn.sigmoid(h1) * h3).astype(BF16)
        rows = slice(base + c * SLOT_CHUNK, base + (c + 1) * SLOT_CHUNK)
        acc_ref[rows, :] += _dot(hid, weight(w2_ref))

    chunks = [(x_ref, c, base) for x_ref, n_slot, base in ((xp_ref, SLOTS_P, 0), (xs_ref, SLOTS_S, SLOTS_P))
              for c in range(n_slot // SLOT_CHUNK)]
    _software_pipeline(chunks, up_project, down_project)

    @pl.when(f == n_f - 1)
    def _():
        for y_ref, g_ref, n_slot, base in ((yp_ref, gp_ref, SLOTS_P, 0), (ys_ref, gs_ref, SLOTS_S, SLOTS_P)):
            for c in range(n_slot // SLOT_CHUNK):
                rows = slice(c * SLOT_CHUNK, (c + 1) * SLOT_CHUNK)
                gate = jnp.concatenate([g_ref[rows, :]] * (D_MODEL // LANES), axis=1)
                y_ref[rows, :] = (acc_ref[base + c * SLOT_CHUNK:base + (c + 1) * SLOT_CHUNK, :]
                                  * gate).astype(y_ref.dtype)


def _expert_ffn(xgp, xgs, gp, gs, w1, w3, w2, layer):
    slots = lambda n, width: pl.BlockSpec((None, n, width), lambda e, f: (e, 0, 0))
    return pl.pallas_call(
        _ffn_kernel,
        grid=(N_EXPERTS, EXPERT_FF // FF_TILE),
        in_specs=[slots(SLOTS_P, D_MODEL), slots(SLOTS_S, D_MODEL), slots(SLOTS_P, LANES), slots(SLOTS_S, LANES),
                  pl.BlockSpec((None, None, D_MODEL, FF_TILE), lambda e, f: (layer, e, 0, f)),
                  pl.BlockSpec((None, None, D_MODEL, FF_TILE), lambda e, f: (layer, e, 0, f)),
                  pl.BlockSpec((None, None, FF_TILE, D_MODEL), lambda e, f: (layer, e, f, 0))],
        out_specs=[slots(SLOTS_P, D_MODEL), slots(SLOTS_S, D_MODEL)],
        out_shape=[jax.ShapeDtypeStruct((N_EXPERTS, SLOTS_P, D_MODEL), BF16),
                   jax.ShapeDtypeStruct((N_EXPERTS, SLOTS_S, D_MODEL), BF16)],
        scratch_shapes=[pltpu.VMEM((SLOTS_P + SLOTS_S, D_MODEL), F32)],
        compiler_params=_params(2),
        name="expert_ffn",
    )(xgp, xgs, gp, gs, w1, w3, w2)


def _combine_prompt_kernel(posm_ref, y_ref, x_ref, mod_ref, exp_ref, lg_ref, lb_ref, o_ref):
    m = mod_ref[...]
    slot = (lax.broadcasted_iota(I32, (SEQ, N_EXPERTS * CAP_P), 1) & (CAP_P - 1)).astype(F32)
    def scatter(r):
        tokens = slice(r * SEQ, (r + 1) * SEQ)
        spread = _dot(posm_ref[tokens, :].astype(BF16), exp_ref[...])
        onehot = jnp.where(spread == slot, 1.0, 0.0).astype(BF16)
        y = jnp.concatenate([y_ref[e, r * CAP_P:(r + 1) * CAP_P, :] for e in range(N_EXPERTS)], axis=0)
        return _dot(onehot, y)

    def finish(r, f):
        tokens = slice(r * SEQ, (r + 1) * SEQ)
        o_ref[tokens, :] = _layer_norm(ALPHA * x_ref[tokens, :] + m[5:6] * f, lg_ref[...], lb_ref[...])

    _software_pipeline(range(REQ_TILE), scatter, finish)


def _combine_prompt(posm, y, x1, mod_l, expand, ln_g, ln_b):
    full = lambda a: pl.BlockSpec(a.shape, lambda b: (0,) * a.ndim)
    return pl.pallas_call(
        _combine_prompt_kernel,
        grid=(BATCH // REQ_TILE,),
        in_specs=[pl.BlockSpec((REQ_TILE * SEQ, N_EXPERTS), lambda b: (b, 0)),
                  pl.BlockSpec((N_EXPERTS, REQ_TILE * CAP_P, D_MODEL), lambda b: (0, b, 0)),
                  pl.BlockSpec((REQ_TILE * SEQ, D_MODEL), lambda b: (b, 0)),
                  pl.BlockSpec((None, 6, D_MODEL), lambda b: (0, 0, 0)),
                  full(expand), full(ln_g), full(ln_b)],
        out_specs=pl.BlockSpec((REQ_TILE * SEQ, D_MODEL), lambda b: (b, 0)),
        out_shape=jax.ShapeDtypeStruct((N_PROMPT, D_MODEL), F32),
        compiler_params=_params(1),
        name="combine_prompt",
    )(posm, y, x1, mod_l, expand, ln_g, ln_b)


def _combine_sample_kernel(start_ref, passes_ref, posm_ref, y_ref, x_ref, mod_ref, exp_ref, off_ref,
                           lg_ref, lb_ref, o_ref):
    first_block = (pl.program_id(0) * STEPS_PER_REQ + pl.program_id(1)) * BLOCKS_PER_STEP
    exp_bf16 = exp_ref[...]
    offset = off_ref[...]
    m = mod_ref[...]

    def tokens_of(j):
        return slice(j * TOKEN_BLOCK, (j + 1) * TOKEN_BLOCK)

    def one_pass(j, k):
        spread = _dot(posm_ref[tokens_of(j), :].astype(BF16), exp_bf16)
        begin_row = jnp.zeros((1, ALL_WINDOWS), F32)
        lower_row = jnp.zeros((1, ALL_WINDOWS), F32)
        windows = []
        for e in range(N_EXPERTS):
            lower, begin = _window_bounds(start_ref, first_block + j, e, k)
            lanes_e = exp_bf16[e:e + 1, :].astype(F32)
            begin_row = begin_row + begin.astype(F32) * lanes_e
            lower_row = lower_row + lower.astype(F32) * lanes_e
            windows.append(y_ref[e, pl.ds(begin, WINDOW_SLOTS), :])
        hit = (spread - begin_row == offset) & (spread >= lower_row)
        onehot = jnp.where(hit, 1.0, 0.0).astype(BF16)
        return _dot(onehot, jnp.concatenate(windows, axis=0))

    def finish(j, f):
        rows = tokens_of(j)
        o_ref[rows, :] = _layer_norm(ALPHA * x_ref[rows, :] + m[5:6] * f, lg_ref[...], lb_ref[...])

    _software_pipeline(range(BLOCKS_PER_STEP), lambda j: one_pass(j, 0), finish)

    for j in range(BLOCKS_PER_STEP):
        n_pass = passes_ref[first_block + j]

        @pl.when(n_pass > 1)
        def _(j=j, n_pass=n_pass):
            f = lax.fori_loop(1, n_pass, lambda k, f: f + one_pass(j, k), one_pass(j, 0))
            finish(j, f)


def _combine_sample(posm, y, x1, mod_l, ln_g, ln_b, starts, passes):
    lane = np.arange(ALL_WINDOWS)
    expand = jnp.asarray((lane[None, :] // WINDOW_SLOTS) == np.arange(N_EXPERTS)[:, None], BF16)
    offset = jnp.asarray((lane % WINDOW_SLOTS).reshape(1, ALL_WINDOWS), F32)
    row = lambda width: pl.BlockSpec((STEP_TOKENS, width), lambda b, t, s, p: (b * STEPS_PER_REQ + t, 0))
    full = lambda a: pl.BlockSpec(a.shape, lambda b, t, s, p: (0,) * a.ndim)
    grid_spec = pltpu.PrefetchScalarGridSpec(
        num_scalar_prefetch=2,
        grid=(DEC_BATCH, STEPS_PER_REQ),
        in_specs=[row(N_EXPERTS),
                  pl.BlockSpec((N_EXPERTS, CAP_S, D_MODEL), lambda b, t, s, p: (0, b, 0)),
                  row(D_MODEL),
                  pl.BlockSpec((None, 6, D_MODEL), lambda b, t, s, p: (1 + b, 0, 0)),
                  full(expand), full(offset), full(ln_g), full(ln_b)],
        out_specs=row(D_MODEL),
    )
    return pl.pallas_call(
        _combine_sample_kernel,
        grid_spec=grid_spec,
        out_shape=jax.ShapeDtypeStruct((N_SAMPLE, D_MODEL), F32),
        compiler_params=_params(2),
        name="combine_sample",
    )(starts, passes, posm, y, x1, mod_l, expand, offset, ln_g, ln_b)


def _moe(prompt, sample, mod_l, w1, w3, w2, layer, ln_g, ln_b, tri, expand):
    x1_p, xm_p, aff_p = prompt
    x1_s, xm_s, aff_s = sample
    posm_p, posr_p, affr_p, _, _ = _route_requests(aff_p, tri, BATCH, SEQ, CAP_P)
    posm_s, posr_s, affr_s, before_s, inside_s = _route_requests(aff_s, tri, DEC_BATCH, DEC_SEQ, CAP_S)
    starts, passes = _window_plan(before_s, inside_s)
    xg_p, g_p = _gather_prompt(xm_p, posr_p, affr_p)
    xg_s, g_s = _gather_sample(xm_s, posr_s, affr_s, starts, passes)
    y_p, y_s = _expert_ffn(xg_p, xg_s, g_p, g_s, w1, w3, w2, layer)
    x2_p = _combine_prompt(posm_p, y_p, x1_p, mod_l, expand, ln_g, ln_b)
    x2_s = _combine_sample(posm_s, y_s, x1_s, mod_l, ln_g, ln_b, starts, passes)
    return x2_p, x2_s


def kernel(x_prompt, x_sample, cache_k, cache_v, state_rglru, c, c_ctx, mod_w, mod_b, ln_mix_g, ln_mix_b, ln_ffn_g, ln_ffn_b, ab_in_w, attn_sink, rnn_conv_w, rnn_conv_b, lru_wa, lru_ba, lru_wx, lru_bx, lru_lambda, ab_out_w, sgu_in_w, sgu_in_b, sgu_ln_g, sgu_ln_b, sgu_spatial_w, sgu_spatial_b, sgu_out_w, router_w, moe_w1, moe_w3, moe_w2):
    xp = x_prompt.reshape(N_PROMPT, D_MODEL)
    xs = x_sample.reshape(N_SAMPLE, D_MODEL)

    cvec = jnp.concatenate([c_ctx[None], c, jnp.zeros((SUBLANES - 1 - DEC_BATCH, D_MODEL), F32)], axis=0)
    mod = _modulation(cvec, mod_w, mod_b).reshape(DEPTH, SUBLANES, 6, D_MODEL)

    idx = np.arange(PREFIX_BLOCK)
    tri = jnp.asarray(idx[None, :] < idx[:, None], BF16)
    lane = np.arange(N_EXPERTS * CAP_P)
    expand = jnp.asarray((lane[None, :] // CAP_P) == np.arange(N_EXPERTS)[:, None], BF16)
    rope_tables = _rope_tables()
    vec = lambda a: a.reshape(1, -1)
    router_pad = lambda w: jnp.pad(w, ((0, 0), (0, LANES - N_EXPERTS))).astype(BF16)

    new_k = new_v = new_state = None
    for l in range(DEPTH):
        e = l // 2
        mod_l = mod[l]
        rw = router_pad(router_w[l])
        lg, lb = vec(ln_mix_g[l]), vec(ln_mix_b[l])
        if l % 2 == 0:
            w_in = ab_in_w[e].astype(BF16)
            w_out = ab_out_w[e].astype(BF16)
            wa = _block_diag_dense(lru_wa[e])
            wx = _block_diag_dense(lru_wx[e])
            rnn_w = (rnn_conv_w[e], vec(rnn_conv_b[e]), wa, lru_ba[e], wx, lru_bx[e], lru_lambda[e])

            q, k, v, xr, xg, new_k, new_v = _ab_in(xp, mod_l, w_in, 0, N_PROMPT, None)
            att = _ctx_attention(q, k, v, attn_sink[e])
            rows = 8 * SEQ
            zeros = jnp.zeros((N_PROMPT // rows, rows // SEQ, RNN_WIDTH), F32)
            rnn, hf_last, hb_first = _rglru(xr, xg, *rnn_w, zeros, zeros, rows, SEQ)
            new_state = jnp.stack([hf_last.reshape(BATCH, RNN_WIDTH), hb_first.reshape(BATCH, RNN_WIDTH)], axis=1)
            prompt = _mixer_out(_mix_out_kernel, "mix_out", (att, rnn), xp, mod_l, (w_out,),
                                lg, lb, rw, 0, N_PROMPT)

            q, k, v, xr, xg = _ab_in(xs, mod_l, w_in, 1, DEC_SEQ, rope_tables)
            att = _lat_attention(q, k, v, cache_k[:, e].reshape(DEC_BATCH * PAST_LEN, KV_WIDTH),
                                 cache_v[:, e].reshape(DEC_BATCH * PAST_LEN, KV_WIDTH), attn_sink[e])
            h0 = state_rglru[:, e]
            rnn, _, _ = _rglru(xr, xg, *rnn_w, h0[:, 0][:, None, :], h0[:, 1][:, None, :], DEC_SEQ, DEC_SEQ)
            sample = _mixer_out(_mix_out_kernel, "mix_out", (att, rnn), xs, mod_l, (w_out,),
                                lg, lb, rw, 1, DEC_SEQ)
        else:
            w_in = sgu_in_w[e].astype(BF16)
            w_out = sgu_out_w[e].astype(BF16)
            spw = sgu_spatial_w[e].astype(BF16)
            spb = jnp.repeat(sgu_spatial_b[e].T, SGU_GROUP_W, axis=1)
            sgu_w = (w_in, vec(sgu_in_b[e]), vec(sgu_ln_g[e]), vec(sgu_ln_b[e]), spw, spb, w_out)
            scratch = (pltpu.VMEM((ROW_TILE, SGU_WIDTH), BF16), pltpu.VMEM((ROW_TILE, SGU_WIDTH), BF16))
            prompt = _mixer_out(_sgu_kernel, "sgu", (), xp, mod_l, sgu_w, lg, lb, rw, 0, N_PROMPT, scratch)
            sample = _mixer_out(_sgu_kernel, "sgu", (), xs, mod_l, sgu_w, lg, lb, rw, 1, DEC_SEQ, scratch)

        xp, xs = _moe(prompt, sample, mod_l, moe_w1, moe_w3, moe_w2, l,
                      vec(ln_ffn_g[l]), vec(ln_ffn_b[l]), tri, expand)

    return (xp.reshape(BATCH, SEQ, D_MODEL),
            xs.reshape(DEC_BATCH, DEC_SEQ, D_MODEL),
            new_k,
            new_v,
            new_state.reshape(BATCH, 1, 2, RNN_WIDTH))
```

```python
import functools

import numpy as np
import jax
import jax.numpy as jnp
from jax import lax
from jax.experimental import pallas as pl
from jax.experimental.pallas import tpu as pltpu

F32 = jnp.float32
BF16 = jnp.bfloat16
I32 = jnp.int32

D_MODEL = 1024
BATCH = 32
SEQ = 256
DEPTH = 2
DEC_BATCH = 4
DEC_SEQ = 2048
PAST_LEN = 256
GRID_W = 64
HEAD_DIM = 128
N_Q_HEADS = 4
N_KV_HEADS = 2
Q_PER_KV = N_Q_HEADS // N_KV_HEADS
ATTN_WIDTH = N_Q_HEADS * HEAD_DIM
KV_WIDTH = N_KV_HEADS * HEAD_DIM
WINDOW = 128
BLOCK = 128
ATTN_SCALE = HEAD_DIM ** -0.5
ROPE_BASE = 10000.0
NEG_INF = -1e30
RNN_WIDTH = 512
RNN_BLOCKS = 8
RNN_BLOCK_W = RNN_WIDTH // RNN_BLOCKS
CONV_W = 4
CONV_PAD_LEFT = 2
LRU_C = 8.0
AB_IN_WIDTH = ATTN_WIDTH + 2 * KV_WIDTH + 2 * RNN_WIDTH
CHUNK = 128
SGU_WIDTH = D_MODEL
SGU_GROUPS = 8
SGU_GROUP_W = SGU_WIDTH // SGU_GROUPS
N_EXPERTS = 16
EXPERT_FF = 2048
EC_FACTOR = 2
ALPHA = (2 * DEPTH) ** 0.25
LN_EPS = 1e-6

N_PROMPT = BATCH * SEQ
N_SAMPLE = DEC_BATCH * DEC_SEQ
CAP_P = EC_FACTOR * SEQ // N_EXPERTS
CAP_S = EC_FACTOR * DEC_SEQ // N_EXPERTS
SLOTS_P = BATCH * CAP_P
SLOTS_S = DEC_BATCH * CAP_S

LANES = 128
SUBLANES = 8
ROW_TILE = 1024
REQ_TILE = 4
PREFIX_BLOCK = 256
VMEM_LIMIT = 56 * 1024 * 1024


def _params(n_axes=1):
    return pltpu.CompilerParams(dimension_semantics=("arbitrary",) * n_axes,
                                vmem_limit_bytes=VMEM_LIMIT)


def _layer_norm(x, g, b):
    mu = jnp.mean(x, axis=-1, keepdims=True)
    xc = x - mu
    var = jnp.mean(xc * xc, axis=-1, keepdims=True)
    return xc * lax.rsqrt(var + LN_EPS) * g + b


_LOG2_E = 1.4426950408889634
_GELU_K0 = -2.0 * 0.7978845608028654 * _LOG2_E
_GELU_K1 = _GELU_K0 * 0.044715


def _gelu_tanh(x):
    return x / (1.0 + jnp.exp2(x * (_GELU_K0 + _GELU_K1 * (x * x))))


def _sigmoid(z):
    return 0.5 + 0.5 * jnp.tanh(0.5 * z)


def _dot(a, b):
    return jnp.dot(a, b, preferred_element_type=F32)


def _dot_nt(a, b):
    return lax.dot_general(a, b, (((1,), (1,)), ((), ())), preferred_element_type=F32)


def _software_pipeline(items, first_stage, *later_stages):
    items = list(items)
    stages = (first_stage,) + later_stages
    carried = {}
    for t in range(len(items) + len(stages) - 1):
        for s, stage in enumerate(stages):
            i = t - s
            if 0 <= i < len(items):
                carried[i] = stage(items[i]) if s == 0 else stage(items[i], carried[i])


def _mod_kernel(c_ref, w_ref, b_ref, o_ref):
    c = c_ref[...]
    s = c * jax.nn.sigmoid(c)
    o_ref[...] = _dot(s.astype(BF16), w_ref[...].astype(BF16)) + b_ref[...]


def _modulation(cvec8, mod_w, mod_b):
    n_col = 6 * D_MODEL // D_MODEL
    return pl.pallas_call(
        _mod_kernel,
        grid=(DEPTH, n_col),
        in_specs=[pl.BlockSpec((SUBLANES, D_MODEL), lambda l, j: (0, 0)),
                  pl.BlockSpec((None, D_MODEL, D_MODEL), lambda l, j: (l, 0, j)),
                  pl.BlockSpec((None, 1, D_MODEL), lambda l, j: (l, 0, j))],
        out_specs=pl.BlockSpec((None, SUBLANES, D_MODEL), lambda l, j: (l, 0, j)),
        out_shape=jax.ShapeDtypeStruct((DEPTH, SUBLANES, 6 * D_MODEL), F32),
        compiler_params=_params(2),
        name="adaln_modulation",
    )(cvec8, mod_w, mod_b.reshape(DEPTH, 1, 6 * D_MODEL))


def _group_map(group0, rows_per_group):
    tiles_per_group = rows_per_group // ROW_TILE
    return lambda i: (group0 + i // tiles_per_group, 0, 0)


def _rope(t, cos, sin_signed):
    lane = lax.broadcasted_iota(I32, t.shape, 1)
    swapped = jnp.where((lane & 1) == 0, pltpu.roll(t, HEAD_DIM - 1, 1), pltpu.roll(t, 1, 1))
    return t * cos + swapped * sin_signed


def _ab_in_kernel(*refs, rope):
    if rope:
        x_ref, mod_ref, w_ref, cos_ref, sin_ref, q_ref, k_ref, v_ref, xr_ref, xg_ref = refs
    else:
        x_ref, mod_ref, w_ref, q_ref, k_ref, v_ref, xr_ref, xg_ref, ck_ref, cv_ref = refs
    m = mod_ref[...]
    h = x_ref[...] * (1.0 + m[1:2]) + m[0:1]
    p = _dot(h.astype(BF16), w_ref[...])
    q = p[:, :ATTN_WIDTH] * (ATTN_SCALE * _LOG2_E)
    k = p[:, ATTN_WIDTH:ATTN_WIDTH + KV_WIDTH]
    v = p[:, ATTN_WIDTH + KV_WIDTH:ATTN_WIDTH + 2 * KV_WIDTH]
    if rope:
        cos = cos_ref[...]
        sin = sin_ref[...]
        q = jnp.concatenate([_rope(q[:, i * HEAD_DIM:(i + 1) * HEAD_DIM], cos, sin)
                             for i in range(N_Q_HEADS)], axis=1)
        k = jnp.concatenate([_rope(k[:, i * HEAD_DIM:(i + 1) * HEAD_DIM], cos, sin)
                             for i in range(N_KV_HEADS)], axis=1)
    q_ref[...] = q.astype(q_ref.dtype)
    k_ref[...] = k.astype(k_ref.dtype)
    v_ref[...] = v.astype(v_ref.dtype)
    xr_ref[...] = p[:, ATTN_WIDTH + 2 * KV_WIDTH:ATTN_WIDTH + 2 * KV_WIDTH + RNN_WIDTH]
    xg_ref[...] = p[:, ATTN_WIDTH + 2 * KV_WIDTH + RNN_WIDTH:]
    if not rope:
        for r in range(ROW_TILE // SEQ):
            for i in range(N_KV_HEADS):
                ck_ref[r, 0, :, i, :] = k[r * SEQ:(r + 1) * SEQ, _head_cols(i)]
                cv_ref[r, 0, :, i, :] = v[r * SEQ:(r + 1) * SEQ, _head_cols(i)]


def _ab_in(x, mod_l, w_bf16, group0, rows_per_group, rope_tables):
    n = x.shape[0]
    rope = rope_tables is not None
    row = lambda width: pl.BlockSpec((ROW_TILE, width), lambda i: (i, 0))
    in_specs = [row(D_MODEL),
                pl.BlockSpec((None, 6, D_MODEL), _group_map(group0, rows_per_group)),
                pl.BlockSpec((D_MODEL, AB_IN_WIDTH), lambda i: (0, 0))]
    args = [x, mod_l, w_bf16]
    out_specs = [row(ATTN_WIDTH), row(KV_WIDTH), row(KV_WIDTH), row(RNN_WIDTH), row(RNN_WIDTH)]
    out_shape = [jax.ShapeDtypeStruct((n, ATTN_WIDTH), BF16),
                 jax.ShapeDtypeStruct((n, KV_WIDTH), BF16),
                 jax.ShapeDtypeStruct((n, KV_WIDTH), BF16),
                 jax.ShapeDtypeStruct((n, RNN_WIDTH), F32),
                 jax.ShapeDtypeStruct((n, RNN_WIDTH), F32)]
    if rope:
        tiles_per_seq = DEC_SEQ // ROW_TILE
        in_specs += [pl.BlockSpec((ROW_TILE, HEAD_DIM), lambda i: (i % tiles_per_seq, 0))] * 2
        args += list(rope_tables)
    else:
        req = ROW_TILE // SEQ
        cache = pl.BlockSpec((req, 1, SEQ, N_KV_HEADS, HEAD_DIM), lambda i: (i, 0, 0, 0, 0))
        out_specs += [cache, cache]
        out_shape += [jax.ShapeDtypeStruct((n // SEQ, 1, SEQ, N_KV_HEADS, HEAD_DIM), F32)] * 2
    return pl.pallas_call(
        functools.partial(_ab_in_kernel, rope=rope),
        grid=(n // ROW_TILE,),
        in_specs=in_specs,
        out_specs=out_specs,
        out_shape=out_shape,
        compiler_params=_params(1),
        name="ab_in_rope" if rope else "ab_in",
    )(*args)


def _rope_tables():
    rows = DEC_SEQ // GRID_W
    row = np.repeat(np.arange(rows, dtype=np.float32), GRID_W)
    col = np.tile(np.arange(GRID_W, dtype=np.float32), rows)
    n_freq = HEAD_DIM // 4
    freqs = np.float32(ROPE_BASE) ** (-np.arange(n_freq, dtype=np.float32) / np.float32(n_freq))
    ang = np.concatenate([row[:, None] * freqs, col[:, None] * freqs], axis=-1)
    cos = np.repeat(np.cos(ang), 2, axis=-1)
    sin = np.repeat(np.sin(ang), 2, axis=-1)
    sign = np.tile(np.array([-1.0, 1.0], np.float32), HEAD_DIM // 2)
    return jnp.asarray(cos, F32), jnp.asarray(sin * sign, F32)


def _sink_attention_head(s_list, v_list, sink):
    sink = sink * _LOG2_E
    m = sink
    for s in s_list:
        m = jnp.maximum(m, jnp.max(s, axis=-1, keepdims=True))
    p_list = [jnp.exp2(s - m) for s in s_list]
    denom = jnp.exp2(sink - m)
    for p in p_list:
        denom = denom + jnp.sum(p, axis=-1, keepdims=True)
    out = None
    for p, v in zip(p_list, v_list):
        o = _dot(p.astype(BF16), v)
        out = o if out is None else out + o
    return out * (1.0 / denom)


def _head_cols(h):
    return slice(h * HEAD_DIM, (h + 1) * HEAD_DIM)


def _ctx_attn_kernel(sink_ref, q_ref, k_ref, v_ref, o_ref):
    def scores(item):
        r, h = item
        rows = slice(r * SEQ, (r + 1) * SEQ)
        kh = k_ref[rows, _head_cols(h // Q_PER_KV)].astype(BF16)
        return _dot_nt(q_ref[rows, _head_cols(h)], kh)

    def finish(item, s):
        r, h = item
        rows = slice(r * SEQ, (r + 1) * SEQ)
        vh = v_ref[rows, _head_cols(h // Q_PER_KV)].astype(BF16)
        o_ref[rows, _head_cols(h)] = _sink_attention_head([s], [vh], sink_ref[h]).astype(o_ref.dtype)

    items = [(r, h) for r in range(REQ_TILE) for h in range(N_Q_HEADS)]
    _software_pipeline(items, scores, finish)


def _ctx_attention(q, k, v, sink):
    seq = lambda width: pl.BlockSpec((REQ_TILE * SEQ, width), lambda b: (b, 0))
    return pl.pallas_call(
        _ctx_attn_kernel,
        grid=(BATCH // REQ_TILE,),
        in_specs=[pl.BlockSpec(memory_space=pltpu.SMEM), seq(ATTN_WIDTH), seq(KV_WIDTH), seq(KV_WIDTH)],
        out_specs=seq(ATTN_WIDTH),
        out_shape=jax.ShapeDtypeStruct((N_PROMPT, ATTN_WIDTH), BF16),
        compiler_params=_params(1),
        name="context_attention",
    )(sink, q, k, v)


LAT_Q = 256


def _lat_attn_kernel(sink_ref, q_ref, kp_ref, kc_ref, kn_ref, vp_ref, vc_ref, vn_ref,
                     kx_ref, vx_ref, o_ref):
    n = pl.program_id(1)
    nb = pl.num_programs(1)
    q = q_ref[...]
    kw = jnp.concatenate([kp_ref[...], kc_ref[...], kn_ref[...]], axis=0)
    vw = jnp.concatenate([vp_ref[...], vc_ref[...], vn_ref[...]], axis=0)
    kx = kx_ref[...].astype(BF16)
    vx = vx_ref[...].astype(BF16)
    n_key = LAT_Q + 2 * WINDOW
    qi = lax.broadcasted_iota(I32, (LAT_Q, n_key), 0)
    kj = lax.broadcasted_iota(I32, (LAT_Q, n_key), 1)
    rel = kj - qi
    valid = (rel >= 0) & (rel <= 2 * WINDOW)
    valid = valid & ((n > 0) | (kj >= WINDOW)) & ((n < nb - 1) | (kj < LAT_Q + WINDOW))
    def scores(h):
        sl = _head_cols(h // Q_PER_KV)
        qh = q[:, _head_cols(h)]
        s_ctx = _dot_nt(qh, kx[:, sl])
        s_win = jnp.where(valid, _dot_nt(qh, kw[:, sl]), NEG_INF)
        return s_ctx, s_win

    def finish(h, s):
        sl = _head_cols(h // Q_PER_KV)
        o_ref[:, _head_cols(h)] = _sink_attention_head(list(s), [vx[:, sl], vw[:, sl]],
                                                       sink_ref[h]).astype(o_ref.dtype)

    _software_pipeline(range(N_Q_HEADS), scores, finish)


def _lat_attention(q, k, v, k_ctx, v_ctx, sink):
    nb = DEC_SEQ // LAT_Q
    nw = DEC_SEQ // WINDOW
    per = LAT_Q // WINDOW
    cur = lambda b, n: (b * nb + n, 0)
    prev = lambda b, n: (b * nw + jnp.maximum(n * per - 1, 0), 0)
    nxt = lambda b, n: (b * nw + jnp.minimum((n + 1) * per, nw - 1), 0)
    tile = lambda width: pl.BlockSpec((LAT_Q, width), cur)
    edge = lambda imap: pl.BlockSpec((WINDOW, KV_WIDTH), imap)
    ctx = pl.BlockSpec((PAST_LEN, KV_WIDTH), lambda b, n: (b, 0))
    return pl.pallas_call(
        _lat_attn_kernel,
        grid=(DEC_BATCH, nb),
        in_specs=[pl.BlockSpec(memory_space=pltpu.SMEM), tile(ATTN_WIDTH),
                  edge(prev), tile(KV_WIDTH), edge(nxt),
                  edge(prev), tile(KV_WIDTH), edge(nxt), ctx, ctx],
        out_specs=tile(ATTN_WIDTH),
        out_shape=jax.ShapeDtypeStruct((N_SAMPLE, ATTN_WIDTH), BF16),
        compiler_params=_params(2),
        name="latent_attention",
    )(sink, q, k, k, k, v, v, v, k_ctx, v_ctx)


RNN_CHUNK = 256


def _rglru_kernel(xr_ref, xg_ref, cw_ref, cb_ref, wa_ref, ba_ref, wx_ref, bx_ref, lam_ref,
                  h0f_ref, h0b_ref, y_ref, hfl_ref, hbf_ref, xc_s, af_s, ab_s, uf_s, ub_s, *, rows, seq_len):
    n_seq = rows // seq_len
    n_chunk = rows // RNN_CHUNK
    cw = cw_ref[...]
    cb = cb_ref[...]
    zeros_halo = jnp.zeros((SUBLANES, RNN_WIDTH), F32)
    row8 = lax.broadcasted_iota(I32, (SUBLANES, RNN_WIDTH), 0)

    def conv_chunk(c):
        r0 = c * RNN_CHUNK
        first = r0 % seq_len == 0
        last = (r0 + RNN_CHUNK) % seq_len == 0
        before = zeros_halo if first else xr_ref[r0 - SUBLANES:r0, :]
        after = zeros_halo if last else xr_ref[r0 + RNN_CHUNK:r0 + RNN_CHUNK + SUBLANES, :]
        win = jnp.concatenate([before, xr_ref[r0:r0 + RNN_CHUNK, :], after], axis=0)
        xc = cb
        n_win = RNN_CHUNK + 2 * SUBLANES
        for i in range(CONV_W):
            shift = (CONV_PAD_LEFT - i) % n_win
            rolled = win if shift == 0 else pltpu.roll(win, shift, 0)
            xc = xc + rolled[SUBLANES:SUBLANES + RNN_CHUNK, :] * cw[i:i + 1, :]
        return xc

    def group_scan(a, u, reverse):
        for k in (1, 2, 4):
            if reverse:
                shift, ok = SUBLANES - k, row8 < SUBLANES - k
            else:
                shift, ok = k, row8 >= k
            a_nb = jnp.where(ok, pltpu.roll(a, shift, 0), 1.0)
            u_nb = jnp.where(ok, pltpu.roll(u, shift, 0), 0.0)
            u = a * u_nb + u
            a = a * a_nb
        return a, u

    for c in range(n_chunk):
        xc_s[c * RNN_CHUNK:(c + 1) * RNN_CHUNK, :] = conv_chunk(c)

    for d, (a_s, u_s) in enumerate(((af_s, uf_s), (ab_s, ub_s))):
        neg = -lam_ref[d:d + 1, :]
        softplus = jnp.maximum(neg, 0.0) + jnp.log1p(jnp.exp(-jnp.abs(neg)))
        decay = -LRU_C * softplus
        wa = wa_ref[d]
        wx = wx_ref[d]
        ba = ba_ref[d:d + 1, :]
        bx = bx_ref[d:d + 1, :]
        for c in range(n_chunk):
            xc = xc_s[c * RNN_CHUNK:(c + 1) * RNN_CHUNK, :]
            xcb = xc.astype(BF16)
            r = _sigmoid(_dot(xcb, wa) + ba)
            i = _sigmoid(_dot(xcb, wx) + bx)
            log_a = r * decay
            a = jnp.exp(log_a)
            a_s[c * RNN_CHUNK:(c + 1) * RNN_CHUNK, :] = a
            one_minus_a2 = -jnp.tanh(log_a) * (a * a + 1.0)
            u_s[c * RNN_CHUNK:(c + 1) * RNN_CHUNK, :] = jnp.sqrt(one_minus_a2) * (i * xc)

    n_group = seq_len // SUBLANES

    def body(g, carries):
        fwd, bwd = carries
        new_f, new_b = [], []
        for s in range(n_seq):
            rf = pl.multiple_of(s * seq_len + g * SUBLANES, SUBLANES)
            a, u = group_scan(af_s[pl.ds(rf, SUBLANES), :], uf_s[pl.ds(rf, SUBLANES), :], False)
            h = u + a * fwd[s]
            uf_s[pl.ds(rf, SUBLANES), :] = h
            new_f.append(h[SUBLANES - 1:SUBLANES, :])
            rb = pl.multiple_of(s * seq_len + (n_group - 1 - g) * SUBLANES, SUBLANES)
            a, u = group_scan(ab_s[pl.ds(rb, SUBLANES), :], ub_s[pl.ds(rb, SUBLANES), :], True)
            h = u + a * bwd[s]
            ub_s[pl.ds(rb, SUBLANES), :] = h
            new_b.append(h[0:1, :])
        return tuple(new_f), tuple(new_b)

    init = (tuple(h0f_ref[s:s + 1, :] for s in range(n_seq)), tuple(h0b_ref[s:s + 1, :] for s in range(n_seq)))
    last_f, first_b = lax.fori_loop(0, n_group, body, init)

    for s in range(n_seq):
        hfl_ref[s:s + 1, :] = last_f[s]
        hbf_ref[s:s + 1, :] = first_b[s]
    for c in range(n_chunk):
        sl = slice(c * RNN_CHUNK, (c + 1) * RNN_CHUNK)
        y_ref[sl, :] = ((uf_s[sl, :] + ub_s[sl, :]) * _gelu_tanh(xg_ref[sl, :])).astype(y_ref.dtype)


def _rglru(xr, xg, conv_w, conv_b, wa, ba, wx, bx, lam, h0f, h0b, rows, seq_len):
    n = xr.shape[0]
    n_seq = rows // seq_len
    row = pl.BlockSpec((rows, RNN_WIDTH), lambda i: (i, 0))
    full = lambda shape: pl.BlockSpec(shape, lambda i: (0,) * len(shape))
    state = pl.BlockSpec((None, n_seq, RNN_WIDTH), lambda i: (i, 0, 0))
    state_shape = jax.ShapeDtypeStruct((n // rows, n_seq, RNN_WIDTH), F32)
    return pl.pallas_call(
        functools.partial(_rglru_kernel, rows=rows, seq_len=seq_len),
        grid=(n // rows,),
        in_specs=[row, row, full((CONV_W, RNN_WIDTH)), full((1, RNN_WIDTH)),
                  full((2, RNN_WIDTH, RNN_WIDTH)), full((2, RNN_WIDTH)),
                  full((2, RNN_WIDTH, RNN_WIDTH)), full((2, RNN_WIDTH)), full((2, RNN_WIDTH)),
                  state, state],
        out_specs=[row, state, state],
        out_shape=[jax.ShapeDtypeStruct((n, RNN_WIDTH), BF16), state_shape, state_shape],
        scratch_shapes=[pltpu.VMEM((rows, RNN_WIDTH), F32)] * 5,
        compiler_params=_params(1),
        name="rglru_%d" % seq_len,
    )(xr, xg, conv_w, conv_b, wa, ba, wx, bx, lam, h0f, h0b)


def _block_diag_dense(w):
    eye = jnp.eye(RNN_BLOCKS, dtype=w.dtype)
    dense = w[:, :, :, None, :] * eye[None, :, None, :, None]
    return dense.reshape(2, RNN_WIDTH, RNN_WIDTH).astype(BF16)


SUB_TILE = 256
SUB_TILES = tuple(slice(s, s + SUB_TILE) for s in range(0, ROW_TILE, SUB_TILE))
SGU_TILE = 512
SGU_TILES = tuple(slice(s, s + SGU_TILE) for s in range(0, ROW_TILE, SGU_TILE))


def _residual_router(rows, x, o, m, lg_ref, lb_ref, rw_ref, x1_ref, xm_ref, aff_ref):
    x1 = _layer_norm(ALPHA * x + m[2:3] * o, lg_ref[...], lb_ref[...])
    x1_ref[rows, :] = x1
    xm = (x1 * (1.0 + m[4:5]) + m[3:4]).astype(BF16)
    xm_ref[rows, :] = xm
    lgt = _dot(xm, rw_ref[...])
    lane = lax.broadcasted_iota(I32, lgt.shape, 1)
    lgt = jnp.where(lane < N_EXPERTS, lgt, NEG_INF)
    ex = jnp.exp(lgt - jnp.max(lgt, axis=-1, keepdims=True))
    aff = ex / jnp.sum(ex, axis=-1, keepdims=True)
    aff_ref[rows, :] = aff[:, :N_EXPERTS]


X_RING = 3


def _mix_out_kernel(att_ref, rnn_ref, x_hbm, mod_ref, w_ref, lg_ref, lb_ref, rw_ref,
                    x1_ref, xm_ref, aff_ref, x_ring, x_sem):
    i = pl.program_id(0)
    n_step = pl.num_programs(0)

    def fetch(step):
        slot = lax.rem(step, X_RING)
        return pltpu.make_async_copy(x_hbm.at[pl.ds(pl.multiple_of(step * ROW_TILE, ROW_TILE), ROW_TILE), :],
                                     x_ring.at[slot], x_sem.at[slot])

    @pl.when(i == 0)
    def _():
        for step in range(X_RING - 1):
            fetch(step).start()

    @pl.when(i + (X_RING - 1) < n_step)
    def _():
        fetch(i + (X_RING - 1)).start()

    m = mod_ref[...]

    def project(rows):
        return _dot(att_ref[rows, :], w_ref[:ATTN_WIDTH, :]) + _dot(rnn_ref[rows, :], w_ref[ATTN_WIDTH:, :])

    fetch(i).wait()
    x_tile = x_ring.at[lax.rem(i, X_RING)]

    def finish(rows, o):
        _residual_router(rows, x_tile[rows, :], o, m, lg_ref, lb_ref, rw_ref, x1_ref, xm_ref, aff_ref)

    _software_pipeline(SUB_TILES, project, finish)


def _sgu_kernel(x_ref, mod_ref, win_ref, bin_ref, sg_ref, sb_ref, spw_ref, spb_ref, w_ref, lg_ref, lb_ref, rw_ref,
                x1_ref, xm_ref, aff_ref, v_s, gated_s):
    m = mod_ref[...]

    def project(rows):
        h = x_ref[rows, :] * (1.0 + m[1:2]) + m[0:1]
        return _dot(h.astype(BF16), win_ref[...])

    def gate(rows, p):
        p = _gelu_tanh(p + bin_ref[...])
        v_s[rows, :] = _layer_norm(p[:, SGU_WIDTH:], sg_ref[...], sb_ref[...]).astype(v_s.dtype)
        for c in range(rows.start, rows.stop, CHUNK):
            local = slice(c - rows.start, c - rows.start + CHUNK)
            for g in range(SGU_GROUPS):
                cols = slice(g * SGU_GROUP_W, (g + 1) * SGU_GROUP_W)
                mixed = _dot(spw_ref[g], v_s[c:c + CHUNK, cols]) + spb_ref[:, cols]
                gated_s[c:c + CHUNK, cols] = (p[local, cols] * mixed).astype(gated_s.dtype)
        return _dot(gated_s[rows, :], w_ref[...])

    def finish(rows, o):
        _residual_router(rows, x_ref[rows, :], o, m, lg_ref, lb_ref, rw_ref, x1_ref, xm_ref, aff_ref)

    _software_pipeline(SGU_TILES, project, gate, finish)


def _mixer_out(kernel, name, acts, x, mod_l, weights, ln_g, ln_b, router_w, group0, rows_per_group,
               scratch=(), x_in_hbm=False):
    n = x.shape[0]
    row = lambda width: pl.BlockSpec((ROW_TILE, width), lambda i: (i, 0))
    full = lambda a: pl.BlockSpec(a.shape, lambda i: (0,) * a.ndim)
    x_spec = pl.BlockSpec(memory_space=pl.ANY) if x_in_hbm else row(D_MODEL)
    return pl.pallas_call(
        kernel,
        grid=(n // ROW_TILE,),
        in_specs=([row(a.shape[1]) for a in acts]
                  + [x_spec, pl.BlockSpec((None, 6, D_MODEL), _group_map(group0, rows_per_group))]
                  + [full(w) for w in weights] + [full(ln_g), full(ln_b), full(router_w)]),
        out_specs=[row(D_MODEL), row(D_MODEL), row(N_EXPERTS)],
        out_shape=[jax.ShapeDtypeStruct((n, D_MODEL), F32),
                   jax.ShapeDtypeStruct((n, D_MODEL), BF16),
                   jax.ShapeDtypeStruct((n, N_EXPERTS), F32)],
        scratch_shapes=list(scratch),
        compiler_params=_params(1),
        name=name,
    )(*acts, x, mod_l, *weights, ln_g, ln_b, router_w)


def _route_kernel(*refs, n_req, tokens, cap, place_in_kernel):
    n_blk = tokens // PREFIX_BLOCK
    if place_in_kernel:
        aff_ref, tri_ref, place_ref, posm_ref, posr_ref, affr_ref, blk_ref = refs
        n_col = place_ref.shape[2]
        aff = jnp.zeros((tokens, n_col), F32)
        for b in range(n_req):
            piece = aff_ref[b * tokens:(b + 1) * tokens, :]
            hi = piece.astype(BF16)
            rem = piece - hi.astype(F32)
            mid = rem.astype(BF16)
            low = (rem - mid.astype(F32)).astype(BF16)
            place = place_ref[b]
            aff = aff + ((_dot(hi, place) + _dot(mid, place)) + _dot(low, place))
    else:
        aff_ref, tri_ref, posm_ref, posr_ref, affr_ref, blk_ref = refs
        aff = aff_ref[...]
        n_col = aff.shape[1]

    def bisect(_, lo_hi):
        lo, hi = lo_hi
        mid = lo + ((hi - lo) >> 1)
        cnt = jnp.sum(jnp.where(aff >= pltpu.bitcast(mid, F32), 1.0, 0.0), axis=0, keepdims=True)
        ge = cnt >= cap
        return jnp.where(ge, mid, lo), jnp.where(ge, hi, mid)

    lo0 = jnp.zeros((1, n_col), I32)
    hi0 = jnp.full((1, n_col), 0x7F800000, I32)
    thr_bits, _ = lax.fori_loop(0, 31, bisect, (lo0, hi0))
    thr = pltpu.bitcast(thr_bits, F32)
    above = pltpu.bitcast(thr_bits + 1, F32)

    tri = tri_ref[...]

    def excl_prefix(x01):
        outs, before, inside = [], [], []
        off = jnp.zeros((1, n_col), F32)
        for blk in range(n_blk):
            xb = x01[blk * PREFIX_BLOCK:(blk + 1) * PREFIX_BLOCK, :]
            outs.append(_dot(tri, xb.astype(BF16)) + off)
            cnt = jnp.sum(xb, axis=0, keepdims=True)
            before.append(off)
            inside.append(cnt)
            off = off + cnt
        return (outs[0] if n_blk == 1 else jnp.concatenate(outs, axis=0)), before + inside

    gt = jnp.where(aff >= above, 1.0, 0.0)
    eq = jnp.where(aff >= thr, 1.0, 0.0) - gt
    need = cap - jnp.sum(gt, axis=0, keepdims=True)
    eq_rank, _ = excl_prefix(eq)
    sel = gt + jnp.where(eq_rank < need, eq, 0.0)
    pos, blk_rows = excl_prefix(sel)
    posm = jnp.where(sel > 0.0, pos, -1.0)
    posr_ref[...] = posm.T
    affr_ref[...] = aff.T
    for i, row in enumerate(blk_rows):
        blk_ref[i:i + 1, :] = row
    if place_in_kernel:
        posm_bf16 = posm.astype(BF16)
        for b in range(n_req):
            posm_ref[b * tokens:(b + 1) * tokens, :] = _dot_nt(posm_bf16, place_ref[b])
    else:
        posm_ref[...] = posm


PLACE_IN_KERNEL_MAX_REQ = 8


def _route_requests(aff, tri, n_req, tokens, cap):
    n_real = n_req * N_EXPERTS
    n_col = -(-n_real // LANES) * LANES
    n_blk = tokens // PREFIX_BLOCK
    place_in_kernel = n_req <= PLACE_IN_KERNEL_MAX_REQ
    whole = lambda shape: pl.BlockSpec(shape, lambda i: (0,) * len(shape))
    if place_in_kernel:
        col = np.arange(n_col)
        place = jnp.asarray(col[None, None, :] == (np.arange(n_req)[:, None, None] * N_EXPERTS
                                                   + np.arange(N_EXPERTS)[None, :, None]), BF16)
        args = (aff, tri, place)
        token_major = (n_req * tokens, N_EXPERTS)
        in_specs = [whole(token_major), whole((PREFIX_BLOCK, PREFIX_BLOCK)), whole(place.shape)]
    else:
        aff_t = aff.reshape(n_req, tokens, N_EXPERTS).transpose(1, 0, 2).reshape(tokens, n_real)
        args = (jnp.pad(aff_t, ((0, 0), (0, n_col - n_real))), tri)
        token_major = (tokens, n_col)
        in_specs = [whole(token_major), whole((PREFIX_BLOCK, PREFIX_BLOCK))]
    posm, posr, affr, blk = pl.pallas_call(
        functools.partial(_route_kernel, n_req=n_req, tokens=tokens, cap=cap, place_in_kernel=place_in_kernel),
        grid=(1,),
        in_specs=in_specs,
        out_specs=[whole(token_major), whole((n_col, tokens)), whole((n_col, tokens)), whole((2 * n_blk, n_col))],
        out_shape=[jax.ShapeDtypeStruct(token_major, F32),
                   jax.ShapeDtypeStruct((n_col, tokens), F32),
                   jax.ShapeDtypeStruct((n_col, tokens), F32),
                   jax.ShapeDtypeStruct((2 * n_blk, n_col), F32)],
        compiler_params=_params(1),
        name="route_%d" % tokens,
    )(*args)
    if not place_in_kernel:
        posm = posm[:, :n_real].reshape(tokens, n_req, N_EXPERTS).transpose(1, 0, 2).reshape(n_req * tokens, N_EXPERTS)
    blk = blk[:, :n_real].astype(I32).reshape(2, n_blk, n_req, N_EXPERTS).transpose(0, 2, 1, 3)
    expert_major = lambda a: a.reshape(n_col // N_EXPERTS, N_EXPERTS, tokens)
    return posm, expert_major(posr), expert_major(affr), blk[0], blk[1]


def _slot_gate(hit, aff_row):
    g = jnp.sum(jnp.where(hit, aff_row, 0.0), axis=-1, keepdims=True)
    return jnp.broadcast_to(g, (hit.shape[0], LANES))


def _gather_prompt_kernel(x_ref, posr_ref, affr_ref, xg_ref, g_ref):
    slot = lax.broadcasted_iota(I32, (CAP_P, SEQ), 0).astype(F32)
    for r in range(REQ_TILE):
        tokens = slice(r * SEQ, (r + 1) * SEQ)
        slots = slice(r * CAP_P, (r + 1) * CAP_P)
        posr = posr_ref[r]
        affr = affr_ref[r]
        hits = [posr[e:e + 1, :] == slot for e in range(N_EXPERTS)]
        onehot = jnp.concatenate([jnp.where(h, 1.0, 0.0) for h in hits], axis=0).astype(BF16)
        xg = _dot(onehot, x_ref[tokens, :])
        for e in range(N_EXPERTS):
            xg_ref[e, slots, :] = xg[e * CAP_P:(e + 1) * CAP_P, :].astype(xg_ref.dtype)
            g_ref[e, slots, :] = _slot_gate(hits[e], affr[e:e + 1, :])


def _gather_prompt(xm, posr, affr):
    return pl.pallas_call(
        _gather_prompt_kernel,
        grid=(BATCH // REQ_TILE,),
        in_specs=[pl.BlockSpec((REQ_TILE * SEQ, D_MODEL), lambda b: (b, 0)),
                  pl.BlockSpec((REQ_TILE, N_EXPERTS, SEQ), lambda b: (b, 0, 0)),
                  pl.BlockSpec((REQ_TILE, N_EXPERTS, SEQ), lambda b: (b, 0, 0))],
        out_specs=[pl.BlockSpec((N_EXPERTS, REQ_TILE * CAP_P, D_MODEL), lambda b: (0, b, 0)),
                   pl.BlockSpec((N_EXPERTS, REQ_TILE * CAP_P, LANES), lambda b: (0, b, 0))],
        out_shape=[jax.ShapeDtypeStruct((N_EXPERTS, SLOTS_P, D_MODEL), BF16),
                   jax.ShapeDtypeStruct((N_EXPERTS, SLOTS_P, LANES), F32)],
        compiler_params=_params(1),
        name="gather_prompt",
    )(xm, posr, affr)


TOKEN_BLOCK = PREFIX_BLOCK
N_TOKEN_BLOCKS = DEC_SEQ // TOKEN_BLOCK
SLOT_ALIGN = 16
WINDOW_SLOTS = 64
LAST_WINDOW = CAP_S - WINDOW_SLOTS
ALL_WINDOWS = N_EXPERTS * WINDOW_SLOTS


def _window_plan(before, inside):
    start = (before // SLOT_ALIGN) * SLOT_ALIGN
    span = before - start + inside
    passes = jnp.where(inside > 0, (span + WINDOW_SLOTS - 1) // WINDOW_SLOTS, 0)
    return start.reshape(-1), jnp.max(passes, axis=-1).reshape(-1)


def _window_bounds(start_ref, step, expert, k):
    lower = start_ref[step * N_EXPERTS + expert] + WINDOW_SLOTS * k
    begin = pl.multiple_of(jnp.minimum(lower, LAST_WINDOW), SLOT_ALIGN)
    return lower, begin


BLOCKS_PER_STEP = 2
STEP_TOKENS = BLOCKS_PER_STEP * TOKEN_BLOCK
STEPS_PER_REQ = N_TOKEN_BLOCKS // BLOCKS_PER_STEP


def _gather_sample_kernel(start_ref, passes_ref, x_ref, posr_ref, affr_ref, xg_ref, g_ref):
    t = pl.program_id(1)
    first_block = (pl.program_id(0) * STEPS_PER_REQ + t) * BLOCKS_PER_STEP

    @pl.when(t == 0)
    def _():
        xg_ref[...] = jnp.zeros_like(xg_ref)
        g_ref[...] = jnp.zeros_like(g_ref)

    row = lax.broadcasted_iota(I32, (WINDOW_SLOTS, TOKEN_BLOCK), 0).astype(F32)

    def pick(j, k):
        tokens = slice(j * TOKEN_BLOCK, (j + 1) * TOKEN_BLOCK)
        hits, begins = [], []
        for e in range(N_EXPERTS):
            lower, begin = _window_bounds(start_ref, first_block + j, e, k)
            pe = posr_ref[e:e + 1, tokens]
            hits.append((pe - begin.astype(F32) == row) & (pe >= lower.astype(F32)))
            begins.append(begin)
        onehot = jnp.concatenate([jnp.where(h, 1.0, 0.0) for h in hits], axis=0).astype(BF16)
        return _dot(onehot, x_ref[tokens, :]), hits, begins

    def place(j, picked):
        rows_x, hits, begins = picked
        tokens = slice(j * TOKEN_BLOCK, (j + 1) * TOKEN_BLOCK)
        for e in range(N_EXPERTS):
            win = pl.ds(begins[e], WINDOW_SLOTS)
            xg_ref[e, win, :] += rows_x[e * WINDOW_SLOTS:(e + 1) * WINDOW_SLOTS, :].astype(xg_ref.dtype)
            g_ref[e, win, :] += _slot_gate(hits[e], affr_ref[e:e + 1, tokens])

    _software_pipeline(range(BLOCKS_PER_STEP), lambda j: pick(j, 0), place)

    for j in range(BLOCKS_PER_STEP):
        def extra_pass(k, carry, j=j):
            place(j, pick(j, k))
            return carry

        lax.fori_loop(1, passes_ref[first_block + j], extra_pass, 0)


def _gather_sample(xm, posr, affr, starts, passes):
    expert_rows = pl.BlockSpec((None, N_EXPERTS, STEP_TOKENS), lambda b, t, s, p: (b, 0, t))
    grid_spec = pltpu.PrefetchScalarGridSpec(
        num_scalar_prefetch=2,
        grid=(DEC_BATCH, STEPS_PER_REQ),
        in_specs=[pl.BlockSpec((STEP_TOKENS, D_MODEL), lambda b, t, s, p: (b * STEPS_PER_REQ + t, 0)),
                  expert_rows, expert_rows],
        out_specs=[pl.BlockSpec((N_EXPERTS, CAP_S, D_MODEL), lambda b, t, s, p: (0, b, 0)),
                   pl.BlockSpec((N_EXPERTS, CAP_S, LANES), lambda b, t, s, p: (0, b, 0))],
    )
    return pl.pallas_call(
        _gather_sample_kernel,
        grid_spec=grid_spec,
        out_shape=[jax.ShapeDtypeStruct((N_EXPERTS, SLOTS_S, D_MODEL), BF16),
                   jax.ShapeDtypeStruct((N_EXPERTS, SLOTS_S, LANES), F32)],
        compiler_params=_params(2),
        name="gather_sample",
    )(starts, passes, xm, posr, affr)


FF_TILE = 512
SLOT_CHUNK = 1024


def _ffn_kernel(xp_ref, xs_ref, gp_ref, gs_ref, w1_ref, w3_ref, w2_ref, yp_ref, ys_ref, acc_ref):
    f = pl.program_id(1)
    n_f = pl.num_programs(1)
    @pl.when(f == 0)
    def _():
        acc_ref[...] = jnp.zeros_like(acc_ref)

    cast = {}

    def weight(ref):
        if id(ref) not in cast:
            cast[id(ref)] = ref[...].astype(BF16)
        return cast[id(ref)]

    def up_project(chunk):
        x_ref, c, _ = chunk
        x = x_ref[c * SLOT_CHUNK:(c + 1) * SLOT_CHUNK, :]
        h1 = _dot(x, weight(w1_ref))
        return h1, _dot(x, weight(w3_ref))

    def down_project(chunk, h):
        _, c, base = chunk
        h1, h3 = h
        hid = (h1 * jax.nn.sigmoid(h1) * h3).astype(BF16)
        rows = slice(base + c * SLOT_CHUNK, base + (c + 1) * SLOT_CHUNK)
        acc_ref[rows, :] += _dot(hid, weight(w2_ref))

    chunks = [(x_ref, c, base) for x_ref, n_slot, base in ((xp_ref, SLOTS_P, 0), (xs_ref, SLOTS_S, SLOTS_P))
              for c in range(n_slot // SLOT_CHUNK)]
    _software_pipeline(chunks, up_project, down_project)

    @pl.when(f == n_f - 1)
    def _():
        for y_ref, g_ref, n_slot, base in ((yp_ref, gp_ref, SLOTS_P, 0), (ys_ref, gs_ref, SLOTS_S, SLOTS_P)):
            for c in range(n_slot // SLOT_CHUNK):
                rows = slice(c * SLOT_CHUNK, (c + 1) * SLOT_CHUNK)
                gate = jnp.concatenate([g_ref[rows, :]] * (D_MODEL // LANES), axis=1)
                y_ref[rows, :] = (acc_ref[base + c * SLOT_CHUNK:base + (c + 1) * SLOT_CHUNK, :]
                                  * gate).astype(y_ref.dtype)


def _expert_ffn(xgp, xgs, gp, gs, w1, w3, w2, layer):
    slots = lambda n, width: pl.BlockSpec((None, n, width), lambda e, f: (e, 0, 0))
    return pl.pallas_call(
        _ffn_kernel,
        grid=(N_EXPERTS, EXPERT_FF // FF_TILE),
        in_specs=[slots(SLOTS_P, D_MODEL), slots(SLOTS_S, D_MODEL), slots(SLOTS_P, LANES), slots(SLOTS_S, LANES),
                  pl.BlockSpec((None, None, D_MODEL, FF_TILE), lambda e, f: (layer, e, 0, f)),
                  pl.BlockSpec((None, None, D_MODEL, FF_TILE), lambda e, f: (layer, e, 0, f)),
                  pl.BlockSpec((None, None, FF_TILE, D_MODEL), lambda e, f: (layer, e, f, 0))],
        out_specs=[slots(SLOTS_P, D_MODEL), slots(SLOTS_S, D_MODEL)],
        out_shape=[jax.ShapeDtypeStruct((N_EXPERTS, SLOTS_P, D_MODEL), BF16),
                   jax.ShapeDtypeStruct((N_EXPERTS, SLOTS_S, D_MODEL), BF16)],
        scratch_shapes=[pltpu.VMEM((SLOTS_P + SLOTS_S, D_MODEL), F32)],
        compiler_params=_params(2),
        name="expert_ffn",
    )(xgp, xgs, gp, gs, w1, w3, w2)


def _combine_prompt_kernel(posm_ref, y_ref, x_ref, mod_ref, exp_ref, lg_ref, lb_ref, o_ref):
    m = mod_ref[...]
    slot = (lax.broadcasted_iota(I32, (SEQ, N_EXPERTS * CAP_P), 1) & (CAP_P - 1)).astype(F32)
    def scatter(r):
        tokens = slice(r * SEQ, (r + 1) * SEQ)
        spread = _dot(posm_ref[tokens, :].astype(BF16), exp_ref[...])
        onehot = jnp.where(spread == slot, 1.0, 0.0).astype(BF16)
        y = jnp.concatenate([y_ref[e, r * CAP_P:(r + 1) * CAP_P, :] for e in range(N_EXPERTS)], axis=0)
        return _dot(onehot, y)

    def finish(r, f):
        tokens = slice(r * SEQ, (r + 1) * SEQ)
        o_ref[tokens, :] = _layer_norm(ALPHA * x_ref[tokens, :] + m[5:6] * f, lg_ref[...], lb_ref[...])

    _software_pipeline(range(REQ_TILE), scatter, finish)


def _combine_prompt(posm, y, x1, mod_l, expand, ln_g, ln_b):
    full = lambda a: pl.BlockSpec(a.shape, lambda b: (0,) * a.ndim)
    return pl.pallas_call(
        _combine_prompt_kernel,
        grid=(BATCH // REQ_TILE,),
        in_specs=[pl.BlockSpec((REQ_TILE * SEQ, N_EXPERTS), lambda b: (b, 0)),
                  pl.BlockSpec((N_EXPERTS, REQ_TILE * CAP_P, D_MODEL), lambda b: (0, b, 0)),
                  pl.BlockSpec((REQ_TILE * SEQ, D_MODEL), lambda b: (b, 0)),
                  pl.BlockSpec((None, 6, D_MODEL), lambda b: (0, 0, 0)),
                  full(expand), full(ln_g), full(ln_b)],
        out_specs=pl.BlockSpec((REQ_TILE * SEQ, D_MODEL), lambda b: (b, 0)),
        out_shape=jax.ShapeDtypeStruct((N_PROMPT, D_MODEL), F32),
        compiler_params=_params(1),
        name="combine_prompt",
    )(posm, y, x1, mod_l, expand, ln_g, ln_b)


def _combine_sample_kernel(start_ref, passes_ref, posm_ref, y_ref, x_ref, mod_ref, exp_ref, off_ref,
                           lg_ref, lb_ref, o_ref):
    first_block = (pl.program_id(0) * STEPS_PER_REQ + pl.program_id(1)) * BLOCKS_PER_STEP
    exp_bf16 = exp_ref[...]
    offset = off_ref[...]
    m = mod_ref[...]

    def tokens_of(j):
        return slice(j * TOKEN_BLOCK, (j + 1) * TOKEN_BLOCK)

    def one_pass(j, k):
        spread = _dot(posm_ref[tokens_of(j), :].astype(BF16), exp_bf16)
        begin_row = jnp.zeros((1, ALL_WINDOWS), F32)
        lower_row = jnp.zeros((1, ALL_WINDOWS), F32)
        windows = []
        for e in range(N_EXPERTS):
            lower, begin = _window_bounds(start_ref, first_block + j, e, k)
            lanes_e = exp_bf16[e:e + 1, :].astype(F32)
            begin_row = begin_row + begin.astype(F32) * lanes_e
            lower_row = lower_row + lower.astype(F32) * lanes_e
            windows.append(y_ref[e, pl.ds(begin, WINDOW_SLOTS), :])
        hit = (spread - begin_row == offset) & (spread >= lower_row)
        onehot = jnp.where(hit, 1.0, 0.0).astype(BF16)
        return _dot(onehot, jnp.concatenate(windows, axis=0))

    def finish(j, f):
        rows = tokens_of(j)
        o_ref[rows, :] = _layer_norm(ALPHA * x_ref[rows, :] + m[5:6] * f, lg_ref[...], lb_ref[...])

    _software_pipeline(range(BLOCKS_PER_STEP), lambda j: one_pass(j, 0), finish)

    for j in range(BLOCKS_PER_STEP):
        n_pass = passes_ref[first_block + j]

        @pl.when(n_pass > 1)
        def _(j=j, n_pass=n_pass):
            f = lax.fori_loop(1, n_pass, lambda k, f: f + one_pass(j, k), one_pass(j, 0))
            finish(j, f)


def _combine_sample(posm, y, x1, mod_l, ln_g, ln_b, starts, passes):
    lane = np.arange(ALL_WINDOWS)
    expand = jnp.asarray((lane[None, :] // WINDOW_SLOTS) == np.arange(N_EXPERTS)[:, None], BF16)
    offset = jnp.asarray((lane % WINDOW_SLOTS).reshape(1, ALL_WINDOWS), F32)
    row = lambda width: pl.BlockSpec((STEP_TOKENS, width), lambda b, t, s, p: (b * STEPS_PER_REQ + t, 0))
    full = lambda a: pl.BlockSpec(a.shape, lambda b, t, s, p: (0,) * a.ndim)
    grid_spec = pltpu.PrefetchScalarGridSpec(
        num_scalar_prefetch=2,
        grid=(DEC_BATCH, STEPS_PER_REQ),
        in_specs=[row(N_EXPERTS),
                  pl.BlockSpec((N_EXPERTS, CAP_S, D_MODEL), lambda b, t, s, p: (0, b, 0)),
                  row(D_MODEL),
                  pl.BlockSpec((None, 6, D_MODEL), lambda b, t, s, p: (1 + b, 0, 0)),
                  full(expand), full(offset), full(ln_g), full(ln_b)],
        out_specs=row(D_MODEL),
    )
    return pl.pallas_call(
        _combine_sample_kernel,
        grid_spec=grid_spec,
        out_shape=jax.ShapeDtypeStruct((N_SAMPLE, D_MODEL), F32),
        compiler_params=_params(2),
        name="combine_sample",
    )(starts, passes, posm, y, x1, mod_l, expand, offset, ln_g, ln_b)


def _moe(prompt, sample, mod_l, w1, w3, w2, layer, ln_g, ln_b, tri, expand):
    x1_p, xm_p, aff_p = prompt
    x1_s, xm_s, aff_s = sample
    posm_p, posr_p, affr_p, _, _ = _route_requests(aff_p, tri, BATCH, SEQ, CAP_P)
    posm_s, posr_s, affr_s, before_s, inside_s = _route_requests(aff_s, tri, DEC_BATCH, DEC_SEQ, CAP_S)
    starts, passes = _window_plan(before_s, inside_s)
    xg_p, g_p = _gather_prompt(xm_p, posr_p, affr_p)
    xg_s, g_s = _gather_sample(xm_s, posr_s, affr_s, starts, passes)
    y_p, y_s = _expert_ffn(xg_p, xg_s, g_p, g_s, w1, w3, w2, layer)
    x2_p = _combine_prompt(posm_p, y_p, x1_p, mod_l, expand, ln_g, ln_b)
    x2_s = _combine_sample(posm_s, y_s, x1_s, mod_l, ln_g, ln_b, starts, passes)
    return x2_p, x2_s


def kernel(x_prompt, x_sample, cache_k, cache_v, state_rglru, c, c_ctx, mod_w, mod_b, ln_mix_g, ln_mix_b, ln_ffn_g, ln_ffn_b, ab_in_w, attn_sink, rnn_conv_w, rnn_conv_b, lru_wa, lru_ba, lru_wx, lru_bx, lru_lambda, ab_out_w, sgu_in_w, sgu_in_b, sgu_ln_g, sgu_ln_b, sgu_spatial_w, sgu_spatial_b, sgu_out_w, router_w, moe_w1, moe_w3, moe_w2):
    xp = x_prompt.reshape(N_PROMPT, D_MODEL)
    xs = x_sample.reshape(N_SAMPLE, D_MODEL)

    cvec = jnp.concatenate([c_ctx[None], c, jnp.zeros((SUBLANES - 1 - DEC_BATCH, D_MODEL), F32)], axis=0)
    mod = _modulation(cvec, mod_w, mod_b).reshape(DEPTH, SUBLANES, 6, D_MODEL)

    idx = np.arange(PREFIX_BLOCK)
    tri = jnp.asarray(idx[None, :] < idx[:, None], BF16)
    lane = np.arange(N_EXPERTS * CAP_P)
    expand = jnp.asarray((lane[None, :] // CAP_P) == np.arange(N_EXPERTS)[:, None], BF16)
    rope_tables = _rope_tables()
    vec = lambda a: a.reshape(1, -1)
    router_pad = lambda w: jnp.pad(w, ((0, 0), (0, LANES - N_EXPERTS))).astype(BF16)

    new_k = new_v = new_state = None
    for l in range(DEPTH):
        e = l // 2
        mod_l = mod[l]
        rw = router_pad(router_w[l])
        lg, lb = vec(ln_mix_g[l]), vec(ln_mix_b[l])
        if l % 2 == 0:
            w_in = ab_in_w[e].astype(BF16)
            w_out = ab_out_w[e].astype(BF16)
            wa = _block_diag_dense(lru_wa[e])
            wx = _block_diag_dense(lru_wx[e])
            rnn_w = (rnn_conv_w[e], vec(rnn_conv_b[e]), wa, lru_ba[e], wx, lru_bx[e], lru_lambda[e])

            q, k, v, xr, xg, new_k, new_v = _ab_in(xp, mod_l, w_in, 0, N_PROMPT, None)
            att = _ctx_attention(q, k, v, attn_sink[e])
            rows = 8 * SEQ
            zeros = jnp.zeros((N_PROMPT // rows, rows // SEQ, RNN_WIDTH), F32)
            rnn, hf_last, hb_first = _rglru(xr, xg, *rnn_w, zeros, zeros, rows, SEQ)
            new_state = jnp.stack([hf_last.reshape(BATCH, RNN_WIDTH), hb_first.reshape(BATCH, RNN_WIDTH)], axis=1)
            x_ring = (pltpu.VMEM((X_RING, ROW_TILE, D_MODEL), F32), pltpu.SemaphoreType.DMA((X_RING,)))
            prompt = _mixer_out(_mix_out_kernel, "mix_out", (att, rnn), xp, mod_l, (w_out,),
                                lg, lb, rw, 0, N_PROMPT, x_ring, x_in_hbm=True)

            q, k, v, xr, xg = _ab_in(xs, mod_l, w_in, 1, DEC_SEQ, rope_tables)
            att = _lat_attention(q, k, v, cache_k[:, e].reshape(DEC_BATCH * PAST_LEN, KV_WIDTH),
                                 cache_v[:, e].reshape(DEC_BATCH * PAST_LEN, KV_WIDTH), attn_sink[e])
            h0 = state_rglru[:, e]
            rnn, _, _ = _rglru(xr, xg, *rnn_w, h0[:, 0][:, None, :], h0[:, 1][:, None, :], DEC_SEQ, DEC_SEQ)
            sample = _mixer_out(_mix_out_kernel, "mix_out", (att, rnn), xs, mod_l, (w_out,),
                                lg, lb, rw, 1, DEC_SEQ, x_ring, x_in_hbm=True)
        else:
            w_in = sgu_in_w[e].astype(BF16)
            w_out = sgu_out_w[e].astype(BF16)
            spw = sgu_spatial_w[e].astype(BF16)
            spb = jnp.repeat(sgu_spatial_b[e].T, SGU_GROUP_W, axis=1)
            sgu_w = (w_in, vec(sgu_in_b[e]), vec(sgu_ln_g[e]), vec(sgu_ln_b[e]), spw, spb, w_out)
            scratch = (pltpu.VMEM((ROW_TILE, SGU_WIDTH), BF16), pltpu.VMEM((ROW_TILE, SGU_WIDTH), BF16))
            prompt = _mixer_out(_sgu_kernel, "sgu", (), xp, mod_l, sgu_w, lg, lb, rw, 0, N_PROMPT, scratch)
            sample = _mixer_out(_sgu_kernel, "sgu", (), xs, mod_l, sgu_w, lg, lb, rw, 1, DEC_SEQ, scratch)

        xp, xs = _moe(prompt, sample, mod_l, moe_w1, moe_w3, moe_w2, l,
                      vec(ln_ffn_g[l]), vec(ln_ffn_b[l]), tri, expand)

    return (xp.reshape(BATCH, SEQ, D_MODEL),
            xs.reshape(DEC_BATCH, DEC_SEQ, D_MODEL),
            new_k,
            new_v,
            new_state.reshape(BATCH, 1, 2, RNN_WIDTH))
```

```python
import functools

import numpy as np
import jax
import jax.numpy as jnp
from jax import lax
from jax.experimental import pallas as pl
from jax.experimental.pallas import tpu as pltpu

F32 = jnp.float32
BF16 = jnp.bfloat16
I32 = jnp.int32

D_MODEL = 1024
BATCH = 32
SEQ = 256
DEPTH = 2
DEC_BATCH = 4
DEC_SEQ = 2048
PAST_LEN = 256
GRID_W = 64
HEAD_DIM = 128
N_Q_HEADS = 4
N_KV_HEADS = 2
Q_PER_KV = N_Q_HEADS // N_KV_HEADS
ATTN_WIDTH = N_Q_HEADS * HEAD_DIM
KV_WIDTH = N_KV_HEADS * HEAD_DIM
WINDOW = 128
BLOCK = 128
ATTN_SCALE = HEAD_DIM ** -0.5
ROPE_BASE = 10000.0
NEG_INF = -1e30
RNN_WIDTH = 512
RNN_BLOCKS = 8
RNN_BLOCK_W = RNN_WIDTH // RNN_BLOCKS
CONV_W = 4
CONV_PAD_LEFT = 2
LRU_C = 8.0
AB_IN_WIDTH = ATTN_WIDTH + 2 * KV_WIDTH + 2 * RNN_WIDTH
CHUNK = 128
SGU_WIDTH = D_MODEL
SGU_GROUPS = 8
SGU_GROUP_W = SGU_WIDTH // SGU_GROUPS
N_EXPERTS = 16
EXPERT_FF = 2048
EC_FACTOR = 2
ALPHA = (2 * DEPTH) ** 0.25
LN_EPS = 1e-6

N_PROMPT = BATCH * SEQ
N_SAMPLE = DEC_BATCH * DEC_SEQ
CAP_P = EC_FACTOR * SEQ // N_EXPERTS
CAP_S = EC_FACTOR * DEC_SEQ // N_EXPERTS
SLOTS_P = BATCH * CAP_P
SLOTS_S = DEC_BATCH * CAP_S

LANES = 128
SUBLANES = 8
ROW_TILE = 1024
REQ_TILE = 4
PREFIX_BLOCK = 256
VMEM_LIMIT = 56 * 1024 * 1024


def _params(n_axes=1):
    return pltpu.CompilerParams(dimension_semantics=("arbitrary",) * n_axes,
                                vmem_limit_bytes=VMEM_LIMIT)


def _layer_norm(x, g, b):
    mu = jnp.mean(x, axis=-1, keepdims=True)
    xc = x - mu
    var = jnp.mean(xc * xc, axis=-1, keepdims=True)
    return xc * lax.rsqrt(var + LN_EPS) * g + b


_LOG2_E = 1.4426950408889634
_GELU_K0 = -2.0 * 0.7978845608028654 * _LOG2_E
_GELU_K1 = _GELU_K0 * 0.044715


def _gelu_tanh(x):
    return x / (1.0 + jnp.exp2(x * (_GELU_K0 + _GELU_K1 * (x * x))))


def _sigmoid(z):
    return 0.5 + 0.5 * jnp.tanh(0.5 * z)


def _dot(a, b):
    return jnp.dot(a, b, preferred_element_type=F32)


def _dot_nt(a, b):
    return lax.dot_general(a, b, (((1,), (1,)), ((), ())), preferred_element_type=F32)


def _software_pipeline(items, first_stage, *later_stages):
    items = list(items)
    stages = (first_stage,) + later_stages
    carried = {}
    for t in range(len(items) + len(stages) - 1):
        for s, stage in enumerate(stages):
            i = t - s
            if 0 <= i < len(items):
                carried[i] = stage(items[i]) if s == 0 else stage(items[i], carried[i])


def _mod_kernel(c_ref, w_ref, b_ref, o_ref):
    c = c_ref[...]
    s = c * jax.nn.sigmoid(c)
    o_ref[...] = _dot(s.astype(BF16), w_ref[...].astype(BF16)) + b_ref[...]


def _modulation(cvec8, mod_w, mod_b):
    n_col = 6 * D_MODEL // D_MODEL
    return pl.pallas_call(
        _mod_kernel,
        grid=(DEPTH, n_col),
        in_specs=[pl.BlockSpec((SUBLANES, D_MODEL), lambda l, j: (0, 0)),
                  pl.BlockSpec((None, D_MODEL, D_MODEL), lambda l, j: (l, 0, j)),
                  pl.BlockSpec((None, 1, D_MODEL), lambda l, j: (l, 0, j))],
        out_specs=pl.BlockSpec((None, SUBLANES, D_MODEL), lambda l, j: (l, 0, j)),
        out_shape=jax.ShapeDtypeStruct((DEPTH, SUBLANES, 6 * D_MODEL), F32),
        compiler_params=_params(2),
        name="adaln_modulation",
    )(cvec8, mod_w, mod_b.reshape(DEPTH, 1, 6 * D_MODEL))


def _group_map(group0, rows_per_group):
    tiles_per_group = rows_per_group // ROW_TILE
    return lambda i: (group0 + i // tiles_per_group, 0, 0)


def _rope(t, cos, sin_signed):
    lane = lax.broadcasted_iota(I32, t.shape, 1)
    swapped = jnp.where((lane & 1) == 0, pltpu.roll(t, HEAD_DIM - 1, 1), pltpu.roll(t, 1, 1))
    return t * cos + swapped * sin_signed


def _ab_in_kernel(*refs, rope):
    if rope:
        x_ref, mod_ref, w_ref, cos_ref, sin_ref, q_ref, k_ref, v_ref, xr_ref, xg_ref = refs
    else:
        x_ref, mod_ref, w_ref, q_ref, k_ref, v_ref, xr_ref, xg_ref, ck_ref, cv_ref = refs
    m = mod_ref[...]
    h = x_ref[...] * (1.0 + m[1:2]) + m[0:1]
    p = _dot(h.astype(BF16), w_ref[...])
    q = p[:, :ATTN_WIDTH] * (ATTN_SCALE * _LOG2_E)
    k = p[:, ATTN_WIDTH:ATTN_WIDTH + KV_WIDTH]
    v = p[:, ATTN_WIDTH + KV_WIDTH:ATTN_WIDTH + 2 * KV_WIDTH]
    if rope:
        cos = cos_ref[...]
        sin = sin_ref[...]
        q = jnp.concatenate([_rope(q[:, i * HEAD_DIM:(i + 1) * HEAD_DIM], cos, sin)
                             for i in range(N_Q_HEADS)], axis=1)
        k = jnp.concatenate([_rope(k[:, i * HEAD_DIM:(i + 1) * HEAD_DIM], cos, sin)
                             for i in range(N_KV_HEADS)], axis=1)
    q_ref[...] = q.astype(q_ref.dtype)
    k_ref[...] = k.astype(k_ref.dtype)
    v_ref[...] = v.astype(v_ref.dtype)
    xr_ref[...] = p[:, ATTN_WIDTH + 2 * KV_WIDTH:ATTN_WIDTH + 2 * KV_WIDTH + RNN_WIDTH]
    xg_ref[...] = p[:, ATTN_WIDTH + 2 * KV_WIDTH + RNN_WIDTH:]
    if not rope:
        for r in range(ROW_TILE // SEQ):
            for i in range(N_KV_HEADS):
                ck_ref[r, 0, :, i, :] = k[r * SEQ:(r + 1) * SEQ, _head_cols(i)]
                cv_ref[r, 0, :, i, :] = v[r * SEQ:(r + 1) * SEQ, _head_cols(i)]


def _ab_in(x, mod_l, w_bf16, group0, rows_per_group, rope_tables):
    n = x.shape[0]
    rope = rope_tables is not None
    row = lambda width: pl.BlockSpec((ROW_TILE, width), lambda i: (i, 0))
    in_specs = [row(D_MODEL),
                pl.BlockSpec((None, 6, D_MODEL), _group_map(group0, rows_per_group)),
                pl.BlockSpec((D_MODEL, AB_IN_WIDTH), lambda i: (0, 0))]
    args = [x, mod_l, w_bf16]
    out_specs = [row(ATTN_WIDTH), row(KV_WIDTH), row(KV_WIDTH), row(RNN_WIDTH), row(RNN_WIDTH)]
    out_shape = [jax.ShapeDtypeStruct((n, ATTN_WIDTH), BF16),
                 jax.ShapeDtypeStruct((n, KV_WIDTH), BF16),
                 jax.ShapeDtypeStruct((n, KV_WIDTH), BF16),
                 jax.ShapeDtypeStruct((n, RNN_WIDTH), F32),
                 jax.ShapeDtypeStruct((n, RNN_WIDTH), F32)]
    if rope:
        tiles_per_seq = DEC_SEQ // ROW_TILE
        in_specs += [pl.BlockSpec((ROW_TILE, HEAD_DIM), lambda i: (i % tiles_per_seq, 0))] * 2
        args += list(rope_tables)
    else:
        req = ROW_TILE // SEQ
        cache = pl.BlockSpec((req, 1, SEQ, N_KV_HEADS, HEAD_DIM), lambda i: (i, 0, 0, 0, 0))
        out_specs += [cache, cache]
        out_shape += [jax.ShapeDtypeStruct((n // SEQ, 1, SEQ, N_KV_HEADS, HEAD_DIM), F32)] * 2
    return pl.pallas_call(
        functools.partial(_ab_in_kernel, rope=rope),
        grid=(n // ROW_TILE,),
        in_specs=in_specs,
        out_specs=out_specs,
        out_shape=out_shape,
        compiler_params=_params(1),
        name="ab_in_rope" if rope else "ab_in",
    )(*args)


def _rope_tables():
    rows = DEC_SEQ // GRID_W
    row = np.repeat(np.arange(rows, dtype=np.float32), GRID_W)
    col = np.tile(np.arange(GRID_W, dtype=np.float32), rows)
    n_freq = HEAD_DIM // 4
    freqs = np.float32(ROPE_BASE) ** (-np.arange(n_freq, dtype=np.float32) / np.float32(n_freq))
    ang = np.concatenate([row[:, None] * freqs, col[:, None] * freqs], axis=-1)
    cos = np.repeat(np.cos(ang), 2, axis=-1)
    sin = np.repeat(np.sin(ang), 2, axis=-1)
    sign = np.tile(np.array([-1.0, 1.0], np.float32), HEAD_DIM // 2)
    return jnp.asarray(cos, F32), jnp.asarray(sin * sign, F32)


def _sink_attention_head(s_list, v_list, sink):
    sink = sink * _LOG2_E
    m = sink
    for s in s_list:
        m = jnp.maximum(m, jnp.max(s, axis=-1, keepdims=True))
    p_list = [jnp.exp2(s - m) for s in s_list]
    denom = jnp.exp2(sink - m)
    for p in p_list:
        denom = denom + jnp.sum(p, axis=-1, keepdims=True)
    out = None
    for p, v in zip(p_list, v_list):
        o = _dot(p.astype(BF16), v)
        out = o if out is None else out + o
    return out * (1.0 / denom)


def _head_cols(h):
    return slice(h * HEAD_DIM, (h + 1) * HEAD_DIM)


def _ctx_attn_kernel(sink_ref, q_ref, k_ref, v_ref, o_ref):
    def scores(item):
        r, h = item
        rows = slice(r * SEQ, (r + 1) * SEQ)
        kh = k_ref[rows, _head_cols(h // Q_PER_KV)].astype(BF16)
        return _dot_nt(q_ref[rows, _head_cols(h)], kh)

    def finish(item, s):
        r, h = item
        rows = slice(r * SEQ, (r + 1) * SEQ)
        vh = v_ref[rows, _head_cols(h // Q_PER_KV)].astype(BF16)
        o_ref[rows, _head_cols(h)] = _sink_attention_head([s], [vh], sink_ref[h]).astype(o_ref.dtype)

    items = [(r, h) for r in range(REQ_TILE) for h in range(N_Q_HEADS)]
    _software_pipeline(items, scores, finish)


def _ctx_attention(q, k, v, sink):
    seq = lambda width: pl.BlockSpec((REQ_TILE * SEQ, width), lambda b: (b, 0))
    return pl.pallas_call(
        _ctx_attn_kernel,
        grid=(BATCH // REQ_TILE,),
        in_specs=[pl.BlockSpec(memory_space=pltpu.SMEM), seq(ATTN_WIDTH), seq(KV_WIDTH), seq(KV_WIDTH)],
        out_specs=seq(ATTN_WIDTH),
        out_shape=jax.ShapeDtypeStruct((N_PROMPT, ATTN_WIDTH), BF16),
        compiler_params=_params(1),
        name="context_attention",
    )(sink, q, k, v)


LAT_Q = 256


def _lat_attn_kernel(sink_ref, q_ref, kp_ref, kc_ref, kn_ref, vp_ref, vc_ref, vn_ref,
                     kx_ref, vx_ref, o_ref):
    n = pl.program_id(1)
    nb = pl.num_programs(1)
    q = q_ref[...]
    kw = jnp.concatenate([kp_ref[...], kc_ref[...], kn_ref[...]], axis=0)
    vw = jnp.concatenate([vp_ref[...], vc_ref[...], vn_ref[...]], axis=0)
    kx = kx_ref[...].astype(BF16)
    vx = vx_ref[...].astype(BF16)
    n_key = LAT_Q + 2 * WINDOW
    qi = lax.broadcasted_iota(I32, (LAT_Q, n_key), 0)
    kj = lax.broadcasted_iota(I32, (LAT_Q, n_key), 1)
    rel = kj - qi
    valid = (rel >= 0) & (rel <= 2 * WINDOW)
    valid = valid & ((n > 0) | (kj >= WINDOW)) & ((n < nb - 1) | (kj < LAT_Q + WINDOW))
    def scores(h):
        sl = _head_cols(h // Q_PER_KV)
        qh = q[:, _head_cols(h)]
        s_ctx = _dot_nt(qh, kx[:, sl])
        s_win = jnp.where(valid, _dot_nt(qh, kw[:, sl]), NEG_INF)
        return s_ctx, s_win

    def finish(h, s):
        sl = _head_cols(h // Q_PER_KV)
        o_ref[:, _head_cols(h)] = _sink_attention_head(list(s), [vx[:, sl], vw[:, sl]],
                                                       sink_ref[h]).astype(o_ref.dtype)

    _software_pipeline(range(N_Q_HEADS), scores, finish)


def _lat_attention(q, k, v, k_ctx, v_ctx, sink):
    nb = DEC_SEQ // LAT_Q
    nw = DEC_SEQ // WINDOW
    per = LAT_Q // WINDOW
    cur = lambda b, n: (b * nb + n, 0)
    prev = lambda b, n: (b * nw + jnp.maximum(n * per - 1, 0), 0)
    nxt = lambda b, n: (b * nw + jnp.minimum((n + 1) * per, nw - 1), 0)
    tile = lambda width: pl.BlockSpec((LAT_Q, width), cur)
    edge = lambda imap: pl.BlockSpec((WINDOW, KV_WIDTH), imap)
    ctx = pl.BlockSpec((PAST_LEN, KV_WIDTH), lambda b, n: (b, 0))
    return pl.pallas_call(
        _lat_attn_kernel,
        grid=(DEC_BATCH, nb),
        in_specs=[pl.BlockSpec(memory_space=pltpu.SMEM), tile(ATTN_WIDTH),
                  edge(prev), tile(KV_WIDTH), edge(nxt),
                  edge(prev), tile(KV_WIDTH), edge(nxt), ctx, ctx],
        out_specs=tile(ATTN_WIDTH),
        out_shape=jax.ShapeDtypeStruct((N_SAMPLE, ATTN_WIDTH), BF16),
        compiler_params=_params(2),
        name="latent_attention",
    )(sink, q, k, k, k, v, v, v, k_ctx, v_ctx)


RNN_CHUNK = 256


def _rglru_kernel(xr_ref, xg_ref, cw_ref, cb_ref, wa_ref, ba_ref, wx_ref, bx_ref, lam_ref,
                  h0f_ref, h0b_ref, y_ref, hfl_ref, hbf_ref, xc_s, af_s, ab_s, uf_s, ub_s, *, rows, seq_len):
    n_seq = rows // seq_len
    n_chunk = rows // RNN_CHUNK
    cw = cw_ref[...]
    cb = cb_ref[...]
    zeros_halo = jnp.zeros((SUBLANES, RNN_WIDTH), F32)
    row8 = lax.broadcasted_iota(I32, (SUBLANES, RNN_WIDTH), 0)

    def conv_chunk(c):
        r0 = c * RNN_CHUNK
        first = r0 % seq_len == 0
        last = (r0 + RNN_CHUNK) % seq_len == 0
        before = zeros_halo if first else xr_ref[r0 - SUBLANES:r0, :]
        after = zeros_halo if last else xr_ref[r0 + RNN_CHUNK:r0 + RNN_CHUNK + SUBLANES, :]
        win = jnp.concatenate([before, xr_ref[r0:r0 + RNN_CHUNK, :], after], axis=0)
        xc = cb
        n_win = RNN_CHUNK + 2 * SUBLANES
        for i in range(CONV_W):
            shift = (CONV_PAD_LEFT - i) % n_win
            rolled = win if shift == 0 else pltpu.roll(win, shift, 0)
            xc = xc + rolled[SUBLANES:SUBLANES + RNN_CHUNK, :] * cw[i:i + 1, :]
        return xc

    def group_scan(a, u, reverse):
        for k in (1, 2, 4):
            if reverse:
                shift, ok = SUBLANES - k, row8 < SUBLANES - k
            else:
                shift, ok = k, row8 >= k
            a_nb = jnp.where(ok, pltpu.roll(a, shift, 0), 1.0)
            u_nb = jnp.where(ok, pltpu.roll(u, shift, 0), 0.0)
            u = a * u_nb + u
            a = a * a_nb
        return a, u

    for c in range(n_chunk):
        xc_s[c * RNN_CHUNK:(c + 1) * RNN_CHUNK, :] = conv_chunk(c)

    for d, (a_s, u_s) in enumerate(((af_s, uf_s), (ab_s, ub_s))):
        neg = -lam_ref[d:d + 1, :]
        softplus = jnp.maximum(neg, 0.0) + jnp.log1p(jnp.exp(-jnp.abs(neg)))
        decay = -LRU_C * softplus
        wa = wa_ref[d]
        wx = wx_ref[d]
        ba = ba_ref[d:d + 1, :]
        bx = bx_ref[d:d + 1, :]
        for c in range(n_chunk):
            xc = xc_s[c * RNN_CHUNK:(c + 1) * RNN_CHUNK, :]
            xcb = xc.astype(BF16)
            r = _sigmoid(_dot(xcb, wa) + ba)
            i = _sigmoid(_dot(xcb, wx) + bx)
            log_a = r * decay
            a = jnp.exp(log_a)
            a_s[c * RNN_CHUNK:(c + 1) * RNN_CHUNK, :] = a
            one_minus_a2 = -jnp.tanh(log_a) * (a * a + 1.0)
            u_s[c * RNN_CHUNK:(c + 1) * RNN_CHUNK, :] = jnp.sqrt(one_minus_a2) * (i * xc)

    n_group = seq_len // SUBLANES

    def body(g, carries):
        fwd, bwd = carries
        new_f, new_b = [], []
        for s in range(n_seq):
            rf = pl.multiple_of(s * seq_len + g * SUBLANES, SUBLANES)
            a, u = group_scan(af_s[pl.ds(rf, SUBLANES), :], uf_s[pl.ds(rf, SUBLANES), :], False)
            h = u + a * fwd[s]
            uf_s[pl.ds(rf, SUBLANES), :] = h
            new_f.append(h[SUBLANES - 1:SUBLANES, :])
            rb = pl.multiple_of(s * seq_len + (n_group - 1 - g) * SUBLANES, SUBLANES)
            a, u = group_scan(ab_s[pl.ds(rb, SUBLANES), :], ub_s[pl.ds(rb, SUBLANES), :], True)
            h = u + a * bwd[s]
            ub_s[pl.ds(rb, SUBLANES), :] = h
            new_b.append(h[0:1, :])
        return tuple(new_f), tuple(new_b)

    init = (tuple(h0f_ref[s:s + 1, :] for s in range(n_seq)), tuple(h0b_ref[s:s + 1, :] for s in range(n_seq)))
    last_f, first_b = lax.fori_loop(0, n_group, body, init)

    for s in range(n_seq):
        hfl_ref[s:s + 1, :] = last_f[s]
        hbf_ref[s:s + 1, :] = first_b[s]
    for c in range(n_chunk):
        sl = slice(c * RNN_CHUNK, (c + 1) * RNN_CHUNK)
        y_ref[sl, :] = ((uf_s[sl, :] + ub_s[sl, :]) * _gelu_tanh(xg_ref[sl, :])).astype(y_ref.dtype)


def _rglru(xr, xg, conv_w, conv_b, wa, ba, wx, bx, lam, h0f, h0b, rows, seq_len):
    n = xr.shape[0]
    n_seq = rows // seq_len
    row = pl.BlockSpec((rows, RNN_WIDTH), lambda i: (i, 0))
    full = lambda shape: pl.BlockSpec(shape, lambda i: (0,) * len(shape))
    state = pl.BlockSpec((None, n_seq, RNN_WIDTH), lambda i: (i, 0, 0))
    state_shape = jax.ShapeDtypeStruct((n // rows, n_seq, RNN_WIDTH), F32)
    return pl.pallas_call(
        functools.partial(_rglru_kernel, rows=rows, seq_len=seq_len),
        grid=(n // rows,),
        in_specs=[row, row, full((CONV_W, RNN_WIDTH)), full((1, RNN_WIDTH)),
                  full((2, RNN_WIDTH, RNN_WIDTH)), full((2, RNN_WIDTH)),
                  full((2, RNN_WIDTH, RNN_WIDTH)), full((2, RNN_WIDTH)), full((2, RNN_WIDTH)),
                  state, state],
        out_specs=[row, state, state],
        out_shape=[jax.ShapeDtypeStruct((n, RNN_WIDTH), BF16), state_shape, state_shape],
        scratch_shapes=[pltpu.VMEM((rows, RNN_WIDTH), F32)] * 5,
        compiler_params=_params(1),
        name="rglru_%d" % seq_len,
    )(xr, xg, conv_w, conv_b, wa, ba, wx, bx, lam, h0f, h0b)


def _block_diag_dense(w):
    eye = jnp.eye(RNN_BLOCKS, dtype=w.dtype)
    dense = w[:, :, :, None, :] * eye[None, :, None, :, None]
    return dense.reshape(2, RNN_WIDTH, RNN_WIDTH).astype(BF16)


SUB_TILE = 256
SUB_TILES = tuple(slice(s, s + SUB_TILE) for s in range(0, ROW_TILE, SUB_TILE))
SGU_TILE = 512
SGU_TILES = tuple(slice(s, s + SGU_TILE) for s in range(0, ROW_TILE, SGU_TILE))


def _residual_router(rows, x, o, m, lg_ref, lb_ref, rw_ref, x1_ref, xm_ref, aff_ref):
    x1 = _layer_norm(ALPHA * x + m[2:3] * o, lg_ref[...], lb_ref[...])
    x1_ref[rows, :] = x1
    xm = (x1 * (1.0 + m[4:5]) + m[3:4]).astype(BF16)
    xm_ref[rows, :] = xm
    lgt = _dot(xm, rw_ref[...])
    lane = lax.broadcasted_iota(I32, lgt.shape, 1)
    lgt = jnp.where(lane < N_EXPERTS, lgt, NEG_INF)
    ex = jnp.exp(lgt - jnp.max(lgt, axis=-1, keepdims=True))
    aff = ex / jnp.sum(ex, axis=-1, keepdims=True)
    aff_ref[rows, :] = aff[:, :N_EXPERTS]


X_RING = 3


def _ring_tile(x_hbm, x_ring, x_sem):
    i = pl.program_id(0)
    n_step = pl.num_programs(0)
    rows = x_ring.shape[1]

    def fetch(step):
        slot = lax.rem(step, X_RING)
        return pltpu.make_async_copy(x_hbm.at[pl.ds(pl.multiple_of(step * rows, rows), rows), :],
                                     x_ring.at[slot], x_sem.at[slot])

    @pl.when(i == 0)
    def _():
        for step in range(X_RING - 1):
            fetch(step).start()

    @pl.when(i + (X_RING - 1) < n_step)
    def _():
        fetch(i + (X_RING - 1)).start()

    fetch(i).wait()
    return x_ring.at[lax.rem(i, X_RING)]


def _mix_out_kernel(att_ref, rnn_ref, x_hbm, mod_ref, w_ref, lg_ref, lb_ref, rw_ref,
                    x1_ref, xm_ref, aff_ref, x_ring, x_sem):
    x_tile = _ring_tile(x_hbm, x_ring, x_sem)
    m = mod_ref[...]

    def project(rows):
        return _dot(att_ref[rows, :], w_ref[:ATTN_WIDTH, :]) + _dot(rnn_ref[rows, :], w_ref[ATTN_WIDTH:, :])

    def finish(rows, o):
        _residual_router(rows, x_tile[rows, :], o, m, lg_ref, lb_ref, rw_ref, x1_ref, xm_ref, aff_ref)

    _software_pipeline(SUB_TILES, project, finish)


def _sgu_kernel(x_ref, mod_ref, win_ref, bin_ref, sg_ref, sb_ref, spw_ref, spb_ref, w_ref, lg_ref, lb_ref, rw_ref,
                x1_ref, xm_ref, aff_ref, v_s, gated_s):
    m = mod_ref[...]

    def project(rows):
        h = x_ref[rows, :] * (1.0 + m[1:2]) + m[0:1]
        return _dot(h.astype(BF16), win_ref[...])

    def gate(rows, p):
        p = _gelu_tanh(p + bin_ref[...])
        v_s[rows, :] = _layer_norm(p[:, SGU_WIDTH:], sg_ref[...], sb_ref[...]).astype(v_s.dtype)
        for c in range(rows.start, rows.stop, CHUNK):
            local = slice(c - rows.start, c - rows.start + CHUNK)
            for g in range(SGU_GROUPS):
                cols = slice(g * SGU_GROUP_W, (g + 1) * SGU_GROUP_W)
                mixed = _dot(spw_ref[g], v_s[c:c + CHUNK, cols]) + spb_ref[:, cols]
                gated_s[c:c + CHUNK, cols] = (p[local, cols] * mixed).astype(gated_s.dtype)
        return _dot(gated_s[rows, :], w_ref[...])

    def finish(rows, o):
        _residual_router(rows, x_ref[rows, :], o, m, lg_ref, lb_ref, rw_ref, x1_ref, xm_ref, aff_ref)

    _software_pipeline(SGU_TILES, project, gate, finish)


def _mixer_out(kernel, name, acts, x, mod_l, weights, ln_g, ln_b, router_w, group0, rows_per_group,
               scratch=(), x_in_hbm=False):
    n = x.shape[0]
    row = lambda width: pl.BlockSpec((ROW_TILE, width), lambda i: (i, 0))
    full = lambda a: pl.BlockSpec(a.shape, lambda i: (0,) * a.ndim)
    x_spec = pl.BlockSpec(memory_space=pl.ANY) if x_in_hbm else row(D_MODEL)
    return pl.pallas_call(
        kernel,
        grid=(n // ROW_TILE,),
        in_specs=([row(a.shape[1]) for a in acts]
                  + [x_spec, pl.BlockSpec((None, 6, D_MODEL), _group_map(group0, rows_per_group))]
                  + [full(w) for w in weights] + [full(ln_g), full(ln_b), full(router_w)]),
        out_specs=[row(D_MODEL), row(D_MODEL), row(N_EXPERTS)],
        out_shape=[jax.ShapeDtypeStruct((n, D_MODEL), F32),
                   jax.ShapeDtypeStruct((n, D_MODEL), BF16),
                   jax.ShapeDtypeStruct((n, N_EXPERTS), F32)],
        scratch_shapes=list(scratch),
        compiler_params=_params(1),
        name=name,
    )(*acts, x, mod_l, *weights, ln_g, ln_b, router_w)


def _route_kernel(*refs, n_req, tokens, cap, place_in_kernel):
    n_blk = tokens // PREFIX_BLOCK
    if place_in_kernel:
        aff_ref, tri_ref, place_ref, posm_ref, posr_ref, affr_ref, blk_ref = refs
        n_col = place_ref.shape[2]
        aff = jnp.zeros((tokens, n_col), F32)
        for b in range(n_req):
            piece = aff_ref[b * tokens:(b + 1) * tokens, :]
            hi = piece.astype(BF16)
            rem = piece - hi.astype(F32)
            mid = rem.astype(BF16)
            low = (rem - mid.astype(F32)).astype(BF16)
            place = place_ref[b]
            aff = aff + ((_dot(hi, place) + _dot(mid, place)) + _dot(low, place))
    else:
        aff_ref, tri_ref, posm_ref, posr_ref, affr_ref, blk_ref = refs
        aff = aff_ref[...]
        n_col = aff.shape[1]

    def bisect(_, lo_hi):
        lo, hi = lo_hi
        mid = lo + ((hi - lo) >> 1)
        cnt = jnp.sum(jnp.where(aff >= pltpu.bitcast(mid, F32), 1.0, 0.0), axis=0, keepdims=True)
        ge = cnt >= cap
        return jnp.where(ge, mid, lo), jnp.where(ge, hi, mid)

    lo0 = jnp.zeros((1, n_col), I32)
    hi0 = jnp.full((1, n_col), 0x7F800000, I32)
    thr_bits, _ = lax.fori_loop(0, 31, bisect, (lo0, hi0))
    thr = pltpu.bitcast(thr_bits, F32)
    above = pltpu.bitcast(thr_bits + 1, F32)

    tri = tri_ref[...]

    def excl_prefix(x01):
        outs, before, inside = [], [], []
        off = jnp.zeros((1, n_col), F32)
        for blk in range(n_blk):
            xb = x01[blk * PREFIX_BLOCK:(blk + 1) * PREFIX_BLOCK, :]
            outs.append(_dot(tri, xb.astype(BF16)) + off)
            cnt = jnp.sum(xb, axis=0, keepdims=True)
            before.append(off)
            inside.append(cnt)
            off = off + cnt
        return (outs[0] if n_blk == 1 else jnp.concatenate(outs, axis=0)), before + inside

    gt = jnp.where(aff >= above, 1.0, 0.0)
    eq = jnp.where(aff >= thr, 1.0, 0.0) - gt
    need = cap - jnp.sum(gt, axis=0, keepdims=True)
    eq_rank, _ = excl_prefix(eq)
    sel = gt + jnp.where(eq_rank < need, eq, 0.0)
    pos, blk_rows = excl_prefix(sel)
    posm = jnp.where(sel > 0.0, pos, -1.0)
    posr_ref[...] = posm.T
    affr_ref[...] = aff.T
    for i, row in enumerate(blk_rows):
        blk_ref[i:i + 1, :] = row
    if place_in_kernel:
        posm_bf16 = posm.astype(BF16)
        for b in range(n_req):
            posm_ref[b * tokens:(b + 1) * tokens, :] = _dot_nt(posm_bf16, place_ref[b])
    else:
        posm_ref[...] = posm


PLACE_IN_KERNEL_MAX_REQ = 8


def _route_requests(aff, tri, n_req, tokens, cap):
    n_real = n_req * N_EXPERTS
    n_col = -(-n_real // LANES) * LANES
    n_blk = tokens // PREFIX_BLOCK
    place_in_kernel = n_req <= PLACE_IN_KERNEL_MAX_REQ
    whole = lambda shape: pl.BlockSpec(shape, lambda i: (0,) * len(shape))
    if place_in_kernel:
        col = np.arange(n_col)
        place = jnp.asarray(col[None, None, :] == (np.arange(n_req)[:, None, None] * N_EXPERTS
                                                   + np.arange(N_EXPERTS)[None, :, None]), BF16)
        args = (aff, tri, place)
        token_major = (n_req * tokens, N_EXPERTS)
        in_specs = [whole(token_major), whole((PREFIX_BLOCK, PREFIX_BLOCK)), whole(place.shape)]
    else:
        aff_t = aff.reshape(n_req, tokens, N_EXPERTS).transpose(1, 0, 2).reshape(tokens, n_real)
        args = (jnp.pad(aff_t, ((0, 0), (0, n_col - n_real))), tri)
        token_major = (tokens, n_col)
        in_specs = [whole(token_major), whole((PREFIX_BLOCK, PREFIX_BLOCK))]
    posm, posr, affr, blk = pl.pallas_call(
        functools.partial(_route_kernel, n_req=n_req, tokens=tokens, cap=cap, place_in_kernel=place_in_kernel),
        grid=(1,),
        in_specs=in_specs,
        out_specs=[whole(token_major), whole((n_col, tokens)), whole((n_col, tokens)), whole((2 * n_blk, n_col))],
        out_shape=[jax.ShapeDtypeStruct(token_major, F32),
                   jax.ShapeDtypeStruct((n_col, tokens), F32),
                   jax.ShapeDtypeStruct((n_col, tokens), F32),
                   jax.ShapeDtypeStruct((2 * n_blk, n_col), F32)],
        compiler_params=_params(1),
        name="route_%d" % tokens,
    )(*args)
    if not place_in_kernel:
        posm = posm[:, :n_real].reshape(tokens, n_req, N_EXPERTS).transpose(1, 0, 2).reshape(n_req * tokens, N_EXPERTS)
    blk = blk[:, :n_real].astype(I32).reshape(2, n_blk, n_req, N_EXPERTS).transpose(0, 2, 1, 3)
    expert_major = lambda a: a.reshape(n_col // N_EXPERTS, N_EXPERTS, tokens)
    return posm, expert_major(posr), expert_major(affr), blk[0], blk[1]


def _slot_gate(hit, aff_row):
    g = jnp.sum(jnp.where(hit, aff_row, 0.0), axis=-1, keepdims=True)
    return jnp.broadcast_to(g, (hit.shape[0], LANES))


def _gather_prompt_kernel(x_ref, posr_ref, affr_ref, xg_ref, g_ref):
    slot = lax.broadcasted_iota(I32, (CAP_P, SEQ), 0).astype(F32)
    for r in range(REQ_TILE):
        tokens = slice(r * SEQ, (r + 1) * SEQ)
        slots = slice(r * CAP_P, (r + 1) * CAP_P)
        posr = posr_ref[r]
        affr = affr_ref[r]
        hits = [posr[e:e + 1, :] == slot for e in range(N_EXPERTS)]
        onehot = jnp.concatenate([jnp.where(h, 1.0, 0.0) for h in hits], axis=0).astype(BF16)
        xg = _dot(onehot, x_ref[tokens, :])
        for e in range(N_EXPERTS):
            xg_ref[e, slots, :] = xg[e * CAP_P:(e + 1) * CAP_P, :].astype(xg_ref.dtype)
            g_ref[e, slots, :] = _slot_gate(hits[e], affr[e:e + 1, :])


def _gather_prompt(xm, posr, affr):
    return pl.pallas_call(
        _gather_prompt_kernel,
        grid=(BATCH // REQ_TILE,),
        in_specs=[pl.BlockSpec((REQ_TILE * SEQ, D_MODEL), lambda b: (b, 0)),
                  pl.BlockSpec((REQ_TILE, N_EXPERTS, SEQ), lambda b: (b, 0, 0)),
                  pl.BlockSpec((REQ_TILE, N_EXPERTS, SEQ), lambda b: (b, 0, 0))],
        out_specs=[pl.BlockSpec((N_EXPERTS, REQ_TILE * CAP_P, D_MODEL), lambda b: (0, b, 0)),
                   pl.BlockSpec((N_EXPERTS, REQ_TILE * CAP_P, LANES), lambda b: (0, b, 0))],
        out_shape=[jax.ShapeDtypeStruct((N_EXPERTS, SLOTS_P, D_MODEL), BF16),
                   jax.ShapeDtypeStruct((N_EXPERTS, SLOTS_P, LANES), F32)],
        compiler_params=_params(1),
        name="gather_prompt",
    )(xm, posr, affr)


TOKEN_BLOCK = PREFIX_BLOCK
N_TOKEN_BLOCKS = DEC_SEQ // TOKEN_BLOCK
SLOT_ALIGN = 16
WINDOW_SLOTS = 64
LAST_WINDOW = CAP_S - WINDOW_SLOTS
ALL_WINDOWS = N_EXPERTS * WINDOW_SLOTS


def _window_plan(before, inside):
    start = (before // SLOT_ALIGN) * SLOT_ALIGN
    span = before - start + inside
    passes = jnp.where(inside > 0, (span + WINDOW_SLOTS - 1) // WINDOW_SLOTS, 0)
    return start.reshape(-1), jnp.max(passes, axis=-1).reshape(-1)


def _window_bounds(start_ref, step, expert, k):
    lower = start_ref[step * N_EXPERTS + expert] + WINDOW_SLOTS * k
    begin = pl.multiple_of(jnp.minimum(lower, LAST_WINDOW), SLOT_ALIGN)
    return lower, begin


BLOCKS_PER_STEP = 2
STEP_TOKENS = BLOCKS_PER_STEP * TOKEN_BLOCK
STEPS_PER_REQ = N_TOKEN_BLOCKS // BLOCKS_PER_STEP


def _gather_sample_kernel(start_ref, passes_ref, x_ref, posr_ref, affr_ref, xg_ref, g_ref):
    t = pl.program_id(1)
    first_block = (pl.program_id(0) * STEPS_PER_REQ + t) * BLOCKS_PER_STEP

    @pl.when(t == 0)
    def _():
        xg_ref[...] = jnp.zeros_like(xg_ref)
        g_ref[...] = jnp.zeros_like(g_ref)

    row = lax.broadcasted_iota(I32, (WINDOW_SLOTS, TOKEN_BLOCK), 0).astype(F32)

    def pick(j, k):
        tokens = slice(j * TOKEN_BLOCK, (j + 1) * TOKEN_BLOCK)
        hits, begins = [], []
        for e in range(N_EXPERTS):
            lower, begin = _window_bounds(start_ref, first_block + j, e, k)
            pe = posr_ref[e:e + 1, tokens]
            hits.append((pe - begin.astype(F32) == row) & (pe >= lower.astype(F32)))
            begins.append(begin)
        onehot = jnp.concatenate([jnp.where(h, 1.0, 0.0) for h in hits], axis=0).astype(BF16)
        return _dot(onehot, x_ref[tokens, :]), hits, begins

    def place(j, picked):
        rows_x, hits, begins = picked
        tokens = slice(j * TOKEN_BLOCK, (j + 1) * TOKEN_BLOCK)
        for e in range(N_EXPERTS):
            win = pl.ds(begins[e], WINDOW_SLOTS)
            xg_ref[e, win, :] += rows_x[e * WINDOW_SLOTS:(e + 1) * WINDOW_SLOTS, :].astype(xg_ref.dtype)
            g_ref[e, win, :] += _slot_gate(hits[e], affr_ref[e:e + 1, tokens])

    _software_pipeline(range(BLOCKS_PER_STEP), lambda j: pick(j, 0), place)

    for j in range(BLOCKS_PER_STEP):
        def extra_pass(k, carry, j=j):
            place(j, pick(j, k))
            return carry

        lax.fori_loop(1, passes_ref[first_block + j], extra_pass, 0)


def _gather_sample(xm, posr, affr, starts, passes):
    expert_rows = pl.BlockSpec((None, N_EXPERTS, STEP_TOKENS), lambda b, t, s, p: (b, 0, t))
    grid_spec = pltpu.PrefetchScalarGridSpec(
        num_scalar_prefetch=2,
        grid=(DEC_BATCH, STEPS_PER_REQ),
        in_specs=[pl.BlockSpec((STEP_TOKENS, D_MODEL), lambda b, t, s, p: (b * STEPS_PER_REQ + t, 0)),
                  expert_rows, expert_rows],
        out_specs=[pl.BlockSpec((N_EXPERTS, CAP_S, D_MODEL), lambda b, t, s, p: (0, b, 0)),
                   pl.BlockSpec((N_EXPERTS, CAP_S, LANES), lambda b, t, s, p: (0, b, 0))],
    )
    return pl.pallas_call(
        _gather_sample_kernel,
        grid_spec=grid_spec,
        out_shape=[jax.ShapeDtypeStruct((N_EXPERTS, SLOTS_S, D_MODEL), BF16),
                   jax.ShapeDtypeStruct((N_EXPERTS, SLOTS_S, LANES), F32)],
        compiler_params=_params(2),
        name="gather_sample",
    )(starts, passes, xm, posr, affr)


FF_TILE = 512
SLOT_CHUNK = 1024


def _ffn_kernel(xp_ref, xs_ref, gp_ref, gs_ref, w1_ref, w3_ref, w2_ref, yp_ref, ys_ref, acc_ref):
    f = pl.program_id(1)
    n_f = pl.num_programs(1)
    @pl.when(f == 0)
    def _():
        acc_ref[...] = jnp.zeros_like(acc_ref)

    cast = {}

    def weight(ref):
        if id(ref) not in cast:
            cast[id(ref)] = ref[...].astype(BF16)
        return cast[id(ref)]

    def up_project(chunk):
        x_ref, c, _ = chunk
        x = x_ref[c * SLOT_CHUNK:(c + 1) * SLOT_CHUNK, :]
        h1 = _dot(x, weight(w1_ref))
        return h1, _dot(x, weight(w3_ref))

    def down_project(chunk, h):
        _, c, base = chunk
        h1, h3 = h
        hid = (h1 * jax.nn.sigmoid(h1) * h3).astype(BF16)
        rows = slice(base + c * SLOT_CHUNK, base + (c + 1) * SLOT_CHUNK)
        acc_ref[rows, :] += _dot(hid, weight(w2_ref))

    chunks = [(x_ref, c, base) for x_ref, n_slot, base in ((xp_ref, SLOTS_P, 0), (xs_ref, SLOTS_S, SLOTS_P))
              for c in range(n_slot // SLOT_CHUNK)]
    _software_pipeline(chunks, up_project, down_project)

    @pl.when(f == n_f - 1)
    def _():
        for y_ref, g_ref, n_slot, base in ((yp_ref, gp_ref, SLOTS_P, 0), (ys_ref, gs_ref, SLOTS_S, SLOTS_P)):
            for c in range(n_slot // SLOT_CHUNK):
                rows = slice(c * SLOT_CHUNK, (c + 1) * SLOT_CHUNK)
                gate = jnp.concatenate([g_ref[rows, :]] * (D_MODEL // LANES), axis=1)
                y_ref[rows, :] = (acc_ref[base + c * SLOT_CHUNK:base + (c + 1) * SLOT_CHUNK, :]
                                  * gate).astype(y_ref.dtype)


def _expert_ffn(xgp, xgs, gp, gs, w1, w3, w2, layer):
    slots = lambda n, width: pl.BlockSpec((None, n, width), lambda e, f: (e, 0, 0))
    return pl.pallas_call(
        _ffn_kernel,
        grid=(N_EXPERTS, EXPERT_FF // FF_TILE),
        in_specs=[slots(SLOTS_P, D_MODEL), slots(SLOTS_S, D_MODEL), slots(SLOTS_P, LANES), slots(SLOTS_S, LANES),
                  pl.BlockSpec((None, None, D_MODEL, FF_TILE), lambda e, f: (layer, e, 0, f)),
                  pl.BlockSpec((None, None, D_MODEL, FF_TILE), lambda e, f: (layer, e, 0, f)),
                  pl.BlockSpec((None, None, FF_TILE, D_MODEL), lambda e, f: (layer, e, f, 0))],
        out_specs=[slots(SLOTS_P, D_MODEL), slots(SLOTS_S, D_MODEL)],
        out_shape=[jax.ShapeDtypeStruct((N_EXPERTS, SLOTS_P, D_MODEL), BF16),
                   jax.ShapeDtypeStruct((N_EXPERTS, SLOTS_S, D_MODEL), BF16)],
        scratch_shapes=[pltpu.VMEM((SLOTS_P + SLOTS_S, D_MODEL), F32)],
        compiler_params=_params(2),
        name="expert_ffn",
    )(xgp, xgs, gp, gs, w1, w3, w2)


def _combine_prompt_kernel(posm_ref, y_ref, x_hbm, mod_ref, exp_ref, lg_ref, lb_ref, o_ref, x_ring, x_sem):
    x_ref = _ring_tile(x_hbm, x_ring, x_sem)
    m = mod_ref[...]
    slot = (lax.broadcasted_iota(I32, (SEQ, N_EXPERTS * CAP_P), 1) & (CAP_P - 1)).astype(F32)
    def scatter(r):
        tokens = slice(r * SEQ, (r + 1) * SEQ)
        spread = _dot(posm_ref[tokens, :].astype(BF16), exp_ref[...])
        onehot = jnp.where(spread == slot, 1.0, 0.0).astype(BF16)
        y = jnp.concatenate([y_ref[e, r * CAP_P:(r + 1) * CAP_P, :] for e in range(N_EXPERTS)], axis=0)
        return _dot(onehot, y)

    def finish(r, f):
        tokens = slice(r * SEQ, (r + 1) * SEQ)
        o_ref[tokens, :] = _layer_norm(ALPHA * x_ref[tokens, :] + m[5:6] * f, lg_ref[...], lb_ref[...])

    _software_pipeline(range(REQ_TILE), scatter, finish)


def _combine_prompt(posm, y, x1, mod_l, expand, ln_g, ln_b):
    full = lambda a: pl.BlockSpec(a.shape, lambda b: (0,) * a.ndim)
    return pl.pallas_call(
        _combine_prompt_kernel,
        grid=(BATCH // REQ_TILE,),
        in_specs=[pl.BlockSpec((REQ_TILE * SEQ, N_EXPERTS), lambda b: (b, 0)),
                  pl.BlockSpec((N_EXPERTS, REQ_TILE * CAP_P, D_MODEL), lambda b: (0, b, 0)),
                  pl.BlockSpec(memory_space=pl.ANY),
                  pl.BlockSpec((None, 6, D_MODEL), lambda b: (0, 0, 0)),
                  full(expand), full(ln_g), full(ln_b)],
        out_specs=pl.BlockSpec((REQ_TILE * SEQ, D_MODEL), lambda b: (b, 0)),
        out_shape=jax.ShapeDtypeStruct((N_PROMPT, D_MODEL), F32),
        scratch_shapes=[pltpu.VMEM((X_RING, REQ_TILE * SEQ, D_MODEL), F32), pltpu.SemaphoreType.DMA((X_RING,))],
        compiler_params=_params(1),
        name="combine_prompt",
    )(posm, y, x1, mod_l, expand, ln_g, ln_b)


def _combine_sample_kernel(start_ref, passes_ref, posm_ref, y_ref, x_ref, mod_ref, exp_ref, off_ref,
                           lg_ref, lb_ref, o_ref):
    first_block = (pl.program_id(0) * STEPS_PER_REQ + pl.program_id(1)) * BLOCKS_PER_STEP
    exp_bf16 = exp_ref[...]
    offset = off_ref[...]
    m = mod_ref[...]

    def tokens_of(j):
        return slice(j * TOKEN_BLOCK, (j + 1) * TOKEN_BLOCK)

    def one_pass(j, k):
        spread = _dot(posm_ref[tokens_of(j), :].astype(BF16), exp_bf16)
        begin_row = jnp.zeros((1, ALL_WINDOWS), F32)
        lower_row = jnp.zeros((1, ALL_WINDOWS), F32)
        windows = []
        for e in range(N_EXPERTS):
            lower, begin = _window_bounds(start_ref, first_block + j, e, k)
            lanes_e = exp_bf16[e:e + 1, :].astype(F32)
            begin_row = begin_row + begin.astype(F32) * lanes_e
            lower_row = lower_row + lower.astype(F32) * lanes_e
            windows.append(y_ref[e, pl.ds(begin, WINDOW_SLOTS), :])
        hit = (spread - begin_row == offset) & (spread >= lower_row)
        onehot = jnp.where(hit, 1.0, 0.0).astype(BF16)
        return _dot(onehot, jnp.concatenate(windows, axis=0))

    def finish(j, f):
        rows = tokens_of(j)
        o_ref[rows, :] = _layer_norm(ALPHA * x_ref[rows, :] + m[5:6] * f, lg_ref[...], lb_ref[...])

    _software_pipeline(range(BLOCKS_PER_STEP), lambda j: one_pass(j, 0), finish)

    for j in range(BLOCKS_PER_STEP):
        n_pass = passes_ref[first_block + j]

        @pl.when(n_pass > 1)
        def _(j=j, n_pass=n_pass):
            f = lax.fori_loop(1, n_pass, lambda k, f: f + one_pass(j, k), one_pass(j, 0))
            finish(j, f)


def _combine_sample(posm, y, x1, mod_l, ln_g, ln_b, starts, passes):
    lane = np.arange(ALL_WINDOWS)
    expand = jnp.asarray((lane[None, :] // WINDOW_SLOTS) == np.arange(N_EXPERTS)[:, None], BF16)
    offset = jnp.asarray((lane % WINDOW_SLOTS).reshape(1, ALL_WINDOWS), F32)
    row = lambda width: pl.BlockSpec((STEP_TOKENS, width), lambda b, t, s, p: (b * STEPS_PER_REQ + t, 0))
    full = lambda a: pl.BlockSpec(a.shape, lambda b, t, s, p: (0,) * a.ndim)
    grid_spec = pltpu.PrefetchScalarGridSpec(
        num_scalar_prefetch=2,
        grid=(DEC_BATCH, STEPS_PER_REQ),
        in_specs=[row(N_EXPERTS),
                  pl.BlockSpec((N_EXPERTS, CAP_S, D_MODEL), lambda b, t, s, p: (0, b, 0)),
                  row(D_MODEL),
                  pl.BlockSpec((None, 6, D_MODEL), lambda b, t, s, p: (1 + b, 0, 0)),
                  full(expand), full(offset), full(ln_g), full(ln_b)],
        out_specs=row(D_MODEL),
    )
    return pl.pallas_call(
        _combine_sample_kernel,
        grid_spec=grid_spec,
        out_shape=jax.ShapeDtypeStruct((N_SAMPLE, D_MODEL), F32),
        compiler_params=_params(2),
        name="combine_sample",
    )(starts, passes, posm, y, x1, mod_l, expand, offset, ln_g, ln_b)


def _moe(prompt, sample, mod_l, w1, w3, w2, layer, ln_g, ln_b, tri, expand):
    x1_p, xm_p, aff_p = prompt
    x1_s, xm_s, aff_s = sample
    posm_p, posr_p, affr_p, _, _ = _route_requests(aff_p, tri, BATCH, SEQ, CAP_P)
    posm_s, posr_s, affr_s, before_s, inside_s = _route_requests(aff_s, tri, DEC_BATCH, DEC_SEQ, CAP_S)
    starts, passes = _window_plan(before_s, inside_s)
    xg_p, g_p = _gather_prompt(xm_p, posr_p, affr_p)
    xg_s, g_s = _gather_sample(xm_s, posr_s, affr_s, starts, passes)
    y_p, y_s = _expert_ffn(xg_p, xg_s, g_p, g_s, w1, w3, w2, layer)
    x2_p = _combine_prompt(posm_p, y_p, x1_p, mod_l, expand, ln_g, ln_b)
    x2_s = _combine_sample(posm_s, y_s, x1_s, mod_l, ln_g, ln_b, starts, passes)
    return x2_p, x2_s


def kernel(x_prompt, x_sample, cache_k, cache_v, state_rglru, c, c_ctx, mod_w, mod_b, ln_mix_g, ln_mix_b, ln_ffn_g, ln_ffn_b, ab_in_w, attn_sink, rnn_conv_w, rnn_conv_b, lru_wa, lru_ba, lru_wx, lru_bx, lru_lambda, ab_out_w, sgu_in_w, sgu_in_b, sgu_ln_g, sgu_ln_b, sgu_spatial_w, sgu_spatial_b, sgu_out_w, router_w, moe_w1, moe_w3, moe_w2):
    xp = x_prompt.reshape(N_PROMPT, D_MODEL)
    xs = x_sample.reshape(N_SAMPLE, D_MODEL)

    cvec = jnp.concatenate([c_ctx[None], c, jnp.zeros((SUBLANES - 1 - DEC_BATCH, D_MODEL), F32)], axis=0)
    mod = _modulation(cvec, mod_w, mod_b).reshape(DEPTH, SUBLANES, 6, D_MODEL)

    idx = np.arange(PREFIX_BLOCK)
    tri = jnp.asarray(idx[None, :] < idx[:, None], BF16)
    lane = np.arange(N_EXPERTS * CAP_P)
    expand = jnp.asarray((lane[None, :] // CAP_P) == np.arange(N_EXPERTS)[:, None], BF16)
    rope_tables = _rope_tables()
    vec = lambda a: a.reshape(1, -1)
    router_pad = lambda w: jnp.pad(w, ((0, 0), (0, LANES - N_EXPERTS))).astype(BF16)

    new_k = new_v = new_state = None
    for l in range(DEPTH):
        e = l // 2
        mod_l = mod[l]
        rw = router_pad(router_w[l])
        lg, lb = vec(ln_mix_g[l]), vec(ln_mix_b[l])
        if l % 2 == 0:
            w_in = ab_in_w[e].astype(BF16)
            w_out = ab_out_w[e].astype(BF16)
            wa = _block_diag_dense(lru_wa[e])
            wx = _block_diag_dense(lru_wx[e])
            rnn_w = (rnn_conv_w[e], vec(rnn_conv_b[e]), wa, lru_ba[e], wx, lru_bx[e], lru_lambda[e])

            q, k, v, xr, xg, new_k, new_v = _ab_in(xp, mod_l, w_in, 0, N_PROMPT, None)
            att = _ctx_attention(q, k, v, attn_sink[e])
            rows = 8 * SEQ
            zeros = jnp.zeros((N_PROMPT // rows, rows // SEQ, RNN_WIDTH), F32)
            rnn, hf_last, hb_first = _rglru(xr, xg, *rnn_w, zeros, zeros, rows, SEQ)
            new_state = jnp.stack([hf_last.reshape(BATCH, RNN_WIDTH), hb_first.reshape(BATCH, RNN_WIDTH)], axis=1)
            x_ring = (pltpu.VMEM((X_RING, ROW_TILE, D_MODEL), F32), pltpu.SemaphoreType.DMA((X_RING,)))
            prompt = _mixer_out(_mix_out_kernel, "mix_out", (att, rnn), xp, mod_l, (w_out,),
                                lg, lb, rw, 0, N_PROMPT, x_ring, x_in_hbm=True)

            q, k, v, xr, xg = _ab_in(xs, mod_l, w_in, 1, DEC_SEQ, rope_tables)
            att = _lat_attention(q, k, v, cache_k[:, e].reshape(DEC_BATCH * PAST_LEN, KV_WIDTH),
                                 cache_v[:, e].reshape(DEC_BATCH * PAST_LEN, KV_WIDTH), attn_sink[e])
            h0 = state_rglru[:, e]
            rnn, _, _ = _rglru(xr, xg, *rnn_w, h0[:, 0][:, None, :], h0[:, 1][:, None, :], DEC_SEQ, DEC_SEQ)
            sample = _mixer_out(_mix_out_kernel, "mix_out", (att, rnn), xs, mod_l, (w_out,),
                                lg, lb, rw, 1, DEC_SEQ, x_ring, x_in_hbm=True)
        else:
            w_in = sgu_in_w[e].astype(BF16)
            w_out = sgu_out_w[e].astype(BF16)
            spw = sgu_spatial_w[e].astype(BF16)
            spb = jnp.repeat(sgu_spatial_b[e].T, SGU_GROUP_W, axis=1)
            sgu_w = (w_in, vec(sgu_in_b[e]), vec(sgu_ln_g[e]), vec(sgu_ln_b[e]), spw, spb, w_out)
            scratch = (pltpu.VMEM((ROW_TILE, SGU_WIDTH), BF16), pltpu.VMEM((ROW_TILE, SGU_WIDTH), BF16))
            prompt = _mixer_out(_sgu_kernel, "sgu", (), xp, mod_l, sgu_w, lg, lb, rw, 0, N_PROMPT, scratch)
            sample = _mixer_out(_sgu_kernel, "sgu", (), xs, mod_l, sgu_w, lg, lb, rw, 1, DEC_SEQ, scratch)

        xp, xs = _moe(prompt, sample, mod_l, moe_w1, moe_w3, moe_w2, l,
                      vec(ln_ffn_g[l]), vec(ln_ffn_b[l]), tri, expand)

    return (xp.reshape(BATCH, SEQ, D_MODEL),
            xs.reshape(DEC_BATCH, DEC_SEQ, D_MODEL),
            new_k,
            new_v,
            new_state.reshape(BATCH, 1, 2, RNN_WIDTH))
```

```python
import functools

import numpy as np
import jax
import jax.numpy as jnp
from jax import lax
from jax.experimental import pallas as pl
from jax.experimental.pallas import tpu as pltpu

F32 = jnp.float32
BF16 = jnp.bfloat16
I32 = jnp.int32

D_MODEL = 1024
BATCH = 32
SEQ = 256
DEPTH = 2
DEC_BATCH = 4
DEC_SEQ = 2048
PAST_LEN = 256
GRID_W = 64
HEAD_DIM = 128
N_Q_HEADS = 4
N_KV_HEADS = 2
Q_PER_KV = N_Q_HEADS // N_KV_HEADS
ATTN_WIDTH = N_Q_HEADS * HEAD_DIM
KV_WIDTH = N_KV_HEADS * HEAD_DIM
WINDOW = 128
BLOCK = 128
ATTN_SCALE = HEAD_DIM ** -0.5
ROPE_BASE = 10000.0
NEG_INF = -1e30
RNN_WIDTH = 512
RNN_BLOCKS = 8
RNN_BLOCK_W = RNN_WIDTH // RNN_BLOCKS
CONV_W = 4
CONV_PAD_LEFT = 2
LRU_C = 8.0
AB_IN_WIDTH = ATTN_WIDTH + 2 * KV_WIDTH + 2 * RNN_WIDTH
CHUNK = 128
SGU_WIDTH = D_MODEL
SGU_GROUPS = 8
SGU_GROUP_W = SGU_WIDTH // SGU_GROUPS
N_EXPERTS = 16
EXPERT_FF = 2048
EC_FACTOR = 2
ALPHA = (2 * DEPTH) ** 0.25
LN_EPS = 1e-6

N_PROMPT = BATCH * SEQ
N_SAMPLE = DEC_BATCH * DEC_SEQ
CAP_P = EC_FACTOR * SEQ // N_EXPERTS
CAP_S = EC_FACTOR * DEC_SEQ // N_EXPERTS
SLOTS_P = BATCH * CAP_P
SLOTS_S = DEC_BATCH * CAP_S

LANES = 128
SUBLANES = 8
ROW_TILE = 1024
REQ_TILE = 4
PREFIX_BLOCK = 256
VMEM_LIMIT = 56 * 1024 * 1024


def _params(n_axes=1):
    return pltpu.CompilerParams(dimension_semantics=("arbitrary",) * n_axes,
                                vmem_limit_bytes=VMEM_LIMIT)


def _layer_norm(x, g, b):
    mu = jnp.mean(x, axis=-1, keepdims=True)
    xc = x - mu
    var = jnp.mean(xc * xc, axis=-1, keepdims=True)
    return xc * lax.rsqrt(var + LN_EPS) * g + b


_LOG2_E = 1.4426950408889634
_GELU_K0 = -2.0 * 0.7978845608028654 * _LOG2_E
_GELU_K1 = _GELU_K0 * 0.044715


def _gelu_tanh(x):
    return x / (1.0 + jnp.exp2(x * (_GELU_K0 + _GELU_K1 * (x * x))))


def _sigmoid(z):
    return 0.5 + 0.5 * jnp.tanh(0.5 * z)


def _dot(a, b):
    return jnp.dot(a, b, preferred_element_type=F32)


def _dot_nt(a, b):
    return lax.dot_general(a, b, (((1,), (1,)), ((), ())), preferred_element_type=F32)


def _software_pipeline(items, first_stage, *later_stages):
    items = list(items)
    stages = (first_stage,) + later_stages
    carried = {}
    for t in range(len(items) + len(stages) - 1):
        for s, stage in enumerate(stages):
            i = t - s
            if 0 <= i < len(items):
                carried[i] = stage(items[i]) if s == 0 else stage(items[i], carried[i])


def _mod_kernel(c_ref, w_ref, b_ref, o_ref):
    c = c_ref[...]
    s = c * jax.nn.sigmoid(c)
    o_ref[...] = _dot(s.astype(BF16), w_ref[...].astype(BF16)) + b_ref[...]


def _modulation(cvec8, mod_w, mod_b):
    n_col = 6 * D_MODEL // D_MODEL
    return pl.pallas_call(
        _mod_kernel,
        grid=(DEPTH, n_col),
        in_specs=[pl.BlockSpec((SUBLANES, D_MODEL), lambda l, j: (0, 0)),
                  pl.BlockSpec((None, D_MODEL, D_MODEL), lambda l, j: (l, 0, j)),
                  pl.BlockSpec((None, 1, D_MODEL), lambda l, j: (l, 0, j))],
        out_specs=pl.BlockSpec((None, SUBLANES, D_MODEL), lambda l, j: (l, 0, j)),
        out_shape=jax.ShapeDtypeStruct((DEPTH, SUBLANES, 6 * D_MODEL), F32),
        compiler_params=_params(2),
        name="adaln_modulation",
    )(cvec8, mod_w, mod_b.reshape(DEPTH, 1, 6 * D_MODEL))


def _group_map(group0, rows_per_group):
    tiles_per_group = rows_per_group // ROW_TILE
    return lambda i: (group0 + i // tiles_per_group, 0, 0)


def _rope(t, cos, sin_signed):
    lane = lax.broadcasted_iota(I32, t.shape, 1)
    swapped = jnp.where((lane & 1) == 0, pltpu.roll(t, HEAD_DIM - 1, 1), pltpu.roll(t, 1, 1))
    return t * cos + swapped * sin_signed


def _ab_in_kernel(*refs, rope):
    if rope:
        x_ref, mod_ref, w_ref, cos_ref, sin_ref, q_ref, k_ref, v_ref, xr_ref, xg_ref = refs
    else:
        x_ref, mod_ref, w_ref, q_ref, k_ref, v_ref, xr_ref, xg_ref, ck_ref, cv_ref = refs
    m = mod_ref[...]
    h = x_ref[...] * (1.0 + m[1:2]) + m[0:1]
    p = _dot(h.astype(BF16), w_ref[...])
    q = p[:, :ATTN_WIDTH] * (ATTN_SCALE * _LOG2_E)
    k = p[:, ATTN_WIDTH:ATTN_WIDTH + KV_WIDTH]
    v = p[:, ATTN_WIDTH + KV_WIDTH:ATTN_WIDTH + 2 * KV_WIDTH]
    if rope:
        cos = cos_ref[...]
        sin = sin_ref[...]
        q = jnp.concatenate([_rope(q[:, i * HEAD_DIM:(i + 1) * HEAD_DIM], cos, sin)
                             for i in range(N_Q_HEADS)], axis=1)
        k = jnp.concatenate([_rope(k[:, i * HEAD_DIM:(i + 1) * HEAD_DIM], cos, sin)
                             for i in range(N_KV_HEADS)], axis=1)
    q_ref[...] = q.astype(q_ref.dtype)
    k_ref[...] = k.astype(k_ref.dtype)
    v_ref[...] = v.astype(v_ref.dtype)
    xr_ref[...] = p[:, ATTN_WIDTH + 2 * KV_WIDTH:ATTN_WIDTH + 2 * KV_WIDTH + RNN_WIDTH]
    xg_ref[...] = p[:, ATTN_WIDTH + 2 * KV_WIDTH + RNN_WIDTH:]
    if not rope:
        for r in range(ROW_TILE // SEQ):
            for i in range(N_KV_HEADS):
                ck_ref[r, 0, :, i, :] = k[r * SEQ:(r + 1) * SEQ, _head_cols(i)]
                cv_ref[r, 0, :, i, :] = v[r * SEQ:(r + 1) * SEQ, _head_cols(i)]


def _ab_in(x, mod_l, w_bf16, group0, rows_per_group, rope_tables):
    n = x.shape[0]
    rope = rope_tables is not None
    row = lambda width: pl.BlockSpec((ROW_TILE, width), lambda i: (i, 0))
    in_specs = [row(D_MODEL),
                pl.BlockSpec((None, 6, D_MODEL), _group_map(group0, rows_per_group)),
                pl.BlockSpec((D_MODEL, AB_IN_WIDTH), lambda i: (0, 0))]
    args = [x, mod_l, w_bf16]
    out_specs = [row(ATTN_WIDTH), row(KV_WIDTH), row(KV_WIDTH), row(RNN_WIDTH), row(RNN_WIDTH)]
    out_shape = [jax.ShapeDtypeStruct((n, ATTN_WIDTH), BF16),
                 jax.ShapeDtypeStruct((n, KV_WIDTH), BF16),
                 jax.ShapeDtypeStruct((n, KV_WIDTH), BF16),
                 jax.ShapeDtypeStruct((n, RNN_WIDTH), F32),
                 jax.ShapeDtypeStruct((n, RNN_WIDTH), F32)]
    if rope:
        tiles_per_seq = DEC_SEQ // ROW_TILE
        in_specs += [pl.BlockSpec((ROW_TILE, HEAD_DIM), lambda i: (i % tiles_per_seq, 0))] * 2
        args += list(rope_tables)
    else:
        req = ROW_TILE // SEQ
        cache = pl.BlockSpec((req, 1, SEQ, N_KV_HEADS, HEAD_DIM), lambda i: (i, 0, 0, 0, 0))
        out_specs += [cache, cache]
        out_shape += [jax.ShapeDtypeStruct((n // SEQ, 1, SEQ, N_KV_HEADS, HEAD_DIM), F32)] * 2
    return pl.pallas_call(
        functools.partial(_ab_in_kernel, rope=rope),
        grid=(n // ROW_TILE,),
        in_specs=in_specs,
        out_specs=out_specs,
        out_shape=out_shape,
        compiler_params=_params(1),
        name="ab_in_rope" if rope else "ab_in",
    )(*args)


def _rope_tables():
    rows = DEC_SEQ // GRID_W
    row = np.repeat(np.arange(rows, dtype=np.float32), GRID_W)
    col = np.tile(np.arange(GRID_W, dtype=np.float32), rows)
    n_freq = HEAD_DIM // 4
    freqs = np.float32(ROPE_BASE) ** (-np.arange(n_freq, dtype=np.float32) / np.float32(n_freq))
    ang = np.concatenate([row[:, None] * freqs, col[:, None] * freqs], axis=-1)
    cos = np.repeat(np.cos(ang), 2, axis=-1)
    sin = np.repeat(np.sin(ang), 2, axis=-1)
    sign = np.tile(np.array([-1.0, 1.0], np.float32), HEAD_DIM // 2)
    return jnp.asarray(cos, F32), jnp.asarray(sin * sign, F32)


def _sink_attention_head(s_list, v_list, sink):
    sink = sink * _LOG2_E
    m = sink
    for s in s_list:
        m = jnp.maximum(m, jnp.max(s, axis=-1, keepdims=True))
    p_list = [jnp.exp2(s - m) for s in s_list]
    denom = jnp.exp2(sink - m)
    for p in p_list:
        denom = denom + jnp.sum(p, axis=-1, keepdims=True)
    out = None
    for p, v in zip(p_list, v_list):
        o = _dot(p.astype(BF16), v)
        out = o if out is None else out + o
    return out * (1.0 / denom)


def _head_cols(h):
    return slice(h * HEAD_DIM, (h + 1) * HEAD_DIM)


def _ctx_attn_kernel(sink_ref, q_ref, k_ref, v_ref, o_ref):
    def scores(item):
        r, h = item
        rows = slice(r * SEQ, (r + 1) * SEQ)
        kh = k_ref[rows, _head_cols(h // Q_PER_KV)].astype(BF16)
        return _dot_nt(q_ref[rows, _head_cols(h)], kh)

    def finish(item, s):
        r, h = item
        rows = slice(r * SEQ, (r + 1) * SEQ)
        vh = v_ref[rows, _head_cols(h // Q_PER_KV)].astype(BF16)
        o_ref[rows, _head_cols(h)] = _sink_attention_head([s], [vh], sink_ref[h]).astype(o_ref.dtype)

    items = [(r, h) for r in range(REQ_TILE) for h in range(N_Q_HEADS)]
    _software_pipeline(items, scores, finish)


def _ctx_attention(q, k, v, sink):
    seq = lambda width: pl.BlockSpec((REQ_TILE * SEQ, width), lambda b: (b, 0))
    return pl.pallas_call(
        _ctx_attn_kernel,
        grid=(BATCH // REQ_TILE,),
        in_specs=[pl.BlockSpec(memory_space=pltpu.SMEM), seq(ATTN_WIDTH), seq(KV_WIDTH), seq(KV_WIDTH)],
        out_specs=seq(ATTN_WIDTH),
        out_shape=jax.ShapeDtypeStruct((N_PROMPT, ATTN_WIDTH), BF16),
        compiler_params=_params(1),
        name="context_attention",
    )(sink, q, k, v)


LAT_Q = 256


def _lat_attn_kernel(sink_ref, q_ref, kp_ref, kc_ref, kn_ref, vp_ref, vc_ref, vn_ref,
                     kx_ref, vx_ref, o_ref):
    n = pl.program_id(1)
    nb = pl.num_programs(1)
    q = q_ref[...]
    kw = jnp.concatenate([kp_ref[...], kc_ref[...], kn_ref[...]], axis=0)
    vw = jnp.concatenate([vp_ref[...], vc_ref[...], vn_ref[...]], axis=0)
    kx = kx_ref[...].astype(BF16)
    vx = vx_ref[...].astype(BF16)
    n_key = LAT_Q + 2 * WINDOW
    qi = lax.broadcasted_iota(I32, (LAT_Q, n_key), 0)
    kj = lax.broadcasted_iota(I32, (LAT_Q, n_key), 1)
    rel = kj - qi
    valid = (rel >= 0) & (rel <= 2 * WINDOW)
    valid = valid & ((n > 0) | (kj >= WINDOW)) & ((n < nb - 1) | (kj < LAT_Q + WINDOW))
    def scores(h):
        sl = _head_cols(h // Q_PER_KV)
        qh = q[:, _head_cols(h)]
        s_ctx = _dot_nt(qh, kx[:, sl])
        s_win = jnp.where(valid, _dot_nt(qh, kw[:, sl]), NEG_INF)
        return s_ctx, s_win

    def finish(h, s):
        sl = _head_cols(h // Q_PER_KV)
        o_ref[:, _head_cols(h)] = _sink_attention_head(list(s), [vx[:, sl], vw[:, sl]],
                                                       sink_ref[h]).astype(o_ref.dtype)

    _software_pipeline(range(N_Q_HEADS), scores, finish)


def _lat_attention(q, k, v, k_ctx, v_ctx, sink):
    nb = DEC_SEQ // LAT_Q
    nw = DEC_SEQ // WINDOW
    per = LAT_Q // WINDOW
    cur = lambda b, n: (b * nb + n, 0)
    prev = lambda b, n: (b * nw + jnp.maximum(n * per - 1, 0), 0)
    nxt = lambda b, n: (b * nw + jnp.minimum((n + 1) * per, nw - 1), 0)
    tile = lambda width: pl.BlockSpec((LAT_Q, width), cur)
    edge = lambda imap: pl.BlockSpec((WINDOW, KV_WIDTH), imap)
    ctx = pl.BlockSpec((PAST_LEN, KV_WIDTH), lambda b, n: (b, 0))
    return pl.pallas_call(
        _lat_attn_kernel,
        grid=(DEC_BATCH, nb),
        in_specs=[pl.BlockSpec(memory_space=pltpu.SMEM), tile(ATTN_WIDTH),
                  edge(prev), tile(KV_WIDTH), edge(nxt),
                  edge(prev), tile(KV_WIDTH), edge(nxt), ctx, ctx],
        out_specs=tile(ATTN_WIDTH),
        out_shape=jax.ShapeDtypeStruct((N_SAMPLE, ATTN_WIDTH), BF16),
        compiler_params=_params(2),
        name="latent_attention",
    )(sink, q, k, k, k, v, v, v, k_ctx, v_ctx)


RNN_CHUNK = 256


def _rglru_kernel(xr_ref, xg_ref, cw_ref, cb_ref, wa_ref, ba_ref, wx_ref, bx_ref, lam_ref,
                  h0f_ref, h0b_ref, y_ref, hfl_ref, hbf_ref, xc_s, af_s, ab_s, uf_s, ub_s, *, rows, seq_len):
    n_seq = rows // seq_len
    n_chunk = rows // RNN_CHUNK
    cw = cw_ref[...]
    cb = cb_ref[...]
    zeros_halo = jnp.zeros((SUBLANES, RNN_WIDTH), F32)
    row8 = lax.broadcasted_iota(I32, (SUBLANES, RNN_WIDTH), 0)

    def conv_chunk(c):
        r0 = c * RNN_CHUNK
        first = r0 % seq_len == 0
        last = (r0 + RNN_CHUNK) % seq_len == 0
        before = zeros_halo if first else xr_ref[r0 - SUBLANES:r0, :]
        after = zeros_halo if last else xr_ref[r0 + RNN_CHUNK:r0 + RNN_CHUNK + SUBLANES, :]
        win = jnp.concatenate([before, xr_ref[r0:r0 + RNN_CHUNK, :], after], axis=0)
        xc = cb
        n_win = RNN_CHUNK + 2 * SUBLANES
        for i in range(CONV_W):
            shift = (CONV_PAD_LEFT - i) % n_win
            rolled = win if shift == 0 else pltpu.roll(win, shift, 0)
            xc = xc + rolled[SUBLANES:SUBLANES + RNN_CHUNK, :] * cw[i:i + 1, :]
        return xc

    def group_scan(a, u, reverse):
        for k in (1, 2, 4):
            if reverse:
                shift, ok = SUBLANES - k, row8 < SUBLANES - k
            else:
                shift, ok = k, row8 >= k
            a_nb = jnp.where(ok, pltpu.roll(a, shift, 0), 1.0)
            u_nb = jnp.where(ok, pltpu.roll(u, shift, 0), 0.0)
            u = a * u_nb + u
            a = a * a_nb
        return a, u

    for c in range(n_chunk):
        xc_s[c * RNN_CHUNK:(c + 1) * RNN_CHUNK, :] = conv_chunk(c)

    for d, (a_s, u_s) in enumerate(((af_s, uf_s), (ab_s, ub_s))):
        neg = -lam_ref[d:d + 1, :]
        softplus = jnp.maximum(neg, 0.0) + jnp.log1p(jnp.exp(-jnp.abs(neg)))
        decay = -LRU_C * softplus
        wa = wa_ref[d]
        wx = wx_ref[d]
        ba = ba_ref[d:d + 1, :]
        bx = bx_ref[d:d + 1, :]
        for c in range(n_chunk):
            xc = xc_s[c * RNN_CHUNK:(c + 1) * RNN_CHUNK, :]
            xcb = xc.astype(BF16)
            r = _sigmoid(_dot(xcb, wa) + ba)
            i = _sigmoid(_dot(xcb, wx) + bx)
            log_a = r * decay
            a = jnp.exp(log_a)
            a_s[c * RNN_CHUNK:(c + 1) * RNN_CHUNK, :] = a
            one_minus_a2 = -jnp.tanh(log_a) * (a * a + 1.0)
            u_s[c * RNN_CHUNK:(c + 1) * RNN_CHUNK, :] = jnp.sqrt(one_minus_a2) * (i * xc)

    n_group = seq_len // SUBLANES

    def body(g, carries):
        fwd, bwd = carries
        new_f, new_b = [], []
        for s in range(n_seq):
            rf = pl.multiple_of(s * seq_len + g * SUBLANES, SUBLANES)
            a, u = group_scan(af_s[pl.ds(rf, SUBLANES), :], uf_s[pl.ds(rf, SUBLANES), :], False)
            h = u + a * fwd[s]
            uf_s[pl.ds(rf, SUBLANES), :] = h
            new_f.append(h[SUBLANES - 1:SUBLANES, :])
            rb = pl.multiple_of(s * seq_len + (n_group - 1 - g) * SUBLANES, SUBLANES)
            a, u = group_scan(ab_s[pl.ds(rb, SUBLANES), :], ub_s[pl.ds(rb, SUBLANES), :], True)
            h = u + a * bwd[s]
            ub_s[pl.ds(rb, SUBLANES), :] = h
            new_b.append(h[0:1, :])
        return tuple(new_f), tuple(new_b)

    init = (tuple(h0f_ref[s:s + 1, :] for s in range(n_seq)), tuple(h0b_ref[s:s + 1, :] for s in range(n_seq)))
    last_f, first_b = lax.fori_loop(0, n_group, body, init)

    for s in range(n_seq):
        hfl_ref[s:s + 1, :] = last_f[s]
        hbf_ref[s:s + 1, :] = first_b[s]
    for c in range(n_chunk):
        sl = slice(c * RNN_CHUNK, (c + 1) * RNN_CHUNK)
        y_ref[sl, :] = ((uf_s[sl, :] + ub_s[sl, :]) * _gelu_tanh(xg_ref[sl, :])).astype(y_ref.dtype)


def _rglru(xr, xg, conv_w, conv_b, wa, ba, wx, bx, lam, h0f, h0b, rows, seq_len):
    n = xr.shape[0]
    n_seq = rows // seq_len
    row = pl.BlockSpec((rows, RNN_WIDTH), lambda i: (i, 0))
    full = lambda shape: pl.BlockSpec(shape, lambda i: (0,) * len(shape))
    state = pl.BlockSpec((None, n_seq, RNN_WIDTH), lambda i: (i, 0, 0))
    state_shape = jax.ShapeDtypeStruct((n // rows, n_seq, RNN_WIDTH), F32)
    return pl.pallas_call(
        functools.partial(_rglru_kernel, rows=rows, seq_len=seq_len),
        grid=(n // rows,),
        in_specs=[row, row, full((CONV_W, RNN_WIDTH)), full((1, RNN_WIDTH)),
                  full((2, RNN_WIDTH, RNN_WIDTH)), full((2, RNN_WIDTH)),
                  full((2, RNN_WIDTH, RNN_WIDTH)), full((2, RNN_WIDTH)), full((2, RNN_WIDTH)),
                  state, state],
        out_specs=[row, state, state],
        out_shape=[jax.ShapeDtypeStruct((n, RNN_WIDTH), BF16), state_shape, state_shape],
        scratch_shapes=[pltpu.VMEM((rows, RNN_WIDTH), F32)] * 5,
        compiler_params=_params(1),
        name="rglru_%d" % seq_len,
    )(xr, xg, conv_w, conv_b, wa, ba, wx, bx, lam, h0f, h0b)


def _block_diag_dense(w):
    eye = jnp.eye(RNN_BLOCKS, dtype=w.dtype)
    dense = w[:, :, :, None, :] * eye[None, :, None, :, None]
    return dense.reshape(2, RNN_WIDTH, RNN_WIDTH).astype(BF16)


SUB_TILE = 256
SUB_TILES = tuple(slice(s, s + SUB_TILE) for s in range(0, ROW_TILE, SUB_TILE))
SGU_TILE = 512
SGU_TILES = tuple(slice(s, s + SGU_TILE) for s in range(0, ROW_TILE, SGU_TILE))


def _residual_router(rows, x, o, m, lg_ref, lb_ref, rw_ref, x1_ref, xm_ref, aff_ref):
    x1 = _layer_norm(ALPHA * x + m[2:3] * o, lg_ref[...], lb_ref[...])
    x1_ref[rows, :] = x1
    xm = (x1 * (1.0 + m[4:5]) + m[3:4]).astype(BF16)
    xm_ref[rows, :] = xm
    lgt = _dot(xm, rw_ref[...])
    lane = lax.broadcasted_iota(I32, lgt.shape, 1)
    lgt = jnp.where(lane < N_EXPERTS, lgt, NEG_INF)
    ex = jnp.exp(lgt - jnp.max(lgt, axis=-1, keepdims=True))
    aff = ex / jnp.sum(ex, axis=-1, keepdims=True)
    aff_ref[rows, :] = aff[:, :N_EXPERTS]


X_RING = 3


def _ring_tile(x_hbm, x_ring, x_sem, i=None, n_step=None):
    if i is None:
        i = pl.program_id(0)
        n_step = pl.num_programs(0)
    rows = x_ring.shape[1]

    def fetch(step):
        slot = lax.rem(step, X_RING)
        return pltpu.make_async_copy(x_hbm.at[pl.ds(pl.multiple_of(step * rows, rows), rows), :],
                                     x_ring.at[slot], x_sem.at[slot])

    @pl.when(i == 0)
    def _():
        for step in range(X_RING - 1):
            fetch(step).start()

    @pl.when(i + (X_RING - 1) < n_step)
    def _():
        fetch(i + (X_RING - 1)).start()

    fetch(i).wait()
    return x_ring.at[lax.rem(i, X_RING)]


def _mix_out_kernel(att_ref, rnn_ref, x_hbm, mod_ref, w_ref, lg_ref, lb_ref, rw_ref,
                    x1_ref, xm_ref, aff_ref, x_ring, x_sem):
    x_tile = _ring_tile(x_hbm, x_ring, x_sem)
    m = mod_ref[...]

    def project(rows):
        return _dot(att_ref[rows, :], w_ref[:ATTN_WIDTH, :]) + _dot(rnn_ref[rows, :], w_ref[ATTN_WIDTH:, :])

    def finish(rows, o):
        _residual_router(rows, x_tile[rows, :], o, m, lg_ref, lb_ref, rw_ref, x1_ref, xm_ref, aff_ref)

    _software_pipeline(SUB_TILES, project, finish)


def _sgu_kernel(x_ref, mod_ref, win_ref, bin_ref, sg_ref, sb_ref, spw_ref, spb_ref, w_ref, lg_ref, lb_ref, rw_ref,
                x1_ref, xm_ref, aff_ref, v_s, gated_s):
    m = mod_ref[...]

    def project(rows):
        h = x_ref[rows, :] * (1.0 + m[1:2]) + m[0:1]
        return _dot(h.astype(BF16), win_ref[...])

    def gate(rows, p):
        p = _gelu_tanh(p + bin_ref[...])
        v_s[rows, :] = _layer_norm(p[:, SGU_WIDTH:], sg_ref[...], sb_ref[...]).astype(v_s.dtype)
        for c in range(rows.start, rows.stop, CHUNK):
            local = slice(c - rows.start, c - rows.start + CHUNK)
            for g in range(SGU_GROUPS):
                cols = slice(g * SGU_GROUP_W, (g + 1) * SGU_GROUP_W)
                mixed = _dot(spw_ref[g], v_s[c:c + CHUNK, cols]) + spb_ref[:, cols]
                gated_s[c:c + CHUNK, cols] = (p[local, cols] * mixed).astype(gated_s.dtype)
        return _dot(gated_s[rows, :], w_ref[...])

    def finish(rows, o):
        _residual_router(rows, x_ref[rows, :], o, m, lg_ref, lb_ref, rw_ref, x1_ref, xm_ref, aff_ref)

    _software_pipeline(SGU_TILES, project, gate, finish)


def _mixer_out(kernel, name, acts, x, mod_l, weights, ln_g, ln_b, router_w, group0, rows_per_group,
               scratch=(), x_in_hbm=False):
    n = x.shape[0]
    row = lambda width: pl.BlockSpec((ROW_TILE, width), lambda i: (i, 0))
    full = lambda a: pl.BlockSpec(a.shape, lambda i: (0,) * a.ndim)
    x_spec = pl.BlockSpec(memory_space=pl.ANY) if x_in_hbm else row(D_MODEL)
    return pl.pallas_call(
        kernel,
        grid=(n // ROW_TILE,),
        in_specs=([row(a.shape[1]) for a in acts]
                  + [x_spec, pl.BlockSpec((None, 6, D_MODEL), _group_map(group0, rows_per_group))]
                  + [full(w) for w in weights] + [full(ln_g), full(ln_b), full(router_w)]),
        out_specs=[row(D_MODEL), row(D_MODEL), row(N_EXPERTS)],
        out_shape=[jax.ShapeDtypeStruct((n, D_MODEL), F32),
                   jax.ShapeDtypeStruct((n, D_MODEL), BF16),
                   jax.ShapeDtypeStruct((n, N_EXPERTS), F32)],
        scratch_shapes=list(scratch),
        compiler_params=_params(1),
        name=name,
    )(*acts, x, mod_l, *weights, ln_g, ln_b, router_w)


def _route_kernel(*refs, n_req, tokens, cap, place_in_kernel):
    n_blk = tokens // PREFIX_BLOCK
    if place_in_kernel:
        aff_ref, tri_ref, place_ref, posm_ref, posr_ref, affr_ref, blk_ref = refs
        n_col = place_ref.shape[2]
        aff = jnp.zeros((tokens, n_col), F32)
        for b in range(n_req):
            piece = aff_ref[b * tokens:(b + 1) * tokens, :]
            hi = piece.astype(BF16)
            rem = piece - hi.astype(F32)
            mid = rem.astype(BF16)
            low = (rem - mid.astype(F32)).astype(BF16)
            place = place_ref[b]
            aff = aff + ((_dot(hi, place) + _dot(mid, place)) + _dot(low, place))
    else:
        aff_ref, tri_ref, posm_ref, posr_ref, affr_ref, blk_ref = refs
        aff = aff_ref[...]
        n_col = aff.shape[1]

    def bisect(_, lo_hi):
        lo, hi = lo_hi
        mid = lo + ((hi - lo) >> 1)
        cnt = jnp.sum(jnp.where(aff >= pltpu.bitcast(mid, F32), 1.0, 0.0), axis=0, keepdims=True)
        ge = cnt >= cap
        return jnp.where(ge, mid, lo), jnp.where(ge, hi, mid)

    lo0 = jnp.zeros((1, n_col), I32)
    hi0 = jnp.full((1, n_col), 0x7F800000, I32)
    thr_bits, _ = lax.fori_loop(0, 31, bisect, (lo0, hi0))
    thr = pltpu.bitcast(thr_bits, F32)
    above = pltpu.bitcast(thr_bits + 1, F32)

    tri = tri_ref[...]

    def excl_prefix(x01):
        outs, before, inside = [], [], []
        off = jnp.zeros((1, n_col), F32)
        for blk in range(n_blk):
            xb = x01[blk * PREFIX_BLOCK:(blk + 1) * PREFIX_BLOCK, :]
            outs.append(_dot(tri, xb.astype(BF16)) + off)
            cnt = jnp.sum(xb, axis=0, keepdims=True)
            before.append(off)
            inside.append(cnt)
            off = off + cnt
        return (outs[0] if n_blk == 1 else jnp.concatenate(outs, axis=0)), before + inside

    gt = jnp.where(aff >= above, 1.0, 0.0)
    eq = jnp.where(aff >= thr, 1.0, 0.0) - gt
    need = cap - jnp.sum(gt, axis=0, keepdims=True)
    eq_rank, _ = excl_prefix(eq)
    sel = gt + jnp.where(eq_rank < need, eq, 0.0)
    pos, blk_rows = excl_prefix(sel)
    posm = jnp.where(sel > 0.0, pos, -1.0)
    posr_ref[...] = posm.T
    affr_ref[...] = aff.T
    for i, row in enumerate(blk_rows):
        blk_ref[i:i + 1, :] = row
    if place_in_kernel:
        posm_bf16 = posm.astype(BF16)
        for b in range(n_req):
            posm_ref[b * tokens:(b + 1) * tokens, :] = _dot_nt(posm_bf16, place_ref[b])
    else:
        posm_ref[...] = posm


PLACE_IN_KERNEL_MAX_REQ = 8


def _route_requests(aff, tri, n_req, tokens, cap):
    n_real = n_req * N_EXPERTS
    n_col = -(-n_real // LANES) * LANES
    n_blk = tokens // PREFIX_BLOCK
    place_in_kernel = n_req <= PLACE_IN_KERNEL_MAX_REQ
    whole = lambda shape: pl.BlockSpec(shape, lambda i: (0,) * len(shape))
    if place_in_kernel:
        col = np.arange(n_col)
        place = jnp.asarray(col[None, None, :] == (np.arange(n_req)[:, None, None] * N_EXPERTS
                                                   + np.arange(N_EXPERTS)[None, :, None]), BF16)
        args = (aff, tri, place)
        token_major = (n_req * tokens, N_EXPERTS)
        in_specs = [whole(token_major), whole((PREFIX_BLOCK, PREFIX_BLOCK)), whole(place.shape)]
    else:
        aff_t = aff.reshape(n_req, tokens, N_EXPERTS).transpose(1, 0, 2).reshape(tokens, n_real)
        args = (jnp.pad(aff_t, ((0, 0), (0, n_col - n_real))), tri)
        token_major = (tokens, n_col)
        in_specs = [whole(token_major), whole((PREFIX_BLOCK, PREFIX_BLOCK))]
    posm, posr, affr, blk = pl.pallas_call(
        functools.partial(_route_kernel, n_req=n_req, tokens=tokens, cap=cap, place_in_kernel=place_in_kernel),
        grid=(1,),
        in_specs=in_specs,
        out_specs=[whole(token_major), whole((n_col, tokens)), whole((n_col, tokens)), whole((2 * n_blk, n_col))],
        out_shape=[jax.ShapeDtypeStruct(token_major, F32),
                   jax.ShapeDtypeStruct((n_col, tokens), F32),
                   jax.ShapeDtypeStruct((n_col, tokens), F32),
                   jax.ShapeDtypeStruct((2 * n_blk, n_col), F32)],
        compiler_params=_params(1),
        name="route_%d" % tokens,
    )(*args)
    if not place_in_kernel:
        posm = posm[:, :n_real].reshape(tokens, n_req, N_EXPERTS).transpose(1, 0, 2).reshape(n_req * tokens, N_EXPERTS)
    blk = blk[:, :n_real].astype(I32).reshape(2, n_blk, n_req, N_EXPERTS).transpose(0, 2, 1, 3)
    expert_major = lambda a: a.reshape(n_col // N_EXPERTS, N_EXPERTS, tokens)
    return posm, expert_major(posr), expert_major(affr), blk[0], blk[1]


def _slot_gate(hit, aff_row):
    g = jnp.sum(jnp.where(hit, aff_row, 0.0), axis=-1, keepdims=True)
    return jnp.broadcast_to(g, (hit.shape[0], LANES))


def _gather_prompt_kernel(x_ref, posr_ref, affr_ref, xg_ref, g_ref):
    slot = lax.broadcasted_iota(I32, (CAP_P, SEQ), 0).astype(F32)
    for r in range(REQ_TILE):
        tokens = slice(r * SEQ, (r + 1) * SEQ)
        slots = slice(r * CAP_P, (r + 1) * CAP_P)
        posr = posr_ref[r]
        affr = affr_ref[r]
        hits = [posr[e:e + 1, :] == slot for e in range(N_EXPERTS)]
        onehot = jnp.concatenate([jnp.where(h, 1.0, 0.0) for h in hits], axis=0).astype(BF16)
        xg = _dot(onehot, x_ref[tokens, :])
        for e in range(N_EXPERTS):
            xg_ref[e, slots, :] = xg[e * CAP_P:(e + 1) * CAP_P, :].astype(xg_ref.dtype)
            g_ref[e, slots, :] = _slot_gate(hits[e], affr[e:e + 1, :])


def _gather_prompt(xm, posr, affr):
    return pl.pallas_call(
        _gather_prompt_kernel,
        grid=(BATCH // REQ_TILE,),
        in_specs=[pl.BlockSpec((REQ_TILE * SEQ, D_MODEL), lambda b: (b, 0)),
                  pl.BlockSpec((REQ_TILE, N_EXPERTS, SEQ), lambda b: (b, 0, 0)),
                  pl.BlockSpec((REQ_TILE, N_EXPERTS, SEQ), lambda b: (b, 0, 0))],
        out_specs=[pl.BlockSpec((N_EXPERTS, REQ_TILE * CAP_P, D_MODEL), lambda b: (0, b, 0)),
                   pl.BlockSpec((N_EXPERTS, REQ_TILE * CAP_P, LANES), lambda b: (0, b, 0))],
        out_shape=[jax.ShapeDtypeStruct((N_EXPERTS, SLOTS_P, D_MODEL), BF16),
                   jax.ShapeDtypeStruct((N_EXPERTS, SLOTS_P, LANES), F32)],
        compiler_params=_params(1),
        name="gather_prompt",
    )(xm, posr, affr)


TOKEN_BLOCK = PREFIX_BLOCK
N_TOKEN_BLOCKS = DEC_SEQ // TOKEN_BLOCK
SLOT_ALIGN = 16
WINDOW_SLOTS = 64
LAST_WINDOW = CAP_S - WINDOW_SLOTS
ALL_WINDOWS = N_EXPERTS * WINDOW_SLOTS


def _window_plan(before, inside):
    start = (before // SLOT_ALIGN) * SLOT_ALIGN
    span = before - start + inside
    passes = jnp.where(inside > 0, (span + WINDOW_SLOTS - 1) // WINDOW_SLOTS, 0)
    return start.reshape(-1), jnp.max(passes, axis=-1).reshape(-1)


def _window_bounds(start_ref, step, expert, k):
    lower = start_ref[step * N_EXPERTS + expert] + WINDOW_SLOTS * k
    begin = pl.multiple_of(jnp.minimum(lower, LAST_WINDOW), SLOT_ALIGN)
    return lower, begin


BLOCKS_PER_STEP = 2
STEP_TOKENS = BLOCKS_PER_STEP * TOKEN_BLOCK
STEPS_PER_REQ = N_TOKEN_BLOCKS // BLOCKS_PER_STEP


def _gather_sample_kernel(start_ref, passes_ref, x_ref, posr_ref, affr_ref, xg_ref, g_ref):
    t = pl.program_id(1)
    first_block = (pl.program_id(0) * STEPS_PER_REQ + t) * BLOCKS_PER_STEP

    @pl.when(t == 0)
    def _():
        xg_ref[...] = jnp.zeros_like(xg_ref)
        g_ref[...] = jnp.zeros_like(g_ref)

    row = lax.broadcasted_iota(I32, (WINDOW_SLOTS, TOKEN_BLOCK), 0).astype(F32)

    def pick(j, k):
        tokens = slice(j * TOKEN_BLOCK, (j + 1) * TOKEN_BLOCK)
        hits, begins = [], []
        for e in range(N_EXPERTS):
            lower, begin = _window_bounds(start_ref, first_block + j, e, k)
            pe = posr_ref[e:e + 1, tokens]
            hits.append((pe - begin.astype(F32) == row) & (pe >= lower.astype(F32)))
            begins.append(begin)
        onehot = jnp.concatenate([jnp.where(h, 1.0, 0.0) for h in hits], axis=0).astype(BF16)
        return _dot(onehot, x_ref[tokens, :]), hits, begins

    def place(j, picked):
        rows_x, hits, begins = picked
        tokens = slice(j * TOKEN_BLOCK, (j + 1) * TOKEN_BLOCK)
        for e in range(N_EXPERTS):
            win = pl.ds(begins[e], WINDOW_SLOTS)
            xg_ref[e, win, :] += rows_x[e * WINDOW_SLOTS:(e + 1) * WINDOW_SLOTS, :].astype(xg_ref.dtype)
            g_ref[e, win, :] += _slot_gate(hits[e], affr_ref[e:e + 1, tokens])

    _software_pipeline(range(BLOCKS_PER_STEP), lambda j: pick(j, 0), place)

    for j in range(BLOCKS_PER_STEP):
        def extra_pass(k, carry, j=j):
            place(j, pick(j, k))
            return carry

        lax.fori_loop(1, passes_ref[first_block + j], extra_pass, 0)


def _gather_sample(xm, posr, affr, starts, passes):
    expert_rows = pl.BlockSpec((None, N_EXPERTS, STEP_TOKENS), lambda b, t, s, p: (b, 0, t))
    grid_spec = pltpu.PrefetchScalarGridSpec(
        num_scalar_prefetch=2,
        grid=(DEC_BATCH, STEPS_PER_REQ),
        in_specs=[pl.BlockSpec((STEP_TOKENS, D_MODEL), lambda b, t, s, p: (b * STEPS_PER_REQ + t, 0)),
                  expert_rows, expert_rows],
        out_specs=[pl.BlockSpec((N_EXPERTS, CAP_S, D_MODEL), lambda b, t, s, p: (0, b, 0)),
                   pl.BlockSpec((N_EXPERTS, CAP_S, LANES), lambda b, t, s, p: (0, b, 0))],
    )
    return pl.pallas_call(
        _gather_sample_kernel,
        grid_spec=grid_spec,
        out_shape=[jax.ShapeDtypeStruct((N_EXPERTS, SLOTS_S, D_MODEL), BF16),
                   jax.ShapeDtypeStruct((N_EXPERTS, SLOTS_S, LANES), F32)],
        compiler_params=_params(2),
        name="gather_sample",
    )(starts, passes, xm, posr, affr)


FF_TILE = 512
SLOT_CHUNK = 1024


def _ffn_kernel(xp_ref, xs_ref, gp_ref, gs_ref, w1_ref, w3_ref, w2_ref, yp_ref, ys_ref, acc_ref):
    f = pl.program_id(1)
    n_f = pl.num_programs(1)
    @pl.when(f == 0)
    def _():
        acc_ref[...] = jnp.zeros_like(acc_ref)

    cast = {}

    def weight(ref):
        if id(ref) not in cast:
            cast[id(ref)] = ref[...].astype(BF16)
        return cast[id(ref)]

    def up_project(chunk):
        x_ref, c, _ = chunk
        x = x_ref[c * SLOT_CHUNK:(c + 1) * SLOT_CHUNK, :]
        h1 = _dot(x, weight(w1_ref))
        return h1, _dot(x, weight(w3_ref))

    def down_project(chunk, h):
        _, c, base = chunk
        h1, h3 = h
        hid = (h1 * jax.nn.sigmoid(h1) * h3).astype(BF16)
        rows = slice(base + c * SLOT_CHUNK, base + (c + 1) * SLOT_CHUNK)
        acc_ref[rows, :] += _dot(hid, weight(w2_ref))

    chunks = [(x_ref, c, base) for x_ref, n_slot, base in ((xp_ref, SLOTS_P, 0), (xs_ref, SLOTS_S, SLOTS_P))
              for c in range(n_slot // SLOT_CHUNK)]
    _software_pipeline(chunks, up_project, down_project)

    @pl.when(f == n_f - 1)
    def _():
        for y_ref, g_ref, n_slot, base in ((yp_ref, gp_ref, SLOTS_P, 0), (ys_ref, gs_ref, SLOTS_S, SLOTS_P)):
            for c in range(n_slot // SLOT_CHUNK):
                rows = slice(c * SLOT_CHUNK, (c + 1) * SLOT_CHUNK)
                gate = jnp.concatenate([g_ref[rows, :]] * (D_MODEL // LANES), axis=1)
                y_ref[rows, :] = (acc_ref[base + c * SLOT_CHUNK:base + (c + 1) * SLOT_CHUNK, :]
                                  * gate).astype(y_ref.dtype)


def _expert_ffn(xgp, xgs, gp, gs, w1, w3, w2, layer):
    slots = lambda n, width: pl.BlockSpec((None, n, width), lambda e, f: (e, 0, 0))
    return pl.pallas_call(
        _ffn_kernel,
        grid=(N_EXPERTS, EXPERT_FF // FF_TILE),
        in_specs=[slots(SLOTS_P, D_MODEL), slots(SLOTS_S, D_MODEL), slots(SLOTS_P, LANES), slots(SLOTS_S, LANES),
                  pl.BlockSpec((None, None, D_MODEL, FF_TILE), lambda e, f: (layer, e, 0, f)),
                  pl.BlockSpec((None, None, D_MODEL, FF_TILE), lambda e, f: (layer, e, 0, f)),
                  pl.BlockSpec((None, None, FF_TILE, D_MODEL), lambda e, f: (layer, e, f, 0))],
        out_specs=[slots(SLOTS_P, D_MODEL), slots(SLOTS_S, D_MODEL)],
        out_shape=[jax.ShapeDtypeStruct((N_EXPERTS, SLOTS_P, D_MODEL), BF16),
                   jax.ShapeDtypeStruct((N_EXPERTS, SLOTS_S, D_MODEL), BF16)],
        scratch_shapes=[pltpu.VMEM((SLOTS_P + SLOTS_S, D_MODEL), F32)],
        compiler_params=_params(2),
        name="expert_ffn",
    )(xgp, xgs, gp, gs, w1, w3, w2)


def _combine_prompt_kernel(posm_ref, y_ref, x_hbm, mod_ref, exp_ref, lg_ref, lb_ref, o_ref, x_ring, x_sem):
    x_ref = _ring_tile(x_hbm, x_ring, x_sem)
    m = mod_ref[...]
    slot = (lax.broadcasted_iota(I32, (SEQ, N_EXPERTS * CAP_P), 1) & (CAP_P - 1)).astype(F32)
    def scatter(r):
        tokens = slice(r * SEQ, (r + 1) * SEQ)
        spread = _dot(posm_ref[tokens, :].astype(BF16), exp_ref[...])
        onehot = jnp.where(spread == slot, 1.0, 0.0).astype(BF16)
        y = jnp.concatenate([y_ref[e, r * CAP_P:(r + 1) * CAP_P, :] for e in range(N_EXPERTS)], axis=0)
        return _dot(onehot, y)

    def finish(r, f):
        tokens = slice(r * SEQ, (r + 1) * SEQ)
        o_ref[tokens, :] = _layer_norm(ALPHA * x_ref[tokens, :] + m[5:6] * f, lg_ref[...], lb_ref[...])

    _software_pipeline(range(REQ_TILE), scatter, finish)


def _combine_prompt(posm, y, x1, mod_l, expand, ln_g, ln_b):
    full = lambda a: pl.BlockSpec(a.shape, lambda b: (0,) * a.ndim)
    return pl.pallas_call(
        _combine_prompt_kernel,
        grid=(BATCH // REQ_TILE,),
        in_specs=[pl.BlockSpec((REQ_TILE * SEQ, N_EXPERTS), lambda b: (b, 0)),
                  pl.BlockSpec((N_EXPERTS, REQ_TILE * CAP_P, D_MODEL), lambda b: (0, b, 0)),
                  pl.BlockSpec(memory_space=pl.ANY),
                  pl.BlockSpec((None, 6, D_MODEL), lambda b: (0, 0, 0)),
                  full(expand), full(ln_g), full(ln_b)],
        out_specs=pl.BlockSpec((REQ_TILE * SEQ, D_MODEL), lambda b: (b, 0)),
        out_shape=jax.ShapeDtypeStruct((N_PROMPT, D_MODEL), F32),
        scratch_shapes=[pltpu.VMEM((X_RING, REQ_TILE * SEQ, D_MODEL), F32), pltpu.SemaphoreType.DMA((X_RING,))],
        compiler_params=_params(1),
        name="combine_prompt",
    )(posm, y, x1, mod_l, expand, ln_g, ln_b)


def _combine_sample_kernel(start_ref, passes_ref, posm_ref, y_ref, x_hbm, mod_ref, exp_ref, off_ref,
                           lg_ref, lb_ref, o_ref, x_ring, x_sem):
    step = pl.program_id(0) * STEPS_PER_REQ + pl.program_id(1)
    x_ref = _ring_tile(x_hbm, x_ring, x_sem, step, pl.num_programs(0) * STEPS_PER_REQ)
    first_block = step * BLOCKS_PER_STEP
    exp_bf16 = exp_ref[...]
    offset = off_ref[...]
    m = mod_ref[...]

    def tokens_of(j):
        return slice(j * TOKEN_BLOCK, (j + 1) * TOKEN_BLOCK)

    def one_pass(j, k):
        spread = _dot(posm_ref[tokens_of(j), :].astype(BF16), exp_bf16)
        begin_row = jnp.zeros((1, ALL_WINDOWS), F32)
        lower_row = jnp.zeros((1, ALL_WINDOWS), F32)
        windows = []
        for e in range(N_EXPERTS):
            lower, begin = _window_bounds(start_ref, first_block + j, e, k)
            lanes_e = exp_bf16[e:e + 1, :].astype(F32)
            begin_row = begin_row + begin.astype(F32) * lanes_e
            lower_row = lower_row + lower.astype(F32) * lanes_e
            windows.append(y_ref[e, pl.ds(begin, WINDOW_SLOTS), :])
        hit = (spread - begin_row == offset) & (spread >= lower_row)
        onehot = jnp.where(hit, 1.0, 0.0).astype(BF16)
        return _dot(onehot, jnp.concatenate(windows, axis=0))

    def finish(j, f):
        rows = tokens_of(j)
        o_ref[rows, :] = _layer_norm(ALPHA * x_ref[rows, :] + m[5:6] * f, lg_ref[...], lb_ref[...])

    _software_pipeline(range(BLOCKS_PER_STEP), lambda j: one_pass(j, 0), finish)

    for j in range(BLOCKS_PER_STEP):
        n_pass = passes_ref[first_block + j]

        @pl.when(n_pass > 1)
        def _(j=j, n_pass=n_pass):
            f = lax.fori_loop(1, n_pass, lambda k, f: f + one_pass(j, k), one_pass(j, 0))
            finish(j, f)


def _combine_sample(posm, y, x1, mod_l, ln_g, ln_b, starts, passes):
    lane = np.arange(ALL_WINDOWS)
    expand = jnp.asarray((lane[None, :] // WINDOW_SLOTS) == np.arange(N_EXPERTS)[:, None], BF16)
    offset = jnp.asarray((lane % WINDOW_SLOTS).reshape(1, ALL_WINDOWS), F32)
    row = lambda width: pl.BlockSpec((STEP_TOKENS, width), lambda b, t, s, p: (b * STEPS_PER_REQ + t, 0))
    full = lambda a: pl.BlockSpec(a.shape, lambda b, t, s, p: (0,) * a.ndim)
    grid_spec = pltpu.PrefetchScalarGridSpec(
        num_scalar_prefetch=2,
        grid=(DEC_BATCH, STEPS_PER_REQ),
        in_specs=[row(N_EXPERTS),
                  pl.BlockSpec((N_EXPERTS, CAP_S, D_MODEL), lambda b, t, s, p: (0, b, 0)),
                  pl.BlockSpec(memory_space=pl.ANY),
                  pl.BlockSpec((None, 6, D_MODEL), lambda b, t, s, p: (1 + b, 0, 0)),
                  full(expand), full(offset), full(ln_g), full(ln_b)],
        out_specs=row(D_MODEL),
        scratch_shapes=[pltpu.VMEM((X_RING, STEP_TOKENS, D_MODEL), F32), pltpu.SemaphoreType.DMA((X_RING,))],
    )
    return pl.pallas_call(
        _combine_sample_kernel,
        grid_spec=grid_spec,
        out_shape=jax.ShapeDtypeStruct((N_SAMPLE, D_MODEL), F32),
        compiler_params=_params(2),
        name="combine_sample",
    )(starts, passes, posm, y, x1, mod_l, expand, offset, ln_g, ln_b)


def _moe(prompt, sample, mod_l, w1, w3, w2, layer, ln_g, ln_b, tri, expand):
    x1_p, xm_p, aff_p = prompt
    x1_s, xm_s, aff_s = sample
    posm_p, posr_p, affr_p, _, _ = _route_requests(aff_p, tri, BATCH, SEQ, CAP_P)
    posm_s, posr_s, affr_s, before_s, inside_s = _route_requests(aff_s, tri, DEC_BATCH, DEC_SEQ, CAP_S)
    starts, passes = _window_plan(before_s, inside_s)
    xg_p, g_p = _gather_prompt(xm_p, posr_p, affr_p)
    xg_s, g_s = _gather_sample(xm_s, posr_s, affr_s, starts, passes)
    y_p, y_s = _expert_ffn(xg_p, xg_s, g_p, g_s, w1, w3, w2, layer)
    x2_p = _combine_prompt(posm_p, y_p, x1_p, mod_l, expand, ln_g, ln_b)
    x2_s = _combine_sample(posm_s, y_s, x1_s, mod_l, ln_g, ln_b, starts, passes)
    return x2_p, x2_s


def kernel(x_prompt, x_sample, cache_k, cache_v, state_rglru, c, c_ctx, mod_w, mod_b, ln_mix_g, ln_mix_b, ln_ffn_g, ln_ffn_b, ab_in_w, attn_sink, rnn_conv_w, rnn_conv_b, lru_wa, lru_ba, lru_wx, lru_bx, lru_lambda, ab_out_w, sgu_in_w, sgu_in_b, sgu_ln_g, sgu_ln_b, sgu_spatial_w, sgu_spatial_b, sgu_out_w, router_w, moe_w1, moe_w3, moe_w2):
    xp = x_prompt.reshape(N_PROMPT, D_MODEL)
    xs = x_sample.reshape(N_SAMPLE, D_MODEL)

    cvec = jnp.concatenate([c_ctx[None], c, jnp.zeros((SUBLANES - 1 - DEC_BATCH, D_MODEL), F32)], axis=0)
    mod = _modulation(cvec, mod_w, mod_b).reshape(DEPTH, SUBLANES, 6, D_MODEL)

    idx = np.arange(PREFIX_BLOCK)
    tri = jnp.asarray(idx[None, :] < idx[:, None], BF16)
    lane = np.arange(N_EXPERTS * CAP_P)
    expand = jnp.asarray((lane[None, :] // CAP_P) == np.arange(N_EXPERTS)[:, None], BF16)
    rope_tables = _rope_tables()
    vec = lambda a: a.reshape(1, -1)
    router_pad = lambda w: jnp.pad(w, ((0, 0), (0, LANES - N_EXPERTS))).astype(BF16)

    new_k = new_v = new_state = None
    for l in range(DEPTH):
        e = l // 2
        mod_l = mod[l]
        rw = router_pad(router_w[l])
        lg, lb = vec(ln_mix_g[l]), vec(ln_mix_b[l])
        if l % 2 == 0:
            w_in = ab_in_w[e].astype(BF16)
            w_out = ab_out_w[e].astype(BF16)
            wa = _block_diag_dense(lru_wa[e])
            wx = _block_diag_dense(lru_wx[e])
            rnn_w = (rnn_conv_w[e], vec(rnn_conv_b[e]), wa, lru_ba[e], wx, lru_bx[e], lru_lambda[e])

            q, k, v, xr, xg, new_k, new_v = _ab_in(xp, mod_l, w_in, 0, N_PROMPT, None)
            att = _ctx_attention(q, k, v, attn_sink[e])
            rows = 8 * SEQ
            zeros = jnp.zeros((N_PROMPT // rows, rows // SEQ, RNN_WIDTH), F32)
            rnn, hf_last, hb_first = _rglru(xr, xg, *rnn_w, zeros, zeros, rows, SEQ)
            new_state = jnp.stack([hf_last.reshape(BATCH, RNN_WIDTH), hb_first.reshape(BATCH, RNN_WIDTH)], axis=1)
            x_ring = (pltpu.VMEM((X_RING, ROW_TILE, D_MODEL), F32), pltpu.SemaphoreType.DMA((X_RING,)))
            prompt = _mixer_out(_mix_out_kernel, "mix_out", (att, rnn), xp, mod_l, (w_out,),
                                lg, lb, rw, 0, N_PROMPT, x_ring, x_in_hbm=True)

            q, k, v, xr, xg = _ab_in(xs, mod_l, w_in, 1, DEC_SEQ, rope_tables)
            att = _lat_attention(q, k, v, cache_k[:, e].reshape(DEC_BATCH * PAST_LEN, KV_WIDTH),
                                 cache_v[:, e].reshape(DEC_BATCH * PAST_LEN, KV_WIDTH), attn_sink[e])
            h0 = state_rglru[:, e]
            rnn, _, _ = _rglru(xr, xg, *rnn_w, h0[:, 0][:, None, :], h0[:, 1][:, None, :], DEC_SEQ, DEC_SEQ)
            sample = _mixer_out(_mix_out_kernel, "mix_out", (att, rnn), xs, mod_l, (w_out,),
                                lg, lb, rw, 1, DEC_SEQ, x_ring, x_in_hbm=True)
        else:
            w_in = sgu_in_w[e].astype(BF16)
            w_out = sgu_out_w[e].astype(BF16)
            spw = sgu_spatial_w[e].astype(BF16)
            spb = jnp.repeat(sgu_spatial_b[e].T, SGU_GROUP_W, axis=1)
            sgu_w = (w_in, vec(sgu_in_b[e]), vec(sgu_ln_g[e]), vec(sgu_ln_b[e]), spw, spb, w_out)
            scratch = (pltpu.VMEM((ROW_TILE, SGU_WIDTH), BF16), pltpu.VMEM((ROW_TILE, SGU_WIDTH), BF16))
            prompt = _mixer_out(_sgu_kernel, "sgu", (), xp, mod_l, sgu_w, lg, lb, rw, 0, N_PROMPT, scratch)
            sample = _mixer_out(_sgu_kernel, "sgu", (), xs, mod_l, sgu_w, lg, lb, rw, 1, DEC_SEQ, scratch)

        xp, xs = _moe(prompt, sample, mod_l, moe_w1, moe_w3, moe_w2, l,
                      vec(ln_ffn_g[l]), vec(ln_ffn_b[l]), tri, expand)

    return (xp.reshape(BATCH, SEQ, D_MODEL),
            xs.reshape(DEC_BATCH, DEC_SEQ, D_MODEL),
            new_k,
            new_v,
            new_state.reshape(BATCH, 1, 2, RNN_WIDTH))
```
